```python
import functools
import jax, jax.numpy as jnp
from jax import lax
import numpy as np

D_MODEL = 1024
BATCH = 16
SEQ = 256
DEPTH = 1
DEC_BATCH = 8
DEC_SEQ = 2048
PAST_LEN = 256

GRID_W = 64
HEAD_DIM = 64
NA_HEADS = 8
NA_ROWS = 8
NA_COLS = 16
SWA_HEADS = 8
SWA_KV_HEADS = 2
SWA_GROUPS = SWA_HEADS // SWA_KV_HEADS
SWA_WINDOW = 128
BLOCK = 128
FFN_DIM = 2816
ROPE_BASE = 10000.0
EPS = 1e-6
NEG_INF = -1e30
N_MOD = 9
NA_WIDTH = NA_HEADS * HEAD_DIM
SWA_Q_WIDTH = SWA_HEADS * HEAD_DIM
SWA_KV_WIDTH = SWA_KV_HEADS * HEAD_DIM
IN_SPLITS = (NA_WIDTH, 2 * NA_WIDTH, 3 * NA_WIDTH,
             3 * NA_WIDTH + SWA_Q_WIDTH,
             3 * NA_WIDTH + SWA_Q_WIDTH + SWA_KV_WIDTH,
             3 * NA_WIDTH + SWA_Q_WIDTH + 2 * SWA_KV_WIDTH)
IN_COLS = IN_SPLITS[-1] + 2 * D_MODEL

kernel_name = "hybrid_diffusion_na_swa_step"


def rmsnorm(x, g):
    x32 = x.astype(jnp.float32)
    y = x32 * lax.rsqrt(jnp.mean(x32 * x32, axis=-1, keepdims=True) + EPS)
    return y.astype(x.dtype) * g


def adaln(cond, w_ada, b_ada):
    return (jax.nn.silu(cond) @ w_ada + b_ada).reshape(cond.shape[0], N_MOD, D_MODEL)


def modulate(x, g, shift, scale):
    return rmsnorm(x, g) * (1.0 + scale) + shift


def swiglu(h, w_gate, w_up, w_down):
    return (jax.nn.silu(h @ w_gate) * (h @ w_up)) @ w_down


def rope_1d(x, pos):
    half = x.shape[-1] // 2
    freqs = jnp.power(ROPE_BASE, -jnp.arange(half, dtype=jnp.float32) / half)
    ang = pos.astype(jnp.float32)[:, None] * freqs[None, :]
    cos = jnp.cos(ang)[None, :, None, :].astype(x.dtype)
    sin = jnp.sin(ang)[None, :, None, :].astype(x.dtype)
    x1, x2 = x[..., :half], x[..., half:]
    return jnp.concatenate([x1 * cos - x2 * sin, x2 * cos + x1 * sin], axis=-1)


def axial_rope(x):
    t = jnp.arange(x.shape[1])
    return jnp.concatenate([rope_1d(x[..., :HEAD_DIM // 2], t // GRID_W),
                            rope_1d(x[..., HEAD_DIM // 2:], t % GRID_W)], axis=-1)


def softmax_f32(logits):
    return jax.nn.softmax(logits.astype(jnp.float32), axis=-1)


def project_in(h, w_in):
    b, n = h.shape[:2]
    qa, ka, va, qb, kb, vb, gates = jnp.split(h @ w_in, IN_SPLITS, axis=-1)
    heads = lambda t, nh: t.reshape(b, n, nh, HEAD_DIM)
    return (heads(qa, NA_HEADS), heads(ka, NA_HEADS), heads(va, NA_HEADS),
            heads(qb, SWA_HEADS), heads(kb, SWA_KV_HEADS), heads(vb, SWA_KV_HEADS), gates)


def merge_branches(y_na, y_swa, gates, w_branch_na, w_branch_swa, w_out):
    g_na, g_swa = jnp.split(gates, 2, axis=-1)
    merged = jax.nn.sigmoid(g_na) * (y_na @ w_branch_na) + jax.nn.sigmoid(g_swa) * (y_swa @ w_branch_swa)
    return merged @ w_out


def ctx_attention(q, k, v, sink):
    b, l, kvh, g, d = q.shape
    scale = d ** -0.5
    q_blocks = q.reshape(b, l // BLOCK, BLOCK, kvh, g, d).transpose(1, 0, 2, 3, 4, 5)

    def one_block(q_blk):
        logits = jnp.einsum('bqkgd,blkd->bkgql', q_blk, k).astype(jnp.float32) * scale
        if sink is not None:
            s = jnp.broadcast_to(sink.reshape(kvh, g, 1, 1).astype(jnp.float32), (b, kvh, g, BLOCK, 1))
            logits = jnp.concatenate([logits, s], axis=-1)
        p = softmax_f32(logits)[..., :l].astype(v.dtype)
        return jnp.einsum('bkgql,blkd->bqkgd', p, v)

    out = lax.map(one_block, q_blocks)
    return out.transpose(1, 0, 2, 3, 4, 5).reshape(b, l, kvh * g * d)


def na_latent(q, k, v, ck, cv, rel_bias):
    b, n, h, d = q.shape
    rows = n // GRID_W
    kr = min(NA_ROWS, rows)
    l = ck.shape[1]
    scale = d ** -0.5
    qg = q.reshape(b, rows, GRID_W, h, d)
    kg = k.reshape(b, rows, GRID_W, h, d)
    vg = v.reshape(b, rows, GRID_W, h, d)
    r = np.arange(rows)
    row_start = np.clip(r - kr // 2, 0, rows - kr)
    row_idx = row_start[:, None] + np.arange(kr)[None, :]
    k_blk = kg[:, row_idx]
    v_blk = vg[:, row_idx]
    cq = np.arange(GRID_W)
    col_start = np.clip(cq - NA_COLS // 2, 0, GRID_W - NA_COLS)
    kc = np.arange(GRID_W)
    col_mask = (kc[None, :] >= col_start[:, None]) & (kc[None, :] < col_start[:, None] + NA_COLS)
    dr = row_idx - r[:, None] + (NA_ROWS - 1)
    dc = np.clip(kc[None, :] - cq[:, None], -(NA_COLS - 1), NA_COLS - 1) + (NA_COLS - 1)
    bias = rel_bias[:, dr[:, None, :, None], dc[None, :, None, :]].astype(jnp.float32)
    nb_logits = jnp.einsum('brqhd,brikhd->bhrqik', qg, k_blk).astype(jnp.float32) * scale + bias[None]
    nb_logits = jnp.where(col_mask[None, None, None, :, None, :], nb_logits, NEG_INF)
    nb_logits = nb_logits.reshape(b, h, rows, GRID_W, kr * GRID_W)
    ctx_logits = jnp.einsum('brqhd,blhd->bhrql', qg, ck).astype(jnp.float32) * scale
    p = softmax_f32(jnp.concatenate([nb_logits, ctx_logits], axis=-1))
    p_nb = p[..., :kr * GRID_W].reshape(b, h, rows, GRID_W, kr, GRID_W).astype(v.dtype)
    p_ctx = p[..., kr * GRID_W:].astype(v.dtype)
    out = (jnp.einsum('bhrqik,brikhd->brqhd', p_nb, v_blk)
           + jnp.einsum('bhrql,blhd->brqhd', p_ctx, cv))
    return out.reshape(b, n, h * d)


def swa_latent(q, k, v, ck, cv, sink):
    b, n = q.shape[:2]
    nb = n // BLOCK
    l = ck.shape[1]
    scale = HEAD_DIM ** -0.5
    qb = q.reshape(b, nb, BLOCK, SWA_KV_HEADS, SWA_GROUPS, HEAD_DIM)

    def band(t):
        tp = jnp.pad(t, ((0, 0), (BLOCK, BLOCK), (0, 0), (0, 0))).reshape(b, nb + 2, BLOCK, SWA_KV_HEADS, HEAD_DIM)
        return jnp.concatenate([tp[:, :-2], tp[:, 1:-1], tp[:, 2:]], axis=2)

    k_band, v_band = band(k), band(v)
    qi = np.arange(BLOCK)[:, None]
    kj = np.arange(3 * BLOCK)[None, :]
    key_pos = (np.arange(nb)[:, None, None] - 1) * BLOCK + kj[None]
    mask = (np.abs(kj - BLOCK - qi)[None] <= SWA_WINDOW) & (key_pos >= 0) & (key_pos < n)
    band_logits = jnp.einsum('bnqkgd,bnjkd->bkgnqj', qb, k_band).astype(jnp.float32) * scale
    band_logits = jnp.where(mask, band_logits, NEG_INF)
    ctx_logits = jnp.einsum('bnqkgd,blkd->bkgnql', qb, ck).astype(jnp.float32) * scale
    sink_col = jnp.broadcast_to(sink.reshape(SWA_KV_HEADS, SWA_GROUPS, 1, 1, 1).astype(jnp.float32),
                                (b, SWA_KV_HEADS, SWA_GROUPS, nb, BLOCK, 1))
    p = softmax_f32(jnp.concatenate([band_logits, ctx_logits, sink_col], axis=-1))
    p_band = p[..., :3 * BLOCK].astype(v.dtype)
    p_ctx = p[..., 3 * BLOCK:3 * BLOCK + l].astype(v.dtype)
    out = (jnp.einsum('bkgnqj,bnjkd->bnqkgd', p_band, v_band)
           + jnp.einsum('bkgnql,blkd->bnqkgd', p_ctx, cv))
    return out.reshape(b, n, SWA_HEADS * HEAD_DIM)


def context_mixer(h, w_in, swa_sink, w_branch_na, w_branch_swa, w_out):
    b, l = h.shape[:2]
    qa, ka, va, qb, kb, vb, gates = project_in(h, w_in)
    y_na = ctx_attention(qa[:, :, :, None, :], ka, va, None)
    y_swa = ctx_attention(qb.reshape(b, l, SWA_KV_HEADS, SWA_GROUPS, HEAD_DIM), kb, vb, swa_sink)
    return merge_branches(y_na, y_swa, gates, w_branch_na, w_branch_swa, w_out), (ka, va, kb, vb)


def latent_mixer(h, ck_na, cv_na, ck_swa, cv_swa, w_in, na_rel_bias, swa_sink, w_branch_na, w_branch_swa, w_out):
    qa, ka, va, qb, kb, vb, gates = project_in(h, w_in)
    y_na = na_latent(qa, ka, va, ck_na, cv_na, na_rel_bias)
    y_swa = swa_latent(axial_rope(qb), axial_rope(kb), vb, ck_swa, cv_swa, swa_sink)
    return merge_branches(y_na, y_swa, gates, w_branch_na, w_branch_swa, w_out), ()


def macaron_layer(x, mod, mixer, norm_ffn1, f1_gate, f1_up, f1_down, norm_mix, norm_ffn2, f2_gate, f2_up, f2_down):
    sh1, sc1, gt1, shm, scm, gtm, sh2, sc2, gt2 = (mod[:, i, None, :] for i in range(N_MOD))
    x = x + 0.5 * gt1 * swiglu(modulate(x, norm_ffn1, sh1, sc1), f1_gate, f1_up, f1_down)
    y, aux = mixer(modulate(x, norm_mix, shm, scm))
    x = x + gtm * y
    x = x + 0.5 * gt2 * swiglu(modulate(x, norm_ffn2, sh2, sc2), f2_gate, f2_up, f2_down)
    return x, aux


def setup_inputs(seed: int = 0) -> dict:
    key = jax.random.key(seed)
    ks = jax.random.split(key, 32)
    nrm = lambda k, shape, s: jax.random.normal(k, shape, jnp.float32) * s
    D, F = D_MODEL, FFN_DIM
    return {
        "x_prompt": nrm(ks[0], (BATCH, SEQ, D), 1.0),
        "x_sample": nrm(ks[1], (DEC_BATCH, DEC_SEQ, D), 1.0),
        "cache_na_k": nrm(ks[2], (DEC_BATCH, DEPTH, PAST_LEN, NA_HEADS, HEAD_DIM), 1.0),
        "cache_na_v": nrm(ks[3], (DEC_BATCH, DEPTH, PAST_LEN, NA_HEADS, HEAD_DIM), 1.0),
        "cache_swa_k": nrm(ks[4], (DEC_BATCH, DEPTH, PAST_LEN, SWA_KV_HEADS, HEAD_DIM), 1.0),
        "cache_swa_v": nrm(ks[5], (DEC_BATCH, DEPTH, PAST_LEN, SWA_KV_HEADS, HEAD_DIM), 1.0),
        "c": nrm(ks[6], (DEC_BATCH, D), 1.0),
        "c_ctx": nrm(ks[7], (D,), 1.0),
        "w_ada": nrm(ks[8], (DEPTH, D, N_MOD * D), 0.5 * D ** -0.5),
        "b_ada": nrm(ks[9], (DEPTH, N_MOD * D), 0.01),
        "norm_ffn1": 1.0 + nrm(ks[10], (DEPTH, D), 0.01),
        "ffn1_w_gate": nrm(ks[11], (DEPTH, D, F), D ** -0.5),
        "ffn1_w_up": nrm(ks[12], (DEPTH, D, F), D ** -0.5),
        "ffn1_w_down": nrm(ks[13], (DEPTH, F, D), F ** -0.5),
        "norm_mix": 1.0 + nrm(ks[14], (DEPTH, D), 0.01),
        "w_in": nrm(ks[15], (DEPTH, D, IN_COLS), D ** -0.5),
        "na_rel_bias": nrm(ks[16], (DEPTH, NA_HEADS, 2 * NA_ROWS - 1, 2 * NA_COLS - 1), 0.1),
        "swa_sink": nrm(ks[17], (DEPTH, SWA_HEADS), 0.5),
        "w_branch_na": nrm(ks[18], (DEPTH, NA_WIDTH, D), NA_WIDTH ** -0.5),
        "w_branch_swa": nrm(ks[19], (DEPTH, SWA_Q_WIDTH, D), SWA_Q_WIDTH ** -0.5),
        "w_out": nrm(ks[20], (DEPTH, D, D), D ** -0.5),
        "norm_ffn2": 1.0 + nrm(ks[21], (DEPTH, D), 0.01),
        "ffn2_w_gate": nrm(ks[22], (DEPTH, D, F), D ** -0.5),
        "ffn2_w_up": nrm(ks[23], (DEPTH, D, F), D ** -0.5),
        "ffn2_w_down": nrm(ks[24], (DEPTH, F, D), F ** -0.5),
        "norm_final": 1.0 + nrm(ks[25], (D,), 0.01),
    }


def reference(x_prompt, x_sample, cache_na_k, cache_na_v, cache_swa_k, cache_swa_v, c, c_ctx,
              w_ada, b_ada, norm_ffn1, ffn1_w_gate, ffn1_w_up, ffn1_w_down, norm_mix, w_in,
              na_rel_bias, swa_sink, w_branch_na, w_branch_swa, w_out, norm_ffn2,
              ffn2_w_gate, ffn2_w_up, ffn2_w_down, norm_final):
    x = x_prompt
    new_na_k, new_na_v, new_swa_k, new_swa_v = [], [], [], []
    for layer in range(DEPTH):
        mod = adaln(c_ctx[None, :], w_ada[layer], b_ada[layer])
        mixer = functools.partial(context_mixer, w_in=w_in[layer], swa_sink=swa_sink[layer],
                                  w_branch_na=w_branch_na[layer], w_branch_swa=w_branch_swa[layer],
                                  w_out=w_out[layer])
        x, (ka, va, kb, vb) = macaron_layer(x, mod, mixer, norm_ffn1[layer], ffn1_w_gate[layer],
                                            ffn1_w_up[layer], ffn1_w_down[layer], norm_mix[layer],
                                            norm_ffn2[layer], ffn2_w_gate[layer], ffn2_w_up[layer],
                                            ffn2_w_down[layer])
        new_na_k.append(ka)
        new_na_v.append(va)
        new_swa_k.append(kb)
        new_swa_v.append(vb)
    y_prompt = rmsnorm(x, norm_final)

    x = x_sample
    for layer in range(DEPTH):
        mod = adaln(c, w_ada[layer], b_ada[layer])
        mixer = functools.partial(latent_mixer, ck_na=cache_na_k[:, layer], cv_na=cache_na_v[:, layer],
                                  ck_swa=cache_swa_k[:, layer], cv_swa=cache_swa_v[:, layer],
                                  w_in=w_in[layer], na_rel_bias=na_rel_bias[layer], swa_sink=swa_sink[layer],
                                  w_branch_na=w_branch_na[layer], w_branch_swa=w_branch_swa[layer],
                                  w_out=w_out[layer])
        x, _ = macaron_layer(x, mod, mixer, norm_ffn1[layer], ffn1_w_gate[layer],
                             ffn1_w_up[layer], ffn1_w_down[layer], norm_mix[layer],
                             norm_ffn2[layer], ffn2_w_gate[layer], ffn2_w_up[layer],
                             ffn2_w_down[layer])
    y_sample = rmsnorm(x, norm_final)

    return (y_prompt, y_sample, jnp.stack(new_na_k, axis=1), jnp.stack(new_na_v, axis=1),
            jnp.stack(new_swa_k, axis=1), jnp.stack(new_swa_v, axis=1))
```

```python
import functools

import numpy as np
import jax
import jax.numpy as jnp
from jax import lax
from jax.experimental import pallas as pl
from jax.experimental.pallas import tpu as pltpu

F32 = jnp.float32
BF16 = jnp.bfloat16

D_MODEL = 1024
FFN_DIM = 2816
HEAD_DIM = 64
N_MOD = 9
GRID_W = 64
NA_HEADS = 8
NA_ROWS = 8
NA_COLS = 16
SWA_HEADS = 8
SWA_KV_HEADS = 2
SWA_WINDOW = 128
SWA_BLOCK = 128
ROPE_BASE = 10000.0
EPS = 1e-6
NEG_INF = -1e30
NA_WIDTH = NA_HEADS * HEAD_DIM
SWA_Q_WIDTH = SWA_HEADS * HEAD_DIM
SWA_KV_WIDTH = SWA_KV_HEADS * HEAD_DIM
QKV_COLS = 3 * NA_WIDTH + SWA_Q_WIDTH + 2 * SWA_KV_WIDTH
LANES = 128
MOD_ROWS = 16
VMEM_LIMIT = 56 * 1024 * 1024
TOKEN_TILE = 512
FFN_CHUNK = 256


def _dot(a, b):
    return jnp.dot(a, b, preferred_element_type=F32)


def _dot_nt(a, b):
    return lax.dot_general(a, b, (((1,), (1,)), ((), ())), preferred_element_type=F32)


def _silu(x):
    return x * jax.nn.sigmoid(x)


def _rms(x, g):
    return x * lax.rsqrt(jnp.mean(x * x, axis=-1, keepdims=True) + EPS) * g


def _modulate(x, g, shift, scale):
    return _rms(x, g) * (1.0 + scale) + shift


def _resident(shape):
    nd = len(shape)
    return pl.BlockSpec(shape, lambda *_: (0,) * nd, pipeline_mode=pl.Buffered(1))


def _params():
    return pltpu.CompilerParams(dimension_semantics=("arbitrary",), vmem_limit_bytes=VMEM_LIMIT)


def _adaln_kernel(c_ref, w_ref, b_ref, o_ref):
    s = _silu(c_ref[...]).astype(BF16)
    o_ref[...] = _dot(s, w_ref[...].astype(BF16)) + b_ref[...]


def _adaln(cond, w_ada, b_ada):
    n = w_ada.shape[1]
    blk = D_MODEL
    return pl.pallas_call(
        _adaln_kernel,
        grid=(n // blk,),
        in_specs=[pl.BlockSpec((MOD_ROWS, D_MODEL), lambda j: (0, 0)),
                  pl.BlockSpec((D_MODEL, blk), lambda j: (0, j)),
                  pl.BlockSpec((1, blk), lambda j: (0, j))],
        out_specs=pl.BlockSpec((MOD_ROWS, blk), lambda j: (0, j)),
        out_shape=jax.ShapeDtypeStruct((MOD_ROWS, n), F32),
        compiler_params=_params(),
        name="adaln",
    )(cond, w_ada, b_ada.reshape(1, n))


def _mod_spec(tokens_per_row, first_row):
    tiles_per_row = tokens_per_row // TOKEN_TILE
    return pl.BlockSpec((1, N_MOD, D_MODEL), lambda i: (first_row + i // tiles_per_row, 0, 0))


def _ffn_kernel(x_ref, mod_ref, g_ref, wg_ref, wu_ref, wd_ref, gf_ref, o_ref, a_scr, *, first, final):
    x = x_ref[...]
    mod = mod_ref[0]
    shift, scale, gate = mod[first:first + 1], mod[first + 1:first + 2], mod[first + 2:first + 3]
    h = _modulate(x, g_ref[...], shift, scale).astype(BF16)
    for c in range(FFN_DIM // FFN_CHUNK):
        sl = slice(c * FFN_CHUNK, (c + 1) * FFN_CHUNK)
        a_scr[:, sl] = (_silu(_dot(h, wg_ref[:, sl])) * _dot(h, wu_ref[:, sl])).astype(BF16)
    y = x + (0.5 * gate) * _dot(a_scr[...], wd_ref[...])
    if final:
        y = _rms(y, gf_ref[...])
    o_ref[...] = y


def _ffn(x, mod, g, wg, wu, wd, gf, *, tokens_per_row, first_row, first, final):
    t = x.shape[0]
    tile = pl.BlockSpec((TOKEN_TILE, D_MODEL), lambda i: (i, 0))
    return pl.pallas_call(
        functools.partial(_ffn_kernel, first=first, final=final),
        grid=(t // TOKEN_TILE,),
        in_specs=[tile, _mod_spec(tokens_per_row, first_row), _resident((1, D_MODEL)),
                  _resident(wg.shape), _resident(wu.shape), _resident(wd.shape), _resident((1, D_MODEL))],
        out_specs=tile,
        out_shape=jax.ShapeDtypeStruct((t, D_MODEL), F32),
        scratch_shapes=[pltpu.VMEM((TOKEN_TILE, FFN_DIM), BF16)],
        compiler_params=_params(),
        name="ffn",
    )(x, mod, g, wg, wu, wd, gf)


def _swap16(x):
    lane = lax.broadcasted_iota(jnp.int32, x.shape, 1)
    return jnp.where(lane % 32 < 16, pltpu.roll(x, LANES - 16, 1), pltpu.roll(x, 16, 1))


def _proj_ctx_kernel(x_ref, mod_ref, g_ref, w_ref, qa_ref, ka_ref, va_ref, qb_ref, kb_ref, vb_ref):
    mod = mod_ref[0]
    h = _modulate(x_ref[...], g_ref[...], mod[3:4], mod[4:5]).astype(BF16)
    o = 0
    for ref, width, scale in ((qa_ref, NA_WIDTH, True), (ka_ref, NA_WIDTH, False), (va_ref, NA_WIDTH, False),
                              (qb_ref, SWA_Q_WIDTH, True), (kb_ref, SWA_KV_WIDTH, False),
                              (vb_ref, SWA_KV_WIDTH, False)):
        y = _dot(h, w_ref[:, o:o + width])
        if scale:
            y = y * (HEAD_DIM ** -0.5)
        ref[...] = y.astype(ref.dtype)
        o += width


def _proj_lat_kernel(x_ref, mod_ref, g_ref, w_ref, cos_ref, sin_ref,
                     qa_ref, ka_ref, va_ref, qb_ref, kb_ref, vb_ref):
    mod = mod_ref[0]
    h = _modulate(x_ref[...], g_ref[...], mod[3:4], mod[4:5]).astype(BF16)
    cos, sin = cos_ref[...], sin_ref[...]
    o = 0
    for ref, width, scale, rope in ((qa_ref, NA_WIDTH, True, False), (ka_ref, NA_WIDTH, False, False),
                                    (va_ref, NA_WIDTH, False, False), (qb_ref, SWA_Q_WIDTH, True, True),
                                    (kb_ref, SWA_KV_WIDTH, False, True), (vb_ref, SWA_KV_WIDTH, False, False)):
        for j in range(width // LANES):
            y = _dot(h, w_ref[:, o + j * LANES:o + (j + 1) * LANES])
            if rope:
                y = y * cos + _swap16(y) * sin
            if scale:
                y = y * (HEAD_DIM ** -0.5)
            ref[:, j * LANES:(j + 1) * LANES] = y.astype(ref.dtype)
        o += width


def _proj(x, mod, g, w_qkv, rope, *, tokens_per_row, first_row):
    t = x.shape[0]
    tile = lambda w: pl.BlockSpec((TOKEN_TILE, w), lambda i: (i, 0))
    widths = (NA_WIDTH, NA_WIDTH, NA_WIDTH, SWA_Q_WIDTH, SWA_KV_WIDTH, SWA_KV_WIDTH)
    in_specs = [tile(D_MODEL), _mod_spec(tokens_per_row, first_row), _resident((1, D_MODEL)),
                _resident(w_qkv.shape)]
    if rope is None:
        body, args = _proj_ctx_kernel, (x, mod, g, w_qkv)
        dtypes = (BF16, F32, F32, BF16, F32, F32)
    else:
        body, args = _proj_lat_kernel, (x, mod, g, w_qkv) + rope
        dtypes = (BF16,) * 6
        tiles_per_row = tokens_per_row // TOKEN_TILE
        in_specs += [pl.BlockSpec((TOKEN_TILE, LANES), lambda i: (i % tiles_per_row, 0))] * 2
    return pl.pallas_call(
        body,
        grid=(t // TOKEN_TILE,),
        in_specs=in_specs,
        out_specs=[tile(w) for w in widths],
        out_shape=[jax.ShapeDtypeStruct((t, w), dt) for w, dt in zip(widths, dtypes)],
        compiler_params=_params(),
        name="proj",
    )(*args)


def _lane_halves(shape):
    lane = lax.broadcasted_iota(jnp.int32, shape, 1) % LANES
    return lane < HEAD_DIM, lane >= HEAD_DIM


def _softmax_parts(parts, sink=None):
    m = functools.reduce(jnp.maximum, [jnp.max(p, axis=-1, keepdims=True) for p in parts])
    if sink is not None:
        m = jnp.maximum(m, sink)
    es = [jnp.exp(p - m) for p in parts]
    l = functools.reduce(jnp.add, [jnp.sum(e, axis=-1, keepdims=True) for e in es])
    if sink is not None:
        l = l + jnp.exp(sink - m)
    return es, l


def _attn_ctx_kernel(sink_ref, qa_ref, ka_ref, va_ref, qb_ref, kb_ref, vb_ref, ya_ref, yb_ref):
    lo, hi = _lane_halves((ka_ref.shape[0], LANES))
    zero = jnp.zeros((), BF16)
    for p in range(NA_HEADS // 2):
        sl = slice(p * LANES, (p + 1) * LANES)
        q = qa_ref[:, sl]
        k = ka_ref[:, sl].astype(BF16)
        v = va_ref[:, sl].astype(BF16)
        out = None
        for half in (lo, hi):
            (e,), l = _softmax_parts([_dot_nt(q, jnp.where(half, k, zero))])
            o = _dot(e.astype(BF16), jnp.where(half, v, zero)) * (1.0 / l)
            out = o if out is None else out + o
        ya_ref[:, sl] = out.astype(ya_ref.dtype)
    kf, vf = kb_ref[...], vb_ref[...]
    kr, vr = pltpu.roll(kf, HEAD_DIM, 1), pltpu.roll(vf, HEAD_DIM, 1)
    for p in range(SWA_HEADS // 2):
        sl = slice(p * LANES, (p + 1) * LANES)
        q = qb_ref[:, sl]
        kv = (2 * p) // (SWA_HEADS // SWA_KV_HEADS)
        out = None
        for idx, half in enumerate((lo, hi)):
            ksrc, vsrc = (kf, vf) if idx == kv else (kr, vr)
            sink = sink_ref[2 * p + idx]
            (e,), l = _softmax_parts([_dot_nt(q, jnp.where(half, ksrc, 0.0).astype(BF16))], sink)
            o = _dot(e.astype(BF16), jnp.where(half, vsrc, 0.0).astype(BF16)) * (1.0 / l)
            out = o if out is None else out + o
        yb_ref[:, sl] = out.astype(yb_ref.dtype)


def _attn_ctx(sink, qa, ka, va, qb, kb, vb, *, seq):
    t = qa.shape[0]
    tile = lambda w: pl.BlockSpec((seq, w), lambda b: (b, 0))
    return pl.pallas_call(
        _attn_ctx_kernel,
        grid=(t // seq,),
        in_specs=[pl.BlockSpec(memory_space=pltpu.SMEM), tile(NA_WIDTH), tile(NA_WIDTH), tile(NA_WIDTH),
                  tile(SWA_Q_WIDTH), tile(SWA_KV_WIDTH), tile(SWA_KV_WIDTH)],
        out_specs=[tile(NA_WIDTH), tile(SWA_Q_WIDTH)],
        out_shape=[jax.ShapeDtypeStruct((t, NA_WIDTH), BF16), jax.ShapeDtypeStruct((t, SWA_Q_WIDTH), BF16)],
        compiler_params=_params(),
        name="attn_ctx",
    )(sink, qa, ka, va, qb, kb, vb)


def _attn_na_kernel(q_ref, k_ref, v_ref, ck_ref, cv_ref, bias_ref, y_ref, k_scr, v_scr, *, n, ctx):
    rows = n // GRID_W
    zero = jnp.zeros((), BF16)
    for src, csrc, dst in ((k_ref, ck_ref, k_scr), (v_ref, cv_ref, v_scr)):
        for idx, half in enumerate(_lane_halves((n, NA_WIDTH))):
            dst[idx, 0:n, :] = jnp.where(half, src[...], zero)
        for idx, half in enumerate(_lane_halves((ctx, NA_WIDTH))):
            dst[idx, n:n + ctx, :] = jnp.where(half, csrc[...].astype(BF16), zero)

    def row_body(r, carry):
        start = jnp.clip(r - NA_ROWS // 2, 0, rows - NA_ROWS)
        off = r - start
        q0 = pl.multiple_of(r * GRID_W, GRID_W)
        k0 = pl.multiple_of(start * GRID_W, GRID_W)
        for p in range(NA_HEADS // 2):
            sl = slice(p * LANES, (p + 1) * LANES)
            q = q_ref[pl.ds(q0, GRID_W), sl]
            out = None
            for idx in range(2):
                s_nb = _dot_nt(q, k_scr[idx, pl.ds(k0, NA_ROWS * GRID_W), sl]) + bias_ref[off, 2 * p + idx]
                s_ctx = _dot_nt(q, k_scr[idx, n:n + ctx, sl])
                (e_nb, e_ctx), l = _softmax_parts([s_nb, s_ctx])
                o = (_dot(e_nb.astype(BF16), v_scr[idx, pl.ds(k0, NA_ROWS * GRID_W), sl])
                     + _dot(e_ctx.astype(BF16), v_scr[idx, n:n + ctx, sl])) * (1.0 / l)
                out = o if out is None else out + o
            y_ref[pl.ds(q0, GRID_W), sl] = out.astype(y_ref.dtype)
        return carry

    lax.fori_loop(0, rows, row_body, 0)


def _attn_na(q, k, v, ck, cv, bias, *, n, ctx):
    t = q.shape[0]
    tile = pl.BlockSpec((n, NA_WIDTH), lambda b: (b, 0))
    ctile = pl.BlockSpec((ctx, NA_WIDTH), lambda b: (b, 0))
    return pl.pallas_call(
        functools.partial(_attn_na_kernel, n=n, ctx=ctx),
        grid=(t // n,),
        in_specs=[tile, tile, tile, ctile, ctile, _resident(bias.shape)],
        out_specs=tile,
        out_shape=jax.ShapeDtypeStruct((t, NA_WIDTH), BF16),
        scratch_shapes=[pltpu.VMEM((2, n + ctx, NA_WIDTH), BF16)] * 2,
        compiler_params=_params(),
        name="attn_na",
    )(q, k, v, ck, cv, bias)


def _na_bias_table(rel_bias):
    off = np.arange(NA_ROWS)
    dr = np.arange(NA_ROWS)[None, :] - off[:, None] + (NA_ROWS - 1)
    cq = np.arange(GRID_W)
    kc = np.arange(GRID_W)
    col_start = np.clip(cq - NA_COLS // 2, 0, GRID_W - NA_COLS)
    col_mask = (kc[None, :] >= col_start[:, None]) & (kc[None, :] < col_start[:, None] + NA_COLS)
    dc = np.clip(kc[None, :] - cq[:, None], -(NA_COLS - 1), NA_COLS - 1) + (NA_COLS - 1)
    tab = rel_bias[:, dr[:, None, :, None], dc[None, :, None, :]]
    tab = jnp.where(col_mask[None, None, :, None, :], tab, NEG_INF)
    return tab.transpose(1, 0, 2, 3, 4).reshape(NA_ROWS, NA_HEADS, GRID_W, NA_ROWS * GRID_W)


def _attn_swa_kernel(sink_ref, q_ref, k_ref, v_ref, ck_ref, cv_ref, y_ref, k_scr, v_scr, *, n, ctx):
    group = SWA_HEADS // SWA_KV_HEADS
    band = 3 * SWA_BLOCK
    for src, csrc, dst in ((k_ref, ck_ref, k_scr), (v_ref, cv_ref, v_scr)):
        for lo_row, hi_row, x in ((0, n, src[...].astype(F32)), (n, n + ctx, csrc[...])):
            xr = pltpu.roll(x, HEAD_DIM, 1)
            halves = _lane_halves(x.shape)
            for kv in range(SWA_KV_HEADS):
                for idx, half in enumerate(halves):
                    dst[2 * kv + idx, lo_row:hi_row, :] = jnp.where(half, x if idx == kv else xr, 0.0).astype(BF16)

    qi = lax.broadcasted_iota(jnp.int32, (SWA_BLOCK, band), 0)
    kj = lax.broadcasted_iota(jnp.int32, (SWA_BLOCK, band), 1)
    top = lax.broadcasted_iota(jnp.int32, (2 * SWA_BLOCK, 1), 0) < SWA_BLOCK

    def block_body(b, carry):
        q0 = pl.multiple_of(b * SWA_BLOCK, SWA_BLOCK)
        k0 = pl.multiple_of(jnp.clip(q0 - SWA_BLOCK, 0, n - band), SWA_BLOCK)
        mask = jnp.where(jnp.abs(kj - qi + (k0 - q0)) <= SWA_WINDOW, 0.0, NEG_INF)
        mask2 = jnp.concatenate([mask, mask], axis=0)
        for kv in range(SWA_KV_HEADS):
            pairs = (group // 2 * kv, group // 2 * kv + 1)
            q = jnp.concatenate([q_ref[pl.ds(q0, SWA_BLOCK), p * LANES:(p + 1) * LANES] for p in pairs], axis=0)
            out = None
            for idx in range(2):
                sink = jnp.where(top, sink_ref[2 * pairs[0] + idx], sink_ref[2 * pairs[1] + idx])
                s_band = _dot_nt(q, k_scr[2 * kv + idx, pl.ds(k0, band), :]) + mask2
                s_ctx = _dot_nt(q, k_scr[2 * kv + idx, n:n + ctx, :])
                (e_band, e_ctx), l = _softmax_parts([s_band, s_ctx], sink)
                o = (_dot(e_band.astype(BF16), v_scr[2 * kv + idx, pl.ds(k0, band), :])
                     + _dot(e_ctx.astype(BF16), v_scr[2 * kv + idx, n:n + ctx, :])) * (1.0 / l)
                out = o if out is None else out + o
            for i, p in enumerate(pairs):
                y_ref[pl.ds(q0, SWA_BLOCK), p * LANES:(p + 1) * LANES] = (
                    out[i * SWA_BLOCK:(i + 1) * SWA_BLOCK].astype(y_ref.dtype))
        return carry

    lax.fori_loop(0, n // SWA_BLOCK, block_body, 0)


def _attn_swa(sink, q, k, v, ck, cv, *, n, ctx):
    t = q.shape[0]
    tile = lambda w: pl.BlockSpec((n, w), lambda b: (b, 0))
    ctile = pl.BlockSpec((ctx, SWA_KV_WIDTH), lambda b: (b, 0))
    return pl.pallas_call(
        functools.partial(_attn_swa_kernel, n=n, ctx=ctx),
        grid=(t // n,),
        in_specs=[pl.BlockSpec(memory_space=pltpu.SMEM), tile(SWA_Q_WIDTH), tile(SWA_KV_WIDTH),
                  tile(SWA_KV_WIDTH), ctile, ctile],
        out_specs=tile(SWA_Q_WIDTH),
        out_shape=jax.ShapeDtypeStruct((t, SWA_Q_WIDTH), BF16),
        scratch_shapes=[pltpu.VMEM((2 * SWA_KV_HEADS, n + ctx, SWA_KV_WIDTH), BF16)] * 2,
        compiler_params=_params(),
        name="attn_swa",
    )(sink, q, k, v, ck, cv)


def _rope_tables(n):
    half = HEAD_DIM // 4
    freqs = jnp.power(ROPE_BASE, -jnp.arange(half, dtype=F32) / half)
    t = jnp.arange(n)
    cos, sin = [], []
    for pos in (t // GRID_W, t % GRID_W):
        ang = pos.astype(F32)[:, None] * freqs[None, :]
        cos += [jnp.cos(ang), jnp.cos(ang)]
        sin += [-jnp.sin(ang), jnp.sin(ang)]
    cos, sin = jnp.concatenate(cos, axis=-1), jnp.concatenate(sin, axis=-1)
    reps = LANES // HEAD_DIM
    return jnp.tile(cos, (1, reps)), jnp.tile(sin, (1, reps))


def _merge_kernel(x_ref, ya_ref, yb_ref, mod_ref, g_ref, wgate_ref, wba_ref, wbb_ref, wout_ref, o_ref):
    x = x_ref[...]
    mod = mod_ref[0]
    h = _modulate(x, g_ref[...], mod[3:4], mod[4:5]).astype(BF16)
    a = jax.nn.sigmoid(_dot(h, wgate_ref[:, :D_MODEL])) * _dot(ya_ref[...], wba_ref[...])
    b = jax.nn.sigmoid(_dot(h, wgate_ref[:, D_MODEL:])) * _dot(yb_ref[...], wbb_ref[...])
    o_ref[...] = x + mod[5:6] * _dot((a + b).astype(BF16), wout_ref[...])


def _merge(x, ya, yb, mod, g, wgate, wba, wbb, wout, *, tokens_per_row, first_row):
    t = x.shape[0]
    tile = lambda w: pl.BlockSpec((TOKEN_TILE, w), lambda i: (i, 0))
    return pl.pallas_call(
        _merge_kernel,
        grid=(t // TOKEN_TILE,),
        in_specs=[tile(D_MODEL), tile(NA_WIDTH), tile(SWA_Q_WIDTH), _mod_spec(tokens_per_row, first_row),
                  _resident((1, D_MODEL)), _resident(wgate.shape), _resident(wba.shape), _resident(wbb.shape),
                  _resident(wout.shape)],
        out_specs=tile(D_MODEL),
        out_shape=jax.ShapeDtypeStruct((t, D_MODEL), F32),
        compiler_params=_params(),
        name="merge",
    )(x, ya, yb, mod, g, wgate, wba, wbb, wout)


def kernel(x_prompt, x_sample, cache_na_k, cache_na_v, cache_swa_k, cache_swa_v, c, c_ctx, w_ada, b_ada,
           norm_ffn1, ffn1_w_gate, ffn1_w_up, ffn1_w_down, norm_mix, w_in, na_rel_bias, swa_sink,
           w_branch_na, w_branch_swa, w_out, norm_ffn2, ffn2_w_gate, ffn2_w_up, ffn2_w_down, norm_final):
    depth = w_ada.shape[0]
    assert depth == 1
    batch, seq, _ = x_prompt.shape
    dec_batch, dec_seq, _ = x_sample.shape
    past = cache_na_k.shape[2]
    layer = 0
    row = lambda v: v.reshape(1, D_MODEL)
    bf = lambda w: w.astype(BF16)

    cond = jnp.zeros((MOD_ROWS, D_MODEL), F32).at[0].set(c_ctx).at[1:1 + dec_batch].set(c)
    mod = _adaln(cond, w_ada[layer], b_ada[layer]).reshape(MOD_ROWS, N_MOD, D_MODEL)

    ffn1 = (row(norm_ffn1[layer]), bf(ffn1_w_gate[layer]), bf(ffn1_w_up[layer]), bf(ffn1_w_down[layer]),
            row(norm_final))
    ffn2 = (row(norm_ffn2[layer]), bf(ffn2_w_gate[layer]), bf(ffn2_w_up[layer]), bf(ffn2_w_down[layer]),
            row(norm_final))
    g_mix = row(norm_mix[layer])
    w_qkv = bf(w_in[layer, :, :QKV_COLS])
    w_gate = bf(w_in[layer, :, QKV_COLS:])
    merge_w = (w_gate, bf(w_branch_na[layer]), bf(w_branch_swa[layer]), bf(w_out[layer]))
    sink = swa_sink[layer]

    where = dict(tokens_per_row=batch * seq, first_row=0)
    x = x_prompt.reshape(batch * seq, D_MODEL)
    x = _ffn(x, mod, *ffn1, first=0, final=False, **where)
    qa, ka, va, qb, kb, vb = _proj(x, mod, g_mix, w_qkv, None, **where)
    ya, yb = _attn_ctx(sink, qa, ka, va, qb, kb, vb, seq=seq)
    x = _merge(x, ya, yb, mod, g_mix, *merge_w, **where)
    y_prompt = _ffn(x, mod, *ffn2, first=6, final=True, **where).reshape(batch, seq, D_MODEL)

    where = dict(tokens_per_row=dec_seq, first_row=1)
    x = x_sample.reshape(dec_batch * dec_seq, D_MODEL)
    x = _ffn(x, mod, *ffn1, first=0, final=False, **where)
    qa, kal, val, qb, kbl, vbl = _proj(x, mod, g_mix, w_qkv, _rope_tables(dec_seq), **where)
    ya = _attn_na(qa, kal, val, cache_na_k[:, layer].reshape(dec_batch * past, NA_WIDTH),
                  cache_na_v[:, layer].reshape(dec_batch * past, NA_WIDTH),
                  _na_bias_table(na_rel_bias[layer]), n=dec_seq, ctx=past)
    yb = _attn_swa(sink, qb, kbl, vbl, cache_swa_k[:, layer].reshape(dec_batch * past, SWA_KV_WIDTH),
                   cache_swa_v[:, layer].reshape(dec_batch * past, SWA_KV_WIDTH), n=dec_seq, ctx=past)
    x = _merge(x, ya, yb, mod, g_mix, *merge_w, **where)
    y_sample = _ffn(x, mod, *ffn2, first=6, final=True, **where).reshape(dec_batch, dec_seq, D_MODEL)

    heads = lambda t, nh: t.reshape(batch, depth, seq, nh, HEAD_DIM)
    return (y_prompt, y_sample, heads(ka, NA_HEADS), heads(va, NA_HEADS),
            heads(kb, SWA_KV_HEADS), heads(vb, SWA_KV_HEADS))
```

```python
import functools

import numpy as np
import jax
import jax.numpy as jnp
from jax import lax
from jax.experimental import pallas as pl
from jax.experimental.pallas import tpu as pltpu

F32 = jnp.float32
BF16 = jnp.bfloat16

D_MODEL = 1024
FFN_DIM = 2816
HEAD_DIM = 64
N_MOD = 9
GRID_W = 64
NA_HEADS = 8
NA_ROWS = 8
NA_COLS = 16
SWA_HEADS = 8
SWA_KV_HEADS = 2
SWA_WINDOW = 128
SWA_BLOCK = 128
ROPE_BASE = 10000.0
EPS = 1e-6
NEG_INF = -1e30
NA_WIDTH = NA_HEADS * HEAD_DIM
SWA_Q_WIDTH = SWA_HEADS * HEAD_DIM
SWA_KV_WIDTH = SWA_KV_HEADS * HEAD_DIM
QKV_COLS = 3 * NA_WIDTH + SWA_Q_WIDTH + 2 * SWA_KV_WIDTH
LANES = 128
MOD_ROWS = 16
VMEM_LIMIT = 56 * 1024 * 1024
TOKEN_TILE = 512
FFN_CHUNK = 256


def _dot(a, b):
    return jnp.dot(a, b, preferred_element_type=F32)


def _dot_nt(a, b):
    return lax.dot_general(a, b, (((1,), (1,)), ((), ())), preferred_element_type=F32)


def _silu(x):
    return x * jax.nn.sigmoid(x)


def _rms(x, g):
    return x * lax.rsqrt(jnp.mean(x * x, axis=-1, keepdims=True) + EPS) * g


def _modulate(x, g, shift, scale):
    return _rms(x, g) * (1.0 + scale) + shift


def _resident(shape):
    nd = len(shape)
    return pl.BlockSpec(shape, lambda *_: (0,) * nd, pipeline_mode=pl.Buffered(1))


def _params():
    return pltpu.CompilerParams(dimension_semantics=("arbitrary",), vmem_limit_bytes=VMEM_LIMIT)


def _adaln_kernel(c_ref, w_ref, b_ref, o_ref):
    s = _silu(c_ref[...]).astype(BF16)
    o_ref[...] = _dot(s, w_ref[...].astype(BF16)) + b_ref[...]


def _adaln(cond, w_ada, b_ada):
    n = w_ada.shape[1]
    blk = D_MODEL
    return pl.pallas_call(
        _adaln_kernel,
        grid=(n // blk,),
        in_specs=[pl.BlockSpec((MOD_ROWS, D_MODEL), lambda j: (0, 0)),
                  pl.BlockSpec((D_MODEL, blk), lambda j: (0, j)),
                  pl.BlockSpec((1, blk), lambda j: (0, j))],
        out_specs=pl.BlockSpec((MOD_ROWS, blk), lambda j: (0, j)),
        out_shape=jax.ShapeDtypeStruct((MOD_ROWS, n), F32),
        compiler_params=_params(),
        name="adaln",
    )(cond, w_ada, b_ada.reshape(1, n))


def _mod_spec(tokens_per_row, first_row):
    tiles_per_row = tokens_per_row // TOKEN_TILE
    return pl.BlockSpec((1, N_MOD, D_MODEL), lambda i: (first_row + i // tiles_per_row, 0, 0))


def _ffn_kernel(x_ref, mod_ref, g_ref, wg_ref, wu_ref, wd_ref, gf_ref, o_ref, a_scr, *, first, final):
    x = x_ref[...]
    mod = mod_ref[0]
    shift, scale, gate = mod[first:first + 1], mod[first + 1:first + 2], mod[first + 2:first + 3]
    h = _modulate(x, g_ref[...], shift, scale).astype(BF16)
    for c in range(FFN_DIM // FFN_CHUNK):
        sl = slice(c * FFN_CHUNK, (c + 1) * FFN_CHUNK)
        a_scr[:, sl] = (_silu(_dot(h, wg_ref[:, sl])) * _dot(h, wu_ref[:, sl])).astype(BF16)
    y = x + (0.5 * gate) * _dot(a_scr[...], wd_ref[...])
    if final:
        y = _rms(y, gf_ref[...])
    o_ref[...] = y


def _ffn(x, mod, g, wg, wu, wd, gf, *, tokens_per_row, first_row, first, final):
    t = x.shape[0]
    tile = pl.BlockSpec((TOKEN_TILE, D_MODEL), lambda i: (i, 0))
    return pl.pallas_call(
        functools.partial(_ffn_kernel, first=first, final=final),
        grid=(t // TOKEN_TILE,),
        in_specs=[tile, _mod_spec(tokens_per_row, first_row), _resident((1, D_MODEL)),
                  _resident(wg.shape), _resident(wu.shape), _resident(wd.shape), _resident((1, D_MODEL))],
        out_specs=tile,
        out_shape=jax.ShapeDtypeStruct((t, D_MODEL), F32),
        scratch_shapes=[pltpu.VMEM((TOKEN_TILE, FFN_DIM), BF16)],
        compiler_params=_params(),
        name="ffn",
    )(x, mod, g, wg, wu, wd, gf)


def _swap16(x):
    lane = lax.broadcasted_iota(jnp.int32, x.shape, 1)
    return jnp.where(lane % 32 < 16, pltpu.roll(x, LANES - 16, 1), pltpu.roll(x, 16, 1))


def _proj_ctx_kernel(x_ref, mod_ref, g_ref, w_ref, qa_ref, ka_ref, va_ref, qb_ref, kb_ref, vb_ref):
    mod = mod_ref[0]
    h = _modulate(x_ref[...], g_ref[...], mod[3:4], mod[4:5]).astype(BF16)
    o = 0
    for ref, width, scale in ((qa_ref, NA_WIDTH, True), (ka_ref, NA_WIDTH, False), (va_ref, NA_WIDTH, False),
                              (qb_ref, SWA_Q_WIDTH, True), (kb_ref, SWA_KV_WIDTH, False),
                              (vb_ref, SWA_KV_WIDTH, False)):
        y = _dot(h, w_ref[:, o:o + width])
        if scale:
            y = y * (HEAD_DIM ** -0.5)
        ref[...] = y.astype(ref.dtype)
        o += width


def _proj_lat_kernel(x_ref, mod_ref, g_ref, w_ref, cos_ref, sin_ref,
                     qa_ref, ka_ref, va_ref, qb_ref, kb_ref, vb_ref):
    mod = mod_ref[0]
    h = _modulate(x_ref[...], g_ref[...], mod[3:4], mod[4:5]).astype(BF16)
    cos, sin = cos_ref[...], sin_ref[...]
    o = 0
    for ref, width, scale, rope in ((qa_ref, NA_WIDTH, True, False), (ka_ref, NA_WIDTH, False, False),
                                    (va_ref, NA_WIDTH, False, False), (qb_ref, SWA_Q_WIDTH, True, True),
                                    (kb_ref, SWA_KV_WIDTH, False, True), (vb_ref, SWA_KV_WIDTH, False, False)):
        for j in range(width // LANES):
            y = _dot(h, w_ref[:, o + j * LANES:o + (j + 1) * LANES])
            if rope:
                y = y * cos + _swap16(y) * sin
            if scale:
                y = y * (HEAD_DIM ** -0.5)
            ref[:, j * LANES:(j + 1) * LANES] = y.astype(ref.dtype)
        o += width


def _proj(x, mod, g, w_qkv, rope, *, tokens_per_row, first_row):
    t = x.shape[0]
    tile = lambda w: pl.BlockSpec((TOKEN_TILE, w), lambda i: (i, 0))
    widths = (NA_WIDTH, NA_WIDTH, NA_WIDTH, SWA_Q_WIDTH, SWA_KV_WIDTH, SWA_KV_WIDTH)
    in_specs = [tile(D_MODEL), _mod_spec(tokens_per_row, first_row), _resident((1, D_MODEL)),
                _resident(w_qkv.shape)]
    if rope is None:
        body, args = _proj_ctx_kernel, (x, mod, g, w_qkv)
        dtypes = (BF16, F32, F32, BF16, F32, F32)
    else:
        body, args = _proj_lat_kernel, (x, mod, g, w_qkv) + rope
        dtypes = (BF16,) * 6
        tiles_per_row = tokens_per_row // TOKEN_TILE
        in_specs += [pl.BlockSpec((TOKEN_TILE, LANES), lambda i: (i % tiles_per_row, 0))] * 2
    return pl.pallas_call(
        body,
        grid=(t // TOKEN_TILE,),
        in_specs=in_specs,
        out_specs=[tile(w) for w in widths],
        out_shape=[jax.ShapeDtypeStruct((t, w), dt) for w, dt in zip(widths, dtypes)],
        compiler_params=_params(),
        name="proj",
    )(*args)


def _lane_halves(shape):
    lane = lax.broadcasted_iota(jnp.int32, shape, 1) % LANES
    return lane < HEAD_DIM, lane >= HEAD_DIM


def _softmax_parts(parts, sink=None):
    m = functools.reduce(jnp.maximum, [jnp.max(p, axis=-1, keepdims=True) for p in parts])
    if sink is not None:
        m = jnp.maximum(m, sink)
    es = [jnp.exp(p - m) for p in parts]
    l = functools.reduce(jnp.add, [jnp.sum(e, axis=-1, keepdims=True) for e in es])
    if sink is not None:
        l = l + jnp.exp(sink - m)
    return es, l


def _attn_ctx_kernel(sink_ref, qa_ref, ka_ref, va_ref, qb_ref, kb_ref, vb_ref, ya_ref, yb_ref):
    lo, hi = _lane_halves((ka_ref.shape[0], LANES))
    zero = jnp.zeros((), BF16)
    for p in range(NA_HEADS // 2):
        sl = slice(p * LANES, (p + 1) * LANES)
        q = qa_ref[:, sl]
        k = ka_ref[:, sl].astype(BF16)
        v = va_ref[:, sl].astype(BF16)
        out = None
        for half in (lo, hi):
            (e,), l = _softmax_parts([_dot_nt(q, jnp.where(half, k, zero))])
            o = _dot(e.astype(BF16), jnp.where(half, v, zero)) * (1.0 / l)
            out = o if out is None else out + o
        ya_ref[:, sl] = out.astype(ya_ref.dtype)
    kf, vf = kb_ref[...], vb_ref[...]
    kr, vr = pltpu.roll(kf, HEAD_DIM, 1), pltpu.roll(vf, HEAD_DIM, 1)
    for p in range(SWA_HEADS // 2):
        sl = slice(p * LANES, (p + 1) * LANES)
        q = qb_ref[:, sl]
        kv = (2 * p) // (SWA_HEADS // SWA_KV_HEADS)
        out = None
        for idx, half in enumerate((lo, hi)):
            ksrc, vsrc = (kf, vf) if idx == kv else (kr, vr)
            sink = sink_ref[2 * p + idx]
            (e,), l = _softmax_parts([_dot_nt(q, jnp.where(half, ksrc, 0.0).astype(BF16))], sink)
            o = _dot(e.astype(BF16), jnp.where(half, vsrc, 0.0).astype(BF16)) * (1.0 / l)
            out = o if out is None else out + o
        yb_ref[:, sl] = out.astype(yb_ref.dtype)


def _attn_ctx(sink, qa, ka, va, qb, kb, vb, *, seq):
    t = qa.shape[0]
    tile = lambda w: pl.BlockSpec((seq, w), lambda b: (b, 0))
    return pl.pallas_call(
        _attn_ctx_kernel,
        grid=(t // seq,),
        in_specs=[pl.BlockSpec(memory_space=pltpu.SMEM), tile(NA_WIDTH), tile(NA_WIDTH), tile(NA_WIDTH),
                  tile(SWA_Q_WIDTH), tile(SWA_KV_WIDTH), tile(SWA_KV_WIDTH)],
        out_specs=[tile(NA_WIDTH), tile(SWA_Q_WIDTH)],
        out_shape=[jax.ShapeDtypeStruct((t, NA_WIDTH), BF16), jax.ShapeDtypeStruct((t, SWA_Q_WIDTH), BF16)],
        compiler_params=_params(),
        name="attn_ctx",
    )(sink, qa, ka, va, qb, kb, vb)


def _attn_na_kernel(q_ref, k_ref, v_ref, ck_ref, cv_ref, bias_ref, y_ref, k_scr, v_scr, *, n, ctx):
    rows = n // GRID_W
    zero = jnp.zeros((), BF16)
    for src, csrc, dst in ((k_ref, ck_ref, k_scr), (v_ref, cv_ref, v_scr)):
        for idx, half in enumerate(_lane_halves((n, NA_WIDTH))):
            dst[idx, 0:n, :] = jnp.where(half, src[...], zero)
        for idx, half in enumerate(_lane_halves((ctx, NA_WIDTH))):
            dst[idx, n:n + ctx, :] = jnp.where(half, csrc[...].astype(BF16), zero)

    def row_body(r, carry):
        start = jnp.clip(r - NA_ROWS // 2, 0, rows - NA_ROWS)
        off = r - start
        q0 = pl.multiple_of(r * GRID_W, GRID_W)
        k0 = pl.multiple_of(start * GRID_W, GRID_W)
        for p in range(NA_HEADS // 2):
            sl = slice(p * LANES, (p + 1) * LANES)
            q = q_ref[pl.ds(q0, GRID_W), sl]
            out = None
            for idx in range(2):
                s_nb = _dot_nt(q, k_scr[idx, pl.ds(k0, NA_ROWS * GRID_W), sl]) + bias_ref[off, 2 * p + idx]
                s_ctx = _dot_nt(q, k_scr[idx, n:n + ctx, sl])
                (e_nb, e_ctx), l = _softmax_parts([s_nb, s_ctx])
                o = (_dot(e_nb.astype(BF16), v_scr[idx, pl.ds(k0, NA_ROWS * GRID_W), sl])
                     + _dot(e_ctx.astype(BF16), v_scr[idx, n:n + ctx, sl])) * (1.0 / l)
                out = o if out is None else out + o
            y_ref[pl.ds(q0, GRID_W), sl] = out.astype(y_ref.dtype)
        return carry

    lax.fori_loop(0, rows, row_body, 0)


def _attn_na(q, k, v, ck, cv, bias, *, n, ctx):
    t = q.shape[0]
    tile = pl.BlockSpec((n, NA_WIDTH), lambda b: (b, 0))
    ctile = pl.BlockSpec((ctx, NA_WIDTH), lambda b: (b, 0))
    return pl.pallas_call(
        functools.partial(_attn_na_kernel, n=n, ctx=ctx),
        grid=(t // n,),
        in_specs=[tile, tile, tile, ctile, ctile, _resident(bias.shape)],
        out_specs=tile,
        out_shape=jax.ShapeDtypeStruct((t, NA_WIDTH), BF16),
        scratch_shapes=[pltpu.VMEM((2, n + ctx, NA_WIDTH), BF16)] * 2,
        compiler_params=_params(),
        name="attn_na",
    )(q, k, v, ck, cv, bias)


def _na_bias_table(rel_bias):
    cq = np.arange(GRID_W)
    kc = np.arange(GRID_W)
    col_start = np.clip(cq - NA_COLS // 2, 0, GRID_W - NA_COLS)
    col_mask = (kc[None, :] >= col_start[:, None]) & (kc[None, :] < col_start[:, None] + NA_COLS)
    dc = np.clip(kc[None, :] - cq[:, None], -(NA_COLS - 1), NA_COLS - 1) + (NA_COLS - 1)
    onehot = dc[None, :, :] == np.arange(2 * NA_COLS - 1)[:, None, None]
    by_dr = jnp.sum(jnp.where(onehot[None, None], rel_bias[:, :, :, None, None], 0.0), axis=2)
    by_dr = jnp.where(col_mask[None, None], by_dr, NEG_INF)
    tab = jnp.stack([by_dr[:, NA_ROWS - 1 - off:2 * NA_ROWS - 1 - off] for off in range(NA_ROWS)])
    return tab.transpose(0, 1, 3, 2, 4).reshape(NA_ROWS, NA_HEADS, GRID_W, NA_ROWS * GRID_W)


def _attn_swa_kernel(sink_ref, q_ref, k_ref, v_ref, ck_ref, cv_ref, y_ref, k_scr, v_scr, *, n, ctx):
    group = SWA_HEADS // SWA_KV_HEADS
    band = 3 * SWA_BLOCK
    for src, csrc, dst in ((k_ref, ck_ref, k_scr), (v_ref, cv_ref, v_scr)):
        for lo_row, hi_row, x in ((0, n, src[...].astype(F32)), (n, n + ctx, csrc[...])):
            xr = pltpu.roll(x, HEAD_DIM, 1)
            halves = _lane_halves(x.shape)
            for kv in range(SWA_KV_HEADS):
                for idx, half in enumerate(halves):
                    dst[2 * kv + idx, lo_row:hi_row, :] = jnp.where(half, x if idx == kv else xr, 0.0).astype(BF16)

    qi = lax.broadcasted_iota(jnp.int32, (SWA_BLOCK, band), 0)
    kj = lax.broadcasted_iota(jnp.int32, (SWA_BLOCK, band), 1)
    top = lax.broadcasted_iota(jnp.int32, (2 * SWA_BLOCK, 1), 0) < SWA_BLOCK

    def block_body(b, carry):
        q0 = pl.multiple_of(b * SWA_BLOCK, SWA_BLOCK)
        k0 = pl.multiple_of(jnp.clip(q0 - SWA_BLOCK, 0, n - band), SWA_BLOCK)
        mask = jnp.where(jnp.abs(kj - qi + (k0 - q0)) <= SWA_WINDOW, 0.0, NEG_INF)
        mask2 = jnp.concatenate([mask, mask], axis=0)
        for kv in range(SWA_KV_HEADS):
            pairs = (group // 2 * kv, group // 2 * kv + 1)
            q = jnp.concatenate([q_ref[pl.ds(q0, SWA_BLOCK), p * LANES:(p + 1) * LANES] for p in pairs], axis=0)
            out = None
            for idx in range(2):
                sink = jnp.where(top, sink_ref[2 * pairs[0] + idx], sink_ref[2 * pairs[1] + idx])
                s_band = _dot_nt(q, k_scr[2 * kv + idx, pl.ds(k0, band), :]) + mask2
                s_ctx = _dot_nt(q, k_scr[2 * kv + idx, n:n + ctx, :])
                (e_band, e_ctx), l = _softmax_parts([s_band, s_ctx], sink)
                o = (_dot(e_band.astype(BF16), v_scr[2 * kv + idx, pl.ds(k0, band), :])
                     + _dot(e_ctx.astype(BF16), v_scr[2 * kv + idx, n:n + ctx, :])) * (1.0 / l)
                out = o if out is None else out + o
            for i, p in enumerate(pairs):
                y_ref[pl.ds(q0, SWA_BLOCK), p * LANES:(p + 1) * LANES] = (
                    out[i * SWA_BLOCK:(i + 1) * SWA_BLOCK].astype(y_ref.dtype))
        return carry

    lax.fori_loop(0, n // SWA_BLOCK, block_body, 0)


def _attn_swa(sink, q, k, v, ck, cv, *, n, ctx):
    t = q.shape[0]
    tile = lambda w: pl.BlockSpec((n, w), lambda b: (b, 0))
    ctile = pl.BlockSpec((ctx, SWA_KV_WIDTH), lambda b: (b, 0))
    return pl.pallas_call(
        functools.partial(_attn_swa_kernel, n=n, ctx=ctx),
        grid=(t // n,),
        in_specs=[pl.BlockSpec(memory_space=pltpu.SMEM), tile(SWA_Q_WIDTH), tile(SWA_KV_WIDTH),
                  tile(SWA_KV_WIDTH), ctile, ctile],
        out_specs=tile(SWA_Q_WIDTH),
        out_shape=jax.ShapeDtypeStruct((t, SWA_Q_WIDTH), BF16),
        scratch_shapes=[pltpu.VMEM((2 * SWA_KV_HEADS, n + ctx, SWA_KV_WIDTH), BF16)] * 2,
        compiler_params=_params(),
        name="attn_swa",
    )(sink, q, k, v, ck, cv)


def _rope_tables(n):
    half = HEAD_DIM // 4
    freqs = jnp.power(ROPE_BASE, -jnp.arange(half, dtype=F32) / half)
    t = jnp.arange(n)
    cos, sin = [], []
    for pos in (t // GRID_W, t % GRID_W):
        ang = pos.astype(F32)[:, None] * freqs[None, :]
        cos += [jnp.cos(ang), jnp.cos(ang)]
        sin += [-jnp.sin(ang), jnp.sin(ang)]
    cos, sin = jnp.concatenate(cos, axis=-1), jnp.concatenate(sin, axis=-1)
    reps = LANES // HEAD_DIM
    return jnp.tile(cos, (1, reps)), jnp.tile(sin, (1, reps))


def _merge_kernel(x_ref, ya_ref, yb_ref, mod_ref, g_ref, wgate_ref, wba_ref, wbb_ref, wout_ref, o_ref):
    x = x_ref[...]
    mod = mod_ref[0]
    h = _modulate(x, g_ref[...], mod[3:4], mod[4:5]).astype(BF16)
    a = jax.nn.sigmoid(_dot(h, wgate_ref[:, :D_MODEL])) * _dot(ya_ref[...], wba_ref[...])
    b = jax.nn.sigmoid(_dot(h, wgate_ref[:, D_MODEL:])) * _dot(yb_ref[...], wbb_ref[...])
    o_ref[...] = x + mod[5:6] * _dot((a + b).astype(BF16), wout_ref[...])


def _merge(x, ya, yb, mod, g, wgate, wba, wbb, wout, *, tokens_per_row, first_row):
    t = x.shape[0]
    tile = lambda w: pl.BlockSpec((TOKEN_TILE, w), lambda i: (i, 0))
    return pl.pallas_call(
        _merge_kernel,
        grid=(t // TOKEN_TILE,),
        in_specs=[tile(D_MODEL), tile(NA_WIDTH), tile(SWA_Q_WIDTH), _mod_spec(tokens_per_row, first_row),
                  _resident((1, D_MODEL)), _resident(wgate.shape), _resident(wba.shape), _resident(wbb.shape),
                  _resident(wout.shape)],
        out_specs=tile(D_MODEL),
        out_shape=jax.ShapeDtypeStruct((t, D_MODEL), F32),
        compiler_params=_params(),
        name="merge",
    )(x, ya, yb, mod, g, wgate, wba, wbb, wout)


def kernel(x_prompt, x_sample, cache_na_k, cache_na_v, cache_swa_k, cache_swa_v, c, c_ctx, w_ada, b_ada,
           norm_ffn1, ffn1_w_gate, ffn1_w_up, ffn1_w_down, norm_mix, w_in, na_rel_bias, swa_sink,
           w_branch_na, w_branch_swa, w_out, norm_ffn2, ffn2_w_gate, ffn2_w_up, ffn2_w_down, norm_final):
    depth = w_ada.shape[0]
    assert depth == 1
    batch, seq, _ = x_prompt.shape
    dec_batch, dec_seq, _ = x_sample.shape
    past = cache_na_k.shape[2]
    layer = 0
    row = lambda v: v.reshape(1, D_MODEL)
    bf = lambda w: w.astype(BF16)

    cond = jnp.zeros((MOD_ROWS, D_MODEL), F32).at[0].set(c_ctx).at[1:1 + dec_batch].set(c)
    mod = _adaln(cond, w_ada[layer], b_ada[layer]).reshape(MOD_ROWS, N_MOD, D_MODEL)

    ffn1 = (row(norm_ffn1[layer]), bf(ffn1_w_gate[layer]), bf(ffn1_w_up[layer]), bf(ffn1_w_down[layer]),
            row(norm_final))
    ffn2 = (row(norm_ffn2[layer]), bf(ffn2_w_gate[layer]), bf(ffn2_w_up[layer]), bf(ffn2_w_down[layer]),
            row(norm_final))
    g_mix = row(norm_mix[layer])
    w_qkv = bf(w_in[layer, :, :QKV_COLS])
    w_gate = bf(w_in[layer, :, QKV_COLS:])
    merge_w = (w_gate, bf(w_branch_na[layer]), bf(w_branch_swa[layer]), bf(w_out[layer]))
    sink = swa_sink[layer]

    where = dict(tokens_per_row=batch * seq, first_row=0)
    x = x_prompt.reshape(batch * seq, D_MODEL)
    x = _ffn(x, mod, *ffn1, first=0, final=False, **where)
    qa, ka, va, qb, kb, vb = _proj(x, mod, g_mix, w_qkv, None, **where)
    ya, yb = _attn_ctx(sink, qa, ka, va, qb, kb, vb, seq=seq)
    x = _merge(x, ya, yb, mod, g_mix, *merge_w, **where)
    y_prompt = _ffn(x, mod, *ffn2, first=6, final=True, **where).reshape(batch, seq, D_MODEL)

    where = dict(tokens_per_row=dec_seq, first_row=1)
    x = x_sample.reshape(dec_batch * dec_seq, D_MODEL)
    x = _ffn(x, mod, *ffn1, first=0, final=False, **where)
    qa, kal, val, qb, kbl, vbl = _proj(x, mod, g_mix, w_qkv, _rope_tables(dec_seq), **where)
    ya = _attn_na(qa, kal, val, cache_na_k[:, layer].reshape(dec_batch * past, NA_WIDTH),
                  cache_na_v[:, layer].reshape(dec_batch * past, NA_WIDTH),
                  _na_bias_table(na_rel_bias[layer]), n=dec_seq, ctx=past)
    yb = _attn_swa(sink, qb, kbl, vbl, cache_swa_k[:, layer].reshape(dec_batch * past, SWA_KV_WIDTH),
                   cache_swa_v[:, layer].reshape(dec_batch * past, SWA_KV_WIDTH), n=dec_seq, ctx=past)
    x = _merge(x, ya, yb, mod, g_mix, *merge_w, **where)
    y_sample = _ffn(x, mod, *ffn2, first=6, final=True, **where).reshape(dec_batch, dec_seq, D_MODEL)

    heads = lambda t, nh: t.reshape(batch, depth, seq, nh, HEAD_DIM)
    return (y_prompt, y_sample, heads(ka, NA_HEADS), heads(va, NA_HEADS),
            heads(kb, SWA_KV_HEADS), heads(vb, SWA_KV_HEADS))
```

```python
import functools

import numpy as np
import jax
import jax.numpy as jnp
from jax import lax
from jax.experimental import pallas as pl
from jax.experimental.pallas import tpu as pltpu

F32 = jnp.float32
BF16 = jnp.bfloat16

D_MODEL = 1024
FFN_DIM = 2816
HEAD_DIM = 64
N_MOD = 9
GRID_W = 64
NA_HEADS = 8
NA_ROWS = 8
NA_COLS = 16
SWA_HEADS = 8
SWA_KV_HEADS = 2
SWA_WINDOW = 128
SWA_BLOCK = 128
ROPE_BASE = 10000.0
EPS = 1e-6
NEG_INF = -1e30
NA_WIDTH = NA_HEADS * HEAD_DIM
SWA_Q_WIDTH = SWA_HEADS * HEAD_DIM
SWA_KV_WIDTH = SWA_KV_HEADS * HEAD_DIM
QKV_COLS = 3 * NA_WIDTH + SWA_Q_WIDTH + 2 * SWA_KV_WIDTH
LANES = 128
MOD_ROWS = 16
VMEM_LIMIT = 56 * 1024 * 1024
TOKEN_TILE = 512
FFN_CHUNK = 256
SWA_QTILE = 256
NA_TILE_ROWS = 4
NA_WINDOW_ROWS = NA_ROWS + NA_TILE_ROWS


def _dot(a, b):
    return jnp.dot(a, b, preferred_element_type=F32)


def _dot_nt(a, b):
    return lax.dot_general(a, b, (((1,), (1,)), ((), ())), preferred_element_type=F32)


def _silu(x):
    return x * jax.nn.sigmoid(x)


def _rms(x, g):
    return x * lax.rsqrt(jnp.mean(x * x, axis=-1, keepdims=True) + EPS) * g


def _modulate(x, g, shift, scale):
    return _rms(x, g) * (1.0 + scale) + shift


def _resident(shape):
    nd = len(shape)
    return pl.BlockSpec(shape, lambda *_: (0,) * nd, pipeline_mode=pl.Buffered(1))


def _params():
    return pltpu.CompilerParams(dimension_semantics=("arbitrary",), vmem_limit_bytes=VMEM_LIMIT)


def _adaln_kernel(c_ref, w_ref, b_ref, o_ref):
    s = _silu(c_ref[...]).astype(BF16)
    o_ref[...] = _dot(s, w_ref[...].astype(BF16)) + b_ref[...]


def _adaln(cond, w_ada, b_ada):
    n = w_ada.shape[1]
    blk = D_MODEL
    return pl.pallas_call(
        _adaln_kernel,
        grid=(n // blk,),
        in_specs=[pl.BlockSpec((MOD_ROWS, D_MODEL), lambda j: (0, 0)),
                  pl.BlockSpec((D_MODEL, blk), lambda j: (0, j)),
                  pl.BlockSpec((1, blk), lambda j: (0, j))],
        out_specs=pl.BlockSpec((MOD_ROWS, blk), lambda j: (0, j)),
        out_shape=jax.ShapeDtypeStruct((MOD_ROWS, n), F32),
        compiler_params=_params(),
        name="adaln",
    )(cond, w_ada, b_ada.reshape(1, n))


def _mod_spec(tokens_per_row, first_row):
    tiles_per_row = tokens_per_row // TOKEN_TILE
    return pl.BlockSpec((1, N_MOD, D_MODEL), lambda i: (first_row + i // tiles_per_row, 0, 0))


def _ffn_kernel(x_ref, mod_ref, g_ref, wg_ref, wu_ref, wd_ref, gf_ref, o_ref, a_scr, *, first, final):
    x = x_ref[...]
    mod = mod_ref[0]
    shift, scale, gate = mod[first:first + 1], mod[first + 1:first + 2], mod[first + 2:first + 3]
    h = _modulate(x, g_ref[...], shift, scale).astype(BF16)
    for c in range(FFN_DIM // FFN_CHUNK):
        sl = slice(c * FFN_CHUNK, (c + 1) * FFN_CHUNK)
        a_scr[:, sl] = (_silu(_dot(h, wg_ref[:, sl])) * _dot(h, wu_ref[:, sl])).astype(BF16)
    y = x + (0.5 * gate) * _dot(a_scr[...], wd_ref[...])
    if final:
        y = _rms(y, gf_ref[...])
    o_ref[...] = y


def _ffn(x, mod, g, wg, wu, wd, gf, *, tokens_per_row, first_row, first, final):
    t = x.shape[0]
    tile = pl.BlockSpec((TOKEN_TILE, D_MODEL), lambda i: (i, 0))
    return pl.pallas_call(
        functools.partial(_ffn_kernel, first=first, final=final),
        grid=(t // TOKEN_TILE,),
        in_specs=[tile, _mod_spec(tokens_per_row, first_row), _resident((1, D_MODEL)),
                  _resident(wg.shape), _resident(wu.shape), _resident(wd.shape), _resident((1, D_MODEL))],
        out_specs=tile,
        out_shape=jax.ShapeDtypeStruct((t, D_MODEL), F32),
        scratch_shapes=[pltpu.VMEM((TOKEN_TILE, FFN_DIM), BF16)],
        compiler_params=_params(),
        name="ffn",
    )(x, mod, g, wg, wu, wd, gf)


def _swap16(x):
    lane = lax.broadcasted_iota(jnp.int32, x.shape, 1)
    return jnp.where(lane % 32 < 16, pltpu.roll(x, LANES - 16, 1), pltpu.roll(x, 16, 1))


def _proj_ctx_kernel(x_ref, mod_ref, g_ref, w_ref, qa_ref, ka_ref, va_ref, qb_ref, kb_ref, vb_ref):
    mod = mod_ref[0]
    h = _modulate(x_ref[...], g_ref[...], mod[3:4], mod[4:5]).astype(BF16)
    o = 0
    for ref, width, scale in ((qa_ref, NA_WIDTH, True), (ka_ref, NA_WIDTH, False), (va_ref, NA_WIDTH, False),
                              (qb_ref, SWA_Q_WIDTH, True), (kb_ref, SWA_KV_WIDTH, False),
                              (vb_ref, SWA_KV_WIDTH, False)):
        y = _dot(h, w_ref[:, o:o + width])
        if scale:
            y = y * (HEAD_DIM ** -0.5)
        ref[...] = y.astype(ref.dtype)
        o += width


def _proj_lat_kernel(x_ref, mod_ref, g_ref, w_ref, cos_ref, sin_ref,
                     qa_ref, ka_ref, va_ref, qb_ref, kb_ref, vb_ref):
    mod = mod_ref[0]
    h = _modulate(x_ref[...], g_ref[...], mod[3:4], mod[4:5]).astype(BF16)
    cos, sin = cos_ref[...], sin_ref[...]
    o = 0
    for ref, width, scale, rope in ((qa_ref, NA_WIDTH, True, False), (ka_ref, NA_WIDTH, False, False),
                                    (va_ref, NA_WIDTH, False, False), (qb_ref, SWA_Q_WIDTH, True, True),
                                    (kb_ref, SWA_KV_WIDTH, False, True), (vb_ref, SWA_KV_WIDTH, False, False)):
        for j in range(width // LANES):
            y = _dot(h, w_ref[:, o + j * LANES:o + (j + 1) * LANES])
            if rope:
                y = y * cos + _swap16(y) * sin
            if scale:
                y = y * (HEAD_DIM ** -0.5)
            ref[:, j * LANES:(j + 1) * LANES] = y.astype(ref.dtype)
        o += width


def _proj(x, mod, g, w_qkv, rope, *, tokens_per_row, first_row):
    t = x.shape[0]
    tile = lambda w: pl.BlockSpec((TOKEN_TILE, w), lambda i: (i, 0))
    widths = (NA_WIDTH, NA_WIDTH, NA_WIDTH, SWA_Q_WIDTH, SWA_KV_WIDTH, SWA_KV_WIDTH)
    in_specs = [tile(D_MODEL), _mod_spec(tokens_per_row, first_row), _resident((1, D_MODEL)),
                _resident(w_qkv.shape)]
    if rope is None:
        body, args = _proj_ctx_kernel, (x, mod, g, w_qkv)
        dtypes = (BF16, F32, F32, BF16, F32, F32)
    else:
        body, args = _proj_lat_kernel, (x, mod, g, w_qkv) + rope
        dtypes = (BF16,) * 6
        tiles_per_row = tokens_per_row // TOKEN_TILE
        in_specs += [pl.BlockSpec((TOKEN_TILE, LANES), lambda i: (i % tiles_per_row, 0))] * 2
    return pl.pallas_call(
        body,
        grid=(t // TOKEN_TILE,),
        in_specs=in_specs,
        out_specs=[tile(w) for w in widths],
        out_shape=[jax.ShapeDtypeStruct((t, w), dt) for w, dt in zip(widths, dtypes)],
        compiler_params=_params(),
        name="proj",
    )(*args)


def _lane_halves(shape):
    lane = lax.broadcasted_iota(jnp.int32, shape, 1) % LANES
    return lane < HEAD_DIM, lane >= HEAD_DIM


def _softmax_cols(cols, sink=None):
    m = jnp.max(functools.reduce(jnp.maximum, cols), axis=-1, keepdims=True)
    if sink is not None:
        m = jnp.maximum(m, sink)
    es = [jnp.exp(c - m) for c in cols]
    l = jnp.sum(functools.reduce(jnp.add, es), axis=-1, keepdims=True)
    if sink is not None:
        l = l + jnp.exp(sink - m)
    return es, l


def _softmax_parts(parts, sink=None):
    m = functools.reduce(jnp.maximum, [jnp.max(p, axis=-1, keepdims=True) for p in parts])
    if sink is not None:
        m = jnp.maximum(m, sink)
    es = [jnp.exp(p - m) for p in parts]
    l = functools.reduce(jnp.add, [jnp.sum(e, axis=-1, keepdims=True) for e in es])
    if sink is not None:
        l = l + jnp.exp(sink - m)
    return es, l


def _attn_ctx_kernel(sink_ref, qa_ref, ka_ref, va_ref, qb_ref, kb_ref, vb_ref, ya_ref, yb_ref):
    lo, hi = _lane_halves((ka_ref.shape[0], LANES))
    zero = jnp.zeros((), BF16)
    for p in range(NA_HEADS // 2):
        sl = slice(p * LANES, (p + 1) * LANES)
        q = qa_ref[:, sl]
        k = ka_ref[:, sl].astype(BF16)
        v = va_ref[:, sl].astype(BF16)
        out = None
        for half in (lo, hi):
            (e,), l = _softmax_parts([_dot_nt(q, jnp.where(half, k, zero))])
            o = _dot(e.astype(BF16), jnp.where(half, v, zero)) * (1.0 / l)
            out = o if out is None else out + o
        ya_ref[:, sl] = out.astype(ya_ref.dtype)
    kf, vf = kb_ref[...], vb_ref[...]
    kr, vr = pltpu.roll(kf, HEAD_DIM, 1), pltpu.roll(vf, HEAD_DIM, 1)
    for p in range(SWA_HEADS // 2):
        sl = slice(p * LANES, (p + 1) * LANES)
        q = qb_ref[:, sl]
        kv = (2 * p) // (SWA_HEADS // SWA_KV_HEADS)
        out = None
        for idx, half in enumerate((lo, hi)):
            ksrc, vsrc = (kf, vf) if idx == kv else (kr, vr)
            sink = sink_ref[2 * p + idx]
            (e,), l = _softmax_parts([_dot_nt(q, jnp.where(half, ksrc, 0.0).astype(BF16))], sink)
            o = _dot(e.astype(BF16), jnp.where(half, vsrc, 0.0).astype(BF16)) * (1.0 / l)
            out = o if out is None else out + o
        yb_ref[:, sl] = out.astype(yb_ref.dtype)


def _attn_ctx(sink, qa, ka, va, qb, kb, vb, *, seq):
    t = qa.shape[0]
    tile = lambda w: pl.BlockSpec((seq, w), lambda b: (b, 0))
    return pl.pallas_call(
        _attn_ctx_kernel,
        grid=(t // seq,),
        in_specs=[pl.BlockSpec(memory_space=pltpu.SMEM), tile(NA_WIDTH), tile(NA_WIDTH), tile(NA_WIDTH),
                  tile(SWA_Q_WIDTH), tile(SWA_KV_WIDTH), tile(SWA_KV_WIDTH)],
        out_specs=[tile(NA_WIDTH), tile(SWA_Q_WIDTH)],
        out_shape=[jax.ShapeDtypeStruct((t, NA_WIDTH), BF16), jax.ShapeDtypeStruct((t, SWA_Q_WIDTH), BF16)],
        compiler_params=_params(),
        name="attn_ctx",
    )(sink, qa, ka, va, qb, kb, vb)


def _attn_na_kernel(q_ref, k_ref, v_ref, ck_ref, cv_ref, bias_ref, rowmask_ref, y_ref, k_scr, v_scr, *, n, ctx):
    rows = n // GRID_W
    tile_q = NA_TILE_ROWS * GRID_W
    win = NA_WINDOW_ROWS * GRID_W
    pairs = NA_WINDOW_ROWS // 2
    zero = jnp.zeros((), BF16)
    for src, csrc, dst in ((k_ref, ck_ref, k_scr), (v_ref, cv_ref, v_scr)):
        for idx, half in enumerate(_lane_halves((n, NA_WIDTH))):
            dst[idx, 0:n, :] = jnp.where(half, src[...], zero)
        for idx, half in enumerate(_lane_halves((ctx, NA_WIDTH))):
            dst[idx, n:n + ctx, :] = jnp.where(half, csrc[...].astype(BF16), zero)

    def tile_body(t, carry):
        r0 = t * NA_TILE_ROWS
        w0 = jnp.clip(r0 - NA_ROWS // 2, 0, rows - NA_WINDOW_ROWS)
        q0 = pl.multiple_of(r0 * GRID_W, tile_q)
        k0 = pl.multiple_of(w0 * GRID_W, tile_q)
        sel = []
        for a in range(NA_TILE_ROWS):
            rq = r0 + a
            rs = jnp.clip(rq - NA_ROWS // 2, 0, rows - NA_ROWS)
            for m in range(pairs):
                kr = w0 + 2 * m
                dr = jnp.clip(kr - rq + NA_ROWS, 0, 2 * NA_ROWS - 1)
                left = ((kr >= rs) & (kr < rs + NA_ROWS)).astype(jnp.int32)
                right = ((kr + 1 >= rs) & (kr + 1 < rs + NA_ROWS)).astype(jnp.int32)
                sel.append((dr, 2 * left + right))
        for p in range(NA_HEADS // 2):
            sl = slice(p * LANES, (p + 1) * LANES)
            q = q_ref[pl.ds(q0, tile_q), sl]
            out = None
            for idx in range(2):
                s_nb = _dot_nt(q, k_scr[idx, pl.ds(k0, win), sl])
                s_ctx = _dot_nt(q, k_scr[idx, n:n + ctx, sl])
                cols = []
                for m in range(pairs):
                    blocks = []
                    for a in range(NA_TILE_ROWS):
                        dr, case = sel[a * pairs + m]
                        blocks.append(s_nb[a * GRID_W:(a + 1) * GRID_W, m * LANES:(m + 1) * LANES]
                                      + bias_ref[2 * p + idx, dr] + rowmask_ref[case])
                    cols.append(jnp.concatenate(blocks, axis=0))
                cols += [s_ctx[:, j * LANES:(j + 1) * LANES] for j in range(ctx // LANES)]
                es, l = _softmax_cols(cols)
                e_nb = jnp.concatenate(es[:pairs], axis=1).astype(BF16)
                e_ctx = jnp.concatenate(es[pairs:], axis=1).astype(BF16)
                o = (_dot(e_nb, v_scr[idx, pl.ds(k0, win), sl]) + _dot(e_ctx, v_scr[idx, n:n + ctx, sl])) * (1.0 / l)
                out = o if out is None else out + o
            y_ref[pl.ds(q0, tile_q), sl] = out.astype(y_ref.dtype)
        return carry

    lax.fori_loop(0, rows // NA_TILE_ROWS, tile_body, 0)


def _attn_na(q, k, v, ck, cv, bias, rowmask, *, n, ctx):
    t = q.shape[0]
    tile = pl.BlockSpec((n, NA_WIDTH), lambda b: (b, 0))
    ctile = pl.BlockSpec((ctx, NA_WIDTH), lambda b: (b, 0))
    return pl.pallas_call(
        functools.partial(_attn_na_kernel, n=n, ctx=ctx),
        grid=(t // n,),
        in_specs=[tile, tile, tile, ctile, ctile, _resident(bias.shape), _resident(rowmask.shape)],
        out_specs=tile,
        out_shape=jax.ShapeDtypeStruct((t, NA_WIDTH), BF16),
        scratch_shapes=[pltpu.VMEM((2, n + ctx, NA_WIDTH), BF16)] * 2,
        compiler_params=_params(),
        name="attn_na",
    )(q, k, v, ck, cv, bias, rowmask)


def _na_row_mask():
    lane_left = np.arange(LANES) < GRID_W
    cases = [np.where(lane_left, 0.0 if c & 2 else NEG_INF, 0.0 if c & 1 else NEG_INF) for c in range(4)]
    return jnp.asarray(np.broadcast_to(np.stack(cases)[:, None, :], (4, GRID_W, LANES)), F32)


def _na_bias_table(rel_bias):
    cq = np.arange(GRID_W)
    kc = np.arange(GRID_W)
    col_start = np.clip(cq - NA_COLS // 2, 0, GRID_W - NA_COLS)
    col_mask = (kc[None, :] >= col_start[:, None]) & (kc[None, :] < col_start[:, None] + NA_COLS)
    dc = np.clip(kc[None, :] - cq[:, None], -(NA_COLS - 1), NA_COLS - 1) + (NA_COLS - 1)
    onehot = dc[None, :, :] == np.arange(2 * NA_COLS - 1)[:, None, None]
    by_dr = jnp.sum(jnp.where(onehot[None, None], rel_bias[:, :, :, None, None], 0.0), axis=2)
    by_dr = jnp.where(col_mask[None, None], by_dr, NEG_INF)
    pad = jnp.full((NA_HEADS, 1, GRID_W, GRID_W), NEG_INF, F32)
    ext = jnp.concatenate([pad, by_dr, pad], axis=1)
    return jnp.concatenate([ext[:, :-1], ext[:, 1:]], axis=-1)


def _attn_swa_kernel(sink_ref, q_ref, k_ref, v_ref, ck_ref, cv_ref, y_ref, k_scr, v_scr, *, n, ctx):
    group = SWA_HEADS // SWA_KV_HEADS
    band = SWA_QTILE + 2 * SWA_BLOCK
    for src, csrc, dst in ((k_ref, ck_ref, k_scr), (v_ref, cv_ref, v_scr)):
        for lo_row, hi_row, x in ((0, n, src[...].astype(F32)), (n, n + ctx, csrc[...])):
            xr = pltpu.roll(x, HEAD_DIM, 1)
            halves = _lane_halves(x.shape)
            for kv in range(SWA_KV_HEADS):
                for idx, half in enumerate(halves):
                    dst[2 * kv + idx, lo_row:hi_row, :] = jnp.where(half, x if idx == kv else xr, 0.0).astype(BF16)

    qi = lax.broadcasted_iota(jnp.int32, (SWA_QTILE, band), 0)
    kj = lax.broadcasted_iota(jnp.int32, (SWA_QTILE, band), 1)
    top = lax.broadcasted_iota(jnp.int32, (2 * SWA_QTILE, 1), 0) < SWA_QTILE

    def block_body(b, carry):
        q0 = pl.multiple_of(b * SWA_QTILE, SWA_QTILE)
        k0 = pl.multiple_of(jnp.clip(q0 - SWA_BLOCK, 0, n - band), SWA_BLOCK)
        mask = jnp.where(jnp.abs(kj - qi + (k0 - q0)) <= SWA_WINDOW, 0.0, NEG_INF)
        mask2 = jnp.concatenate([mask, mask], axis=0)
        for kv in range(SWA_KV_HEADS):
            pairs = (group // 2 * kv, group // 2 * kv + 1)
            q = jnp.concatenate([q_ref[pl.ds(q0, SWA_QTILE), p * LANES:(p + 1) * LANES] for p in pairs], axis=0)
            out = None
            for idx in range(2):
                sink = jnp.where(top, sink_ref[2 * pairs[0] + idx], sink_ref[2 * pairs[1] + idx])
                s_band = _dot_nt(q, k_scr[2 * kv + idx, pl.ds(k0, band), :]) + mask2
                s_ctx = _dot_nt(q, k_scr[2 * kv + idx, n:n + ctx, :])
                cols = ([s_band[:, j * LANES:(j + 1) * LANES] for j in range(band // LANES)]
                        + [s_ctx[:, j * LANES:(j + 1) * LANES] for j in range(ctx // LANES)])
                es, l = _softmax_cols(cols, sink)
                e_band = jnp.concatenate(es[:band // LANES], axis=1).astype(BF16)
                e_ctx = jnp.concatenate(es[band // LANES:], axis=1).astype(BF16)
                o = (_dot(e_band, v_scr[2 * kv + idx, pl.ds(k0, band), :])
                     + _dot(e_ctx, v_scr[2 * kv + idx, n:n + ctx, :])) * (1.0 / l)
                out = o if out is None else out + o
            for i, p in enumerate(pairs):
                y_ref[pl.ds(q0, SWA_QTILE), p * LANES:(p + 1) * LANES] = (
                    out[i * SWA_QTILE:(i + 1) * SWA_QTILE].astype(y_ref.dtype))
        return carry

    lax.fori_loop(0, n // SWA_QTILE, block_body, 0)


def _attn_swa(sink, q, k, v, ck, cv, *, n, ctx):
    t = q.shape[0]
    tile = lambda w: pl.BlockSpec((n, w), lambda b: (b, 0))
    ctile = pl.BlockSpec((ctx, SWA_KV_WIDTH), lambda b: (b, 0))
    return pl.pallas_call(
        functools.partial(_attn_swa_kernel, n=n, ctx=ctx),
        grid=(t // n,),
        in_specs=[pl.BlockSpec(memory_space=pltpu.SMEM), tile(SWA_Q_WIDTH), tile(SWA_KV_WIDTH),
                  tile(SWA_KV_WIDTH), ctile, ctile],
        out_specs=tile(SWA_Q_WIDTH),
        out_shape=jax.ShapeDtypeStruct((t, SWA_Q_WIDTH), BF16),
        scratch_shapes=[pltpu.VMEM((2 * SWA_KV_HEADS, n + ctx, SWA_KV_WIDTH), BF16)] * 2,
        compiler_params=_params(),
        name="attn_swa",
    )(sink, q, k, v, ck, cv)


def _rope_tables(n):
    half = HEAD_DIM // 4
    freqs = jnp.power(ROPE_BASE, -jnp.arange(half, dtype=F32) / half)
    t = jnp.arange(n)
    cos, sin = [], []
    for pos in (t // GRID_W, t % GRID_W):
        ang = pos.astype(F32)[:, None] * freqs[None, :]
        cos += [jnp.cos(ang), jnp.cos(ang)]
        sin += [-jnp.sin(ang), jnp.sin(ang)]
    cos, sin = jnp.concatenate(cos, axis=-1), jnp.concatenate(sin, axis=-1)
    reps = LANES // HEAD_DIM
    return jnp.tile(cos, (1, reps)), jnp.tile(sin, (1, reps))


def _merge_kernel(x_ref, ya_ref, yb_ref, mod_ref, g_ref, wgate_ref, wba_ref, wbb_ref, wout_ref, o_ref):
    x = x_ref[...]
    mod = mod_ref[0]
    h = _modulate(x, g_ref[...], mod[3:4], mod[4:5]).astype(BF16)
    a = jax.nn.sigmoid(_dot(h, wgate_ref[:, :D_MODEL])) * _dot(ya_ref[...], wba_ref[...])
    b = jax.nn.sigmoid(_dot(h, wgate_ref[:, D_MODEL:])) * _dot(yb_ref[...], wbb_ref[...])
    o_ref[...] = x + mod[5:6] * _dot((a + b).astype(BF16), wout_ref[...])


def _merge(x, ya, yb, mod, g, wgate, wba, wbb, wout, *, tokens_per_row, first_row):
    t = x.shape[0]
    tile = lambda w: pl.BlockSpec((TOKEN_TILE, w), lambda i: (i, 0))
    return pl.pallas_call(
        _merge_kernel,
        grid=(t // TOKEN_TILE,),
        in_specs=[tile(D_MODEL), tile(NA_WIDTH), tile(SWA_Q_WIDTH), _mod_spec(tokens_per_row, first_row),
                  _resident((1, D_MODEL)), _resident(wgate.shape), _resident(wba.shape), _resident(wbb.shape),
                  _resident(wout.shape)],
        out_specs=tile(D_MODEL),
        out_shape=jax.ShapeDtypeStruct((t, D_MODEL), F32),
        compiler_params=_params(),
        name="merge",
    )(x, ya, yb, mod, g, wgate, wba, wbb, wout)


def kernel(x_prompt, x_sample, cache_na_k, cache_na_v, cache_swa_k, cache_swa_v, c, c_ctx, w_ada, b_ada,
           norm_ffn1, ffn1_w_gate, ffn1_w_up, ffn1_w_down, norm_mix, w_in, na_rel_bias, swa_sink,
           w_branch_na, w_branch_swa, w_out, norm_ffn2, ffn2_w_gate, ffn2_w_up, ffn2_w_down, norm_final):
    depth = w_ada.shape[0]
    assert depth == 1
    batch, seq, _ = x_prompt.shape
    dec_batch, dec_seq, _ = x_sample.shape
    past = cache_na_k.shape[2]
    layer = 0
    row = lambda v: v.reshape(1, D_MODEL)
    bf = lambda w: w.astype(BF16)

    cond = jnp.zeros((MOD_ROWS, D_MODEL), F32).at[0].set(c_ctx).at[1:1 + dec_batch].set(c)
    mod = _adaln(cond, w_ada[layer], b_ada[layer]).reshape(MOD_ROWS, N_MOD, D_MODEL)

    ffn1 = (row(norm_ffn1[layer]), bf(ffn1_w_gate[layer]), bf(ffn1_w_up[layer]), bf(ffn1_w_down[layer]),
            row(norm_final))
    ffn2 = (row(norm_ffn2[layer]), bf(ffn2_w_gate[layer]), bf(ffn2_w_up[layer]), bf(ffn2_w_down[layer]),
            row(norm_final))
    g_mix = row(norm_mix[layer])
    w_qkv = bf(w_in[layer, :, :QKV_COLS])
    w_gate = bf(w_in[layer, :, QKV_COLS:])
    merge_w = (w_gate, bf(w_branch_na[layer]), bf(w_branch_swa[layer]), bf(w_out[layer]))
    sink = swa_sink[layer]

    where = dict(tokens_per_row=batch * seq, first_row=0)
    x = x_prompt.reshape(batch * seq, D_MODEL)
    x = _ffn(x, mod, *ffn1, first=0, final=False, **where)
    qa, ka, va, qb, kb, vb = _proj(x, mod, g_mix, w_qkv, None, **where)
    ya, yb = _attn_ctx(sink, qa, ka, va, qb, kb, vb, seq=seq)
    x = _merge(x, ya, yb, mod, g_mix, *merge_w, **where)
    y_prompt = _ffn(x, mod, *ffn2, first=6, final=True, **where).reshape(batch, seq, D_MODEL)

    where = dict(tokens_per_row=dec_seq, first_row=1)
    x = x_sample.reshape(dec_batch * dec_seq, D_MODEL)
    x = _ffn(x, mod, *ffn1, first=0, final=False, **where)
    qa, kal, val, qb, kbl, vbl = _proj(x, mod, g_mix, w_qkv, _rope_tables(dec_seq), **where)
    ya = _attn_na(qa, kal, val, cache_na_k[:, layer].reshape(dec_batch * past, NA_WIDTH),
                  cache_na_v[:, layer].reshape(dec_batch * past, NA_WIDTH),
                  _na_bias_table(na_rel_bias[layer]), _na_row_mask(), n=dec_seq, ctx=past)
    yb = _attn_swa(sink, qb, kbl, vbl, cache_swa_k[:, layer].reshape(dec_batch * past, SWA_KV_WIDTH),
                   cache_swa_v[:, layer].reshape(dec_batch * past, SWA_KV_WIDTH), n=dec_seq, ctx=past)
    x = _merge(x, ya, yb, mod, g_mix, *merge_w, **where)
    y_sample = _ffn(x, mod, *ffn2, first=6, final=True, **where).reshape(dec_batch, dec_seq, D_MODEL)

    heads = lambda t, nh: t.reshape(batch, depth, seq, nh, HEAD_DIM)
    return (y_prompt, y_sample, heads(ka, NA_HEADS), heads(va, NA_HEADS),
            heads(kb, SWA_KV_HEADS), heads(vb, SWA_KV_HEADS))
```

```python
import functools

import numpy as np
import jax
import jax.numpy as jnp
from jax import lax
from jax.experimental import pallas as pl
from jax.experimental.pallas import tpu as pltpu

F32 = jnp.float32
BF16 = jnp.bfloat16

D_MODEL = 1024
FFN_DIM = 2816
HEAD_DIM = 64
N_MOD = 9
GRID_W = 64
NA_HEADS = 8
NA_ROWS = 8
NA_COLS = 16
SWA_HEADS = 8
SWA_KV_HEADS = 2
SWA_WINDOW = 128
SWA_BLOCK = 128
ROPE_BASE = 10000.0
EPS = 1e-6
NEG_INF = -1e30
NA_WIDTH = NA_HEADS * HEAD_DIM
SWA_Q_WIDTH = SWA_HEADS * HEAD_DIM
SWA_KV_WIDTH = SWA_KV_HEADS * HEAD_DIM
QKV_COLS = 3 * NA_WIDTH + SWA_Q_WIDTH + 2 * SWA_KV_WIDTH
LANES = 128
MOD_ROWS = 16
VMEM_LIMIT = 56 * 1024 * 1024
TOKEN_TILE = 1024
FFN_TILE = 1024
FFN_CHUNK = 256
SWA_QTILE = 256
NA_TILE_ROWS = 4
LOG2E = 1.4426950408889634
Q_SCALE = HEAD_DIM ** -0.5 * LOG2E


def _dot(a, b):
    return jnp.dot(a, b, preferred_element_type=F32)


def _dot_nt(a, b):
    return lax.dot_general(a, b, (((1,), (1,)), ((), ())), preferred_element_type=F32)


def _silu(x):
    return x * jax.nn.sigmoid(x)


def _rms(x, g):
    return x * lax.rsqrt(jnp.mean(x * x, axis=-1, keepdims=True) + EPS) * g


def _modulate(x, g, shift, scale):
    return _rms(x, g) * (1.0 + scale) + shift


def _resident(shape):
    nd = len(shape)
    return pl.BlockSpec(shape, lambda *_: (0,) * nd, pipeline_mode=pl.Buffered(1))


def _params():
    return pltpu.CompilerParams(dimension_semantics=("arbitrary",), vmem_limit_bytes=VMEM_LIMIT)


def _adaln_kernel(c_ref, w_ref, b_ref, o_ref):
    s = _silu(c_ref[...]).astype(BF16)
    o_ref[...] = _dot(s, w_ref[...].astype(BF16)) + b_ref[...]


def _adaln(cond, w_ada, b_ada):
    n = w_ada.shape[1]
    blk = D_MODEL
    return pl.pallas_call(
        _adaln_kernel,
        grid=(n // blk,),
        in_specs=[pl.BlockSpec((MOD_ROWS, D_MODEL), lambda j: (0, 0)),
                  pl.BlockSpec((D_MODEL, blk), lambda j: (0, j)),
                  pl.BlockSpec((1, blk), lambda j: (0, j))],
        out_specs=pl.BlockSpec((MOD_ROWS, blk), lambda j: (0, j)),
        out_shape=jax.ShapeDtypeStruct((MOD_ROWS, n), F32),
        compiler_params=_params(),
        name="adaln",
    )(cond, w_ada, b_ada.reshape(1, n))


def _mod_spec(tokens_per_row, first_row, tile=TOKEN_TILE):
    tiles_per_row = tokens_per_row // tile
    return pl.BlockSpec((1, N_MOD, D_MODEL), lambda i: (first_row + i // tiles_per_row, 0, 0))


def _ffn_kernel(x_ref, mod_ref, g_ref, wg_ref, wu_ref, wd_ref, gf_ref, o_ref, a_scr, *, first, final):
    x = x_ref[...]
    mod = mod_ref[0]
    shift, scale, gate = mod[first:first + 1], mod[first + 1:first + 2], mod[first + 2:first + 3]
    h = _modulate(x, g_ref[...], shift, scale).astype(BF16)
    for c in range(FFN_DIM // FFN_CHUNK):
        sl = slice(c * FFN_CHUNK, (c + 1) * FFN_CHUNK)
        a_scr[:, sl] = (_silu(_dot(h, wg_ref[:, sl])) * _dot(h, wu_ref[:, sl])).astype(BF16)
    y = x + (0.5 * gate) * _dot(a_scr[...], wd_ref[...])
    if final:
        y = _rms(y, gf_ref[...])
    o_ref[...] = y


def _ffn(x, mod, g, wg, wu, wd, gf, *, tokens_per_row, first_row, first, final):
    t = x.shape[0]
    tile = pl.BlockSpec((FFN_TILE, D_MODEL), lambda i: (i, 0))
    return pl.pallas_call(
        functools.partial(_ffn_kernel, first=first, final=final),
        grid=(t // FFN_TILE,),
        in_specs=[tile, _mod_spec(tokens_per_row, first_row, FFN_TILE), _resident((1, D_MODEL)),
                  _resident(wg.shape), _resident(wu.shape), _resident(wd.shape), _resident((1, D_MODEL))],
        out_specs=tile,
        out_shape=jax.ShapeDtypeStruct((t, D_MODEL), F32),
        scratch_shapes=[pltpu.VMEM((FFN_TILE, FFN_DIM), BF16)],
        compiler_params=_params(),
        name="ffn",
    )(x, mod, g, wg, wu, wd, gf)


def _swap16(x):
    lane = lax.broadcasted_iota(jnp.int32, x.shape, 1)
    return jnp.where(lane % 32 < 16, pltpu.roll(x, LANES - 16, 1), pltpu.roll(x, 16, 1))


def _proj_ctx_kernel(x_ref, mod_ref, g_ref, w_ref, wkvt_ref, qa_ref, qb_ref, ka_ref, va_ref, kb_ref, vb_ref):
    mod = mod_ref[0]
    h = _modulate(x_ref[...], g_ref[...], mod[3:4], mod[4:5]).astype(BF16)
    qa_ref[...] = (_dot(h, w_ref[:, :NA_WIDTH]) * Q_SCALE).astype(qa_ref.dtype)
    qb_ref[...] = (_dot(h, w_ref[:, 3 * NA_WIDTH:3 * NA_WIDTH + SWA_Q_WIDTH]) * Q_SCALE).astype(qb_ref.dtype)
    kvt = _dot_nt(wkvt_ref[...], h)
    requests, _, _, seq = ka_ref.shape
    o = 0
    for ref in (ka_ref, va_ref, kb_ref, vb_ref):
        width = ref.shape[1] * HEAD_DIM
        for r in range(requests):
            ref[r] = kvt[o:o + width, r * seq:(r + 1) * seq].reshape(ref.shape[1:])
        o += width


def _rope(y, cos, sin):
    blocks = [y[:, j * LANES:(j + 1) * LANES] for j in range(y.shape[1] // LANES)]
    return [b * cos + _swap16(b) * sin for b in blocks]


def _proj_lat_kernel(x_ref, mod_ref, g_ref, w_ref, cos_ref, sin_ref,
                     qa_ref, ka_ref, va_ref, qb_ref, kb_ref, vb_ref):
    mod = mod_ref[0]
    h = _modulate(x_ref[...], g_ref[...], mod[3:4], mod[4:5]).astype(BF16)
    cos, sin = cos_ref[...], sin_ref[...]
    o = 0
    for ref, scale in ((qa_ref, True), (ka_ref, False), (va_ref, False)):
        y = _dot(h, w_ref[:, o:o + NA_WIDTH])
        ref[...] = (y * Q_SCALE if scale else y).astype(ref.dtype)
        o += NA_WIDTH
    for j, b in enumerate(_rope(_dot(h, w_ref[:, o:o + SWA_Q_WIDTH]), cos, sin)):
        qb_ref[:, j * LANES:(j + 1) * LANES] = (b * Q_SCALE).astype(qb_ref.dtype)
    o += SWA_Q_WIDTH
    y = _dot(h, w_ref[:, o:o + 2 * SWA_KV_WIDTH])
    kb_ref[...] = _rope(y[:, :SWA_KV_WIDTH], cos, sin)[0].astype(kb_ref.dtype)
    vb_ref[...] = y[:, SWA_KV_WIDTH:].astype(vb_ref.dtype)


def _proj_ctx(x, mod, g, w_qkv, w_kvt, *, seq, tokens_per_row, first_row):
    t = x.shape[0]
    tile = lambda w: pl.BlockSpec((TOKEN_TILE, w), lambda i: (i, 0))
    cache = lambda heads: pl.BlockSpec((TOKEN_TILE // seq, heads, HEAD_DIM, seq), lambda i: (i, 0, 0, 0))
    cache_shape = lambda heads: jax.ShapeDtypeStruct((t // seq, heads, HEAD_DIM, seq), F32)
    kv_heads = (NA_HEADS, NA_HEADS, SWA_KV_HEADS, SWA_KV_HEADS)
    return pl.pallas_call(
        _proj_ctx_kernel,
        grid=(t // TOKEN_TILE,),
        in_specs=[tile(D_MODEL), _mod_spec(tokens_per_row, first_row), _resident((1, D_MODEL)),
                  _resident(w_qkv.shape), _resident(w_kvt.shape)],
        out_specs=[tile(NA_WIDTH), tile(SWA_Q_WIDTH)] + [cache(nh) for nh in kv_heads],
        out_shape=[jax.ShapeDtypeStruct((t, NA_WIDTH), BF16), jax.ShapeDtypeStruct((t, SWA_Q_WIDTH), BF16)]
        + [cache_shape(nh) for nh in kv_heads],
        compiler_params=_params(),
        name="proj_ctx",
    )(x, mod, g, w_qkv, w_kvt)


def _proj_lat(x, mod, g, w_qkv, rope, *, tokens_per_row, first_row):
    t = x.shape[0]
    tile = lambda w: pl.BlockSpec((TOKEN_TILE, w), lambda i: (i, 0))
    widths = (NA_WIDTH, NA_WIDTH, NA_WIDTH, SWA_Q_WIDTH, SWA_KV_WIDTH, SWA_KV_WIDTH)
    tiles_per_row = tokens_per_row // TOKEN_TILE
    return pl.pallas_call(
        _proj_lat_kernel,
        grid=(t // TOKEN_TILE,),
        in_specs=[tile(D_MODEL), _mod_spec(tokens_per_row, first_row), _resident((1, D_MODEL)),
                  _resident(w_qkv.shape)]
        + [pl.BlockSpec((TOKEN_TILE, LANES), lambda i: (i % tiles_per_row, 0))] * 2,
        out_specs=[tile(w) for w in widths],
        out_shape=[jax.ShapeDtypeStruct((t, w), BF16) for w in widths],
        compiler_params=_params(),
        name="proj_lat",
    )(x, mod, g, w_qkv, *rope)


def _lane_halves(shape):
    lane = lax.broadcasted_iota(jnp.int32, shape, 1) % LANES
    return lane < HEAD_DIM, lane >= HEAD_DIM


def _softmax_cols(cols, sink=None):
    m = jnp.max(functools.reduce(jnp.maximum, cols), axis=-1, keepdims=True)
    if sink is not None:
        m = jnp.maximum(m, sink)
    es = [jnp.exp2(c - m) for c in cols]
    l = jnp.sum(functools.reduce(jnp.add, es), axis=-1, keepdims=True)
    if sink is not None:
        l = l + jnp.exp2(sink - m)
    return es, l


def _cols(x):
    return [x[:, j * LANES:(j + 1) * LANES] for j in range(x.shape[1] // LANES)]


def _attn_ctx_kernel(sink_ref, qa_ref, qb_ref, ka_ref, va_ref, kb_ref, vb_ref, ya_ref, yb_ref):
    seq = qa_ref.shape[0]
    lo, hi = _lane_halves((seq, LANES))
    top = lax.broadcasted_iota(jnp.int32, (2 * seq, 1), 0) < seq
    zero = jnp.zeros((), BF16)

    def heads_of_block(q, kt, vt, sink):
        q = jnp.concatenate([jnp.where(lo, q, zero), jnp.where(hi, q, zero)], axis=0)
        es, l = _softmax_cols(_cols(_dot(q, kt)), sink)
        o = _dot_nt(jnp.concatenate(es, axis=1).astype(BF16), vt) * (1.0 / l)
        return jnp.where(lo, o[:seq], o[seq:])

    for p in range(NA_HEADS // 2):
        sl = slice(p * LANES, (p + 1) * LANES)
        kt = ka_ref[2 * p:2 * p + 2].reshape(LANES, seq).astype(BF16)
        vt = va_ref[2 * p:2 * p + 2].reshape(LANES, seq).astype(BF16)
        ya_ref[:, sl] = heads_of_block(qa_ref[:, sl], kt, vt, None).astype(ya_ref.dtype)
    for p in range(SWA_HEADS // 2):
        sl = slice(p * LANES, (p + 1) * LANES)
        kv = (2 * p) // (SWA_HEADS // SWA_KV_HEADS)
        kt, vt = kb_ref[kv].astype(BF16), vb_ref[kv].astype(BF16)
        kt, vt = jnp.concatenate([kt, kt], axis=0), jnp.concatenate([vt, vt], axis=0)
        sink = jnp.where(top, sink_ref[2 * p], sink_ref[2 * p + 1]) * LOG2E
        yb_ref[:, sl] = heads_of_block(qb_ref[:, sl], kt, vt, sink).astype(yb_ref.dtype)


def _attn_ctx(sink, qa, qb, ka, va, kb, vb, *, seq):
    t = qa.shape[0]
    tile = lambda w: pl.BlockSpec((seq, w), lambda b: (b, 0))
    cache = lambda heads: pl.BlockSpec((None, heads, HEAD_DIM, seq), lambda b: (b, 0, 0, 0))
    return pl.pallas_call(
        _attn_ctx_kernel,
        grid=(t // seq,),
        in_specs=[pl.BlockSpec(memory_space=pltpu.SMEM), tile(NA_WIDTH), tile(SWA_Q_WIDTH),
                  cache(NA_HEADS), cache(NA_HEADS), cache(SWA_KV_HEADS), cache(SWA_KV_HEADS)],
        out_specs=[tile(NA_WIDTH), tile(SWA_Q_WIDTH)],
        out_shape=[jax.ShapeDtypeStruct((t, NA_WIDTH), BF16), jax.ShapeDtypeStruct((t, SWA_Q_WIDTH), BF16)],
        compiler_params=_params(),
        name="attn_ctx",
    )(sink, qa, qb, ka, va, kb, vb)


def _attn_na_kernel(q_ref, k_ref, v_ref, ck_ref, cv_ref, bias_ref, y_ref, ck_scr, cv_scr, *, n, ctx):
    rows = n // GRID_W
    tile_q = NA_TILE_ROWS * GRID_W
    zero = jnp.zeros((), BF16)
    ck_scr[...] = ck_ref[...].reshape(NA_WIDTH, ctx).astype(BF16)
    cv_scr[...] = cv_ref[...].reshape(NA_WIDTH, ctx).astype(BF16)
    lo, hi = _lane_halves((tile_q, LANES))
    left_half = lax.broadcasted_iota(jnp.int32, (GRID_W, LANES), 1) < GRID_W
    empty = jnp.zeros((GRID_W, LANES), BF16)

    def tile(q0, k0, key_rows, lead, first):
        pairs = key_rows // 2
        for p in range(NA_HEADS // 2):
            sl = slice(p * LANES, (p + 1) * LANES)
            q = q_ref[pl.ds(q0, tile_q), sl]
            q = jnp.concatenate([jnp.where(lo, q, zero), jnp.where(hi, q, zero)], axis=0)
            s_nb = _dot_nt(q, k_ref[pl.ds(k0, key_rows * GRID_W), sl])
            s_ctx = _dot(q, ck_scr[sl, :])
            e_nb, e_ctx, ls = [], [], []
            for idx in range(2):
                for a in range(NA_TILE_ROWS):
                    qa = slice(idx * tile_q + a * GRID_W, idx * tile_q + (a + 1) * GRID_W)
                    inside = lambda i: first[a] <= i < first[a] + NA_ROWS
                    cols, where = [], []
                    for m in range(pairs):
                        if not (inside(2 * m) or inside(2 * m + 1)):
                            continue
                        blk = s_nb[qa, m * LANES:(m + 1) * LANES] + bias_ref[2 * p + idx, 2 * m - a - lead + NA_ROWS]
                        if not inside(2 * m + 1):
                            blk = jnp.where(left_half, blk, NEG_INF)
                        elif not inside(2 * m):
                            blk = jnp.where(left_half, NEG_INF, blk)
                        cols.append(blk)
                        where.append(m)
                    cols += [s_ctx[qa, j * LANES:(j + 1) * LANES] for j in range(ctx // LANES)]
                    es, l = _softmax_cols(cols)
                    es = [e.astype(BF16) for e in es]
                    e_nb.append(jnp.concatenate(
                        [es[where.index(m)] if m in where else empty for m in range(pairs)], axis=1))
                    e_ctx.append(jnp.concatenate(es[len(where):], axis=1))
                    ls.append(l)
            o = (_dot(jnp.concatenate(e_nb, axis=0), v_ref[pl.ds(k0, key_rows * GRID_W), sl])
                 + _dot_nt(jnp.concatenate(e_ctx, axis=0), cv_scr[sl, :]))
            o = o * (1.0 / jnp.concatenate(ls, axis=0))
            y_ref[pl.ds(q0, tile_q), sl] = jnp.where(lo, o[:tile_q], o[tile_q:]).astype(y_ref.dtype)

    half = NA_ROWS // 2
    tile(0, 0, NA_ROWS, 0, (0,) * NA_TILE_ROWS)

    for t in range(1, rows // NA_TILE_ROWS - 1):
        tile(t * tile_q, t * tile_q - half * GRID_W, NA_ROWS + NA_TILE_ROWS, half, tuple(range(NA_TILE_ROWS)))
    tile(n - tile_q, n - NA_ROWS * GRID_W, NA_ROWS, half, (0,) * NA_TILE_ROWS)


def _attn_na(q, k, v, ck, cv, bias, *, n, ctx):
    t = q.shape[0]
    tile = pl.BlockSpec((n, NA_WIDTH), lambda b: (b, 0))
    ctile = pl.BlockSpec((None, NA_HEADS, HEAD_DIM, ctx), lambda b: (b, 0, 0, 0))
    return pl.pallas_call(
        functools.partial(_attn_na_kernel, n=n, ctx=ctx),
        grid=(t // n,),
        in_specs=[tile, tile, tile, ctile, ctile, _resident(bias.shape)],
        out_specs=tile,
        out_shape=jax.ShapeDtypeStruct((t, NA_WIDTH), BF16),
        scratch_shapes=[pltpu.VMEM((NA_WIDTH, ctx), BF16)] * 2,
        compiler_params=_params(),
        name="attn_na",
    )(q, k, v, ck, cv, bias)


def _na_bias_table(rel_bias):
    cq = np.arange(GRID_W)
    kc = np.arange(GRID_W)
    col_start = np.clip(cq - NA_COLS // 2, 0, GRID_W - NA_COLS)
    col_mask = (kc[None, :] >= col_start[:, None]) & (kc[None, :] < col_start[:, None] + NA_COLS)
    side = GRID_W - NA_COLS
    z = jnp.pad(rel_bias, ((0, 0), (0, 0), (side, side + 1)))
    shifted = jnp.tile(z, (1, 1, GRID_W))[:, :, :GRID_W * (LANES - 1)]
    shifted = shifted.reshape(NA_HEADS, 2 * NA_ROWS - 1, GRID_W, LANES - 1)
    by_dr = shifted[:, :, :, GRID_W - 1:LANES - 1]
    by_dr = jnp.where(col_mask[None, None], by_dr * LOG2E, NEG_INF)
    pad = jnp.full((NA_HEADS, 1, GRID_W, GRID_W), NEG_INF, F32)
    ext = jnp.concatenate([pad, by_dr, pad], axis=1)
    return jnp.concatenate([ext[:, :-1], ext[:, 1:]], axis=-1)


def _attn_swa_kernel(sink_ref, q_ref, k_ref, v_ref, ck_ref, cv_ref, y_ref, k_scr, v_scr, ck_scr, cv_scr, *, n, ctx):
    group = SWA_HEADS // SWA_KV_HEADS
    band = SWA_QTILE + 2 * SWA_BLOCK
    for src, csrc, dst, cdst in ((k_ref, ck_ref, k_scr, ck_scr), (v_ref, cv_ref, v_scr, cv_scr)):
        x = src[...].astype(F32)
        xr = pltpu.roll(x, HEAD_DIM, 1)
        first_half, second_half = _lane_halves(x.shape)
        dst[0] = jnp.where(first_half, x, xr).astype(BF16)
        dst[1] = jnp.where(second_half, x, xr).astype(BF16)
        for kv in range(SWA_KV_HEADS):
            c = csrc[kv].astype(BF16)
            cdst[kv] = jnp.concatenate([c, c], axis=0)

    qi = lax.broadcasted_iota(jnp.int32, (SWA_QTILE, band), 0)
    kj = lax.broadcasted_iota(jnp.int32, (SWA_QTILE, band), 1)
    lo, hi = _lane_halves((SWA_QTILE, LANES))
    top = lax.broadcasted_iota(jnp.int32, (2 * SWA_QTILE, 1), 0) < SWA_QTILE
    zero = jnp.zeros((), BF16)

    def block_body(b, carry):
        q0 = pl.multiple_of(b * SWA_QTILE, SWA_QTILE)
        k0 = pl.multiple_of(jnp.clip(q0 - SWA_BLOCK, 0, n - band), SWA_BLOCK)
        mask = jnp.where(jnp.abs(kj - qi + (k0 - q0)) <= SWA_WINDOW, 0.0, NEG_INF)
        mask2 = jnp.concatenate([mask, mask], axis=0)
        for p in range(SWA_HEADS // 2):
            kv = 2 * p // group
            sl = slice(p * LANES, (p + 1) * LANES)
            q = q_ref[pl.ds(q0, SWA_QTILE), sl]
            q = jnp.concatenate([jnp.where(lo, q, zero), jnp.where(hi, q, zero)], axis=0)
            s_band = _dot_nt(q, k_scr[kv, pl.ds(k0, band), :])
            s_ctx = _dot(q, ck_scr[kv])
            sink = jnp.where(top, sink_ref[2 * p], sink_ref[2 * p + 1]) * LOG2E
            es, l = _softmax_cols(_cols(s_band + mask2) + _cols(s_ctx), sink)
            e_band = jnp.concatenate(es[:band // LANES], axis=1).astype(BF16)
            e_ctx = jnp.concatenate(es[band // LANES:], axis=1).astype(BF16)
            o = (_dot(e_band, v_scr[kv, pl.ds(k0, band), :]) + _dot_nt(e_ctx, cv_scr[kv])) * (1.0 / l)
            y_ref[pl.ds(q0, SWA_QTILE), sl] = jnp.where(lo, o[:SWA_QTILE], o[SWA_QTILE:]).astype(y_ref.dtype)
        return carry

    lax.fori_loop(0, n // SWA_QTILE, block_body, 0, unroll=4)


def _attn_swa(sink, q, k, v, ck, cv, *, n, ctx):
    t = q.shape[0]
    tile = lambda w: pl.BlockSpec((n, w), lambda b: (b, 0))
    ctile = pl.BlockSpec((None, SWA_KV_HEADS, HEAD_DIM, ctx), lambda b: (b, 0, 0, 0))
    return pl.pallas_call(
        functools.partial(_attn_swa_kernel, n=n, ctx=ctx),
        grid=(t // n,),
        in_specs=[pl.BlockSpec(memory_space=pltpu.SMEM), tile(SWA_Q_WIDTH), tile(SWA_KV_WIDTH),
                  tile(SWA_KV_WIDTH), ctile, ctile],
        out_specs=tile(SWA_Q_WIDTH),
        out_shape=jax.ShapeDtypeStruct((t, SWA_Q_WIDTH), BF16),
        scratch_shapes=[pltpu.VMEM((SWA_KV_HEADS, n, SWA_KV_WIDTH), BF16)] * 2
        + [pltpu.VMEM((SWA_KV_HEADS, LANES, ctx), BF16)] * 2,
        compiler_params=_params(),
        name="attn_swa",
    )(sink, q, k, v, ck, cv)


def _rope_tables(n):
    half = HEAD_DIM // 4
    freqs = jnp.power(ROPE_BASE, -jnp.arange(half, dtype=F32) / half)
    t = jnp.arange(n)
    cos, sin = [], []
    for pos in (t // GRID_W, t % GRID_W):
        ang = pos.astype(F32)[:, None] * freqs[None, :]
        cos += [jnp.cos(ang), jnp.cos(ang)]
        sin += [-jnp.sin(ang), jnp.sin(ang)]
    cos, sin = jnp.concatenate(cos, axis=-1), jnp.concatenate(sin, axis=-1)
    reps = LANES // HEAD_DIM
    return jnp.tile(cos, (1, reps)), jnp.tile(sin, (1, reps))


def _merge_kernel(x_ref, ya_ref, yb_ref, mod_ref, g_ref, wgate_ref, wba_ref, wbb_ref, wout_ref, o_ref):
    x = x_ref[...]
    mod = mod_ref[0]
    h = _modulate(x, g_ref[...], mod[3:4], mod[4:5]).astype(BF16)
    a = jax.nn.sigmoid(_dot(h, wgate_ref[:, :D_MODEL])) * _dot(ya_ref[...], wba_ref[...])
    b = jax.nn.sigmoid(_dot(h, wgate_ref[:, D_MODEL:])) * _dot(yb_ref[...], wbb_ref[...])
    o_ref[...] = x + mod[5:6] * _dot((a + b).astype(BF16), wout_ref[...])


def _merge(x, ya, yb, mod, g, wgate, wba, wbb, wout, *, tokens_per_row, first_row):
    t = x.shape[0]
    tile = lambda w: pl.BlockSpec((TOKEN_TILE, w), lambda i: (i, 0))
    return pl.pallas_call(
        _merge_kernel,
        grid=(t // TOKEN_TILE,),
        in_specs=[tile(D_MODEL), tile(NA_WIDTH), tile(SWA_Q_WIDTH), _mod_spec(tokens_per_row, first_row),
                  _resident((1, D_MODEL)), _resident(wgate.shape), _resident(wba.shape), _resident(wbb.shape),
                  _resident(wout.shape)],
        out_specs=tile(D_MODEL),
        out_shape=jax.ShapeDtypeStruct((t, D_MODEL), F32),
        compiler_params=_params(),
        name="merge",
    )(x, ya, yb, mod, g, wgate, wba, wbb, wout)


def kernel(x_prompt, x_sample, cache_na_k, cache_na_v, cache_swa_k, cache_swa_v, c, c_ctx, w_ada, b_ada,
           norm_ffn1, ffn1_w_gate, ffn1_w_up, ffn1_w_down, norm_mix, w_in, na_rel_bias, swa_sink,
           w_branch_na, w_branch_swa, w_out, norm_ffn2, ffn2_w_gate, ffn2_w_up, ffn2_w_down, norm_final):
    depth = w_ada.shape[0]
    assert depth == 1
    batch, seq, _ = x_prompt.shape
    dec_batch, dec_seq, _ = x_sample.shape
    past = cache_na_k.shape[2]
    layer = 0
    row = lambda v: v.reshape(1, D_MODEL)
    bf = lambda w: w.astype(BF16)

    cond = jnp.zeros((MOD_ROWS, D_MODEL), F32).at[0].set(c_ctx).at[1:1 + dec_batch].set(c)
    mod = _adaln(cond, w_ada[layer], b_ada[layer]).reshape(MOD_ROWS, N_MOD, D_MODEL)

    ffn1 = (row(norm_ffn1[layer]), bf(ffn1_w_gate[layer]), bf(ffn1_w_up[layer]), bf(ffn1_w_down[layer]),
            row(norm_final))
    ffn2 = (row(norm_ffn2[layer]), bf(ffn2_w_gate[layer]), bf(ffn2_w_up[layer]), bf(ffn2_w_down[layer]),
            row(norm_final))
    g_mix = row(norm_mix[layer])
    w_qkv = bf(w_in[layer, :, :QKV_COLS])
    w_kvt = bf(jnp.concatenate([w_in[layer, :, NA_WIDTH:3 * NA_WIDTH],
                                w_in[layer, :, 3 * NA_WIDTH + SWA_Q_WIDTH:QKV_COLS]], axis=1).T)
    w_gate = bf(w_in[layer, :, QKV_COLS:])
    merge_w = (w_gate, bf(w_branch_na[layer]), bf(w_branch_swa[layer]), bf(w_out[layer]))
    sink = swa_sink[layer]

    where = dict(tokens_per_row=batch * seq, first_row=0)
    x = x_prompt.reshape(batch * seq, D_MODEL)
    x = _ffn(x, mod, *ffn1, first=0, final=False, **where)
    qa, qb, *new_cache = _proj_ctx(x, mod, g_mix, w_qkv, w_kvt, seq=seq, **where)
    ya, yb = _attn_ctx(sink, qa, qb, *new_cache, seq=seq)
    x = _merge(x, ya, yb, mod, g_mix, *merge_w, **where)
    y_prompt = _ffn(x, mod, *ffn2, first=6, final=True, **where).reshape(batch, seq, D_MODEL)

    where = dict(tokens_per_row=dec_seq, first_row=1)
    x = x_sample.reshape(dec_batch * dec_seq, D_MODEL)
    x = _ffn(x, mod, *ffn1, first=0, final=False, **where)
    qa, kal, val, qb, kbl, vbl = _proj_lat(x, mod, g_mix, w_qkv, _rope_tables(dec_seq), **where)
    transposed = lambda cache: jnp.transpose(cache[:, layer], (0, 2, 3, 1))
    ya = _attn_na(qa, kal, val, transposed(cache_na_k), transposed(cache_na_v),
                  _na_bias_table(na_rel_bias[layer]), n=dec_seq, ctx=past)
    yb = _attn_swa(sink, qb, kbl, vbl, transposed(cache_swa_k), transposed(cache_swa_v), n=dec_seq, ctx=past)
    x = _merge(x, ya, yb, mod, g_mix, *merge_w, **where)
    y_sample = _ffn(x, mod, *ffn2, first=6, final=True, **where).reshape(dec_batch, dec_seq, D_MODEL)

    new_cache = [jnp.transpose(t, (0, 3, 1, 2))[:, None] for t in new_cache]
    return (y_prompt, y_sample, *new_cache)
```

```python
import functools

import numpy as np
import jax
import jax.numpy as jnp
from jax import lax
from jax.experimental import pallas as pl
from jax.experimental.pallas import tpu as pltpu

F32 = jnp.float32
BF16 = jnp.bfloat16

D_MODEL = 1024
FFN_DIM = 2816
HEAD_DIM = 64
N_MOD = 9
GRID_W = 64
NA_HEADS = 8
NA_ROWS = 8
NA_COLS = 16
SWA_HEADS = 8
SWA_KV_HEADS = 2
SWA_WINDOW = 128
SWA_BLOCK = 128
ROPE_BASE = 10000.0
EPS = 1e-6
NEG_INF = -1e30
NA_WIDTH = NA_HEADS * HEAD_DIM
SWA_Q_WIDTH = SWA_HEADS * HEAD_DIM
SWA_KV_WIDTH = SWA_KV_HEADS * HEAD_DIM
QKV_COLS = 3 * NA_WIDTH + SWA_Q_WIDTH + 2 * SWA_KV_WIDTH
LANES = 128
MOD_ROWS = 16
VMEM_LIMIT = 56 * 1024 * 1024
TOKEN_TILE = 1024
FFN_TILE = 1024
FFN_CHUNK = 256
SWA_QTILE = 256
NA_TILE_ROWS = 4
LOG2E = 1.4426950408889634
Q_SCALE = HEAD_DIM ** -0.5 * LOG2E


def _dot(a, b):
    return jnp.dot(a, b, preferred_element_type=F32)


def _dot_nt(a, b):
    return lax.dot_general(a, b, (((1,), (1,)), ((), ())), preferred_element_type=F32)


def _silu(x):
    return x * jax.nn.sigmoid(x)


def _rms(x, g):
    return x * lax.rsqrt(jnp.mean(x * x, axis=-1, keepdims=True) + EPS) * g


def _modulate(x, g, shift, scale):
    return _rms(x, g) * (1.0 + scale) + shift


def _resident(shape):
    nd = len(shape)
    return pl.BlockSpec(shape, lambda *_: (0,) * nd, pipeline_mode=pl.Buffered(1))


def _params():
    return pltpu.CompilerParams(dimension_semantics=("arbitrary",), vmem_limit_bytes=VMEM_LIMIT)


def _adaln_kernel(c_ref, w_ref, b_ref, o_ref):
    s = _silu(c_ref[...]).astype(BF16)
    o_ref[...] = _dot(s, w_ref[...].astype(BF16)) + b_ref[...]


def _adaln(cond, w_ada, b_ada):
    n = w_ada.shape[1]
    blk = D_MODEL
    return pl.pallas_call(
        _adaln_kernel,
        grid=(n // blk,),
        in_specs=[pl.BlockSpec((MOD_ROWS, D_MODEL), lambda j: (0, 0)),
                  pl.BlockSpec((D_MODEL, blk), lambda j: (0, j)),
                  pl.BlockSpec((1, blk), lambda j: (0, j))],
        out_specs=pl.BlockSpec((MOD_ROWS, blk), lambda j: (0, j)),
        out_shape=jax.ShapeDtypeStruct((MOD_ROWS, n), F32),
        compiler_params=_params(),
        name="adaln",
    )(cond, w_ada, b_ada.reshape(1, n))


def _mod_spec(tokens_per_row, first_row, tile=TOKEN_TILE):
    tiles_per_row = tokens_per_row // tile
    return pl.BlockSpec((1, N_MOD, D_MODEL), lambda i: (first_row + i // tiles_per_row, 0, 0))


def _ffn_kernel(x_ref, mod_ref, g_ref, wg_ref, wu_ref, wd_ref, gf_ref, o_ref, a_scr, *, first, final):
    x = x_ref[...]
    mod = mod_ref[0]
    shift, scale, gate = mod[first:first + 1], mod[first + 1:first + 2], mod[first + 2:first + 3]
    h = _modulate(x, g_ref[...], shift, scale).astype(BF16)
    for c in range(FFN_DIM // FFN_CHUNK):
        sl = slice(c * FFN_CHUNK, (c + 1) * FFN_CHUNK)
        a_scr[:, sl] = (_silu(_dot(h, wg_ref[:, sl])) * _dot(h, wu_ref[:, sl])).astype(BF16)
    y = x + (0.5 * gate) * _dot(a_scr[...], wd_ref[...])
    if final:
        y = _rms(y, gf_ref[...])
    o_ref[...] = y


def _ffn(x, mod, g, wg, wu, wd, gf, *, tokens_per_row, first_row, first, final):
    t = x.shape[0]
    tile = pl.BlockSpec((FFN_TILE, D_MODEL), lambda i: (i, 0))
    return pl.pallas_call(
        functools.partial(_ffn_kernel, first=first, final=final),
        grid=(t // FFN_TILE,),
        in_specs=[tile, _mod_spec(tokens_per_row, first_row, FFN_TILE), _resident((1, D_MODEL)),
                  _resident(wg.shape), _resident(wu.shape), _resident(wd.shape), _resident((1, D_MODEL))],
        out_specs=tile,
        out_shape=jax.ShapeDtypeStruct((t, D_MODEL), F32),
        scratch_shapes=[pltpu.VMEM((FFN_TILE, FFN_DIM), BF16)],
        compiler_params=_params(),
        name="ffn",
    )(x, mod, g, wg, wu, wd, gf)


def _swap16(x):
    lane = lax.broadcasted_iota(jnp.int32, x.shape, 1)
    return jnp.where(lane % 32 < 16, pltpu.roll(x, LANES - 16, 1), pltpu.roll(x, 16, 1))


def _proj_ctx_kernel(x_ref, mod_ref, g_ref, w_ref, wkvt_ref, qa_ref, qb_ref, ka_ref, va_ref, kb_ref, vb_ref):
    mod = mod_ref[0]
    h = _modulate(x_ref[...], g_ref[...], mod[3:4], mod[4:5]).astype(BF16)
    qa_ref[...] = (_dot(h, w_ref[:, :NA_WIDTH]) * Q_SCALE).astype(qa_ref.dtype)
    qb_ref[...] = (_dot(h, w_ref[:, 3 * NA_WIDTH:3 * NA_WIDTH + SWA_Q_WIDTH]) * Q_SCALE).astype(qb_ref.dtype)
    kvt = _dot_nt(wkvt_ref[...], h)
    requests, _, _, seq = ka_ref.shape
    o = 0
    for ref in (ka_ref, va_ref, kb_ref, vb_ref):
        width = ref.shape[1] * HEAD_DIM
        for r in range(requests):
            ref[r] = kvt[o:o + width, r * seq:(r + 1) * seq].reshape(ref.shape[1:])
        o += width


def _rope(y, cos, sin):
    blocks = [y[:, j * LANES:(j + 1) * LANES] for j in range(y.shape[1] // LANES)]
    return [b * cos + _swap16(b) * sin for b in blocks]


def _proj_lat_kernel(x_ref, mod_ref, g_ref, w_ref, cos_ref, sin_ref,
                     qa_ref, ka_ref, va_ref, qb_ref, kb_ref, vb_ref):
    mod = mod_ref[0]
    h = _modulate(x_ref[...], g_ref[...], mod[3:4], mod[4:5]).astype(BF16)
    cos, sin = cos_ref[...], sin_ref[...]
    o = 0
    for ref, scale in ((qa_ref, True), (ka_ref, False), (va_ref, False)):
        y = _dot(h, w_ref[:, o:o + NA_WIDTH])
        ref[...] = (y * Q_SCALE if scale else y).astype(ref.dtype)
        o += NA_WIDTH
    for j, b in enumerate(_rope(_dot(h, w_ref[:, o:o + SWA_Q_WIDTH]), cos, sin)):
        qb_ref[:, j * LANES:(j + 1) * LANES] = (b * Q_SCALE).astype(qb_ref.dtype)
    o += SWA_Q_WIDTH
    y = _dot(h, w_ref[:, o:o + 2 * SWA_KV_WIDTH])
    kb_ref[...] = _rope(y[:, :SWA_KV_WIDTH], cos, sin)[0].astype(kb_ref.dtype)
    vb_ref[...] = y[:, SWA_KV_WIDTH:].astype(vb_ref.dtype)


def _proj_ctx(x, mod, g, w_qkv, w_kvt, *, seq, tokens_per_row, first_row):
    t = x.shape[0]
    tile = lambda w: pl.BlockSpec((TOKEN_TILE, w), lambda i: (i, 0))
    cache = lambda heads: pl.BlockSpec((TOKEN_TILE // seq, heads, HEAD_DIM, seq), lambda i: (i, 0, 0, 0))
    cache_shape = lambda heads: jax.ShapeDtypeStruct((t // seq, heads, HEAD_DIM, seq), F32)
    kv_heads = (NA_HEADS, NA_HEADS, SWA_KV_HEADS, SWA_KV_HEADS)
    return pl.pallas_call(
        _proj_ctx_kernel,
        grid=(t // TOKEN_TILE,),
        in_specs=[tile(D_MODEL), _mod_spec(tokens_per_row, first_row), _resident((1, D_MODEL)),
                  _resident(w_qkv.shape), _resident(w_kvt.shape)],
        out_specs=[tile(NA_WIDTH), tile(SWA_Q_WIDTH)] + [cache(nh) for nh in kv_heads],
        out_shape=[jax.ShapeDtypeStruct((t, NA_WIDTH), BF16), jax.ShapeDtypeStruct((t, SWA_Q_WIDTH), BF16)]
        + [cache_shape(nh) for nh in kv_heads],
        compiler_params=_params(),
        name="proj_ctx",
    )(x, mod, g, w_qkv, w_kvt)


def _proj_lat(x, mod, g, w_qkv, rope, *, tokens_per_row, first_row):
    t = x.shape[0]
    tile = lambda w: pl.BlockSpec((TOKEN_TILE, w), lambda i: (i, 0))
    widths = (NA_WIDTH, NA_WIDTH, NA_WIDTH, SWA_Q_WIDTH, SWA_KV_WIDTH, SWA_KV_WIDTH)
    tiles_per_row = tokens_per_row // TOKEN_TILE
    return pl.pallas_call(
        _proj_lat_kernel,
        grid=(t // TOKEN_TILE,),
        in_specs=[tile(D_MODEL), _mod_spec(tokens_per_row, first_row), _resident((1, D_MODEL)),
                  _resident(w_qkv.shape)]
        + [pl.BlockSpec((TOKEN_TILE, LANES), lambda i: (i % tiles_per_row, 0))] * 2,
        out_specs=[tile(w) for w in widths],
        out_shape=[jax.ShapeDtypeStruct((t, w), BF16) for w in widths],
        compiler_params=_params(),
        name="proj_lat",
    )(x, mod, g, w_qkv, *rope)


def _lane_halves(shape):
    lane = lax.broadcasted_iota(jnp.int32, shape, 1) % LANES
    return lane < HEAD_DIM, lane >= HEAD_DIM


def _softmax_cols(cols, sink=None):
    m = jnp.max(functools.reduce(jnp.maximum, cols), axis=-1, keepdims=True)
    if sink is not None:
        m = jnp.maximum(m, sink)
    es = [jnp.exp2(c - m) for c in cols]
    l = jnp.sum(functools.reduce(jnp.add, es), axis=-1, keepdims=True)
    if sink is not None:
        l = l + jnp.exp2(sink - m)
    return es, l


def _cols(x):
    return [x[:, j * LANES:(j + 1) * LANES] for j in range(x.shape[1] // LANES)]


def _attn_ctx_kernel(sink_ref, qa_ref, qb_ref, ka_ref, va_ref, kb_ref, vb_ref, ya_ref, yb_ref):
    seq = qa_ref.shape[0]
    lo, hi = _lane_halves((seq, LANES))
    top = lax.broadcasted_iota(jnp.int32, (2 * seq, 1), 0) < seq
    zero = jnp.zeros((), BF16)

    def heads_of_block(q, kt, vt, sink):
        q = jnp.concatenate([jnp.where(lo, q, zero), jnp.where(hi, q, zero)], axis=0)
        es, l = _softmax_cols(_cols(_dot(q, kt)), sink)
        o = _dot_nt(jnp.concatenate(es, axis=1).astype(BF16), vt) * (1.0 / l)
        return jnp.where(lo, o[:seq], o[seq:])

    for p in range(NA_HEADS // 2):
        sl = slice(p * LANES, (p + 1) * LANES)
        kt = ka_ref[2 * p:2 * p + 2].reshape(LANES, seq).astype(BF16)
        vt = va_ref[2 * p:2 * p + 2].reshape(LANES, seq).astype(BF16)
        ya_ref[:, sl] = heads_of_block(qa_ref[:, sl], kt, vt, None).astype(ya_ref.dtype)
    for p in range(SWA_HEADS // 2):
        sl = slice(p * LANES, (p + 1) * LANES)
        kv = (2 * p) // (SWA_HEADS // SWA_KV_HEADS)
        kt, vt = kb_ref[kv].astype(BF16), vb_ref[kv].astype(BF16)
        kt, vt = jnp.concatenate([kt, kt], axis=0), jnp.concatenate([vt, vt], axis=0)
        sink = jnp.where(top, sink_ref[2 * p], sink_ref[2 * p + 1]) * LOG2E
        yb_ref[:, sl] = heads_of_block(qb_ref[:, sl], kt, vt, sink).astype(yb_ref.dtype)


def _attn_ctx(sink, qa, qb, ka, va, kb, vb, *, seq):
    t = qa.shape[0]
    tile = lambda w: pl.BlockSpec((seq, w), lambda b: (b, 0))
    cache = lambda heads: pl.BlockSpec((None, heads, HEAD_DIM, seq), lambda b: (b, 0, 0, 0))
    return pl.pallas_call(
        _attn_ctx_kernel,
        grid=(t // seq,),
        in_specs=[pl.BlockSpec(memory_space=pltpu.SMEM), tile(NA_WIDTH), tile(SWA_Q_WIDTH),
                  cache(NA_HEADS), cache(NA_HEADS), cache(SWA_KV_HEADS), cache(SWA_KV_HEADS)],
        out_specs=[tile(NA_WIDTH), tile(SWA_Q_WIDTH)],
        out_shape=[jax.ShapeDtypeStruct((t, NA_WIDTH), BF16), jax.ShapeDtypeStruct((t, SWA_Q_WIDTH), BF16)],
        compiler_params=_params(),
        name="attn_ctx",
    )(sink, qa, qb, ka, va, kb, vb)


def _attn_na_kernel(q_ref, k_ref, v_ref, ck_ref, cv_ref, rows_ref, y_ref, ck_scr, cv_scr, bias_ref, *, n, ctx):
    rows = n // GRID_W
    tile_q = NA_TILE_ROWS * GRID_W
    zero = jnp.zeros((), BF16)
    @pl.when(pl.program_id(0) == 0)
    def _():
        _fill_na_bias(rows_ref, bias_ref)

    ck_scr[...] = ck_ref[...].reshape(NA_WIDTH, ctx).astype(BF16)
    cv_scr[...] = cv_ref[...].reshape(NA_WIDTH, ctx).astype(BF16)
    lo, hi = _lane_halves((tile_q, LANES))
    left_half = lax.broadcasted_iota(jnp.int32, (GRID_W, LANES), 1) < GRID_W
    empty = jnp.zeros((GRID_W, LANES), BF16)

    def tile(q0, k0, key_rows, lead, first):
        pairs = key_rows // 2
        for p in range(NA_HEADS // 2):
            sl = slice(p * LANES, (p + 1) * LANES)
            q = q_ref[pl.ds(q0, tile_q), sl]
            q = jnp.concatenate([jnp.where(lo, q, zero), jnp.where(hi, q, zero)], axis=0)
            s_nb = _dot_nt(q, k_ref[pl.ds(k0, key_rows * GRID_W), sl])
            s_ctx = _dot(q, ck_scr[sl, :])
            e_nb, e_ctx, ls = [], [], []
            for idx in range(2):
                for a in range(NA_TILE_ROWS):
                    qa = slice(idx * tile_q + a * GRID_W, idx * tile_q + (a + 1) * GRID_W)
                    inside = lambda i: first[a] <= i < first[a] + NA_ROWS
                    cols, where = [], []
                    for m in range(pairs):
                        if not (inside(2 * m) or inside(2 * m + 1)):
                            continue
                        blk = s_nb[qa, m * LANES:(m + 1) * LANES] + bias_ref[2 * p + idx, 2 * m - a - lead + NA_ROWS]
                        if not inside(2 * m + 1):
                            blk = jnp.where(left_half, blk, NEG_INF)
                        elif not inside(2 * m):
                            blk = jnp.where(left_half, NEG_INF, blk)
                        cols.append(blk)
                        where.append(m)
                    cols += [s_ctx[qa, j * LANES:(j + 1) * LANES] for j in range(ctx // LANES)]
                    es, l = _softmax_cols(cols)
                    es = [e.astype(BF16) for e in es]
                    e_nb.append(jnp.concatenate(
                        [es[where.index(m)] if m in where else empty for m in range(pairs)], axis=1))
                    e_ctx.append(jnp.concatenate(es[len(where):], axis=1))
                    ls.append(l)
            o = (_dot(jnp.concatenate(e_nb, axis=0), v_ref[pl.ds(k0, key_rows * GRID_W), sl])
                 + _dot_nt(jnp.concatenate(e_ctx, axis=0), cv_scr[sl, :]))
            o = o * (1.0 / jnp.concatenate(ls, axis=0))
            y_ref[pl.ds(q0, tile_q), sl] = jnp.where(lo, o[:tile_q], o[tile_q:]).astype(y_ref.dtype)

    half = NA_ROWS // 2
    tile(0, 0, NA_ROWS, 0, (0,) * NA_TILE_ROWS)

    for t in range(1, rows // NA_TILE_ROWS - 1):
        tile(t * tile_q, t * tile_q - half * GRID_W, NA_ROWS + NA_TILE_ROWS, half, tuple(range(NA_TILE_ROWS)))
    tile(n - tile_q, n - NA_ROWS * GRID_W, NA_ROWS, half, (0,) * NA_TILE_ROWS)


def _attn_na(q, k, v, ck, cv, bias_rows, *, n, ctx):
    t = q.shape[0]
    tile = pl.BlockSpec((n, NA_WIDTH), lambda b: (b, 0))
    ctile = pl.BlockSpec((None, NA_HEADS, HEAD_DIM, ctx), lambda b: (b, 0, 0, 0))
    return pl.pallas_call(
        functools.partial(_attn_na_kernel, n=n, ctx=ctx),
        grid=(t // n,),
        in_specs=[tile, tile, tile, ctile, ctile, _resident(bias_rows.shape)],
        out_specs=tile,
        out_shape=jax.ShapeDtypeStruct((t, NA_WIDTH), BF16),
        scratch_shapes=[pltpu.VMEM((NA_WIDTH, ctx), BF16)] * 2
        + [pltpu.VMEM((NA_HEADS, 2 * NA_ROWS, GRID_W, LANES), F32)],
        compiler_params=_params(),
        name="attn_na",
    )(q, k, v, ck, cv, bias_rows)


def _na_bias_rows(rel_bias):
    side = GRID_W - NA_COLS
    z = jnp.pad(rel_bias * LOG2E, ((0, 0), (0, 0), (side, side + 1)))
    return jnp.pad(z, ((0, 0), (1, 1), (0, 0)), constant_values=NEG_INF)


def _fill_na_bias(rows_ref, bias_scr):
    q = lax.broadcasted_iota(jnp.int32, (GRID_W, LANES), 0)
    lane = lax.broadcasted_iota(jnp.int32, (GRID_W, LANES), 1)
    kc = lane % GRID_W
    col_start = jnp.clip(q - NA_COLS // 2, 0, GRID_W - NA_COLS)
    in_window = (kc >= col_start) & (kc < col_start + NA_COLS)
    for h in range(NA_HEADS):
        for j in range(2 * NA_ROWS):
            left = jnp.broadcast_to(rows_ref[h, j:j + 1, :], (GRID_W, LANES))
            right = jnp.broadcast_to(rows_ref[h, j + 1:j + 2, :], (GRID_W, LANES))
            left = pltpu.roll(left, LANES - GRID_W + 1, 1, stride=1, stride_axis=0)
            right = pltpu.roll(right, 1, 1, stride=1, stride_axis=0)
            bias_scr[h, j] = jnp.where(in_window, jnp.where(lane < GRID_W, left, right), NEG_INF)


def _attn_swa_kernel(sink_ref, q_ref, k_ref, v_ref, ck_ref, cv_ref, y_ref, k_scr, v_scr, ck_scr, cv_scr, *, n, ctx):
    group = SWA_HEADS // SWA_KV_HEADS
    band = SWA_QTILE + 2 * SWA_BLOCK
    for src, csrc, dst, cdst in ((k_ref, ck_ref, k_scr, ck_scr), (v_ref, cv_ref, v_scr, cv_scr)):
        x = src[...].astype(F32)
        xr = pltpu.roll(x, HEAD_DIM, 1)
        first_half, second_half = _lane_halves(x.shape)
        dst[0] = jnp.where(first_half, x, xr).astype(BF16)
        dst[1] = jnp.where(second_half, x, xr).astype(BF16)
        for kv in range(SWA_KV_HEADS):
            c = csrc[kv].astype(BF16)
            cdst[kv] = jnp.concatenate([c, c], axis=0)

    qi = lax.broadcasted_iota(jnp.int32, (SWA_QTILE, band), 0)
    kj = lax.broadcasted_iota(jnp.int32, (SWA_QTILE, band), 1)
    lo, hi = _lane_halves((SWA_QTILE, LANES))
    top = lax.broadcasted_iota(jnp.int32, (2 * SWA_QTILE, 1), 0) < SWA_QTILE
    zero = jnp.zeros((), BF16)

    def block_body(b, carry):
        q0 = pl.multiple_of(b * SWA_QTILE, SWA_QTILE)
        k0 = pl.multiple_of(jnp.clip(q0 - SWA_BLOCK, 0, n - band), SWA_BLOCK)
        mask = jnp.where(jnp.abs(kj - qi + (k0 - q0)) <= SWA_WINDOW, 0.0, NEG_INF)
        mask2 = jnp.concatenate([mask, mask], axis=0)
        for p in range(SWA_HEADS // 2):
            kv = 2 * p // group
            sl = slice(p * LANES, (p + 1) * LANES)
            q = q_ref[pl.ds(q0, SWA_QTILE), sl]
            q = jnp.concatenate([jnp.where(lo, q, zero), jnp.where(hi, q, zero)], axis=0)
            s_band = _dot_nt(q, k_scr[kv, pl.ds(k0, band), :])
            s_ctx = _dot(q, ck_scr[kv])
            sink = jnp.where(top, sink_ref[2 * p], sink_ref[2 * p + 1]) * LOG2E
            es, l = _softmax_cols(_cols(s_band + mask2) + _cols(s_ctx), sink)
            e_band = jnp.concatenate(es[:band // LANES], axis=1).astype(BF16)
            e_ctx = jnp.concatenate(es[band // LANES:], axis=1).astype(BF16)
            o = (_dot(e_band, v_scr[kv, pl.ds(k0, band), :]) + _dot_nt(e_ctx, cv_scr[kv])) * (1.0 / l)
            y_ref[pl.ds(q0, SWA_QTILE), sl] = jnp.where(lo, o[:SWA_QTILE], o[SWA_QTILE:]).astype(y_ref.dtype)
        return carry

    lax.fori_loop(0, n // SWA_QTILE, block_body, 0, unroll=True)


def _attn_swa(sink, q, k, v, ck, cv, *, n, ctx):
    t = q.shape[0]
    tile = lambda w: pl.BlockSpec((n, w), lambda b: (b, 0))
    ctile = pl.BlockSpec((None, SWA_KV_HEADS, HEAD_DIM, ctx), lambda b: (b, 0, 0, 0))
    return pl.pallas_call(
        functools.partial(_attn_swa_kernel, n=n, ctx=ctx),
        grid=(t // n,),
        in_specs=[pl.BlockSpec(memory_space=pltpu.SMEM), tile(SWA_Q_WIDTH), tile(SWA_KV_WIDTH),
                  tile(SWA_KV_WIDTH), ctile, ctile],
        out_specs=tile(SWA_Q_WIDTH),
        out_shape=jax.ShapeDtypeStruct((t, SWA_Q_WIDTH), BF16),
        scratch_shapes=[pltpu.VMEM((SWA_KV_HEADS, n, SWA_KV_WIDTH), BF16)] * 2
        + [pltpu.VMEM((SWA_KV_HEADS, LANES, ctx), BF16)] * 2,
        compiler_params=_params(),
        name="attn_swa",
    )(sink, q, k, v, ck, cv)


def _rope_tables(n):
    half = HEAD_DIM // 4
    freqs = jnp.power(ROPE_BASE, -jnp.arange(half, dtype=F32) / half)
    t = jnp.arange(n)
    cos, sin = [], []
    for pos in (t // GRID_W, t % GRID_W):
        ang = pos.astype(F32)[:, None] * freqs[None, :]
        cos += [jnp.cos(ang), jnp.cos(ang)]
        sin += [-jnp.sin(ang), jnp.sin(ang)]
    cos, sin = jnp.concatenate(cos, axis=-1), jnp.concatenate(sin, axis=-1)
    reps = LANES // HEAD_DIM
    return jnp.tile(cos, (1, reps)), jnp.tile(sin, (1, reps))


def _merge_kernel(x_ref, ya_ref, yb_ref, mod_ref, g_ref, wgate_ref, wba_ref, wbb_ref, wout_ref, o_ref):
    x = x_ref[...]
    mod = mod_ref[0]
    h = _modulate(x, g_ref[...], mod[3:4], mod[4:5]).astype(BF16)
    a = jax.nn.sigmoid(_dot(h, wgate_ref[:, :D_MODEL])) * _dot(ya_ref[...], wba_ref[...])
    b = jax.nn.sigmoid(_dot(h, wgate_ref[:, D_MODEL:])) * _dot(yb_ref[...], wbb_ref[...])
    o_ref[...] = x + mod[5:6] * _dot((a + b).astype(BF16), wout_ref[...])


def _merge(x, ya, yb, mod, g, wgate, wba, wbb, wout, *, tokens_per_row, first_row):
    t = x.shape[0]
    tile = lambda w: pl.BlockSpec((TOKEN_TILE, w), lambda i: (i, 0))
    return pl.pallas_call(
        _merge_kernel,
        grid=(t // TOKEN_TILE,),
        in_specs=[tile(D_MODEL), tile(NA_WIDTH), tile(SWA_Q_WIDTH), _mod_spec(tokens_per_row, first_row),
                  _resident((1, D_MODEL)), _resident(wgate.shape), _resident(wba.shape), _resident(wbb.shape),
                  _resident(wout.shape)],
        out_specs=tile(D_MODEL),
        out_shape=jax.ShapeDtypeStruct((t, D_MODEL), F32),
        compiler_params=_params(),
        name="merge",
    )(x, ya, yb, mod, g, wgate, wba, wbb, wout)


def kernel(x_prompt, x_sample, cache_na_k, cache_na_v, cache_swa_k, cache_swa_v, c, c_ctx, w_ada, b_ada,
           norm_ffn1, ffn1_w_gate, ffn1_w_up, ffn1_w_down, norm_mix, w_in, na_rel_bias, swa_sink,
           w_branch_na, w_branch_swa, w_out, norm_ffn2, ffn2_w_gate, ffn2_w_up, ffn2_w_down, norm_final):
    depth = w_ada.shape[0]
    assert depth == 1
    batch, seq, _ = x_prompt.shape
    dec_batch, dec_seq, _ = x_sample.shape
    past = cache_na_k.shape[2]
    layer = 0
    row = lambda v: v.reshape(1, D_MODEL)
    bf = lambda w: w.astype(BF16)

    cond = jnp.zeros((MOD_ROWS, D_MODEL), F32).at[0].set(c_ctx).at[1:1 + dec_batch].set(c)
    mod = _adaln(cond, w_ada[layer], b_ada[layer]).reshape(MOD_ROWS, N_MOD, D_MODEL)

    ffn1 = (row(norm_ffn1[layer]), bf(ffn1_w_gate[layer]), bf(ffn1_w_up[layer]), bf(ffn1_w_down[layer]),
            row(norm_final))
    ffn2 = (row(norm_ffn2[layer]), bf(ffn2_w_gate[layer]), bf(ffn2_w_up[layer]), bf(ffn2_w_down[layer]),
            row(norm_final))
    g_mix = row(norm_mix[layer])
    w_qkv = bf(w_in[layer, :, :QKV_COLS])
    w_kvt = bf(jnp.concatenate([w_in[layer, :, NA_WIDTH:3 * NA_WIDTH],
                                w_in[layer, :, 3 * NA_WIDTH + SWA_Q_WIDTH:QKV_COLS]], axis=1).T)
    w_gate = bf(w_in[layer, :, QKV_COLS:])
    merge_w = (w_gate, bf(w_branch_na[layer]), bf(w_branch_swa[layer]), bf(w_out[layer]))
    sink = swa_sink[layer]

    where = dict(tokens_per_row=batch * seq, first_row=0)
    x = x_prompt.reshape(batch * seq, D_MODEL)
    x = _ffn(x, mod, *ffn1, first=0, final=False, **where)
    qa, qb, *new_cache = _proj_ctx(x, mod, g_mix, w_qkv, w_kvt, seq=seq, **where)
    ya, yb = _attn_ctx(sink, qa, qb, *new_cache, seq=seq)
    x = _merge(x, ya, yb, mod, g_mix, *merge_w, **where)
    y_prompt = _ffn(x, mod, *ffn2, first=6, final=True, **where).reshape(batch, seq, D_MODEL)

    where = dict(tokens_per_row=dec_seq, first_row=1)
    x = x_sample.reshape(dec_batch * dec_seq, D_MODEL)
    x = _ffn(x, mod, *ffn1, first=0, final=False, **where)
    qa, kal, val, qb, kbl, vbl = _proj_lat(x, mod, g_mix, w_qkv, _rope_tables(dec_seq), **where)
    transposed = lambda cache: jnp.transpose(cache[:, layer], (0, 2, 3, 1))
    ya = _attn_na(qa, kal, val, transposed(cache_na_k), transposed(cache_na_v),
                  _na_bias_rows(na_rel_bias[layer]), n=dec_seq, ctx=past)
    yb = _attn_swa(sink, qb, kbl, vbl, transposed(cache_swa_k), transposed(cache_swa_v), n=dec_seq, ctx=past)
    x = _merge(x, ya, yb, mod, g_mix, *merge_w, **where)
    y_sample = _ffn(x, mod, *ffn2, first=6, final=True, **where).reshape(dec_batch, dec_seq, D_MODEL)

    new_cache = [jnp.transpose(t, (0, 3, 1, 2))[:, None] for t in new_cache]
    return (y_prompt, y_sample, *new_cache)
```

```python
import functools

import numpy as np
import jax
import jax.numpy as jnp
from jax import lax
from jax.experimental import pallas as pl
from jax.experimental.pallas import tpu as pltpu

F32 = jnp.float32
BF16 = jnp.bfloat16

D_MODEL = 1024
FFN_DIM = 2816
HEAD_DIM = 64
N_MOD = 9
GRID_W = 64
NA_HEADS = 8
NA_ROWS = 8
NA_COLS = 16
SWA_HEADS = 8
SWA_KV_HEADS = 2
SWA_WINDOW = 128
SWA_BLOCK = 128
ROPE_BASE = 10000.0
EPS = 1e-6
NEG_INF = -1e30
NA_WIDTH = NA_HEADS * HEAD_DIM
SWA_Q_WIDTH = SWA_HEADS * HEAD_DIM
SWA_KV_WIDTH = SWA_KV_HEADS * HEAD_DIM
QKV_COLS = 3 * NA_WIDTH + SWA_Q_WIDTH + 2 * SWA_KV_WIDTH
LANES = 128
MOD_ROWS = 16
VMEM_LIMIT = 56 * 1024 * 1024
TOKEN_TILE = 1024
FFN_TILE = 1024
FFN_CHUNK = 256
SWA_QTILE = 256
NA_TILE_ROWS = 4
LOG2E = 1.4426950408889634
Q_SCALE = HEAD_DIM ** -0.5 * LOG2E


def _dot(a, b):
    return jnp.dot(a, b, preferred_element_type=F32)


def _dot_nt(a, b):
    return lax.dot_general(a, b, (((1,), (1,)), ((), ())), preferred_element_type=F32)


def _silu(x):
    return x * jax.nn.sigmoid(x)


def _rms(x, g):
    return x * lax.rsqrt(jnp.mean(x * x, axis=-1, keepdims=True) + EPS) * g


def _modulate(x, g, shift, scale):
    return _rms(x, g) * (1.0 + scale) + shift


def _resident(shape):
    nd = len(shape)
    return pl.BlockSpec(shape, lambda *_: (0,) * nd, pipeline_mode=pl.Buffered(1))


def _params():
    return pltpu.CompilerParams(dimension_semantics=("arbitrary",), vmem_limit_bytes=VMEM_LIMIT)


def _adaln_kernel(c_ref, w_ref, b_ref, o_ref):
    s = _silu(c_ref[...]).astype(BF16)
    o_ref[...] = _dot(s, w_ref[...].astype(BF16)) + b_ref[...]


def _adaln(cond, w_ada, b_ada):
    n = w_ada.shape[1]
    blk = D_MODEL
    return pl.pallas_call(
        _adaln_kernel,
        grid=(n // blk,),
        in_specs=[pl.BlockSpec((MOD_ROWS, D_MODEL), lambda j: (0, 0)),
                  pl.BlockSpec((D_MODEL, blk), lambda j: (0, j)),
                  pl.BlockSpec((1, blk), lambda j: (0, j))],
        out_specs=pl.BlockSpec((MOD_ROWS, blk), lambda j: (0, j)),
        out_shape=jax.ShapeDtypeStruct((MOD_ROWS, n), F32),
        compiler_params=_params(),
        name="adaln",
    )(cond, w_ada, b_ada.reshape(1, n))


def _mod_spec(tokens_per_row, first_row, tile=TOKEN_TILE):
    tiles_per_row = tokens_per_row // tile
    return pl.BlockSpec((1, N_MOD, D_MODEL), lambda i: (first_row + i // tiles_per_row, 0, 0))


def _ffn_kernel(x_ref, mod_ref, g_ref, wg_ref, wu_ref, wd_ref, gf_ref, o_ref, a_scr, *, first, final):
    x = x_ref[...]
    mod = mod_ref[0]
    shift, scale, gate = mod[first:first + 1], mod[first + 1:first + 2], mod[first + 2:first + 3]
    h = _modulate(x, g_ref[...], shift, scale).astype(BF16)
    for c in range(FFN_DIM // FFN_CHUNK):
        sl = slice(c * FFN_CHUNK, (c + 1) * FFN_CHUNK)
        a_scr[:, sl] = (_silu(_dot(h, wg_ref[:, sl])) * _dot(h, wu_ref[:, sl])).astype(BF16)
    y = x + (0.5 * gate) * _dot(a_scr[...], wd_ref[...])
    if final:
        y = _rms(y, gf_ref[...])
    o_ref[...] = y


def _ffn(x, mod, g, wg, wu, wd, gf, *, tokens_per_row, first_row, first, final):
    t = x.shape[0]
    tile = pl.BlockSpec((FFN_TILE, D_MODEL), lambda i: (i, 0))
    return pl.pallas_call(
        functools.partial(_ffn_kernel, first=first, final=final),
        grid=(t // FFN_TILE,),
        in_specs=[tile, _mod_spec(tokens_per_row, first_row, FFN_TILE), _resident((1, D_MODEL)),
                  _resident(wg.shape), _resident(wu.shape), _resident(wd.shape), _resident((1, D_MODEL))],
        out_specs=tile,
        out_shape=jax.ShapeDtypeStruct((t, D_MODEL), F32),
        scratch_shapes=[pltpu.VMEM((FFN_TILE, FFN_DIM), BF16)],
        compiler_params=_params(),
        name="ffn",
    )(x, mod, g, wg, wu, wd, gf)


def _swap16(x):
    lane = lax.broadcasted_iota(jnp.int32, x.shape, 1)
    return jnp.where(lane % 32 < 16, pltpu.roll(x, LANES - 16, 1), pltpu.roll(x, 16, 1))


def _proj_ctx_kernel(x_ref, mod_ref, g_ref, w_ref, wkvt_ref, qa_ref, qb_ref, ka_ref, va_ref, kb_ref, vb_ref):
    mod = mod_ref[0]
    h = _modulate(x_ref[...], g_ref[...], mod[3:4], mod[4:5]).astype(BF16)
    qa_ref[...] = (_dot(h, w_ref[:, :NA_WIDTH]) * Q_SCALE).astype(qa_ref.dtype)
    qb_ref[...] = (_dot(h, w_ref[:, 3 * NA_WIDTH:3 * NA_WIDTH + SWA_Q_WIDTH]) * Q_SCALE).astype(qb_ref.dtype)
    kvt = _dot_nt(wkvt_ref[...], h)
    requests, _, _, seq = ka_ref.shape
    o = 0
    for ref in (ka_ref, va_ref, kb_ref, vb_ref):
        width = ref.shape[1] * HEAD_DIM
        for r in range(requests):
            ref[r] = kvt[o:o + width, r * seq:(r + 1) * seq].reshape(ref.shape[1:])
        o += width


def _rope(y, cos, sin):
    blocks = [y[:, j * LANES:(j + 1) * LANES] for j in range(y.shape[1] // LANES)]
    return [b * cos + _swap16(b) * sin for b in blocks]


def _proj_lat_kernel(x_ref, mod_ref, g_ref, w_ref, cos_ref, sin_ref,
                     qa_ref, ka_ref, va_ref, qb_ref, kb_ref, vb_ref):
    mod = mod_ref[0]
    h = _modulate(x_ref[...], g_ref[...], mod[3:4], mod[4:5]).astype(BF16)
    cos, sin = cos_ref[...], sin_ref[...]
    o = 3 * NA_WIDTH
    for j, b in enumerate(_rope(_dot(h, w_ref[:, o:o + SWA_Q_WIDTH]), cos, sin)):
        qb_ref[:, j * LANES:(j + 1) * LANES] = (b * Q_SCALE).astype(qb_ref.dtype)
    o += SWA_Q_WIDTH
    y = _dot(h, w_ref[:, o:o + 2 * SWA_KV_WIDTH])
    kb_ref[...] = _rope(y[:, :SWA_KV_WIDTH], cos, sin)[0].astype(kb_ref.dtype)
    vb_ref[...] = y[:, SWA_KV_WIDTH:].astype(vb_ref.dtype)
    o = 0
    for ref, scale in ((qa_ref, True), (ka_ref, False), (va_ref, False)):
        y = _dot(h, w_ref[:, o:o + NA_WIDTH])
        ref[...] = (y * Q_SCALE if scale else y).astype(ref.dtype)
        o += NA_WIDTH


def _proj_ctx(x, mod, g, w_qkv, w_kvt, *, seq, tokens_per_row, first_row):
    t = x.shape[0]
    tile = lambda w: pl.BlockSpec((TOKEN_TILE, w), lambda i: (i, 0))
    cache = lambda heads: pl.BlockSpec((TOKEN_TILE // seq, heads, HEAD_DIM, seq), lambda i: (i, 0, 0, 0))
    cache_shape = lambda heads: jax.ShapeDtypeStruct((t // seq, heads, HEAD_DIM, seq), F32)
    kv_heads = (NA_HEADS, NA_HEADS, SWA_KV_HEADS, SWA_KV_HEADS)
    return pl.pallas_call(
        _proj_ctx_kernel,
        grid=(t // TOKEN_TILE,),
        in_specs=[tile(D_MODEL), _mod_spec(tokens_per_row, first_row), _resident((1, D_MODEL)),
                  _resident(w_qkv.shape), _resident(w_kvt.shape)],
        out_specs=[tile(NA_WIDTH), tile(SWA_Q_WIDTH)] + [cache(nh) for nh in kv_heads],
        out_shape=[jax.ShapeDtypeStruct((t, NA_WIDTH), BF16), jax.ShapeDtypeStruct((t, SWA_Q_WIDTH), BF16)]
        + [cache_shape(nh) for nh in kv_heads],
        compiler_params=_params(),
        name="proj_ctx",
    )(x, mod, g, w_qkv, w_kvt)


def _proj_lat(x, mod, g, w_qkv, rope, *, tokens_per_row, first_row):
    t = x.shape[0]
    tile = lambda w: pl.BlockSpec((TOKEN_TILE, w), lambda i: (i, 0))
    widths = (NA_WIDTH, NA_WIDTH, NA_WIDTH, SWA_Q_WIDTH, SWA_KV_WIDTH, SWA_KV_WIDTH)
    tiles_per_row = tokens_per_row // TOKEN_TILE
    return pl.pallas_call(
        _proj_lat_kernel,
        grid=(t // TOKEN_TILE,),
        in_specs=[tile(D_MODEL), _mod_spec(tokens_per_row, first_row), _resident((1, D_MODEL)),
                  _resident(w_qkv.shape)]
        + [pl.BlockSpec((TOKEN_TILE, LANES), lambda i: (i % tiles_per_row, 0))] * 2,
        out_specs=[tile(w) for w in widths],
        out_shape=[jax.ShapeDtypeStruct((t, w), BF16) for w in widths],
        compiler_params=_params(),
        name="proj_lat",
    )(x, mod, g, w_qkv, *rope)


def _lane_halves(shape):
    lane = lax.broadcasted_iota(jnp.int32, shape, 1) % LANES
    return lane < HEAD_DIM, lane >= HEAD_DIM


def _softmax_cols(cols, sink=None):
    m = jnp.max(functools.reduce(jnp.maximum, cols), axis=-1, keepdims=True)
    if sink is not None:
        m = jnp.maximum(m, sink)
    es = [jnp.exp2(c - m) for c in cols]
    l = jnp.sum(functools.reduce(jnp.add, es), axis=-1, keepdims=True)
    if sink is not None:
        l = l + jnp.exp2(sink - m)
    return es, l


def _cols(x):
    return [x[:, j * LANES:(j + 1) * LANES] for j in range(x.shape[1] // LANES)]


def _attn_ctx_kernel(sink_ref, qa_ref, qb_ref, ka_ref, va_ref, kb_ref, vb_ref, ya_ref, yb_ref):
    seq = qa_ref.shape[0]
    lo, hi = _lane_halves((seq, LANES))
    top = lax.broadcasted_iota(jnp.int32, (2 * seq, 1), 0) < seq
    zero = jnp.zeros((), BF16)

    def heads_of_block(q, kt, vt, sink):
        q = jnp.concatenate([jnp.where(lo, q, zero), jnp.where(hi, q, zero)], axis=0)
        es, l = _softmax_cols(_cols(_dot(q, kt)), sink)
        o = _dot_nt(jnp.concatenate(es, axis=1).astype(BF16), vt) * (1.0 / l)
        return jnp.where(lo, o[:seq], o[seq:])

    for p in range(NA_HEADS // 2):
        sl = slice(p * LANES, (p + 1) * LANES)
        kt = ka_ref[2 * p:2 * p + 2].reshape(LANES, seq).astype(BF16)
        vt = va_ref[2 * p:2 * p + 2].reshape(LANES, seq).astype(BF16)
        ya_ref[:, sl] = heads_of_block(qa_ref[:, sl], kt, vt, None).astype(ya_ref.dtype)
    for p in range(SWA_HEADS // 2):
        sl = slice(p * LANES, (p + 1) * LANES)
        kv = (2 * p) // (SWA_HEADS // SWA_KV_HEADS)
        kt, vt = kb_ref[kv].astype(BF16), vb_ref[kv].astype(BF16)
        kt, vt = jnp.concatenate([kt, kt], axis=0), jnp.concatenate([vt, vt], axis=0)
        sink = jnp.where(top, sink_ref[2 * p], sink_ref[2 * p + 1]) * LOG2E
        yb_ref[:, sl] = heads_of_block(qb_ref[:, sl], kt, vt, sink).astype(yb_ref.dtype)


def _attn_ctx(sink, qa, qb, ka, va, kb, vb, *, seq):
    t = qa.shape[0]
    tile = lambda w: pl.BlockSpec((seq, w), lambda b: (b, 0))
    cache = lambda heads: pl.BlockSpec((None, heads, HEAD_DIM, seq), lambda b: (b, 0, 0, 0))
    return pl.pallas_call(
        _attn_ctx_kernel,
        grid=(t // seq,),
        in_specs=[pl.BlockSpec(memory_space=pltpu.SMEM), tile(NA_WIDTH), tile(SWA_Q_WIDTH),
                  cache(NA_HEADS), cache(NA_HEADS), cache(SWA_KV_HEADS), cache(SWA_KV_HEADS)],
        out_specs=[tile(NA_WIDTH), tile(SWA_Q_WIDTH)],
        out_shape=[jax.ShapeDtypeStruct((t, NA_WIDTH), BF16), jax.ShapeDtypeStruct((t, SWA_Q_WIDTH), BF16)],
        compiler_params=_params(),
        name="attn_ctx",
    )(sink, qa, qb, ka, va, kb, vb)


def _attn_na_kernel(q_ref, k_ref, v_ref, ck_ref, cv_ref, rows_ref, y_ref, ck_scr, cv_scr, bias_ref, *, n, ctx):
    rows = n // GRID_W
    tile_q = NA_TILE_ROWS * GRID_W
    zero = jnp.zeros((), BF16)
    @pl.when(pl.program_id(0) == 0)
    def _():
        _fill_na_bias(rows_ref, bias_ref)

    ck_scr[...] = ck_ref[...].reshape(NA_WIDTH, ctx).astype(BF16)
    cv_scr[...] = cv_ref[...].reshape(NA_WIDTH, ctx).astype(BF16)
    lo, hi = _lane_halves((tile_q, LANES))
    left_half = lax.broadcasted_iota(jnp.int32, (GRID_W, LANES), 1) < GRID_W
    empty = jnp.zeros((GRID_W, LANES), BF16)

    def tile(q0, k0, key_rows, lead, first):
        pairs = key_rows // 2
        for p in range(NA_HEADS // 2):
            sl = slice(p * LANES, (p + 1) * LANES)
            q = q_ref[pl.ds(q0, tile_q), sl]
            q = jnp.concatenate([jnp.where(lo, q, zero), jnp.where(hi, q, zero)], axis=0)
            s_nb = _dot_nt(q, k_ref[pl.ds(k0, key_rows * GRID_W), sl])
            s_ctx = _dot(q, ck_scr[sl, :])
            e_nb, e_ctx, ls = [], [], []
            for idx in range(2):
                for a in range(NA_TILE_ROWS):
                    qa = slice(idx * tile_q + a * GRID_W, idx * tile_q + (a + 1) * GRID_W)
                    inside = lambda i: first[a] <= i < first[a] + NA_ROWS
                    cols, where = [], []
                    for m in range(pairs):
                        if not (inside(2 * m) or inside(2 * m + 1)):
                            continue
                        blk = s_nb[qa, m * LANES:(m + 1) * LANES] + bias_ref[2 * p + idx, 2 * m - a - lead + NA_ROWS]
                        if not inside(2 * m + 1):
                            blk = jnp.where(left_half, blk, NEG_INF)
                        elif not inside(2 * m):
                            blk = jnp.where(left_half, NEG_INF, blk)
                        cols.append(blk)
                        where.append(m)
                    cols += [s_ctx[qa, j * LANES:(j + 1) * LANES] for j in range(ctx // LANES)]
                    es, l = _softmax_cols(cols)
                    es = [e.astype(BF16) for e in es]
                    e_nb.append(jnp.concatenate(
                        [es[where.index(m)] if m in where else empty for m in range(pairs)], axis=1))
                    e_ctx.append(jnp.concatenate(es[len(where):], axis=1))
                    ls.append(l)
            o = (_dot(jnp.concatenate(e_nb, axis=0), v_ref[pl.ds(k0, key_rows * GRID_W), sl])
                 + _dot_nt(jnp.concatenate(e_ctx, axis=0), cv_scr[sl, :]))
            o = o * (1.0 / jnp.concatenate(ls, axis=0))
            y_ref[pl.ds(q0, tile_q), sl] = jnp.where(lo, o[:tile_q], o[tile_q:]).astype(y_ref.dtype)

    half = NA_ROWS // 2
    tile(0, 0, NA_ROWS, 0, (0,) * NA_TILE_ROWS)

    for t in range(1, rows // NA_TILE_ROWS - 1):
        tile(t * tile_q, t * tile_q - half * GRID_W, NA_ROWS + NA_TILE_ROWS, half, tuple(range(NA_TILE_ROWS)))
    tile(n - tile_q, n - NA_ROWS * GRID_W, NA_ROWS, half, (0,) * NA_TILE_ROWS)


def _attn_na(q, k, v, ck, cv, bias_rows, *, n, ctx):
    t = q.shape[0]
    tile = pl.BlockSpec((n, NA_WIDTH), lambda b: (b, 0))
    ctile = pl.BlockSpec((None, NA_HEADS, HEAD_DIM, ctx), lambda b: (b, 0, 0, 0))
    return pl.pallas_call(
        functools.partial(_attn_na_kernel, n=n, ctx=ctx),
        grid=(t // n,),
        in_specs=[tile, tile, tile, ctile, ctile, _resident(bias_rows.shape)],
        out_specs=tile,
        out_shape=jax.ShapeDtypeStruct((t, NA_WIDTH), BF16),
        scratch_shapes=[pltpu.VMEM((NA_WIDTH, ctx), BF16)] * 2
        + [pltpu.VMEM((NA_HEADS, 2 * NA_ROWS, GRID_W, LANES), F32)],
        compiler_params=_params(),
        name="attn_na",
    )(q, k, v, ck, cv, bias_rows)


def _na_bias_rows(rel_bias):
    side = GRID_W - NA_COLS
    z = jnp.pad(rel_bias * LOG2E, ((0, 0), (0, 0), (side, side + 1)))
    return jnp.pad(z, ((0, 0), (1, 1), (0, 0)), constant_values=NEG_INF)


def _fill_na_bias(rows_ref, bias_scr):
    q = lax.broadcasted_iota(jnp.int32, (GRID_W, LANES), 0)
    lane = lax.broadcasted_iota(jnp.int32, (GRID_W, LANES), 1)
    kc = lane % GRID_W
    col_start = jnp.clip(q - NA_COLS // 2, 0, GRID_W - NA_COLS)
    in_window = (kc >= col_start) & (kc < col_start + NA_COLS)
    for h in range(NA_HEADS):
        blocks = []
        for j in range(2 * NA_ROWS + 1):
            row = jnp.broadcast_to(rows_ref[h, j:j + 1, :], (GRID_W, LANES))
            rolled = pltpu.roll(row, LANES - GRID_W + 1, 1, stride=1, stride_axis=0)
            blocks.append(jnp.where(in_window, rolled, NEG_INF))
        for j in range(2 * NA_ROWS):
            bias_scr[h, j] = jnp.where(lane < GRID_W, blocks[j], pltpu.roll(blocks[j + 1], GRID_W, 1))


def _attn_swa_kernel(sink_ref, q_ref, k_ref, v_ref, ck_ref, cv_ref, y_ref, k_scr, v_scr, ck_scr, cv_scr, *, n, ctx):
    group = SWA_HEADS // SWA_KV_HEADS
    band = SWA_QTILE + 2 * SWA_BLOCK
    for src, csrc, dst, cdst in ((k_ref, ck_ref, k_scr, ck_scr), (v_ref, cv_ref, v_scr, cv_scr)):
        x = src[...].astype(F32)
        xr = pltpu.roll(x, HEAD_DIM, 1)
        first_half, second_half = _lane_halves(x.shape)
        dst[0] = jnp.where(first_half, x, xr).astype(BF16)
        dst[1] = jnp.where(second_half, x, xr).astype(BF16)
        for kv in range(SWA_KV_HEADS):
            c = csrc[kv].astype(BF16)
            cdst[kv] = jnp.concatenate([c, c], axis=0)

    qi = lax.broadcasted_iota(jnp.int32, (SWA_QTILE, band), 0)
    kj = lax.broadcasted_iota(jnp.int32, (SWA_QTILE, band), 1)
    lo, hi = _lane_halves((SWA_QTILE, LANES))
    top = lax.broadcasted_iota(jnp.int32, (2 * SWA_QTILE, 1), 0) < SWA_QTILE
    zero = jnp.zeros((), BF16)

    def block_body(b, carry):
        q0 = pl.multiple_of(b * SWA_QTILE, SWA_QTILE)
        k0 = pl.multiple_of(jnp.clip(q0 - SWA_BLOCK, 0, n - band), SWA_BLOCK)
        mask = jnp.where(jnp.abs(kj - qi + (k0 - q0)) <= SWA_WINDOW, 0.0, NEG_INF)
        mask2 = jnp.concatenate([mask, mask], axis=0)
        for p in range(SWA_HEADS // 2):
            kv = 2 * p // group
            sl = slice(p * LANES, (p + 1) * LANES)
            q = q_ref[pl.ds(q0, SWA_QTILE), sl]
            q = jnp.concatenate([jnp.where(lo, q, zero), jnp.where(hi, q, zero)], axis=0)
            s_band = _dot_nt(q, k_scr[kv, pl.ds(k0, band), :])
            s_ctx = _dot(q, ck_scr[kv])
            sink = jnp.where(top, sink_ref[2 * p], sink_ref[2 * p + 1]) * LOG2E
            es, l = _softmax_cols(_cols(s_band + mask2) + _cols(s_ctx), sink)
            e_band = jnp.concatenate(es[:band // LANES], axis=1).astype(BF16)
            e_ctx = jnp.concatenate(es[band // LANES:], axis=1).astype(BF16)
            o = (_dot(e_band, v_scr[kv, pl.ds(k0, band), :]) + _dot_nt(e_ctx, cv_scr[kv])) * (1.0 / l)
            y_ref[pl.ds(q0, SWA_QTILE), sl] = jnp.where(lo, o[:SWA_QTILE], o[SWA_QTILE:]).astype(y_ref.dtype)
        return carry

    lax.fori_loop(0, n // SWA_QTILE, block_body, 0, unroll=True)


def _attn_swa(sink, q, k, v, ck, cv, *, n, ctx):
    t = q.shape[0]
    tile = lambda w: pl.BlockSpec((n, w), lambda b: (b, 0))
    ctile = pl.BlockSpec((None, SWA_KV_HEADS, HEAD_DIM, ctx), lambda b: (b, 0, 0, 0))
    return pl.pallas_call(
        functools.partial(_attn_swa_kernel, n=n, ctx=ctx),
        grid=(t // n,),
        in_specs=[pl.BlockSpec(memory_space=pltpu.SMEM), tile(SWA_Q_WIDTH), tile(SWA_KV_WIDTH),
                  tile(SWA_KV_WIDTH), ctile, ctile],
        out_specs=tile(SWA_Q_WIDTH),
        out_shape=jax.ShapeDtypeStruct((t, SWA_Q_WIDTH), BF16),
        scratch_shapes=[pltpu.VMEM((SWA_KV_HEADS, n, SWA_KV_WIDTH), BF16)] * 2
        + [pltpu.VMEM((SWA_KV_HEADS, LANES, ctx), BF16)] * 2,
        compiler_params=_params(),
        name="attn_swa",
    )(sink, q, k, v, ck, cv)


def _rope_tables(n):
    half = HEAD_DIM // 4
    freqs = jnp.power(ROPE_BASE, -jnp.arange(half, dtype=F32) / half)
    t = jnp.arange(n)
    cos, sin = [], []
    for pos in (t // GRID_W, t % GRID_W):
        ang = pos.astype(F32)[:, None] * freqs[None, :]
        cos += [jnp.cos(ang), jnp.cos(ang)]
        sin += [-jnp.sin(ang), jnp.sin(ang)]
    cos, sin = jnp.concatenate(cos, axis=-1), jnp.concatenate(sin, axis=-1)
    reps = LANES // HEAD_DIM
    return jnp.tile(cos, (1, reps)), jnp.tile(sin, (1, reps))


def _merge_kernel(x_ref, ya_ref, yb_ref, mod_ref, g_ref, win_ref, wba_ref, wbb_ref, wout_ref, o_ref):
    x = x_ref[...]
    mod = mod_ref[0]
    h = _modulate(x, g_ref[...], mod[3:4], mod[4:5]).astype(BF16)
    a = jax.nn.sigmoid(_dot(h, win_ref[:, QKV_COLS:QKV_COLS + D_MODEL])) * _dot(ya_ref[...], wba_ref[...])
    b = jax.nn.sigmoid(_dot(h, win_ref[:, QKV_COLS + D_MODEL:])) * _dot(yb_ref[...], wbb_ref[...])
    o_ref[...] = x + mod[5:6] * _dot((a + b).astype(BF16), wout_ref[...])


def _merge(x, ya, yb, mod, g, wgate, wba, wbb, wout, *, tokens_per_row, first_row):
    t = x.shape[0]
    tile = lambda w: pl.BlockSpec((TOKEN_TILE, w), lambda i: (i, 0))
    return pl.pallas_call(
        _merge_kernel,
        grid=(t // TOKEN_TILE,),
        in_specs=[tile(D_MODEL), tile(NA_WIDTH), tile(SWA_Q_WIDTH), _mod_spec(tokens_per_row, first_row),
                  _resident((1, D_MODEL)), _resident(wgate.shape), _resident(wba.shape), _resident(wbb.shape),
                  _resident(wout.shape)],
        out_specs=tile(D_MODEL),
        out_shape=jax.ShapeDtypeStruct((t, D_MODEL), F32),
        compiler_params=_params(),
        name="merge",
    )(x, ya, yb, mod, g, wgate, wba, wbb, wout)


def kernel(x_prompt, x_sample, cache_na_k, cache_na_v, cache_swa_k, cache_swa_v, c, c_ctx, w_ada, b_ada,
           norm_ffn1, ffn1_w_gate, ffn1_w_up, ffn1_w_down, norm_mix, w_in, na_rel_bias, swa_sink,
           w_branch_na, w_branch_swa, w_out, norm_ffn2, ffn2_w_gate, ffn2_w_up, ffn2_w_down, norm_final):
    depth = w_ada.shape[0]
    assert depth == 1
    batch, seq, _ = x_prompt.shape
    dec_batch, dec_seq, _ = x_sample.shape
    past = cache_na_k.shape[2]
    layer = 0
    row = lambda v: v.reshape(1, D_MODEL)
    bf = lambda w: w.astype(BF16)

    cond = jnp.zeros((MOD_ROWS, D_MODEL), F32).at[0].set(c_ctx).at[1:1 + dec_batch].set(c)
    mod = _adaln(cond, w_ada[layer], b_ada[layer]).reshape(MOD_ROWS, N_MOD, D_MODEL)

    ffn1 = (row(norm_ffn1[layer]), bf(ffn1_w_gate[layer]), bf(ffn1_w_up[layer]), bf(ffn1_w_down[layer]),
            row(norm_final))
    ffn2 = (row(norm_ffn2[layer]), bf(ffn2_w_gate[layer]), bf(ffn2_w_up[layer]), bf(ffn2_w_down[layer]),
            row(norm_final))
    g_mix = row(norm_mix[layer])
    w_qkv = bf(w_in[layer])
    w_kvt = jnp.concatenate([w_qkv[:, NA_WIDTH:3 * NA_WIDTH], w_qkv[:, 3 * NA_WIDTH + SWA_Q_WIDTH:QKV_COLS]],
                            axis=1).T
    merge_w = (w_qkv, bf(w_branch_na[layer]), bf(w_branch_swa[layer]), bf(w_out[layer]))
    sink = swa_sink[layer]

    where = dict(tokens_per_row=batch * seq, first_row=0)
    x = x_prompt.reshape(batch * seq, D_MODEL)
    x = _ffn(x, mod, *ffn1, first=0, final=False, **where)
    qa, qb, *new_cache = _proj_ctx(x, mod, g_mix, w_qkv, w_kvt, seq=seq, **where)
    ya, yb = _attn_ctx(sink, qa, qb, *new_cache, seq=seq)
    x = _merge(x, ya, yb, mod, g_mix, *merge_w, **where)
    y_prompt = _ffn(x, mod, *ffn2, first=6, final=True, **where).reshape(batch, seq, D_MODEL)

    where = dict(tokens_per_row=dec_seq, first_row=1)
    x = x_sample.reshape(dec_batch * dec_seq, D_MODEL)
    x = _ffn(x, mod, *ffn1, first=0, final=False, **where)
    qa, kal, val, qb, kbl, vbl = _proj_lat(x, mod, g_mix, w_qkv, _rope_tables(dec_seq), **where)
    transposed = lambda cache: jnp.transpose(cache[:, layer], (0, 2, 3, 1))
    ya = _attn_na(qa, kal, val, transposed(cache_na_k), transposed(cache_na_v),
                  _na_bias_rows(na_rel_bias[layer]), n=dec_seq, ctx=past)
    yb = _attn_swa(sink, qb, kbl, vbl, transposed(cache_swa_k), transposed(cache_swa_v), n=dec_seq, ctx=past)
    x = _merge(x, ya, yb, mod, g_mix, *merge_w, **where)
    y_sample = _ffn(x, mod, *ffn2, first=6, final=True, **where).reshape(dec_batch, dec_seq, D_MODEL)

    new_cache = [jnp.transpose(t, (0, 3, 1, 2))[:, None] for t in new_cache]
    return (y_prompt, y_sample, *new_cache)
```

```python
import functools

import numpy as np
import jax
import jax.numpy as jnp
from jax import lax
from jax.experimental import pallas as pl
from jax.experimental.pallas import tpu as pltpu

F32 = jnp.float32
BF16 = jnp.bfloat16

D_MODEL = 1024
FFN_DIM = 2816
HEAD_DIM = 64
N_MOD = 9
GRID_W = 64
NA_HEADS = 8
NA_ROWS = 8
NA_COLS = 16
SWA_HEADS = 8
SWA_KV_HEADS = 2
SWA_WINDOW = 128
SWA_BLOCK = 128
ROPE_BASE = 10000.0
EPS = 1e-6
NEG_INF = -1e30
NA_WIDTH = NA_HEADS * HEAD_DIM
SWA_Q_WIDTH = SWA_HEADS * HEAD_DIM
SWA_KV_WIDTH = SWA_KV_HEADS * HEAD_DIM
QKV_COLS = 3 * NA_WIDTH + SWA_Q_WIDTH + 2 * SWA_KV_WIDTH
LANES = 128
MOD_ROWS = 16
VMEM_LIMIT = 56 * 1024 * 1024
TOKEN_TILE = 1024
FFN_TILE = 1024
FFN_CHUNK = 256
SWA_QTILE = 256
NA_TILE_ROWS = 4
LOG2E = 1.4426950408889634
Q_SCALE = HEAD_DIM ** -0.5 * LOG2E


def _dot(a, b):
    return jnp.dot(a, b, preferred_element_type=F32)


def _dot_nt(a, b):
    return lax.dot_general(a, b, (((1,), (1,)), ((), ())), preferred_element_type=F32)


def _silu(x):
    return x * jax.nn.sigmoid(x)


def _rms(x, g):
    return x * lax.rsqrt(jnp.mean(x * x, axis=-1, keepdims=True) + EPS) * g


def _modulate(x, g, shift, scale):
    return _rms(x, g) * (1.0 + scale) + shift


def _resident(shape):
    nd = len(shape)
    return pl.BlockSpec(shape, lambda *_: (0,) * nd, pipeline_mode=pl.Buffered(1))


def _params():
    return pltpu.CompilerParams(dimension_semantics=("arbitrary",), vmem_limit_bytes=VMEM_LIMIT)


def _adaln_kernel(c_ref, w_ref, b_ref, o_ref):
    s = _silu(c_ref[...]).astype(BF16)
    o_ref[...] = _dot(s, w_ref[...].astype(BF16)) + b_ref[...]


def _adaln(cond, w_ada, b_ada):
    n = w_ada.shape[1]
    blk = D_MODEL
    return pl.pallas_call(
        _adaln_kernel,
        grid=(n // blk,),
        in_specs=[pl.BlockSpec((MOD_ROWS, D_MODEL), lambda j: (0, 0)),
                  pl.BlockSpec((D_MODEL, blk), lambda j: (0, j)),
                  pl.BlockSpec((1, blk), lambda j: (0, j))],
        out_specs=pl.BlockSpec((MOD_ROWS, blk), lambda j: (0, j)),
        out_shape=jax.ShapeDtypeStruct((MOD_ROWS, n), F32),
        compiler_params=_params(),
        name="adaln",
    )(cond, w_ada, b_ada.reshape(1, n))


def _mod_spec(tokens_per_row, first_row, tile=TOKEN_TILE):
    tiles_per_row = tokens_per_row // tile
    return pl.BlockSpec((1, N_MOD, D_MODEL), lambda i: (first_row + i // tiles_per_row, 0, 0))


def _ffn_kernel(x_ref, mod_ref, g_ref, wg_ref, wu_ref, wd_ref, gf_ref, *rest, first, final):
    n_cast = (len(rest) - 2) // 2
    o_ref, a_scr = rest[n_cast], rest[-1]
    for src, dst in zip(rest[:n_cast], rest[n_cast + 1:-1]):
        dst[...] = src[...].astype(BF16)
    x = x_ref[...]
    mod = mod_ref[0]
    shift, scale, gate = mod[first:first + 1], mod[first + 1:first + 2], mod[first + 2:first + 3]
    h = _modulate(x, g_ref[...], shift, scale).astype(BF16)
    for c in range(FFN_DIM // FFN_CHUNK):
        sl = slice(c * FFN_CHUNK, (c + 1) * FFN_CHUNK)
        a_scr[:, sl] = (_silu(_dot(h, wg_ref[:, sl])) * _dot(h, wu_ref[:, sl])).astype(BF16)
    y = x + (0.5 * gate) * _dot(a_scr[...], wd_ref[...])
    if final:
        y = _rms(y, gf_ref[...])
    o_ref[...] = y


def _ffn(x, mod, g, wg, wu, wd, gf, *, tokens_per_row, first_row, first, final, cast=()):
    t = x.shape[0]
    steps = t // FFN_TILE
    tile = pl.BlockSpec((FFN_TILE, D_MODEL), lambda i: (i, 0))
    chunk = lambda w: pl.BlockSpec((w.shape[0] // steps, w.shape[1]), lambda i: (i, 0))
    assert all(w.shape[0] % (steps * 16) == 0 for w in cast)
    out = pl.pallas_call(
        functools.partial(_ffn_kernel, first=first, final=final),
        grid=(steps,),
        in_specs=[tile, _mod_spec(tokens_per_row, first_row, FFN_TILE), _resident((1, D_MODEL)),
                  _resident(wg.shape), _resident(wu.shape), _resident(wd.shape), _resident((1, D_MODEL))]
        + [chunk(w) for w in cast],
        out_specs=[tile] + [chunk(w) for w in cast],
        out_shape=[jax.ShapeDtypeStruct((t, D_MODEL), F32)] + [jax.ShapeDtypeStruct(w.shape, BF16) for w in cast],
        scratch_shapes=[pltpu.VMEM((FFN_TILE, FFN_DIM), BF16)],
        compiler_params=_params(),
        name="ffn",
    )(x, mod, g, wg, wu, wd, gf, *cast)
    return out if cast else out[0]


def _swap16(x):
    lane = lax.broadcasted_iota(jnp.int32, x.shape, 1)
    return jnp.where(lane % 32 < 16, pltpu.roll(x, LANES - 16, 1), pltpu.roll(x, 16, 1))


def _proj_ctx_kernel(x_ref, mod_ref, g_ref, w_ref, wkvt_ref, qa_ref, qb_ref, ka_ref, va_ref, kb_ref, vb_ref):
    mod = mod_ref[0]
    h = _modulate(x_ref[...], g_ref[...], mod[3:4], mod[4:5]).astype(BF16)
    qa_ref[...] = (_dot(h, w_ref[:, :NA_WIDTH]) * Q_SCALE).astype(qa_ref.dtype)
    qb_ref[...] = (_dot(h, w_ref[:, 3 * NA_WIDTH:3 * NA_WIDTH + SWA_Q_WIDTH]) * Q_SCALE).astype(qb_ref.dtype)
    kvt = _dot_nt(wkvt_ref[...], h)
    requests, _, _, seq = ka_ref.shape
    o = 0
    for ref in (ka_ref, va_ref, kb_ref, vb_ref):
        width = ref.shape[1] * HEAD_DIM
        for r in range(requests):
            ref[r] = kvt[o:o + width, r * seq:(r + 1) * seq].reshape(ref.shape[1:])
        o += width


def _rope(y, cos, sin):
    blocks = [y[:, j * LANES:(j + 1) * LANES] for j in range(y.shape[1] // LANES)]
    return [b * cos + _swap16(b) * sin for b in blocks]


def _proj_lat_kernel(x_ref, mod_ref, g_ref, w_ref, cos_ref, sin_ref,
                     qa_ref, ka_ref, va_ref, qb_ref, kb_ref, vb_ref):
    mod = mod_ref[0]
    h = _modulate(x_ref[...], g_ref[...], mod[3:4], mod[4:5]).astype(BF16)
    cos, sin = cos_ref[...], sin_ref[...]
    o = 3 * NA_WIDTH
    for j, b in enumerate(_rope(_dot(h, w_ref[:, o:o + SWA_Q_WIDTH]), cos, sin)):
        qb_ref[:, j * LANES:(j + 1) * LANES] = (b * Q_SCALE).astype(qb_ref.dtype)
    o += SWA_Q_WIDTH
    y = _dot(h, w_ref[:, o:o + 2 * SWA_KV_WIDTH])
    kb_ref[...] = _rope(y[:, :SWA_KV_WIDTH], cos, sin)[0].astype(kb_ref.dtype)
    vb_ref[...] = y[:, SWA_KV_WIDTH:].astype(vb_ref.dtype)
    o = 0
    for ref, scale in ((qa_ref, True), (ka_ref, False), (va_ref, False)):
        y = _dot(h, w_ref[:, o:o + NA_WIDTH])
        ref[...] = (y * Q_SCALE if scale else y).astype(ref.dtype)
        o += NA_WIDTH


def _proj_ctx(x, mod, g, w_qkv, w_kvt, *, seq, tokens_per_row, first_row):
    t = x.shape[0]
    tile = lambda w: pl.BlockSpec((TOKEN_TILE, w), lambda i: (i, 0))
    cache = lambda heads: pl.BlockSpec((TOKEN_TILE // seq, heads, HEAD_DIM, seq), lambda i: (i, 0, 0, 0))
    cache_shape = lambda heads: jax.ShapeDtypeStruct((t // seq, heads, HEAD_DIM, seq), F32)
    kv_heads = (NA_HEADS, NA_HEADS, SWA_KV_HEADS, SWA_KV_HEADS)
    return pl.pallas_call(
        _proj_ctx_kernel,
        grid=(t // TOKEN_TILE,),
        in_specs=[tile(D_MODEL), _mod_spec(tokens_per_row, first_row), _resident((1, D_MODEL)),
                  _resident(w_qkv.shape), _resident(w_kvt.shape)],
        out_specs=[tile(NA_WIDTH), tile(SWA_Q_WIDTH)] + [cache(nh) for nh in kv_heads],
        out_shape=[jax.ShapeDtypeStruct((t, NA_WIDTH), BF16), jax.ShapeDtypeStruct((t, SWA_Q_WIDTH), BF16)]
        + [cache_shape(nh) for nh in kv_heads],
        compiler_params=_params(),
        name="proj_ctx",
    )(x, mod, g, w_qkv, w_kvt)


def _proj_lat(x, mod, g, w_qkv, rope, *, tokens_per_row, first_row):
    t = x.shape[0]
    tile = lambda w: pl.BlockSpec((TOKEN_TILE, w), lambda i: (i, 0))
    widths = (NA_WIDTH, NA_WIDTH, NA_WIDTH, SWA_Q_WIDTH, SWA_KV_WIDTH, SWA_KV_WIDTH)
    tiles_per_row = tokens_per_row // TOKEN_TILE
    return pl.pallas_call(
        _proj_lat_kernel,
        grid=(t // TOKEN_TILE,),
        in_specs=[tile(D_MODEL), _mod_spec(tokens_per_row, first_row), _resident((1, D_MODEL)),
                  _resident(w_qkv.shape)]
        + [pl.BlockSpec((TOKEN_TILE, LANES), lambda i: (i % tiles_per_row, 0))] * 2,
        out_specs=[tile(w) for w in widths],
        out_shape=[jax.ShapeDtypeStruct((t, w), BF16) for w in widths],
        compiler_params=_params(),
        name="proj_lat",
    )(x, mod, g, w_qkv, *rope)


def _lane_halves(shape):
    lane = lax.broadcasted_iota(jnp.int32, shape, 1) % LANES
    return lane < HEAD_DIM, lane >= HEAD_DIM


def _softmax_cols(cols, sink=None):
    m = jnp.max(functools.reduce(jnp.maximum, cols), axis=-1, keepdims=True)
    if sink is not None:
        m = jnp.maximum(m, sink)
    es = [jnp.exp2(c - m) for c in cols]
    l = jnp.sum(functools.reduce(jnp.add, es), axis=-1, keepdims=True)
    if sink is not None:
        l = l + jnp.exp2(sink - m)
    return es, l


def _cols(x):
    return [x[:, j * LANES:(j + 1) * LANES] for j in range(x.shape[1] // LANES)]


def _attn_ctx_kernel(sink_ref, qa_ref, qb_ref, ka_ref, va_ref, kb_ref, vb_ref, ya_ref, yb_ref):
    seq = qa_ref.shape[0]
    lo, hi = _lane_halves((seq, LANES))
    top = lax.broadcasted_iota(jnp.int32, (2 * seq, 1), 0) < seq
    zero = jnp.zeros((), BF16)

    def heads_of_block(q, kt, vt, sink):
        q = jnp.concatenate([jnp.where(lo, q, zero), jnp.where(hi, q, zero)], axis=0)
        es, l = _softmax_cols(_cols(_dot(q, kt)), sink)
        o = _dot_nt(jnp.concatenate(es, axis=1).astype(BF16), vt) * (1.0 / l)
        return jnp.where(lo, o[:seq], o[seq:])

    for p in range(NA_HEADS // 2):
        sl = slice(p * LANES, (p + 1) * LANES)
        kt = ka_ref[2 * p:2 * p + 2].reshape(LANES, seq).astype(BF16)
        vt = va_ref[2 * p:2 * p + 2].reshape(LANES, seq).astype(BF16)
        ya_ref[:, sl] = heads_of_block(qa_ref[:, sl], kt, vt, None).astype(ya_ref.dtype)
    for p in range(SWA_HEADS // 2):
        sl = slice(p * LANES, (p + 1) * LANES)
        kv = (2 * p) // (SWA_HEADS // SWA_KV_HEADS)
        kt, vt = kb_ref[kv].astype(BF16), vb_ref[kv].astype(BF16)
        kt, vt = jnp.concatenate([kt, kt], axis=0), jnp.concatenate([vt, vt], axis=0)
        sink = jnp.where(top, sink_ref[2 * p], sink_ref[2 * p + 1]) * LOG2E
        yb_ref[:, sl] = heads_of_block(qb_ref[:, sl], kt, vt, sink).astype(yb_ref.dtype)


def _attn_ctx(sink, qa, qb, ka, va, kb, vb, *, seq):
    t = qa.shape[0]
    tile = lambda w: pl.BlockSpec((seq, w), lambda b: (b, 0))
    cache = lambda heads: pl.BlockSpec((None, heads, HEAD_DIM, seq), lambda b: (b, 0, 0, 0))
    return pl.pallas_call(
        _attn_ctx_kernel,
        grid=(t // seq,),
        in_specs=[pl.BlockSpec(memory_space=pltpu.SMEM), tile(NA_WIDTH), tile(SWA_Q_WIDTH),
                  cache(NA_HEADS), cache(NA_HEADS), cache(SWA_KV_HEADS), cache(SWA_KV_HEADS)],
        out_specs=[tile(NA_WIDTH), tile(SWA_Q_WIDTH)],
        out_shape=[jax.ShapeDtypeStruct((t, NA_WIDTH), BF16), jax.ShapeDtypeStruct((t, SWA_Q_WIDTH), BF16)],
        compiler_params=_params(),
        name="attn_ctx",
    )(sink, qa, qb, ka, va, kb, vb)


def _attn_na_kernel(q_ref, k_ref, v_ref, ck_ref, cv_ref, rows_ref, y_ref, ck_scr, cv_scr, bias_ref, *, n, ctx):
    rows = n // GRID_W
    tile_q = NA_TILE_ROWS * GRID_W
    zero = jnp.zeros((), BF16)
    @pl.when(pl.program_id(0) == 0)
    def _():
        _fill_na_bias(rows_ref, bias_ref)

    ck_scr[...] = ck_ref[...].reshape(NA_WIDTH, ctx).astype(BF16)
    cv_scr[...] = cv_ref[...].reshape(NA_WIDTH, ctx).astype(BF16)
    lo, hi = _lane_halves((tile_q, LANES))
    left_half = lax.broadcasted_iota(jnp.int32, (GRID_W, LANES), 1) < GRID_W
    empty = jnp.zeros((GRID_W, LANES), BF16)

    def tile(q0, k0, key_rows, lead, first):
        pairs = key_rows // 2
        for p in range(NA_HEADS // 2):
            sl = slice(p * LANES, (p + 1) * LANES)
            q = q_ref[pl.ds(q0, tile_q), sl]
            q = jnp.concatenate([jnp.where(lo, q, zero), jnp.where(hi, q, zero)], axis=0)
            s_nb = _dot_nt(q, k_ref[pl.ds(k0, key_rows * GRID_W), sl])
            s_ctx = _dot(q, ck_scr[sl, :])
            e_nb, e_ctx, ls = [], [], []
            for idx in range(2):
                for a in range(NA_TILE_ROWS):
                    qa = slice(idx * tile_q + a * GRID_W, idx * tile_q + (a + 1) * GRID_W)
                    inside = lambda i: first[a] <= i < first[a] + NA_ROWS
                    cols, where = [], []
                    for m in range(pairs):
                        if not (inside(2 * m) or inside(2 * m + 1)):
                            continue
                        blk = s_nb[qa, m * LANES:(m + 1) * LANES] + bias_ref[2 * p + idx, 2 * m - a - lead + NA_ROWS]
                        if not inside(2 * m + 1):
                            blk = jnp.where(left_half, blk, NEG_INF)
                        elif not inside(2 * m):
                            blk = jnp.where(left_half, NEG_INF, blk)
                        cols.append(blk)
                        where.append(m)
                    cols += [s_ctx[qa, j * LANES:(j + 1) * LANES] for j in range(ctx // LANES)]
                    es, l = _softmax_cols(cols)
                    es = [e.astype(BF16) for e in es]
                    e_nb.append(jnp.concatenate(
                        [es[where.index(m)] if m in where else empty for m in range(pairs)], axis=1))
                    e_ctx.append(jnp.concatenate(es[len(where):], axis=1))
                    ls.append(l)
            o = (_dot(jnp.concatenate(e_nb, axis=0), v_ref[pl.ds(k0, key_rows * GRID_W), sl])
                 + _dot_nt(jnp.concatenate(e_ctx, axis=0), cv_scr[sl, :]))
            o = o * (1.0 / jnp.concatenate(ls, axis=0))
            y_ref[pl.ds(q0, tile_q), sl] = jnp.where(lo, o[:tile_q], o[tile_q:]).astype(y_ref.dtype)

    half = NA_ROWS // 2
    tile(0, 0, NA_ROWS, 0, (0,) * NA_TILE_ROWS)

    for t in range(1, rows // NA_TILE_ROWS - 1):
        tile(t * tile_q, t * tile_q - half * GRID_W, NA_ROWS + NA_TILE_ROWS, half, tuple(range(NA_TILE_ROWS)))
    tile(n - tile_q, n - NA_ROWS * GRID_W, NA_ROWS, half, (0,) * NA_TILE_ROWS)


def _attn_na(q, k, v, ck, cv, bias_rows, *, n, ctx):
    t = q.shape[0]
    tile = pl.BlockSpec((n, NA_WIDTH), lambda b: (b, 0))
    ctile = pl.BlockSpec((None, NA_HEADS, HEAD_DIM, ctx), lambda b: (b, 0, 0, 0))
    return pl.pallas_call(
        functools.partial(_attn_na_kernel, n=n, ctx=ctx),
        grid=(t // n,),
        in_specs=[tile, tile, tile, ctile, ctile, _resident(bias_rows.shape)],
        out_specs=tile,
        out_shape=jax.ShapeDtypeStruct((t, NA_WIDTH), BF16),
        scratch_shapes=[pltpu.VMEM((NA_WIDTH, ctx), BF16)] * 2
        + [pltpu.VMEM((NA_HEADS, 2 * NA_ROWS, GRID_W, LANES), F32)],
        compiler_params=_params(),
        name="attn_na",
    )(q, k, v, ck, cv, bias_rows)


def _na_bias_rows(rel_bias):
    side = GRID_W - NA_COLS
    z = jnp.pad(rel_bias * LOG2E, ((0, 0), (0, 0), (side, side + 1)))
    return jnp.pad(z, ((0, 0), (1, 1), (0, 0)), constant_values=NEG_INF)


def _fill_na_bias(rows_ref, bias_scr):
    q = lax.broadcasted_iota(jnp.int32, (GRID_W, LANES), 0)
    lane = lax.broadcasted_iota(jnp.int32, (GRID_W, LANES), 1)
    kc = lane % GRID_W
    col_start = jnp.clip(q - NA_COLS // 2, 0, GRID_W - NA_COLS)
    in_window = (kc >= col_start) & (kc < col_start + NA_COLS)
    for h in range(NA_HEADS):
        blocks = []
        for j in range(2 * NA_ROWS + 1):
            row = jnp.broadcast_to(rows_ref[h, j:j + 1, :], (GRID_W, LANES))
            rolled = pltpu.roll(row, LANES - GRID_W + 1, 1, stride=1, stride_axis=0)
            blocks.append(jnp.where(in_window, rolled, NEG_INF))
        for j in range(2 * NA_ROWS):
            bias_scr[h, j] = jnp.where(lane < GRID_W, blocks[j], pltpu.roll(blocks[j + 1], GRID_W, 1))


def _attn_swa_kernel(sink_ref, q_ref, k_ref, v_ref, ck_ref, cv_ref, y_ref, k_scr, v_scr, ck_scr, cv_scr, *, n, ctx):
    group = SWA_HEADS // SWA_KV_HEADS
    band = SWA_QTILE + 2 * SWA_BLOCK
    for src, csrc, dst, cdst in ((k_ref, ck_ref, k_scr, ck_scr), (v_ref, cv_ref, v_scr, cv_scr)):
        x = src[...].astype(F32)
        xr = pltpu.roll(x, HEAD_DIM, 1)
        first_half, second_half = _lane_halves(x.shape)
        dst[0] = jnp.where(first_half, x, xr).astype(BF16)
        dst[1] = jnp.where(second_half, x, xr).astype(BF16)
        for kv in range(SWA_KV_HEADS):
            c = csrc[kv].astype(BF16)
            cdst[kv] = jnp.concatenate([c, c], axis=0)

    qi = lax.broadcasted_iota(jnp.int32, (SWA_QTILE, band), 0)
    kj = lax.broadcasted_iota(jnp.int32, (SWA_QTILE, band), 1)
    lo, hi = _lane_halves((SWA_QTILE, LANES))
    top = lax.broadcasted_iota(jnp.int32, (2 * SWA_QTILE, 1), 0) < SWA_QTILE
    zero = jnp.zeros((), BF16)

    def block_body(b, carry):
        q0 = pl.multiple_of(b * SWA_QTILE, SWA_QTILE)
        k0 = pl.multiple_of(jnp.clip(q0 - SWA_BLOCK, 0, n - band), SWA_BLOCK)
        mask = jnp.where(jnp.abs(kj - qi + (k0 - q0)) <= SWA_WINDOW, 0.0, NEG_INF)
        mask2 = jnp.concatenate([mask, mask], axis=0)
        for p in range(SWA_HEADS // 2):
            kv = 2 * p // group
            sl = slice(p * LANES, (p + 1) * LANES)
            q = q_ref[pl.ds(q0, SWA_QTILE), sl]
            q = jnp.concatenate([jnp.where(lo, q, zero), jnp.where(hi, q, zero)], axis=0)
            s_band = _dot_nt(q, k_scr[kv, pl.ds(k0, band), :])
            s_ctx = _dot(q, ck_scr[kv])
            sink = jnp.where(top, sink_ref[2 * p], sink_ref[2 * p + 1]) * LOG2E
            es, l = _softmax_cols(_cols(s_band + mask2) + _cols(s_ctx), sink)
            e_band = jnp.concatenate(es[:band // LANES], axis=1).astype(BF16)
            e_ctx = jnp.concatenate(es[band // LANES:], axis=1).astype(BF16)
            o = (_dot(e_band, v_scr[kv, pl.ds(k0, band), :]) + _dot_nt(e_ctx, cv_scr[kv])) * (1.0 / l)
            y_ref[pl.ds(q0, SWA_QTILE), sl] = jnp.where(lo, o[:SWA_QTILE], o[SWA_QTILE:]).astype(y_ref.dtype)
        return carry

    lax.fori_loop(0, n // SWA_QTILE, block_body, 0, unroll=True)


def _attn_swa(sink, q, k, v, ck, cv, *, n, ctx):
    t = q.shape[0]
    tile = lambda w: pl.BlockSpec((n, w), lambda b: (b, 0))
    ctile = pl.BlockSpec((None, SWA_KV_HEADS, HEAD_DIM, ctx), lambda b: (b, 0, 0, 0))
    return pl.pallas_call(
        functools.partial(_attn_swa_kernel, n=n, ctx=ctx),
        grid=(t // n,),
        in_specs=[pl.BlockSpec(memory_space=pltpu.SMEM), tile(SWA_Q_WIDTH), tile(SWA_KV_WIDTH),
                  tile(SWA_KV_WIDTH), ctile, ctile],
        out_specs=tile(SWA_Q_WIDTH),
        out_shape=jax.ShapeDtypeStruct((t, SWA_Q_WIDTH), BF16),
        scratch_shapes=[pltpu.VMEM((SWA_KV_HEADS, n, SWA_KV_WIDTH), BF16)] * 2
        + [pltpu.VMEM((SWA_KV_HEADS, LANES, ctx), BF16)] * 2,
        compiler_params=_params(),
        name="attn_swa",
    )(sink, q, k, v, ck, cv)


def _rope_tables(n):
    half = HEAD_DIM // 4
    freqs = jnp.power(ROPE_BASE, -jnp.arange(half, dtype=F32) / half)
    t = jnp.arange(n)
    cos, sin = [], []
    for pos in (t // GRID_W, t % GRID_W):
        ang = pos.astype(F32)[:, None] * freqs[None, :]
        cos += [jnp.cos(ang), jnp.cos(ang)]
        sin += [-jnp.sin(ang), jnp.sin(ang)]
    cos, sin = jnp.concatenate(cos, axis=-1), jnp.concatenate(sin, axis=-1)
    reps = LANES // HEAD_DIM
    return jnp.tile(cos, (1, reps)), jnp.tile(sin, (1, reps))


def _merge_kernel(x_ref, ya_ref, yb_ref, mod_ref, g_ref, win_ref, wba_ref, wbb_ref, wout_ref, o_ref):
    x = x_ref[...]
    mod = mod_ref[0]
    h = _modulate(x, g_ref[...], mod[3:4], mod[4:5]).astype(BF16)
    a = jax.nn.sigmoid(_dot(h, win_ref[:, QKV_COLS:QKV_COLS + D_MODEL])) * _dot(ya_ref[...], wba_ref[...])
    b = jax.nn.sigmoid(_dot(h, win_ref[:, QKV_COLS + D_MODEL:])) * _dot(yb_ref[...], wbb_ref[...])
    o_ref[...] = x + mod[5:6] * _dot((a + b).astype(BF16), wout_ref[...])


def _merge(x, ya, yb, mod, g, wgate, wba, wbb, wout, *, tokens_per_row, first_row):
    t = x.shape[0]
    tile = lambda w: pl.BlockSpec((TOKEN_TILE, w), lambda i: (i, 0))
    return pl.pallas_call(
        _merge_kernel,
        grid=(t // TOKEN_TILE,),
        in_specs=[tile(D_MODEL), tile(NA_WIDTH), tile(SWA_Q_WIDTH), _mod_spec(tokens_per_row, first_row),
                  _resident((1, D_MODEL)), _resident(wgate.shape), _resident(wba.shape), _resident(wbb.shape),
                  _resident(wout.shape)],
        out_specs=tile(D_MODEL),
        out_shape=jax.ShapeDtypeStruct((t, D_MODEL), F32),
        compiler_params=_params(),
        name="merge",
    )(x, ya, yb, mod, g, wgate, wba, wbb, wout)


def kernel(x_prompt, x_sample, cache_na_k, cache_na_v, cache_swa_k, cache_swa_v, c, c_ctx, w_ada, b_ada,
           norm_ffn1, ffn1_w_gate, ffn1_w_up, ffn1_w_down, norm_mix, w_in, na_rel_bias, swa_sink,
           w_branch_na, w_branch_swa, w_out, norm_ffn2, ffn2_w_gate, ffn2_w_up, ffn2_w_down, norm_final):
    depth = w_ada.shape[0]
    assert depth == 1
    batch, seq, _ = x_prompt.shape
    dec_batch, dec_seq, _ = x_sample.shape
    past = cache_na_k.shape[2]
    layer = 0
    row = lambda v: v.reshape(1, D_MODEL)
    bf = lambda w: w.astype(BF16)

    cond = jnp.zeros((MOD_ROWS, D_MODEL), F32).at[0].set(c_ctx).at[1:1 + dec_batch].set(c)
    mod = _adaln(cond, w_ada[layer], b_ada[layer]).reshape(MOD_ROWS, N_MOD, D_MODEL)

    ffn1 = (row(norm_ffn1[layer]), bf(ffn1_w_gate[layer]), bf(ffn1_w_up[layer]), bf(ffn1_w_down[layer]),
            row(norm_final))
    g_mix = row(norm_mix[layer])
    sink = swa_sink[layer]
    ctx_rows = dict(tokens_per_row=batch * seq, first_row=0)
    lat_rows = dict(tokens_per_row=dec_seq, first_row=1)

    later = (ffn2_w_gate[layer], ffn2_w_up[layer], ffn2_w_down[layer], w_in[layer],
             w_branch_na[layer], w_branch_swa[layer], w_out[layer])
    x_lat, *later = _ffn(x_sample.reshape(dec_batch * dec_seq, D_MODEL), mod, *ffn1, first=0, final=False,
                         cast=later, **lat_rows)
    ffn2 = (row(norm_ffn2[layer]), *later[:3], row(norm_final))
    w_qkv = later[3]
    merge_w = tuple(later[3:])
    w_kvt = jnp.concatenate([w_qkv[:, NA_WIDTH:3 * NA_WIDTH], w_qkv[:, 3 * NA_WIDTH + SWA_Q_WIDTH:QKV_COLS]],
                            axis=1).T

    where = ctx_rows
    x = x_prompt.reshape(batch * seq, D_MODEL)
    x = _ffn(x, mod, *ffn1, first=0, final=False, **where)
    qa, qb, *new_cache = _proj_ctx(x, mod, g_mix, w_qkv, w_kvt, seq=seq, **where)
    ya, yb = _attn_ctx(sink, qa, qb, *new_cache, seq=seq)
    x = _merge(x, ya, yb, mod, g_mix, *merge_w, **where)
    y_prompt = _ffn(x, mod, *ffn2, first=6, final=True, **where).reshape(batch, seq, D_MODEL)

    where = lat_rows
    x = x_lat
    qa, kal, val, qb, kbl, vbl = _proj_lat(x, mod, g_mix, w_qkv, _rope_tables(dec_seq), **where)
    transposed = lambda cache: jnp.transpose(cache[:, layer], (0, 2, 3, 1))
    ya = _attn_na(qa, kal, val, transposed(cache_na_k), transposed(cache_na_v),
                  _na_bias_rows(na_rel_bias[layer]), n=dec_seq, ctx=past)
    yb = _attn_swa(sink, qb, kbl, vbl, transposed(cache_swa_k), transposed(cache_swa_v), n=dec_seq, ctx=past)
    x = _merge(x, ya, yb, mod, g_mix, *merge_w, **where)
    y_sample = _ffn(x, mod, *ffn2, first=6, final=True, **where).reshape(dec_batch, dec_seq, D_MODEL)

    new_cache = [jnp.transpose(t, (0, 3, 1, 2))[:, None] for t in new_cache]
    return (y_prompt, y_sample, *new_cache)
```

```python
import functools

import numpy as np
import jax
import jax.numpy as jnp
from jax import lax
from jax.experimental import pallas as pl
from jax.experimental.pallas import tpu as pltpu

F32 = jnp.float32
BF16 = jnp.bfloat16

D_MODEL = 1024
FFN_DIM = 2816
HEAD_DIM = 64
N_MOD = 9
GRID_W = 64
NA_HEADS = 8
NA_ROWS = 8
NA_COLS = 16
SWA_HEADS = 8
SWA_KV_HEADS = 2
SWA_WINDOW = 128
SWA_BLOCK = 128
ROPE_BASE = 10000.0
EPS = 1e-6
NEG_INF = -1e30
NA_WIDTH = NA_HEADS * HEAD_DIM
SWA_Q_WIDTH = SWA_HEADS * HEAD_DIM
SWA_KV_WIDTH = SWA_KV_HEADS * HEAD_DIM
QKV_COLS = 3 * NA_WIDTH + SWA_Q_WIDTH + 2 * SWA_KV_WIDTH
LANES = 128
MOD_ROWS = 16
VMEM_LIMIT = 56 * 1024 * 1024
TOKEN_TILE = 1024
FFN_TILE = 1024
FFN_CHUNK = 256
SWA_QTILE = 256
NA_TILE_ROWS = 4
LOG2E = 1.4426950408889634
Q_SCALE = HEAD_DIM ** -0.5 * LOG2E


def _dot(a, b):
    return jnp.dot(a, b, preferred_element_type=F32)


def _dot_nt(a, b):
    return lax.dot_general(a, b, (((1,), (1,)), ((), ())), preferred_element_type=F32)


def _silu(x):
    return x * jax.nn.sigmoid(x)


def _rms(x, g):
    return x * lax.rsqrt(jnp.mean(x * x, axis=-1, keepdims=True) + EPS) * g


def _modulate(x, g, shift, scale):
    return _rms(x, g) * (1.0 + scale) + shift


def _resident(shape):
    nd = len(shape)
    return pl.BlockSpec(shape, lambda *_: (0,) * nd, pipeline_mode=pl.Buffered(1))


def _params():
    return pltpu.CompilerParams(dimension_semantics=("arbitrary",), vmem_limit_bytes=VMEM_LIMIT)


def _adaln_kernel(c_ref, w_ref, b_ref, o_ref):
    s = _silu(c_ref[...]).astype(BF16)
    o_ref[...] = _dot(s, w_ref[...].astype(BF16)) + b_ref[...]


def _adaln(cond, w_ada, b_ada):
    n = w_ada.shape[1]
    blk = D_MODEL
    return pl.pallas_call(
        _adaln_kernel,
        grid=(n // blk,),
        in_specs=[pl.BlockSpec((MOD_ROWS, D_MODEL), lambda j: (0, 0)),
                  pl.BlockSpec((D_MODEL, blk), lambda j: (0, j)),
                  pl.BlockSpec((1, blk), lambda j: (0, j))],
        out_specs=pl.BlockSpec((MOD_ROWS, blk), lambda j: (0, j)),
        out_shape=jax.ShapeDtypeStruct((MOD_ROWS, n), F32),
        compiler_params=_params(),
        name="adaln",
    )(cond, w_ada, b_ada.reshape(1, n))


def _mod_spec(tokens_per_row, first_row, tile=TOKEN_TILE):
    tiles_per_row = tokens_per_row // tile
    return pl.BlockSpec((1, N_MOD, D_MODEL), lambda i: (first_row + i // tiles_per_row, 0, 0))


def _ffn_kernel(x_ref, mod_ref, g_ref, wg_ref, wu_ref, wd_ref, gf_ref, *rest, first, final):
    n_cast = (len(rest) - 2) // 2
    o_ref, a_scr = rest[n_cast], rest[-1]
    for src, dst in zip(rest[:n_cast], rest[n_cast + 1:-1]):
        dst[...] = src[...].astype(BF16)
    x = x_ref[...]
    mod = mod_ref[0]
    shift, scale, gate = mod[first:first + 1], mod[first + 1:first + 2], mod[first + 2:first + 3]
    h = _modulate(x, g_ref[...], shift, scale).astype(BF16)
    for c in range(FFN_DIM // FFN_CHUNK):
        sl = slice(c * FFN_CHUNK, (c + 1) * FFN_CHUNK)
        a_scr[:, sl] = (_silu(_dot(h, wg_ref[:, sl])) * _dot(h, wu_ref[:, sl])).astype(BF16)
    y = x + (0.5 * gate) * _dot(a_scr[...], wd_ref[...])
    if final:
        y = _rms(y, gf_ref[...])
    o_ref[...] = y


def _ffn(x, mod, g, wg, wu, wd, gf, *, tokens_per_row, first_row, first, final, cast=()):
    t = x.shape[0]
    steps = t // FFN_TILE
    tile = pl.BlockSpec((FFN_TILE, D_MODEL), lambda i: (i, 0))
    chunk = lambda w: pl.BlockSpec((w.shape[0] // steps, w.shape[1]), lambda i: (i, 0))
    assert all(w.shape[0] % (steps * 16) == 0 for w in cast)
    out = pl.pallas_call(
        functools.partial(_ffn_kernel, first=first, final=final),
        grid=(steps,),
        in_specs=[tile, _mod_spec(tokens_per_row, first_row, FFN_TILE), _resident((1, D_MODEL)),
                  _resident(wg.shape), _resident(wu.shape), _resident(wd.shape), _resident((1, D_MODEL))]
        + [chunk(w) for w in cast],
        out_specs=[tile] + [chunk(w) for w in cast],
        out_shape=[jax.ShapeDtypeStruct((t, D_MODEL), F32)] + [jax.ShapeDtypeStruct(w.shape, BF16) for w in cast],
        scratch_shapes=[pltpu.VMEM((FFN_TILE, FFN_DIM), BF16)],
        compiler_params=_params(),
        name="ffn",
    )(x, mod, g, wg, wu, wd, gf, *cast)
    return out if cast else out[0]


def _swap16(x):
    lane = lax.broadcasted_iota(jnp.int32, x.shape, 1)
    return jnp.where(lane % 32 < 16, pltpu.roll(x, LANES - 16, 1), pltpu.roll(x, 16, 1))


def _proj_ctx_kernel(x_ref, mod_ref, g_ref, w_ref, wkvt_ref, qa_ref, qb_ref, ka_ref, va_ref, kb_ref, vb_ref):
    mod = mod_ref[0]
    h = _modulate(x_ref[...], g_ref[...], mod[3:4], mod[4:5]).astype(BF16)
    qa_ref[...] = (_dot(h, w_ref[:, :NA_WIDTH]) * Q_SCALE).astype(qa_ref.dtype)
    qb_ref[...] = (_dot(h, w_ref[:, 3 * NA_WIDTH:3 * NA_WIDTH + SWA_Q_WIDTH]) * Q_SCALE).astype(qb_ref.dtype)
    kvt = _dot_nt(wkvt_ref[...], h)
    requests, _, _, seq = ka_ref.shape
    o = 0
    for ref in (ka_ref, va_ref, kb_ref, vb_ref):
        width = ref.shape[1] * HEAD_DIM
        for r in range(requests):
            ref[r] = kvt[o:o + width, r * seq:(r + 1) * seq].reshape(ref.shape[1:])
        o += width


def _rope(y, cos, sin):
    blocks = [y[:, j * LANES:(j + 1) * LANES] for j in range(y.shape[1] // LANES)]
    return [b * cos + _swap16(b) * sin for b in blocks]


def _proj_lat_kernel(x_ref, mod_ref, g_ref, w_ref, cos_ref, sin_ref,
                     qa_ref, ka_ref, va_ref, qb_ref, kb_ref, vb_ref):
    mod = mod_ref[0]
    h = _modulate(x_ref[...], g_ref[...], mod[3:4], mod[4:5]).astype(BF16)
    cos, sin = cos_ref[...], sin_ref[...]
    o = 3 * NA_WIDTH
    for j, b in enumerate(_rope(_dot(h, w_ref[:, o:o + SWA_Q_WIDTH]), cos, sin)):
        qb_ref[:, j * LANES:(j + 1) * LANES] = (b * Q_SCALE).astype(qb_ref.dtype)
    o += SWA_Q_WIDTH
    y = _dot(h, w_ref[:, o:o + 2 * SWA_KV_WIDTH])
    kb_ref[...] = _rope(y[:, :SWA_KV_WIDTH], cos, sin)[0].astype(kb_ref.dtype)
    vb_ref[...] = y[:, SWA_KV_WIDTH:].astype(vb_ref.dtype)
    o = 0
    for ref, scale in ((qa_ref, True), (ka_ref, False), (va_ref, False)):
        y = _dot(h, w_ref[:, o:o + NA_WIDTH])
        ref[...] = (y * Q_SCALE if scale else y).astype(ref.dtype)
        o += NA_WIDTH


def _proj_ctx(x, mod, g, w_qkv, w_kvt, *, seq, tokens_per_row, first_row):
    t = x.shape[0]
    tile = lambda w: pl.BlockSpec((TOKEN_TILE, w), lambda i: (i, 0))
    cache = lambda heads: pl.BlockSpec((TOKEN_TILE // seq, heads, HEAD_DIM, seq), lambda i: (i, 0, 0, 0))
    cache_shape = lambda heads: jax.ShapeDtypeStruct((t // seq, heads, HEAD_DIM, seq), F32)
    kv_heads = (NA_HEADS, NA_HEADS, SWA_KV_HEADS, SWA_KV_HEADS)
    return pl.pallas_call(
        _proj_ctx_kernel,
        grid=(t // TOKEN_TILE,),
        in_specs=[tile(D_MODEL), _mod_spec(tokens_per_row, first_row), _resident((1, D_MODEL)),
                  _resident(w_qkv.shape), _resident(w_kvt.shape)],
        out_specs=[tile(NA_WIDTH), tile(SWA_Q_WIDTH)] + [cache(nh) for nh in kv_heads],
        out_shape=[jax.ShapeDtypeStruct((t, NA_WIDTH), BF16), jax.ShapeDtypeStruct((t, SWA_Q_WIDTH), BF16)]
        + [cache_shape(nh) for nh in kv_heads],
        compiler_params=_params(),
        name="proj_ctx",
    )(x, mod, g, w_qkv, w_kvt)


def _proj_lat(x, mod, g, w_qkv, rope, *, tokens_per_row, first_row):
    t = x.shape[0]
    tile = lambda w: pl.BlockSpec((TOKEN_TILE, w), lambda i: (i, 0))
    widths = (NA_WIDTH, NA_WIDTH, NA_WIDTH, SWA_Q_WIDTH, SWA_KV_WIDTH, SWA_KV_WIDTH)
    tiles_per_row = tokens_per_row // TOKEN_TILE
    return pl.pallas_call(
        _proj_lat_kernel,
        grid=(t // TOKEN_TILE,),
        in_specs=[tile(D_MODEL), _mod_spec(tokens_per_row, first_row), _resident((1, D_MODEL)),
                  _resident(w_qkv.shape)]
        + [pl.BlockSpec((TOKEN_TILE, LANES), lambda i: (i % tiles_per_row, 0))] * 2,
        out_specs=[tile(w) for w in widths],
        out_shape=[jax.ShapeDtypeStruct((t, w), BF16) for w in widths],
        compiler_params=_params(),
        name="proj_lat",
    )(x, mod, g, w_qkv, *rope)


def _lane_halves(shape):
    lane = lax.broadcasted_iota(jnp.int32, shape, 1) % LANES
    return lane < HEAD_DIM, lane >= HEAD_DIM


def _softmax_cols(cols, sink=None):
    m = jnp.max(functools.reduce(jnp.maximum, cols), axis=-1, keepdims=True)
    if sink is not None:
        m = jnp.maximum(m, sink)
    es = [jnp.exp2(c - m) for c in cols]
    l = jnp.sum(functools.reduce(jnp.add, es), axis=-1, keepdims=True)
    if sink is not None:
        l = l + jnp.exp2(sink - m)
    return es, l


def _cols(x):
    return [x[:, j * LANES:(j + 1) * LANES] for j in range(x.shape[1] // LANES)]


def _attn_ctx_kernel(sink_ref, qa_ref, qb_ref, ka_ref, va_ref, kb_ref, vb_ref, ya_ref, yb_ref):
    seq = qa_ref.shape[0]
    lo, hi = _lane_halves((seq, LANES))
    top = lax.broadcasted_iota(jnp.int32, (2 * seq, 1), 0) < seq
    zero = jnp.zeros((), BF16)

    def heads_of_block(q, kt, vt, sink):
        q = jnp.concatenate([jnp.where(lo, q, zero), jnp.where(hi, q, zero)], axis=0)
        es, l = _softmax_cols(_cols(_dot(q, kt)), sink)
        o = _dot_nt(jnp.concatenate(es, axis=1).astype(BF16), vt) * (1.0 / l)
        return jnp.where(lo, o[:seq], o[seq:])

    for p in range(NA_HEADS // 2):
        sl = slice(p * LANES, (p + 1) * LANES)
        kt = ka_ref[2 * p:2 * p + 2].reshape(LANES, seq).astype(BF16)
        vt = va_ref[2 * p:2 * p + 2].reshape(LANES, seq).astype(BF16)
        ya_ref[:, sl] = heads_of_block(qa_ref[:, sl], kt, vt, None).astype(ya_ref.dtype)
    for p in range(SWA_HEADS // 2):
        sl = slice(p * LANES, (p + 1) * LANES)
        kv = (2 * p) // (SWA_HEADS // SWA_KV_HEADS)
        kt, vt = kb_ref[kv].astype(BF16), vb_ref[kv].astype(BF16)
        kt, vt = jnp.concatenate([kt, kt], axis=0), jnp.concatenate([vt, vt], axis=0)
        sink = jnp.where(top, sink_ref[2 * p], sink_ref[2 * p + 1]) * LOG2E
        yb_ref[:, sl] = heads_of_block(qb_ref[:, sl], kt, vt, sink).astype(yb_ref.dtype)


def _attn_ctx(sink, qa, qb, ka, va, kb, vb, *, seq):
    t = qa.shape[0]
    tile = lambda w: pl.BlockSpec((seq, w), lambda b: (b, 0))
    cache = lambda heads: pl.BlockSpec((None, heads, HEAD_DIM, seq), lambda b: (b, 0, 0, 0))
    return pl.pallas_call(
        _attn_ctx_kernel,
        grid=(t // seq,),
        in_specs=[pl.BlockSpec(memory_space=pltpu.SMEM), tile(NA_WIDTH), tile(SWA_Q_WIDTH),
                  cache(NA_HEADS), cache(NA_HEADS), cache(SWA_KV_HEADS), cache(SWA_KV_HEADS)],
        out_specs=[tile(NA_WIDTH), tile(SWA_Q_WIDTH)],
        out_shape=[jax.ShapeDtypeStruct((t, NA_WIDTH), BF16), jax.ShapeDtypeStruct((t, SWA_Q_WIDTH), BF16)],
        compiler_params=_params(),
        name="attn_ctx",
    )(sink, qa, qb, ka, va, kb, vb)


def _attn_na_kernel(q_ref, k_ref, v_ref, ck_ref, cv_ref, bias_ref, y_ref, ck_scr, cv_scr, *, n, ctx):
    rows = n // GRID_W
    tile_q = NA_TILE_ROWS * GRID_W
    zero = jnp.zeros((), BF16)
    ck_scr[...] = ck_ref[...].reshape(NA_WIDTH, ctx).astype(BF16)
    cv_scr[...] = cv_ref[...].reshape(NA_WIDTH, ctx).astype(BF16)
    lo, hi = _lane_halves((tile_q, LANES))
    left_half = lax.broadcasted_iota(jnp.int32, (GRID_W, LANES), 1) < GRID_W
    empty = jnp.zeros((GRID_W, LANES), BF16)

    def tile(q0, k0, key_rows, lead, first):
        pairs = key_rows // 2
        for p in range(NA_HEADS // 2):
            sl = slice(p * LANES, (p + 1) * LANES)
            q = q_ref[pl.ds(q0, tile_q), sl]
            q = jnp.concatenate([jnp.where(lo, q, zero), jnp.where(hi, q, zero)], axis=0)
            s_nb = _dot_nt(q, k_ref[pl.ds(k0, key_rows * GRID_W), sl])
            s_ctx = _dot(q, ck_scr[sl, :])
            e_nb, e_ctx, ls = [], [], []
            for idx in range(2):
                for a in range(NA_TILE_ROWS):
                    qa = slice(idx * tile_q + a * GRID_W, idx * tile_q + (a + 1) * GRID_W)
                    inside = lambda i: first[a] <= i < first[a] + NA_ROWS
                    cols, where = [], []
                    for m in range(pairs):
                        if not (inside(2 * m) or inside(2 * m + 1)):
                            continue
                        blk = s_nb[qa, m * LANES:(m + 1) * LANES] + bias_ref[2 * p + idx, 2 * m - a - lead + NA_ROWS]
                        if not inside(2 * m + 1):
                            blk = jnp.where(left_half, blk, NEG_INF)
                        elif not inside(2 * m):
                            blk = jnp.where(left_half, NEG_INF, blk)
                        cols.append(blk)
                        where.append(m)
                    cols += [s_ctx[qa, j * LANES:(j + 1) * LANES] for j in range(ctx // LANES)]
                    es, l = _softmax_cols(cols)
                    es = [e.astype(BF16) for e in es]
                    e_nb.append(jnp.concatenate(
                        [es[where.index(m)] if m in where else empty for m in range(pairs)], axis=1))
                    e_ctx.append(jnp.concatenate(es[len(where):], axis=1))
                    ls.append(l)
            o = (_dot(jnp.concatenate(e_nb, axis=0), v_ref[pl.ds(k0, key_rows * GRID_W), sl])
                 + _dot_nt(jnp.concatenate(e_ctx, axis=0), cv_scr[sl, :]))
            o = o * (1.0 / jnp.concatenate(ls, axis=0))
            y_ref[pl.ds(q0, tile_q), sl] = jnp.where(lo, o[:tile_q], o[tile_q:]).astype(y_ref.dtype)

    half = NA_ROWS // 2
    tile(0, 0, NA_ROWS, 0, (0,) * NA_TILE_ROWS)

    for t in range(1, rows // NA_TILE_ROWS - 1):
        tile(t * tile_q, t * tile_q - half * GRID_W, NA_ROWS + NA_TILE_ROWS, half, tuple(range(NA_TILE_ROWS)))
    tile(n - tile_q, n - NA_ROWS * GRID_W, NA_ROWS, half, (0,) * NA_TILE_ROWS)


def _attn_na(q, k, v, ck, cv, bias, *, n, ctx):
    t = q.shape[0]
    tile = pl.BlockSpec((n, NA_WIDTH), lambda b: (b, 0))
    ctile = pl.BlockSpec((None, NA_HEADS, HEAD_DIM, ctx), lambda b: (b, 0, 0, 0))
    return pl.pallas_call(
        functools.partial(_attn_na_kernel, n=n, ctx=ctx),
        grid=(t // n,),
        in_specs=[tile, tile, tile, ctile, ctile, _resident(bias.shape)],
        out_specs=tile,
        out_shape=jax.ShapeDtypeStruct((t, NA_WIDTH), BF16),
        scratch_shapes=[pltpu.VMEM((NA_WIDTH, ctx), BF16)] * 2,
        compiler_params=_params(),
        name="attn_na",
    )(q, k, v, ck, cv, bias)


def _na_bias_table(rel_bias):
    rows = _na_bias_rows(rel_bias)
    shape = (NA_HEADS, 2 * NA_ROWS, GRID_W, LANES)
    return pl.pallas_call(
        _fill_na_bias,
        grid=(1,),
        in_specs=[pl.BlockSpec(rows.shape, lambda i: (0, 0, 0))],
        out_specs=pl.BlockSpec(shape, lambda i: (0, 0, 0, 0)),
        out_shape=jax.ShapeDtypeStruct(shape, F32),
        compiler_params=_params(),
        name="na_bias",
    )(rows)


def _na_bias_rows(rel_bias):
    side = GRID_W - NA_COLS
    z = jnp.pad(rel_bias * LOG2E, ((0, 0), (0, 0), (side, side + 1)))
    return jnp.pad(z, ((0, 0), (1, 1), (0, 0)), constant_values=NEG_INF)


def _fill_na_bias(rows_ref, bias_scr):
    q = lax.broadcasted_iota(jnp.int32, (GRID_W, LANES), 0)
    lane = lax.broadcasted_iota(jnp.int32, (GRID_W, LANES), 1)
    kc = lane % GRID_W
    col_start = jnp.clip(q - NA_COLS // 2, 0, GRID_W - NA_COLS)
    in_window = (kc >= col_start) & (kc < col_start + NA_COLS)
    for h in range(NA_HEADS):
        blocks = []
        for j in range(2 * NA_ROWS + 1):
            row = jnp.broadcast_to(rows_ref[h, j:j + 1, :], (GRID_W, LANES))
            rolled = pltpu.roll(row, LANES - GRID_W + 1, 1, stride=1, stride_axis=0)
            blocks.append(jnp.where(in_window, rolled, NEG_INF))
        for j in range(2 * NA_ROWS):
            bias_scr[h, j] = jnp.where(lane < GRID_W, blocks[j], pltpu.roll(blocks[j + 1], GRID_W, 1))


def _attn_swa_kernel(sink_ref, q_ref, k_ref, v_ref, ck_ref, cv_ref, y_ref, k_scr, v_scr, ck_scr, cv_scr, *, n, ctx):
    group = SWA_HEADS // SWA_KV_HEADS
    band = SWA_QTILE + 2 * SWA_BLOCK
    for src, csrc, dst, cdst in ((k_ref, ck_ref, k_scr, ck_scr), (v_ref, cv_ref, v_scr, cv_scr)):
        x = src[...].astype(F32)
        xr = pltpu.roll(x, HEAD_DIM, 1)
        first_half, second_half = _lane_halves(x.shape)
        dst[0] = jnp.where(first_half, x, xr).astype(BF16)
        dst[1] = jnp.where(second_half, x, xr).astype(BF16)
        for kv in range(SWA_KV_HEADS):
            c = csrc[kv].astype(BF16)
            cdst[kv] = jnp.concatenate([c, c], axis=0)

    qi = lax.broadcasted_iota(jnp.int32, (SWA_QTILE, band), 0)
    kj = lax.broadcasted_iota(jnp.int32, (SWA_QTILE, band), 1)
    lo, hi = _lane_halves((SWA_QTILE, LANES))
    top = lax.broadcasted_iota(jnp.int32, (2 * SWA_QTILE, 1), 0) < SWA_QTILE
    zero = jnp.zeros((), BF16)

    def block_body(b, carry):
        q0 = pl.multiple_of(b * SWA_QTILE, SWA_QTILE)
        k0 = pl.multiple_of(jnp.clip(q0 - SWA_BLOCK, 0, n - band), SWA_BLOCK)
        mask = jnp.where(jnp.abs(kj - qi + (k0 - q0)) <= SWA_WINDOW, 0.0, NEG_INF)
        mask2 = jnp.concatenate([mask, mask], axis=0)
        for p in range(SWA_HEADS // 2):
            kv = 2 * p // group
            sl = slice(p * LANES, (p + 1) * LANES)
            q = q_ref[pl.ds(q0, SWA_QTILE), sl]
            q = jnp.concatenate([jnp.where(lo, q, zero), jnp.where(hi, q, zero)], axis=0)
            s_band = _dot_nt(q, k_scr[kv, pl.ds(k0, band), :])
            s_ctx = _dot(q, ck_scr[kv])
            sink = jnp.where(top, sink_ref[2 * p], sink_ref[2 * p + 1]) * LOG2E
            es, l = _softmax_cols(_cols(s_band + mask2) + _cols(s_ctx), sink)
            e_band = jnp.concatenate(es[:band // LANES], axis=1).astype(BF16)
            e_ctx = jnp.concatenate(es[band // LANES:], axis=1).astype(BF16)
            o = (_dot(e_band, v_scr[kv, pl.ds(k0, band), :]) + _dot_nt(e_ctx, cv_scr[kv])) * (1.0 / l)
            y_ref[pl.ds(q0, SWA_QTILE), sl] = jnp.where(lo, o[:SWA_QTILE], o[SWA_QTILE:]).astype(y_ref.dtype)
        return carry

    lax.fori_loop(0, n // SWA_QTILE, block_body, 0, unroll=True)


def _attn_swa(sink, q, k, v, ck, cv, *, n, ctx):
    t = q.shape[0]
    tile = lambda w: pl.BlockSpec((n, w), lambda b: (b, 0))
    ctile = pl.BlockSpec((None, SWA_KV_HEADS, HEAD_DIM, ctx), lambda b: (b, 0, 0, 0))
    return pl.pallas_call(
        functools.partial(_attn_swa_kernel, n=n, ctx=ctx),
        grid=(t // n,),
        in_specs=[pl.BlockSpec(memory_space=pltpu.SMEM), tile(SWA_Q_WIDTH), tile(SWA_KV_WIDTH),
                  tile(SWA_KV_WIDTH), ctile, ctile],
        out_specs=tile(SWA_Q_WIDTH),
        out_shape=jax.ShapeDtypeStruct((t, SWA_Q_WIDTH), BF16),
        scratch_shapes=[pltpu.VMEM((SWA_KV_HEADS, n, SWA_KV_WIDTH), BF16)] * 2
        + [pltpu.VMEM((SWA_KV_HEADS, LANES, ctx), BF16)] * 2,
        compiler_params=_params(),
        name="attn_swa",
    )(sink, q, k, v, ck, cv)


def _rope_tables(n):
    half = HEAD_DIM // 4
    freqs = jnp.power(ROPE_BASE, -jnp.arange(half, dtype=F32) / half)
    t = jnp.arange(n)
    cos, sin = [], []
    for pos in (t // GRID_W, t % GRID_W):
        ang = pos.astype(F32)[:, None] * freqs[None, :]
        cos += [jnp.cos(ang), jnp.cos(ang)]
        sin += [-jnp.sin(ang), jnp.sin(ang)]
    cos, sin = jnp.concatenate(cos, axis=-1), jnp.concatenate(sin, axis=-1)
    reps = LANES // HEAD_DIM
    return jnp.tile(cos, (1, reps)), jnp.tile(sin, (1, reps))


def _merge_kernel(x_ref, ya_ref, yb_ref, mod_ref, g_ref, win_ref, wba_ref, wbb_ref, wout_ref, o_ref):
    x = x_ref[...]
    mod = mod_ref[0]
    h = _modulate(x, g_ref[...], mod[3:4], mod[4:5]).astype(BF16)
    a = jax.nn.sigmoid(_dot(h, win_ref[:, QKV_COLS:QKV_COLS + D_MODEL])) * _dot(ya_ref[...], wba_ref[...])
    b = jax.nn.sigmoid(_dot(h, win_ref[:, QKV_COLS + D_MODEL:])) * _dot(yb_ref[...], wbb_ref[...])
    o_ref[...] = x + mod[5:6] * _dot((a + b).astype(BF16), wout_ref[...])


def _merge(x, ya, yb, mod, g, wgate, wba, wbb, wout, *, tokens_per_row, first_row):
    t = x.shape[0]
    tile = lambda w: pl.BlockSpec((TOKEN_TILE, w), lambda i: (i, 0))
    return pl.pallas_call(
        _merge_kernel,
        grid=(t // TOKEN_TILE,),
        in_specs=[tile(D_MODEL), tile(NA_WIDTH), tile(SWA_Q_WIDTH), _mod_spec(tokens_per_row, first_row),
                  _resident((1, D_MODEL)), _resident(wgate.shape), _resident(wba.shape), _resident(wbb.shape),
                  _resident(wout.shape)],
        out_specs=tile(D_MODEL),
        out_shape=jax.ShapeDtypeStruct((t, D_MODEL), F32),
        compiler_params=_params(),
        name="merge",
    )(x, ya, yb, mod, g, wgate, wba, wbb, wout)


def kernel(x_prompt, x_sample, cache_na_k, cache_na_v, cache_swa_k, cache_swa_v, c, c_ctx, w_ada, b_ada,
           norm_ffn1, ffn1_w_gate, ffn1_w_up, ffn1_w_down, norm_mix, w_in, na_rel_bias, swa_sink,
           w_branch_na, w_branch_swa, w_out, norm_ffn2, ffn2_w_gate, ffn2_w_up, ffn2_w_down, norm_final):
    depth = w_ada.shape[0]
    assert depth == 1
    batch, seq, _ = x_prompt.shape
    dec_batch, dec_seq, _ = x_sample.shape
    past = cache_na_k.shape[2]
    layer = 0
    row = lambda v: v.reshape(1, D_MODEL)
    bf = lambda w: w.astype(BF16)

    cond = jnp.zeros((MOD_ROWS, D_MODEL), F32).at[0].set(c_ctx).at[1:1 + dec_batch].set(c)
    mod = _adaln(cond, w_ada[layer], b_ada[layer]).reshape(MOD_ROWS, N_MOD, D_MODEL)

    ffn1 = (row(norm_ffn1[layer]), bf(ffn1_w_gate[layer]), bf(ffn1_w_up[layer]), bf(ffn1_w_down[layer]),
            row(norm_final))
    g_mix = row(norm_mix[layer])
    sink = swa_sink[layer]
    ctx_rows = dict(tokens_per_row=batch * seq, first_row=0)
    lat_rows = dict(tokens_per_row=dec_seq, first_row=1)

    later = (ffn2_w_gate[layer], ffn2_w_up[layer], ffn2_w_down[layer], w_in[layer],
             w_branch_na[layer], w_branch_swa[layer], w_out[layer])
    x_lat, *later = _ffn(x_sample.reshape(dec_batch * dec_seq, D_MODEL), mod, *ffn1, first=0, final=False,
                         cast=later, **lat_rows)
    ffn2 = (row(norm_ffn2[layer]), *later[:3], row(norm_final))
    w_qkv = later[3]
    merge_w = tuple(later[3:])
    w_kvt = bf(jnp.concatenate([w_in[layer, :, NA_WIDTH:3 * NA_WIDTH],
                                w_in[layer, :, 3 * NA_WIDTH + SWA_Q_WIDTH:QKV_COLS]], axis=1).T)

    where = ctx_rows
    x = x_prompt.reshape(batch * seq, D_MODEL)
    x = _ffn(x, mod, *ffn1, first=0, final=False, **where)
    qa, qb, *new_cache = _proj_ctx(x, mod, g_mix, w_qkv, w_kvt, seq=seq, **where)
    ya, yb = _attn_ctx(sink, qa, qb, *new_cache, seq=seq)
    x = _merge(x, ya, yb, mod, g_mix, *merge_w, **where)
    y_prompt = _ffn(x, mod, *ffn2, first=6, final=True, **where).reshape(batch, seq, D_MODEL)

    where = lat_rows
    x = x_lat
    qa, kal, val, qb, kbl, vbl = _proj_lat(x, mod, g_mix, w_qkv, _rope_tables(dec_seq), **where)
    transposed = lambda cache: jnp.transpose(cache[:, layer], (0, 2, 3, 1))
    ya = _attn_na(qa, kal, val, transposed(cache_na_k), transposed(cache_na_v),
                  _na_bias_table(na_rel_bias[layer]), n=dec_seq, ctx=past)
    yb = _attn_swa(sink, qb, kbl, vbl, transposed(cache_swa_k), transposed(cache_swa_v), n=dec_seq, ctx=past)
    x = _merge(x, ya, yb, mod, g_mix, *merge_w, **where)
    y_sample = _ffn(x, mod, *ffn2, first=6, final=True, **where).reshape(dec_batch, dec_seq, D_MODEL)

    new_cache = [jnp.transpose(t, (0, 3, 1, 2))[:, None] for t in new_cache]
    return (y_prompt, y_sample, *new_cache)
```

```python
import functools

import numpy as np
import jax
import jax.numpy as jnp
from jax import lax
from jax.experimental import pallas as pl
from jax.experimental.pallas import tpu as pltpu

F32 = jnp.float32
BF16 = jnp.bfloat16

D_MODEL = 1024
FFN_DIM = 2816
HEAD_DIM = 64
N_MOD = 9
GRID_W = 64
NA_HEADS = 8
NA_ROWS = 8
NA_COLS = 16
SWA_HEADS = 8
SWA_KV_HEADS = 2
SWA_WINDOW = 128
SWA_BLOCK = 128
ROPE_BASE = 10000.0
EPS = 1e-6
NEG_INF = -1e30
NA_WIDTH = NA_HEADS * HEAD_DIM
SWA_Q_WIDTH = SWA_HEADS * HEAD_DIM
SWA_KV_WIDTH = SWA_KV_HEADS * HEAD_DIM
QKV_COLS = 3 * NA_WIDTH + SWA_Q_WIDTH + 2 * SWA_KV_WIDTH
LANES = 128
MOD_ROWS = 16
VMEM_LIMIT = 56 * 1024 * 1024
TOKEN_TILE = 1024
FFN_TILE = 1024
FFN_CHUNK = 256
SWA_QTILE = 256
NA_TILE_ROWS = 4
LOG2E = 1.4426950408889634
Q_SCALE = HEAD_DIM ** -0.5 * LOG2E


def _dot(a, b):
    return jnp.dot(a, b, preferred_element_type=F32)


def _dot_nt(a, b):
    return lax.dot_general(a, b, (((1,), (1,)), ((), ())), preferred_element_type=F32)


def _silu(x):
    return x * jax.nn.sigmoid(x)


def _rms(x, g):
    return x * lax.rsqrt(jnp.mean(x * x, axis=-1, keepdims=True) + EPS) * g


def _modulate(x, g, shift, scale):
    return _rms(x, g) * (1.0 + scale) + shift


def _resident(shape):
    nd = len(shape)
    return pl.BlockSpec(shape, lambda *_: (0,) * nd, pipeline_mode=pl.Buffered(1))


def _params():
    return pltpu.CompilerParams(dimension_semantics=("arbitrary",), vmem_limit_bytes=VMEM_LIMIT)


def _adaln_kernel(c_ref, w_ref, b_ref, o_ref):
    s = _silu(c_ref[...]).astype(BF16)
    o_ref[...] = _dot(s, w_ref[...].astype(BF16)) + b_ref[...]


def _adaln(cond, w_ada, b_ada):
    n = w_ada.shape[1]
    blk = D_MODEL
    return pl.pallas_call(
        _adaln_kernel,
        grid=(n // blk,),
        in_specs=[pl.BlockSpec((MOD_ROWS, D_MODEL), lambda j: (0, 0)),
                  pl.BlockSpec((D_MODEL, blk), lambda j: (0, j)),
                  pl.BlockSpec((1, blk), lambda j: (0, j))],
        out_specs=pl.BlockSpec((MOD_ROWS, blk), lambda j: (0, j)),
        out_shape=jax.ShapeDtypeStruct((MOD_ROWS, n), F32),
        compiler_params=_params(),
        name="adaln",
    )(cond, w_ada, b_ada.reshape(1, n))


def _mod_spec(tokens_per_row, first_row, tile=TOKEN_TILE):
    tiles_per_row = tokens_per_row // tile
    return pl.BlockSpec((1, N_MOD, D_MODEL), lambda i: (first_row + i // tiles_per_row, 0, 0))


def _ffn_kernel(x_ref, mod_ref, g_ref, wg_ref, wu_ref, wd_ref, gf_ref, *rest, first, final):
    n_cast = (len(rest) - 2) // 2
    o_ref, a_scr = rest[n_cast], rest[-1]
    for src, dst in zip(rest[:n_cast], rest[n_cast + 1:-1]):
        dst[...] = src[...].astype(BF16)
    x = x_ref[...]
    mod = mod_ref[0]
    shift, scale, gate = mod[first:first + 1], mod[first + 1:first + 2], mod[first + 2:first + 3]
    h = _modulate(x, g_ref[...], shift, scale).astype(BF16)
    for c in range(FFN_DIM // FFN_CHUNK):
        sl = slice(c * FFN_CHUNK, (c + 1) * FFN_CHUNK)
        a_scr[:, sl] = (_silu(_dot(h, wg_ref[:, sl])) * _dot(h, wu_ref[:, sl])).astype(BF16)
    y = x + (0.5 * gate) * _dot(a_scr[...], wd_ref[...])
    if final:
        y = _rms(y, gf_ref[...])
    o_ref[...] = y


def _ffn(x, mod, g, wg, wu, wd, gf, *, tokens_per_row, first_row, first, final, cast=()):
    t = x.shape[0]
    steps = t // FFN_TILE
    tile = pl.BlockSpec((FFN_TILE, D_MODEL), lambda i: (i, 0))
    chunk = lambda w: pl.BlockSpec((w.shape[0] // steps, w.shape[1]), lambda i: (i, 0))
    assert all(w.shape[0] % (steps * 16) == 0 for w in cast)
    out = pl.pallas_call(
        functools.partial(_ffn_kernel, first=first, final=final),
        grid=(steps,),
        in_specs=[tile, _mod_spec(tokens_per_row, first_row, FFN_TILE), _resident((1, D_MODEL)),
                  _resident(wg.shape), _resident(wu.shape), _resident(wd.shape), _resident((1, D_MODEL))]
        + [chunk(w) for w in cast],
        out_specs=[tile] + [chunk(w) for w in cast],
        out_shape=[jax.ShapeDtypeStruct((t, D_MODEL), F32)] + [jax.ShapeDtypeStruct(w.shape, BF16) for w in cast],
        scratch_shapes=[pltpu.VMEM((FFN_TILE, FFN_DIM), BF16)],
        compiler_params=_params(),
        name="ffn",
    )(x, mod, g, wg, wu, wd, gf, *cast)
    return out if cast else out[0]


def _swap16(x):
    lane = lax.broadcasted_iota(jnp.int32, x.shape, 1)
    return jnp.where(lane % 32 < 16, pltpu.roll(x, LANES - 16, 1), pltpu.roll(x, 16, 1))


def _proj_ctx_kernel(x_ref, mod_ref, g_ref, w_ref, wkvt_ref, qa_ref, qb_ref, ka_ref, va_ref, kb_ref, vb_ref):
    mod = mod_ref[0]
    h = _modulate(x_ref[...], g_ref[...], mod[3:4], mod[4:5]).astype(BF16)
    qa_ref[...] = (_dot(h, w_ref[:, :NA_WIDTH]) * Q_SCALE).astype(qa_ref.dtype)
    qb_ref[...] = (_dot(h, w_ref[:, 3 * NA_WIDTH:3 * NA_WIDTH + SWA_Q_WIDTH]) * Q_SCALE).astype(qb_ref.dtype)
    kvt = _dot_nt(wkvt_ref[...], h)
    requests, _, _, seq = ka_ref.shape
    o = 0
    for ref in (ka_ref, va_ref, kb_ref, vb_ref):
        width = ref.shape[1] * HEAD_DIM
        for r in range(requests):
            ref[r] = kvt[o:o + width, r * seq:(r + 1) * seq].reshape(ref.shape[1:])
        o += width


def _rope(y, cos, sin):
    blocks = [y[:, j * LANES:(j + 1) * LANES] for j in range(y.shape[1] // LANES)]
    return [b * cos + _swap16(b) * sin for b in blocks]


def _proj_lat_kernel(x_ref, mod_ref, g_ref, w_ref, cos_ref, sin_ref,
                     qa_ref, ka_ref, va_ref, qb_ref, kb_ref, vb_ref):
    mod = mod_ref[0]
    h = _modulate(x_ref[...], g_ref[...], mod[3:4], mod[4:5]).astype(BF16)
    cos, sin = cos_ref[...], sin_ref[...]
    o = 3 * NA_WIDTH
    for j, b in enumerate(_rope(_dot(h, w_ref[:, o:o + SWA_Q_WIDTH]), cos, sin)):
        qb_ref[:, j * LANES:(j + 1) * LANES] = (b * Q_SCALE).astype(qb_ref.dtype)
    o += SWA_Q_WIDTH
    y = _dot(h, w_ref[:, o:o + 2 * SWA_KV_WIDTH])
    kb_ref[...] = _rope(y[:, :SWA_KV_WIDTH], cos, sin)[0].astype(kb_ref.dtype)
    vb_ref[...] = y[:, SWA_KV_WIDTH:].astype(vb_ref.dtype)
    o = 0
    for ref, scale in ((qa_ref, True), (ka_ref, False), (va_ref, False)):
        y = _dot(h, w_ref[:, o:o + NA_WIDTH])
        ref[...] = (y * Q_SCALE if scale else y).astype(ref.dtype)
        o += NA_WIDTH


def _proj_ctx(x, mod, g, w_qkv, w_kvt, *, seq, tokens_per_row, first_row):
    t = x.shape[0]
    tile = lambda w: pl.BlockSpec((TOKEN_TILE, w), lambda i: (i, 0))
    cache = lambda heads: pl.BlockSpec((TOKEN_TILE // seq, heads, HEAD_DIM, seq), lambda i: (i, 0, 0, 0))
    cache_shape = lambda heads: jax.ShapeDtypeStruct((t // seq, heads, HEAD_DIM, seq), F32)
    kv_heads = (NA_HEADS, NA_HEADS, SWA_KV_HEADS, SWA_KV_HEADS)
    return pl.pallas_call(
        _proj_ctx_kernel,
        grid=(t // TOKEN_TILE,),
        in_specs=[tile(D_MODEL), _mod_spec(tokens_per_row, first_row), _resident((1, D_MODEL)),
                  _resident(w_qkv.shape), _resident(w_kvt.shape)],
        out_specs=[tile(NA_WIDTH), tile(SWA_Q_WIDTH)] + [cache(nh) for nh in kv_heads],
        out_shape=[jax.ShapeDtypeStruct((t, NA_WIDTH), BF16), jax.ShapeDtypeStruct((t, SWA_Q_WIDTH), BF16)]
        + [cache_shape(nh) for nh in kv_heads],
        compiler_params=_params(),
        name="proj_ctx",
    )(x, mod, g, w_qkv, w_kvt)


def _proj_lat(x, mod, g, w_qkv, rope, *, tokens_per_row, first_row):
    t = x.shape[0]
    tile = lambda w: pl.BlockSpec((TOKEN_TILE, w), lambda i: (i, 0))
    widths = (NA_WIDTH, NA_WIDTH, NA_WIDTH, SWA_Q_WIDTH, SWA_KV_WIDTH, SWA_KV_WIDTH)
    tiles_per_row = tokens_per_row // TOKEN_TILE
    return pl.pallas_call(
        _proj_lat_kernel,
        grid=(t // TOKEN_TILE,),
        in_specs=[tile(D_MODEL), _mod_spec(tokens_per_row, first_row), _resident((1, D_MODEL)),
                  _resident(w_qkv.shape)]
        + [pl.BlockSpec((TOKEN_TILE, LANES), lambda i: (i % tiles_per_row, 0))] * 2,
        out_specs=[tile(w) for w in widths],
        out_shape=[jax.ShapeDtypeStruct((t, w), BF16) for w in widths],
        compiler_params=_params(),
        name="proj_lat",
    )(x, mod, g, w_qkv, *rope)


def _lane_halves(shape):
    lane = lax.broadcasted_iota(jnp.int32, shape, 1) % LANES
    return lane < HEAD_DIM, lane >= HEAD_DIM


def _softmax_cols(cols, sink=None):
    m = jnp.max(functools.reduce(jnp.maximum, cols), axis=-1, keepdims=True)
    if sink is not None:
        m = jnp.maximum(m, sink)
    es = [jnp.exp2(c - m) for c in cols]
    l = jnp.sum(functools.reduce(jnp.add, es), axis=-1, keepdims=True)
    if sink is not None:
        l = l + jnp.exp2(sink - m)
    return es, l


def _cols(x):
    return [x[:, j * LANES:(j + 1) * LANES] for j in range(x.shape[1] // LANES)]


def _attn_ctx_kernel(sink_ref, qa_ref, qb_ref, ka_ref, va_ref, kb_ref, vb_ref, ya_ref, yb_ref):
    seq = qa_ref.shape[0]
    lo, hi = _lane_halves((seq, LANES))
    top = lax.broadcasted_iota(jnp.int32, (2 * seq, 1), 0) < seq
    zero = jnp.zeros((), BF16)

    def heads_of_block(q, kt, vt, sink):
        q = jnp.concatenate([jnp.where(lo, q, zero), jnp.where(hi, q, zero)], axis=0)
        es, l = _softmax_cols(_cols(_dot(q, kt)), sink)
        o = _dot_nt(jnp.concatenate(es, axis=1).astype(BF16), vt) * (1.0 / l)
        return jnp.where(lo, o[:seq], o[seq:])

    for p in range(NA_HEADS // 2):
        sl = slice(p * LANES, (p + 1) * LANES)
        kt = ka_ref[2 * p:2 * p + 2].reshape(LANES, seq).astype(BF16)
        vt = va_ref[2 * p:2 * p + 2].reshape(LANES, seq).astype(BF16)
        ya_ref[:, sl] = heads_of_block(qa_ref[:, sl], kt, vt, None).astype(ya_ref.dtype)
    for p in range(SWA_HEADS // 2):
        sl = slice(p * LANES, (p + 1) * LANES)
        kv = (2 * p) // (SWA_HEADS // SWA_KV_HEADS)
        kt, vt = kb_ref[kv].astype(BF16), vb_ref[kv].astype(BF16)
        kt, vt = jnp.concatenate([kt, kt], axis=0), jnp.concatenate([vt, vt], axis=0)
        sink = jnp.where(top, sink_ref[2 * p], sink_ref[2 * p + 1]) * LOG2E
        yb_ref[:, sl] = heads_of_block(qb_ref[:, sl], kt, vt, sink).astype(yb_ref.dtype)


def _attn_ctx(sink, qa, qb, ka, va, kb, vb, *, seq):
    t = qa.shape[0]
    tile = lambda w: pl.BlockSpec((seq, w), lambda b: (b, 0))
    cache = lambda heads: pl.BlockSpec((None, heads, HEAD_DIM, seq), lambda b: (b, 0, 0, 0))
    return pl.pallas_call(
        _attn_ctx_kernel,
        grid=(t // seq,),
        in_specs=[pl.BlockSpec(memory_space=pltpu.SMEM), tile(NA_WIDTH), tile(SWA_Q_WIDTH),
                  cache(NA_HEADS), cache(NA_HEADS), cache(SWA_KV_HEADS), cache(SWA_KV_HEADS)],
        out_specs=[tile(NA_WIDTH), tile(SWA_Q_WIDTH)],
        out_shape=[jax.ShapeDtypeStruct((t, NA_WIDTH), BF16), jax.ShapeDtypeStruct((t, SWA_Q_WIDTH), BF16)],
        compiler_params=_params(),
        name="attn_ctx",
    )(sink, qa, qb, ka, va, kb, vb)


def _attn_na_kernel(q_ref, k_ref, v_ref, ck_ref, cv_ref, bias_ref, y_ref, ck_scr, cv_scr, *, n, ctx):
    rows = n // GRID_W
    tile_q = NA_TILE_ROWS * GRID_W
    zero = jnp.zeros((), BF16)
    ck_scr[...] = ck_ref[...].reshape(NA_WIDTH, ctx).astype(BF16)
    cv_scr[...] = cv_ref[...].reshape(NA_WIDTH, ctx).astype(BF16)
    lo, hi = _lane_halves((tile_q, LANES))
    left_half = lax.broadcasted_iota(jnp.int32, (GRID_W, LANES), 1) < GRID_W
    empty = jnp.zeros((GRID_W, LANES), BF16)

    def tile(q0, k0, key_rows, lead, first):
        pairs = key_rows // 2
        for p in range(NA_HEADS // 2):
            sl = slice(p * LANES, (p + 1) * LANES)
            q = q_ref[pl.ds(q0, tile_q), sl]
            q = jnp.concatenate([jnp.where(lo, q, zero), jnp.where(hi, q, zero)], axis=0)
            s_nb = _dot_nt(q, k_ref[pl.ds(k0, key_rows * GRID_W), sl])
            s_ctx = _dot(q, ck_scr[sl, :])
            e_nb, e_ctx, ls = [], [], []
            for idx in range(2):
                for a in range(NA_TILE_ROWS):
                    qa = slice(idx * tile_q + a * GRID_W, idx * tile_q + (a + 1) * GRID_W)
                    inside = lambda i: first[a] <= i < first[a] + NA_ROWS
                    cols, where = [], []
                    for m in range(pairs):
                        if not (inside(2 * m) or inside(2 * m + 1)):
                            continue
                        blk = s_nb[qa, m * LANES:(m + 1) * LANES] + bias_ref[2 * p + idx, 2 * m - a - lead + NA_ROWS]
                        if not inside(2 * m + 1):
                            blk = jnp.where(left_half, blk, NEG_INF)
                        elif not inside(2 * m):
                            blk = jnp.where(left_half, NEG_INF, blk)
                        cols.append(blk)
                        where.append(m)
                    cols += [s_ctx[qa, j * LANES:(j + 1) * LANES] for j in range(ctx // LANES)]
                    es, l = _softmax_cols(cols)
                    es = [e.astype(BF16) for e in es]
                    e_nb.append(jnp.concatenate(
                        [es[where.index(m)] if m in where else empty for m in range(pairs)], axis=1))
                    e_ctx.append(jnp.concatenate(es[len(where):], axis=1))
                    ls.append(l)
            o = (_dot(jnp.concatenate(e_nb, axis=0), v_ref[pl.ds(k0, key_rows * GRID_W), sl])
                 + _dot_nt(jnp.concatenate(e_ctx, axis=0), cv_scr[sl, :]))
            o = o * (1.0 / jnp.concatenate(ls, axis=0))
            y_ref[pl.ds(q0, tile_q), sl] = jnp.where(lo, o[:tile_q], o[tile_q:]).astype(y_ref.dtype)

    half = NA_ROWS // 2
    tile(0, 0, NA_ROWS, 0, (0,) * NA_TILE_ROWS)

    for t in range(1, rows // NA_TILE_ROWS - 1):
        tile(t * tile_q, t * tile_q - half * GRID_W, NA_ROWS + NA_TILE_ROWS, half, tuple(range(NA_TILE_ROWS)))
    tile(n - tile_q, n - NA_ROWS * GRID_W, NA_ROWS, half, (0,) * NA_TILE_ROWS)


def _attn_na(q, k, v, ck, cv, bias, *, n, ctx):
    t = q.shape[0]
    tile = pl.BlockSpec((n, NA_WIDTH), lambda b: (b, 0))
    ctile = pl.BlockSpec((None, NA_HEADS, HEAD_DIM, ctx), lambda b: (b, 0, 0, 0))
    return pl.pallas_call(
        functools.partial(_attn_na_kernel, n=n, ctx=ctx),
        grid=(t // n,),
        in_specs=[tile, tile, tile, ctile, ctile, _resident(bias.shape)],
        out_specs=tile,
        out_shape=jax.ShapeDtypeStruct((t, NA_WIDTH), BF16),
        scratch_shapes=[pltpu.VMEM((NA_WIDTH, ctx), BF16)] * 2,
        compiler_params=_params(),
        name="attn_na",
    )(q, k, v, ck, cv, bias)


def _na_bias_table(rel_bias):
    rows = _na_bias_rows(rel_bias)
    shape = (NA_HEADS, 2 * NA_ROWS, GRID_W, LANES)
    return pl.pallas_call(
        _fill_na_bias,
        grid=(1,),
        in_specs=[pl.BlockSpec(rows.shape, lambda i: (0, 0, 0))],
        out_specs=pl.BlockSpec(shape, lambda i: (0, 0, 0, 0)),
        out_shape=jax.ShapeDtypeStruct(shape, F32),
        compiler_params=_params(),
        name="na_bias",
    )(rows)


def _na_bias_rows(rel_bias):
    side = GRID_W - NA_COLS
    z = jnp.pad(rel_bias * LOG2E, ((0, 0), (0, 0), (side, side + 1)))
    return jnp.pad(z, ((0, 0), (1, 1), (0, 0)), constant_values=NEG_INF)


def _fill_na_bias(rows_ref, bias_scr):
    q = lax.broadcasted_iota(jnp.int32, (GRID_W, LANES), 0)
    lane = lax.broadcasted_iota(jnp.int32, (GRID_W, LANES), 1)
    kc = lane % GRID_W
    col_start = jnp.clip(q - NA_COLS // 2, 0, GRID_W - NA_COLS)
    in_window = (kc >= col_start) & (kc < col_start + NA_COLS)
    for h in range(NA_HEADS):
        blocks = []
        for j in range(2 * NA_ROWS + 1):
            row = jnp.broadcast_to(rows_ref[h, j:j + 1, :], (GRID_W, LANES))
            rolled = pltpu.roll(row, LANES - GRID_W + 1, 1, stride=1, stride_axis=0)
            blocks.append(jnp.where(in_window, rolled, NEG_INF))
        for j in range(2 * NA_ROWS):
            bias_scr[h, j] = jnp.where(lane < GRID_W, blocks[j], pltpu.roll(blocks[j + 1], GRID_W, 1))


def _attn_swa_kernel(sink_ref, q_ref, k_ref, v_ref, ck_ref, cv_ref, y_ref, k_scr, v_scr, ck_scr, cv_scr, *, n, ctx):
    group = SWA_HEADS // SWA_KV_HEADS
    band = SWA_QTILE + 2 * SWA_BLOCK
    for src, csrc, dst, cdst in ((k_ref, ck_ref, k_scr, ck_scr), (v_ref, cv_ref, v_scr, cv_scr)):
        x = src[...].astype(F32)
        xr = pltpu.roll(x, HEAD_DIM, 1)
        first_half, second_half = _lane_halves(x.shape)
        dst[0] = jnp.where(first_half, x, xr).astype(BF16)
        dst[1] = jnp.where(second_half, x, xr).astype(BF16)
        for kv in range(SWA_KV_HEADS):
            c = csrc[kv].astype(BF16)
            cdst[kv] = jnp.concatenate([c, c], axis=0)

    qi = lax.broadcasted_iota(jnp.int32, (SWA_QTILE, band), 0)
    kj = lax.broadcasted_iota(jnp.int32, (SWA_QTILE, band), 1)
    lo, hi = _lane_halves((SWA_QTILE, LANES))
    top = lax.broadcasted_iota(jnp.int32, (2 * SWA_QTILE, 1), 0) < SWA_QTILE
    zero = jnp.zeros((), BF16)

    def block_body(b, carry):
        q0 = pl.multiple_of(b * SWA_QTILE, SWA_QTILE)
        k0 = pl.multiple_of(jnp.clip(q0 - SWA_BLOCK, 0, n - band), SWA_BLOCK)
        mask = jnp.where(jnp.abs(kj - qi + (k0 - q0)) <= SWA_WINDOW, 0.0, NEG_INF)
        mask2 = jnp.concatenate([mask, mask], axis=0)
        for p in range(SWA_HEADS // 2):
            kv = 2 * p // group
            sl = slice(p * LANES, (p + 1) * LANES)
            q = q_ref[pl.ds(q0, SWA_QTILE), sl]
            q = jnp.concatenate([jnp.where(lo, q, zero), jnp.where(hi, q, zero)], axis=0)
            s_band = _dot_nt(q, k_scr[kv, pl.ds(k0, band), :])
            s_ctx = _dot(q, ck_scr[kv])
            sink = jnp.where(top, sink_ref[2 * p], sink_ref[2 * p + 1]) * LOG2E
            es, l = _softmax_cols(_cols(s_band + mask2) + _cols(s_ctx), sink)
            e_band = jnp.concatenate(es[:band // LANES], axis=1).astype(BF16)
            e_ctx = jnp.concatenate(es[band // LANES:], axis=1).astype(BF16)
            o = (_dot(e_band, v_scr[kv, pl.ds(k0, band), :]) + _dot_nt(e_ctx, cv_scr[kv])) * (1.0 / l)
            y_ref[pl.ds(q0, SWA_QTILE), sl] = jnp.where(lo, o[:SWA_QTILE], o[SWA_QTILE:]).astype(y_ref.dtype)
        return carry

    lax.fori_loop(0, n // SWA_QTILE, block_body, 0, unroll=4)


def _attn_swa(sink, q, k, v, ck, cv, *, n, ctx):
    t = q.shape[0]
    tile = lambda w: pl.BlockSpec((n, w), lambda b: (b, 0))
    ctile = pl.BlockSpec((None, SWA_KV_HEADS, HEAD_DIM, ctx), lambda b: (b, 0, 0, 0))
    return pl.pallas_call(
        functools.partial(_attn_swa_kernel, n=n, ctx=ctx),
        grid=(t // n,),
        in_specs=[pl.BlockSpec(memory_space=pltpu.SMEM), tile(SWA_Q_WIDTH), tile(SWA_KV_WIDTH),
                  tile(SWA_KV_WIDTH), ctile, ctile],
        out_specs=tile(SWA_Q_WIDTH),
        out_shape=jax.ShapeDtypeStruct((t, SWA_Q_WIDTH), BF16),
        scratch_shapes=[pltpu.VMEM((SWA_KV_HEADS, n, SWA_KV_WIDTH), BF16)] * 2
        + [pltpu.VMEM((SWA_KV_HEADS, LANES, ctx), BF16)] * 2,
        compiler_params=_params(),
        name="attn_swa",
    )(sink, q, k, v, ck, cv)


def _rope_tables(n):
    half = HEAD_DIM // 4
    freqs = jnp.power(ROPE_BASE, -jnp.arange(half, dtype=F32) / half)
    t = jnp.arange(n)
    cos, sin = [], []
    for pos in (t // GRID_W, t % GRID_W):
        ang = pos.astype(F32)[:, None] * freqs[None, :]
        cos += [jnp.cos(ang), jnp.cos(ang)]
        sin += [-jnp.sin(ang), jnp.sin(ang)]
    cos, sin = jnp.concatenate(cos, axis=-1), jnp.concatenate(sin, axis=-1)
    reps = LANES // HEAD_DIM
    return jnp.tile(cos, (1, reps)), jnp.tile(sin, (1, reps))


def _merge_kernel(x_ref, ya_ref, yb_ref, mod_ref, g_ref, win_ref, wba_ref, wbb_ref, wout_ref, o_ref):
    x = x_ref[...]
    mod = mod_ref[0]
    h = _modulate(x, g_ref[...], mod[3:4], mod[4:5]).astype(BF16)
    a = jax.nn.sigmoid(_dot(h, win_ref[:, QKV_COLS:QKV_COLS + D_MODEL])) * _dot(ya_ref[...], wba_ref[...])
    b = jax.nn.sigmoid(_dot(h, win_ref[:, QKV_COLS + D_MODEL:])) * _dot(yb_ref[...], wbb_ref[...])
    o_ref[...] = x + mod[5:6] * _dot((a + b).astype(BF16), wout_ref[...])


def _merge(x, ya, yb, mod, g, wgate, wba, wbb, wout, *, tokens_per_row, first_row):
    t = x.shape[0]
    tile = lambda w: pl.BlockSpec((TOKEN_TILE, w), lambda i: (i, 0))
    return pl.pallas_call(
        _merge_kernel,
        grid=(t // TOKEN_TILE,),
        in_specs=[tile(D_MODEL), tile(NA_WIDTH), tile(SWA_Q_WIDTH), _mod_spec(tokens_per_row, first_row),
                  _resident((1, D_MODEL)), _resident(wgate.shape), _resident(wba.shape), _resident(wbb.shape),
                  _resident(wout.shape)],
        out_specs=tile(D_MODEL),
        out_shape=jax.ShapeDtypeStruct((t, D_MODEL), F32),
        compiler_params=_params(),
        name="merge",
    )(x, ya, yb, mod, g, wgate, wba, wbb, wout)


def kernel(x_prompt, x_sample, cache_na_k, cache_na_v, cache_swa_k, cache_swa_v, c, c_ctx, w_ada, b_ada,
           norm_ffn1, ffn1_w_gate, ffn1_w_up, ffn1_w_down, norm_mix, w_in, na_rel_bias, swa_sink,
           w_branch_na, w_branch_swa, w_out, norm_ffn2, ffn2_w_gate, ffn2_w_up, ffn2_w_down, norm_final):
    depth = w_ada.shape[0]
    assert depth == 1
    batch, seq, _ = x_prompt.shape
    dec_batch, dec_seq, _ = x_sample.shape
    past = cache_na_k.shape[2]
    layer = 0
    row = lambda v: v.reshape(1, D_MODEL)
    bf = lambda w: w.astype(BF16)

    cond = jnp.zeros((MOD_ROWS, D_MODEL), F32).at[0].set(c_ctx).at[1:1 + dec_batch].set(c)
    mod = _adaln(cond, w_ada[layer], b_ada[layer]).reshape(MOD_ROWS, N_MOD, D_MODEL)

    ffn1 = (row(norm_ffn1[layer]), bf(ffn1_w_gate[layer]), bf(ffn1_w_up[layer]), bf(ffn1_w_down[layer]),
            row(norm_final))
    g_mix = row(norm_mix[layer])
    sink = swa_sink[layer]
    ctx_rows = dict(tokens_per_row=batch * seq, first_row=0)
    lat_rows = dict(tokens_per_row=dec_seq, first_row=1)

    later = (ffn2_w_gate[layer], ffn2_w_up[layer], ffn2_w_down[layer], w_in[layer],
             w_branch_na[layer], w_branch_swa[layer], w_out[layer])
    x_lat, *later = _ffn(x_sample.reshape(dec_batch * dec_seq, D_MODEL), mod, *ffn1, first=0, final=False,
                         cast=later, **lat_rows)
    ffn2 = (row(norm_ffn2[layer]), *later[:3], row(norm_final))
    w_qkv = later[3]
    merge_w = tuple(later[3:])
    w_kvt = bf(jnp.concatenate([w_in[layer, :, NA_WIDTH:3 * NA_WIDTH],
                                w_in[layer, :, 3 * NA_WIDTH + SWA_Q_WIDTH:QKV_COLS]], axis=1).T)

    where = ctx_rows
    x = x_prompt.reshape(batch * seq, D_MODEL)
    x = _ffn(x, mod, *ffn1, first=0, final=False, **where)
    qa, qb, *new_cache = _proj_ctx(x, mod, g_mix, w_qkv, w_kvt, seq=seq, **where)
    ya, yb = _attn_ctx(sink, qa, qb, *new_cache, seq=seq)
    x = _merge(x, ya, yb, mod, g_mix, *merge_w, **where)
    y_prompt = _ffn(x, mod, *ffn2, first=6, final=True, **where).reshape(batch, seq, D_MODEL)

    where = lat_rows
    x = x_lat
    qa, kal, val, qb, kbl, vbl = _proj_lat(x, mod, g_mix, w_qkv, _rope_tables(dec_seq), **where)
    transposed = lambda cache: jnp.transpose(cache[:, layer], (0, 2, 3, 1))
    ya = _attn_na(qa, kal, val, transposed(cache_na_k), transposed(cache_na_v),
                  _na_bias_table(na_rel_bias[layer]), n=dec_seq, ctx=past)
    yb = _attn_swa(sink, qb, kbl, vbl, transposed(cache_swa_k), transposed(cache_swa_v), n=dec_seq, ctx=past)
    x = _merge(x, ya, yb, mod, g_mix, *merge_w, **where)
    y_sample = _ffn(x, mod, *ffn2, first=6, final=True, **where).reshape(dec_batch, dec_seq, D_MODEL)

    new_cache = [jnp.transpose(t, (0, 3, 1, 2))[:, None] for t in new_cache]
    return (y_prompt, y_sample, *new_cache)
```

```python
import functools

import numpy as np
import jax
import jax.numpy as jnp
from jax import lax
from jax.experimental import pallas as pl
from jax.experimental.pallas import tpu as pltpu

F32 = jnp.float32
BF16 = jnp.bfloat16

D_MODEL = 1024
FFN_DIM = 2816
HEAD_DIM = 64
N_MOD = 9
GRID_W = 64
NA_HEADS = 8
NA_ROWS = 8
NA_COLS = 16
SWA_HEADS = 8
SWA_KV_HEADS = 2
SWA_WINDOW = 128
SWA_BLOCK = 128
ROPE_BASE = 10000.0
EPS = 1e-6
NEG_INF = -1e30
NA_WIDTH = NA_HEADS * HEAD_DIM
SWA_Q_WIDTH = SWA_HEADS * HEAD_DIM
SWA_KV_WIDTH = SWA_KV_HEADS * HEAD_DIM
QKV_COLS = 3 * NA_WIDTH + SWA_Q_WIDTH + 2 * SWA_KV_WIDTH
LANES = 128
MOD_ROWS = 16
VMEM_LIMIT = 56 * 1024 * 1024
TOKEN_TILE = 1024
FFN_TILE = 1024
FFN_CHUNK = 256
ADALN_BLOCK = 2304
SWA_QTILE = 256
NA_TILE_ROWS = 4
LOG2E = 1.4426950408889634
Q_SCALE = HEAD_DIM ** -0.5 * LOG2E


def _dot(a, b):
    return jnp.dot(a, b, preferred_element_type=F32)


def _dot_nt(a, b):
    return lax.dot_general(a, b, (((1,), (1,)), ((), ())), preferred_element_type=F32)


def _silu(x):
    return x * jax.nn.sigmoid(x)


def _rms(x, g):
    return x * lax.rsqrt(jnp.mean(x * x, axis=-1, keepdims=True) + EPS) * g


def _modulate(x, g, shift, scale):
    return _rms(x, g) * (1.0 + scale) + shift


def _resident(shape):
    nd = len(shape)
    return pl.BlockSpec(shape, lambda *_: (0,) * nd, pipeline_mode=pl.Buffered(1))


def _params():
    return pltpu.CompilerParams(dimension_semantics=("arbitrary",), vmem_limit_bytes=VMEM_LIMIT)


def _adaln_kernel(c_ref, w_ref, b_ref, o_ref):
    s = _silu(c_ref[...]).astype(BF16)
    o_ref[...] = _dot(s, w_ref[...].astype(BF16)) + b_ref[...]


def _adaln(cond, w_ada, b_ada):
    n = w_ada.shape[1]
    blk = ADALN_BLOCK
    return pl.pallas_call(
        _adaln_kernel,
        grid=(n // blk,),
        in_specs=[pl.BlockSpec((MOD_ROWS, D_MODEL), lambda j: (0, 0)),
                  pl.BlockSpec((D_MODEL, blk), lambda j: (0, j)),
                  pl.BlockSpec((1, blk), lambda j: (0, j))],
        out_specs=pl.BlockSpec((MOD_ROWS, blk), lambda j: (0, j)),
        out_shape=jax.ShapeDtypeStruct((MOD_ROWS, n), F32),
        compiler_params=_params(),
        name="adaln",
    )(cond, w_ada, b_ada.reshape(1, n))


def _mod_spec(tokens_per_row, first_row, tile=TOKEN_TILE):
    tiles_per_row = tokens_per_row // tile
    return pl.BlockSpec((1, N_MOD, D_MODEL), lambda i: (first_row + i // tiles_per_row, 0, 0))


def _ffn_kernel(x_ref, mod_ref, g_ref, wg_ref, wu_ref, wd_ref, gf_ref, *rest, first, final):
    n_cast = (len(rest) - 2) // 2
    o_ref, a_scr = rest[n_cast], rest[-1]
    for src, dst in zip(rest[:n_cast], rest[n_cast + 1:-1]):
        dst[...] = src[...].astype(BF16)
    x = x_ref[...]
    mod = mod_ref[0]
    shift, scale, gate = mod[first:first + 1], mod[first + 1:first + 2], mod[first + 2:first + 3]
    h = _modulate(x, g_ref[...], shift, scale).astype(BF16)
    for c in range(FFN_DIM // FFN_CHUNK):
        sl = slice(c * FFN_CHUNK, (c + 1) * FFN_CHUNK)
        a_scr[:, sl] = (_silu(_dot(h, wg_ref[:, sl])) * _dot(h, wu_ref[:, sl])).astype(BF16)
    y = x + (0.5 * gate) * _dot(a_scr[...], wd_ref[...])
    if final:
        y = _rms(y, gf_ref[...])
    o_ref[...] = y


def _ffn(x, mod, g, wg, wu, wd, gf, *, tokens_per_row, first_row, first, final, cast=()):
    t = x.shape[0]
    steps = t // FFN_TILE
    tile = pl.BlockSpec((FFN_TILE, D_MODEL), lambda i: (i, 0))
    chunk = lambda w: pl.BlockSpec((w.shape[0] // steps, w.shape[1]), lambda i: (i, 0))
    assert all(w.shape[0] % (steps * 16) == 0 for w in cast)
    out = pl.pallas_call(
        functools.partial(_ffn_kernel, first=first, final=final),
        grid=(steps,),
        in_specs=[tile, _mod_spec(tokens_per_row, first_row, FFN_TILE), _resident((1, D_MODEL)),
                  _resident(wg.shape), _resident(wu.shape), _resident(wd.shape), _resident((1, D_MODEL))]
        + [chunk(w) for w in cast],
        out_specs=[tile] + [chunk(w) for w in cast],
        out_shape=[jax.ShapeDtypeStruct((t, D_MODEL), F32)] + [jax.ShapeDtypeStruct(w.shape, BF16) for w in cast],
        scratch_shapes=[pltpu.VMEM((FFN_TILE, FFN_DIM), BF16)],
        compiler_params=_params(),
        name="ffn",
    )(x, mod, g, wg, wu, wd, gf, *cast)
    return out if cast else out[0]


def _swap16(x):
    lane = lax.broadcasted_iota(jnp.int32, x.shape, 1)
    return jnp.where(lane % 32 < 16, pltpu.roll(x, LANES - 16, 1), pltpu.roll(x, 16, 1))


def _proj_ctx_kernel(x_ref, mod_ref, g_ref, w_ref, wkvt_ref, qa_ref, qb_ref, ka_ref, va_ref, kb_ref, vb_ref):
    mod = mod_ref[0]
    h = _modulate(x_ref[...], g_ref[...], mod[3:4], mod[4:5]).astype(BF16)
    qa_ref[...] = (_dot(h, w_ref[:, :NA_WIDTH]) * Q_SCALE).astype(qa_ref.dtype)
    qb_ref[...] = (_dot(h, w_ref[:, 3 * NA_WIDTH:3 * NA_WIDTH + SWA_Q_WIDTH]) * Q_SCALE).astype(qb_ref.dtype)
    kvt = _dot_nt(wkvt_ref[...], h)
    requests, _, _, seq = ka_ref.shape
    o = 0
    for ref in (ka_ref, va_ref, kb_ref, vb_ref):
        width = ref.shape[1] * HEAD_DIM
        for r in range(requests):
            ref[r] = kvt[o:o + width, r * seq:(r + 1) * seq].reshape(ref.shape[1:])
        o += width


def _rope(y, cos, sin):
    blocks = [y[:, j * LANES:(j + 1) * LANES] for j in range(y.shape[1] // LANES)]
    return [b * cos + _swap16(b) * sin for b in blocks]


def _proj_lat_kernel(x_ref, mod_ref, g_ref, w_ref, cos_ref, sin_ref,
                     qa_ref, ka_ref, va_ref, qb_ref, kb_ref, vb_ref):
    mod = mod_ref[0]
    h = _modulate(x_ref[...], g_ref[...], mod[3:4], mod[4:5]).astype(BF16)
    cos, sin = cos_ref[...], sin_ref[...]
    o = 3 * NA_WIDTH
    for j, b in enumerate(_rope(_dot(h, w_ref[:, o:o + SWA_Q_WIDTH]), cos, sin)):
        qb_ref[:, j * LANES:(j + 1) * LANES] = (b * Q_SCALE).astype(qb_ref.dtype)
    o += SWA_Q_WIDTH
    y = _dot(h, w_ref[:, o:o + 2 * SWA_KV_WIDTH])
    kb_ref[...] = _rope(y[:, :SWA_KV_WIDTH], cos, sin)[0].astype(kb_ref.dtype)
    vb_ref[...] = y[:, SWA_KV_WIDTH:].astype(vb_ref.dtype)
    o = 0
    for ref, scale in ((qa_ref, True), (ka_ref, False), (va_ref, False)):
        y = _dot(h, w_ref[:, o:o + NA_WIDTH])
        ref[...] = (y * Q_SCALE if scale else y).astype(ref.dtype)
        o += NA_WIDTH


def _proj_ctx(x, mod, g, w_qkv, w_kvt, *, seq, tokens_per_row, first_row):
    t = x.shape[0]
    tile = lambda w: pl.BlockSpec((TOKEN_TILE, w), lambda i: (i, 0))
    cache = lambda heads: pl.BlockSpec((TOKEN_TILE // seq, heads, HEAD_DIM, seq), lambda i: (i, 0, 0, 0))
    cache_shape = lambda heads: jax.ShapeDtypeStruct((t // seq, heads, HEAD_DIM, seq), F32)
    kv_heads = (NA_HEADS, NA_HEADS, SWA_KV_HEADS, SWA_KV_HEADS)
    return pl.pallas_call(
        _proj_ctx_kernel,
        grid=(t // TOKEN_TILE,),
        in_specs=[tile(D_MODEL), _mod_spec(tokens_per_row, first_row), _resident((1, D_MODEL)),
                  _resident(w_qkv.shape), _resident(w_kvt.shape)],
        out_specs=[tile(NA_WIDTH), tile(SWA_Q_WIDTH)] + [cache(nh) for nh in kv_heads],
        out_shape=[jax.ShapeDtypeStruct((t, NA_WIDTH), BF16), jax.ShapeDtypeStruct((t, SWA_Q_WIDTH), BF16)]
        + [cache_shape(nh) for nh in kv_heads],
        compiler_params=_params(),
        name="proj_ctx",
    )(x, mod, g, w_qkv, w_kvt)


def _proj_lat(x, mod, g, w_qkv, rope, *, tokens_per_row, first_row):
    t = x.shape[0]
    tile = lambda w: pl.BlockSpec((TOKEN_TILE, w), lambda i: (i, 0))
    widths = (NA_WIDTH, NA_WIDTH, NA_WIDTH, SWA_Q_WIDTH, SWA_KV_WIDTH, SWA_KV_WIDTH)
    tiles_per_row = tokens_per_row // TOKEN_TILE
    return pl.pallas_call(
        _proj_lat_kernel,
        grid=(t // TOKEN_TILE,),
        in_specs=[tile(D_MODEL), _mod_spec(tokens_per_row, first_row), _resident((1, D_MODEL)),
                  _resident(w_qkv.shape)]
        + [pl.BlockSpec((TOKEN_TILE, LANES), lambda i: (i % tiles_per_row, 0))] * 2,
        out_specs=[tile(w) for w in widths],
        out_shape=[jax.ShapeDtypeStruct((t, w), BF16) for w in widths],
        compiler_params=_params(),
        name="proj_lat",
    )(x, mod, g, w_qkv, *rope)


def _lane_halves(shape):
    lane = lax.broadcasted_iota(jnp.int32, shape, 1) % LANES
    return lane < HEAD_DIM, lane >= HEAD_DIM


def _softmax_cols(cols, sink=None):
    m = jnp.max(functools.reduce(jnp.maximum, cols), axis=-1, keepdims=True)
    if sink is not None:
        m = jnp.maximum(m, sink)
    es = [jnp.exp2(c - m) for c in cols]
    l = jnp.sum(functools.reduce(jnp.add, es), axis=-1, keepdims=True)
    if sink is not None:
        l = l + jnp.exp2(sink - m)
    return es, l


def _cols(x):
    return [x[:, j * LANES:(j + 1) * LANES] for j in range(x.shape[1] // LANES)]


def _attn_ctx_kernel(sink_ref, qa_ref, qb_ref, ka_ref, va_ref, kb_ref, vb_ref, ya_ref, yb_ref):
    seq = qa_ref.shape[0]
    lo, hi = _lane_halves((seq, LANES))
    top = lax.broadcasted_iota(jnp.int32, (2 * seq, 1), 0) < seq
    zero = jnp.zeros((), BF16)

    def heads_of_block(q, kt, vt, sink):
        q = jnp.concatenate([jnp.where(lo, q, zero), jnp.where(hi, q, zero)], axis=0)
        es, l = _softmax_cols(_cols(_dot(q, kt)), sink)
        o = _dot_nt(jnp.concatenate(es, axis=1).astype(BF16), vt) * (1.0 / l)
        return jnp.where(lo, o[:seq], o[seq:])

    for p in range(NA_HEADS // 2):
        sl = slice(p * LANES, (p + 1) * LANES)
        kt = ka_ref[2 * p:2 * p + 2].reshape(LANES, seq).astype(BF16)
        vt = va_ref[2 * p:2 * p + 2].reshape(LANES, seq).astype(BF16)
        ya_ref[:, sl] = heads_of_block(qa_ref[:, sl], kt, vt, None).astype(ya_ref.dtype)
    for p in range(SWA_HEADS // 2):
        sl = slice(p * LANES, (p + 1) * LANES)
        kv = (2 * p) // (SWA_HEADS // SWA_KV_HEADS)
        kt, vt = kb_ref[kv].astype(BF16), vb_ref[kv].astype(BF16)
        kt, vt = jnp.concatenate([kt, kt], axis=0), jnp.concatenate([vt, vt], axis=0)
        sink = jnp.where(top, sink_ref[2 * p], sink_ref[2 * p + 1]) * LOG2E
        yb_ref[:, sl] = heads_of_block(qb_ref[:, sl], kt, vt, sink).astype(yb_ref.dtype)


def _attn_ctx(sink, qa, qb, ka, va, kb, vb, *, seq):
    t = qa.shape[0]
    tile = lambda w: pl.BlockSpec((seq, w), lambda b: (b, 0))
    cache = lambda heads: pl.BlockSpec((None, heads, HEAD_DIM, seq), lambda b: (b, 0, 0, 0))
    return pl.pallas_call(
        _attn_ctx_kernel,
        grid=(t // seq,),
        in_specs=[pl.BlockSpec(memory_space=pltpu.SMEM), tile(NA_WIDTH), tile(SWA_Q_WIDTH),
                  cache(NA_HEADS), cache(NA_HEADS), cache(SWA_KV_HEADS), cache(SWA_KV_HEADS)],
        out_specs=[tile(NA_WIDTH), tile(SWA_Q_WIDTH)],
        out_shape=[jax.ShapeDtypeStruct((t, NA_WIDTH), BF16), jax.ShapeDtypeStruct((t, SWA_Q_WIDTH), BF16)],
        compiler_params=_params(),
        name="attn_ctx",
    )(sink, qa, qb, ka, va, kb, vb)


def _attn_na_kernel(q_ref, k_ref, v_ref, ck_ref, cv_ref, bias_ref, y_ref, ck_scr, cv_scr, *, n, ctx):
    rows = n // GRID_W
    tile_q = NA_TILE_ROWS * GRID_W
    zero = jnp.zeros((), BF16)
    ck_scr[...] = ck_ref[...].reshape(NA_WIDTH, ctx).astype(BF16)
    cv_scr[...] = cv_ref[...].reshape(NA_WIDTH, ctx).astype(BF16)
    lo, hi = _lane_halves((tile_q, LANES))
    left_half = lax.broadcasted_iota(jnp.int32, (GRID_W, LANES), 1) < GRID_W
    empty = jnp.zeros((GRID_W, LANES), BF16)

    def tile(q0, k0, key_rows, lead, first):
        pairs = key_rows // 2
        for p in range(NA_HEADS // 2):
            sl = slice(p * LANES, (p + 1) * LANES)
            q = q_ref[pl.ds(q0, tile_q), sl]
            q = jnp.concatenate([jnp.where(lo, q, zero), jnp.where(hi, q, zero)], axis=0)
            s_nb = _dot_nt(q, k_ref[pl.ds(k0, key_rows * GRID_W), sl])
            s_ctx = _dot(q, ck_scr[sl, :])
            e_nb, e_ctx, ls = [], [], []
            for idx in range(2):
                for a in range(NA_TILE_ROWS):
                    qa = slice(idx * tile_q + a * GRID_W, idx * tile_q + (a + 1) * GRID_W)
                    inside = lambda i: first[a] <= i < first[a] + NA_ROWS
                    cols, where = [], []
                    for m in range(pairs):
                        if not (inside(2 * m) or inside(2 * m + 1)):
                            continue
                        blk = s_nb[qa, m * LANES:(m + 1) * LANES] + bias_ref[2 * p + idx, 2 * m - a - lead + NA_ROWS]
                        if not inside(2 * m + 1):
                            blk = jnp.where(left_half, blk, NEG_INF)
                        elif not inside(2 * m):
                            blk = jnp.where(left_half, NEG_INF, blk)
                        cols.append(blk)
                        where.append(m)
                    cols += [s_ctx[qa, j * LANES:(j + 1) * LANES] for j in range(ctx // LANES)]
                    es, l = _softmax_cols(cols)
                    es = [e.astype(BF16) for e in es]
                    e_nb.append(jnp.concatenate(
                        [es[where.index(m)] if m in where else empty for m in range(pairs)], axis=1))
                    e_ctx.append(jnp.concatenate(es[len(where):], axis=1))
                    ls.append(l)
            o = (_dot(jnp.concatenate(e_nb, axis=0), v_ref[pl.ds(k0, key_rows * GRID_W), sl])
                 + _dot_nt(jnp.concatenate(e_ctx, axis=0), cv_scr[sl, :]))
            o = o * (1.0 / jnp.concatenate(ls, axis=0))
            y_ref[pl.ds(q0, tile_q), sl] = jnp.where(lo, o[:tile_q], o[tile_q:]).astype(y_ref.dtype)

    half = NA_ROWS // 2
    tile(0, 0, NA_ROWS, 0, (0,) * NA_TILE_ROWS)

    for t in range(1, rows // NA_TILE_ROWS - 1):
        tile(t * tile_q, t * tile_q - half * GRID_W, NA_ROWS + NA_TILE_ROWS, half, tuple(range(NA_TILE_ROWS)))
    tile(n - tile_q, n - NA_ROWS * GRID_W, NA_ROWS, half, (0,) * NA_TILE_ROWS)


def _attn_na(q, k, v, ck, cv, bias, *, n, ctx):
    t = q.shape[0]
    tile = pl.BlockSpec((n, NA_WIDTH), lambda b: (b, 0))
    ctile = pl.BlockSpec((None, NA_HEADS, HEAD_DIM, ctx), lambda b: (b, 0, 0, 0))
    return pl.pallas_call(
        functools.partial(_attn_na_kernel, n=n, ctx=ctx),
        grid=(t // n,),
        in_specs=[tile, tile, tile, ctile, ctile, _resident(bias.shape)],
        out_specs=tile,
        out_shape=jax.ShapeDtypeStruct((t, NA_WIDTH), BF16),
        scratch_shapes=[pltpu.VMEM((NA_WIDTH, ctx), BF16)] * 2,
        compiler_params=_params(),
        name="attn_na",
    )(q, k, v, ck, cv, bias)


def _na_bias_table(rel_bias):
    rows = _na_bias_rows(rel_bias)
    shape = (NA_HEADS, 2 * NA_ROWS, GRID_W, LANES)
    return pl.pallas_call(
        _fill_na_bias,
        grid=(1,),
        in_specs=[pl.BlockSpec(rows.shape, lambda i: (0, 0, 0))],
        out_specs=pl.BlockSpec(shape, lambda i: (0, 0, 0, 0)),
        out_shape=jax.ShapeDtypeStruct(shape, F32),
        compiler_params=_params(),
        name="na_bias",
    )(rows)


def _na_bias_rows(rel_bias):
    side = GRID_W - NA_COLS
    z = jnp.pad(rel_bias * LOG2E, ((0, 0), (0, 0), (side, side + 1)))
    return jnp.pad(z, ((0, 0), (1, 1), (0, 0)), constant_values=NEG_INF)


def _fill_na_bias(rows_ref, bias_scr):
    q = lax.broadcasted_iota(jnp.int32, (GRID_W, LANES), 0)
    lane = lax.broadcasted_iota(jnp.int32, (GRID_W, LANES), 1)
    kc = lane % GRID_W
    col_start = jnp.clip(q - NA_COLS // 2, 0, GRID_W - NA_COLS)
    in_window = (kc >= col_start) & (kc < col_start + NA_COLS)
    for h in range(NA_HEADS):
        blocks = []
        for j in range(2 * NA_ROWS + 1):
            row = jnp.broadcast_to(rows_ref[h, j:j + 1, :], (GRID_W, LANES))
            rolled = pltpu.roll(row, LANES - GRID_W + 1, 1, stride=1, stride_axis=0)
            blocks.append(jnp.where(in_window, rolled, NEG_INF))
        for j in range(2 * NA_ROWS):
            bias_scr[h, j] = jnp.where(lane < GRID_W, blocks[j], pltpu.roll(blocks[j + 1], GRID_W, 1))


def _attn_swa_kernel(sink_ref, q_ref, k_ref, v_ref, ck_ref, cv_ref, y_ref, k_scr, v_scr, ck_scr, cv_scr, *, n, ctx):
    group = SWA_HEADS // SWA_KV_HEADS
    band = SWA_QTILE + 2 * SWA_BLOCK
    for src, csrc, dst, cdst in ((k_ref, ck_ref, k_scr, ck_scr), (v_ref, cv_ref, v_scr, cv_scr)):
        x = src[...].astype(F32)
        xr = pltpu.roll(x, HEAD_DIM, 1)
        first_half, second_half = _lane_halves(x.shape)
        dst[0] = jnp.where(first_half, x, xr).astype(BF16)
        dst[1] = jnp.where(second_half, x, xr).astype(BF16)
        for kv in range(SWA_KV_HEADS):
            c = csrc[kv].astype(BF16)
            cdst[kv] = jnp.concatenate([c, c], axis=0)

    qi = lax.broadcasted_iota(jnp.int32, (SWA_QTILE, band), 0)
    kj = lax.broadcasted_iota(jnp.int32, (SWA_QTILE, band), 1)
    lo, hi = _lane_halves((SWA_QTILE, LANES))
    top = lax.broadcasted_iota(jnp.int32, (2 * SWA_QTILE, 1), 0) < SWA_QTILE
    zero = jnp.zeros((), BF16)

    def block_body(b, carry):
        q0 = pl.multiple_of(b * SWA_QTILE, SWA_QTILE)
        k0 = pl.multiple_of(jnp.clip(q0 - SWA_BLOCK, 0, n - band), SWA_BLOCK)
        mask = jnp.where(jnp.abs(kj - qi + (k0 - q0)) <= SWA_WINDOW, 0.0, NEG_INF)
        mask2 = jnp.concatenate([mask, mask], axis=0)
        for p in range(SWA_HEADS // 2):
            kv = 2 * p // group
            sl = slice(p * LANES, (p + 1) * LANES)
            q = q_ref[pl.ds(q0, SWA_QTILE), sl]
            q = jnp.concatenate([jnp.where(lo, q, zero), jnp.where(hi, q, zero)], axis=0)
            s_band = _dot_nt(q, k_scr[kv, pl.ds(k0, band), :])
            s_ctx = _dot(q, ck_scr[kv])
            sink = jnp.where(top, sink_ref[2 * p], sink_ref[2 * p + 1]) * LOG2E
            es, l = _softmax_cols(_cols(s_band + mask2) + _cols(s_ctx), sink)
            e_band = jnp.concatenate(es[:band // LANES], axis=1).astype(BF16)
            e_ctx = jnp.concatenate(es[band // LANES:], axis=1).astype(BF16)
            o = (_dot(e_band, v_scr[kv, pl.ds(k0, band), :]) + _dot_nt(e_ctx, cv_scr[kv])) * (1.0 / l)
            y_ref[pl.ds(q0, SWA_QTILE), sl] = jnp.where(lo, o[:SWA_QTILE], o[SWA_QTILE:]).astype(y_ref.dtype)
        return carry

    lax.fori_loop(0, n // SWA_QTILE, block_body, 0, unroll=4)


def _attn_swa(sink, q, k, v, ck, cv, *, n, ctx):
    t = q.shape[0]
    tile = lambda w: pl.BlockSpec((n, w), lambda b: (b, 0))
    ctile = pl.BlockSpec((None, SWA_KV_HEADS, HEAD_DIM, ctx), lambda b: (b, 0, 0, 0))
    return pl.pallas_call(
        functools.partial(_attn_swa_kernel, n=n, ctx=ctx),
        grid=(t // n,),
        in_specs=[pl.BlockSpec(memory_space=pltpu.SMEM), tile(SWA_Q_WIDTH), tile(SWA_KV_WIDTH),
                  tile(SWA_KV_WIDTH), ctile, ctile],
        out_specs=tile(SWA_Q_WIDTH),
        out_shape=jax.ShapeDtypeStruct((t, SWA_Q_WIDTH), BF16),
        scratch_shapes=[pltpu.VMEM((SWA_KV_HEADS, n, SWA_KV_WIDTH), BF16)] * 2
        + [pltpu.VMEM((SWA_KV_HEADS, LANES, ctx), BF16)] * 2,
        compiler_params=_params(),
        name="attn_swa",
    )(sink, q, k, v, ck, cv)


def _rope_tables(n):
    half = HEAD_DIM // 4
    freqs = jnp.power(ROPE_BASE, -jnp.arange(half, dtype=F32) / half)
    t = jnp.arange(n)
    cos, sin = [], []
    for pos in (t // GRID_W, t % GRID_W):
        ang = pos.astype(F32)[:, None] * freqs[None, :]
        cos += [jnp.cos(ang), jnp.cos(ang)]
        sin += [-jnp.sin(ang), jnp.sin(ang)]
    cos, sin = jnp.concatenate(cos, axis=-1), jnp.concatenate(sin, axis=-1)
    reps = LANES // HEAD_DIM
    return jnp.tile(cos, (1, reps)), jnp.tile(sin, (1, reps))


def _merge_kernel(x_ref, ya_ref, yb_ref, mod_ref, g_ref, win_ref, wba_ref, wbb_ref, wout_ref, o_ref):
    x = x_ref[...]
    mod = mod_ref[0]
    h = _modulate(x, g_ref[...], mod[3:4], mod[4:5]).astype(BF16)
    a = jax.nn.sigmoid(_dot(h, win_ref[:, QKV_COLS:QKV_COLS + D_MODEL])) * _dot(ya_ref[...], wba_ref[...])
    b = jax.nn.sigmoid(_dot(h, win_ref[:, QKV_COLS + D_MODEL:])) * _dot(yb_ref[...], wbb_ref[...])
    o_ref[...] = x + mod[5:6] * _dot((a + b).astype(BF16), wout_ref[...])


def _merge(x, ya, yb, mod, g, wgate, wba, wbb, wout, *, tokens_per_row, first_row):
    t = x.shape[0]
    tile = lambda w: pl.BlockSpec((TOKEN_TILE, w), lambda i: (i, 0))
    return pl.pallas_call(
        _merge_kernel,
        grid=(t // TOKEN_TILE,),
        in_specs=[tile(D_MODEL), tile(NA_WIDTH), tile(SWA_Q_WIDTH), _mod_spec(tokens_per_row, first_row),
                  _resident((1, D_MODEL)), _resident(wgate.shape), _resident(wba.shape), _resident(wbb.shape),
                  _resident(wout.shape)],
        out_specs=tile(D_MODEL),
        out_shape=jax.ShapeDtypeStruct((t, D_MODEL), F32),
        compiler_params=_params(),
        name="merge",
    )(x, ya, yb, mod, g, wgate, wba, wbb, wout)


def kernel(x_prompt, x_sample, cache_na_k, cache_na_v, cache_swa_k, cache_swa_v, c, c_ctx, w_ada, b_ada,
           norm_ffn1, ffn1_w_gate, ffn1_w_up, ffn1_w_down, norm_mix, w_in, na_rel_bias, swa_sink,
           w_branch_na, w_branch_swa, w_out, norm_ffn2, ffn2_w_gate, ffn2_w_up, ffn2_w_down, norm_final):
    depth = w_ada.shape[0]
    assert depth == 1
    batch, seq, _ = x_prompt.shape
    dec_batch, dec_seq, _ = x_sample.shape
    past = cache_na_k.shape[2]
    layer = 0
    row = lambda v: v.reshape(1, D_MODEL)
    bf = lambda w: w.astype(BF16)

    cond = jnp.zeros((MOD_ROWS, D_MODEL), F32).at[0].set(c_ctx).at[1:1 + dec_batch].set(c)
    mod = _adaln(cond, w_ada[layer], b_ada[layer]).reshape(MOD_ROWS, N_MOD, D_MODEL)

    ffn1 = (row(norm_ffn1[layer]), bf(ffn1_w_gate[layer]), bf(ffn1_w_up[layer]), bf(ffn1_w_down[layer]),
            row(norm_final))
    g_mix = row(norm_mix[layer])
    sink = swa_sink[layer]
    ctx_rows = dict(tokens_per_row=batch * seq, first_row=0)
    lat_rows = dict(tokens_per_row=dec_seq, first_row=1)

    later = (ffn2_w_gate[layer], ffn2_w_up[layer], ffn2_w_down[layer], w_in[layer],
             w_branch_na[layer], w_branch_swa[layer], w_out[layer])
    x_lat, *later = _ffn(x_sample.reshape(dec_batch * dec_seq, D_MODEL), mod, *ffn1, first=0, final=False,
                         cast=later, **lat_rows)
    ffn2 = (row(norm_ffn2[layer]), *later[:3], row(norm_final))
    w_qkv = later[3]
    merge_w = tuple(later[3:])
    w_kvt = lax.optimization_barrier(jnp.concatenate(
        [w_qkv[:, NA_WIDTH:3 * NA_WIDTH], w_qkv[:, 3 * NA_WIDTH + SWA_Q_WIDTH:QKV_COLS]], axis=1)).T

    where = ctx_rows
    x = x_prompt.reshape(batch * seq, D_MODEL)
    x = _ffn(x, mod, *ffn1, first=0, final=False, **where)
    qa, qb, *new_cache = _proj_ctx(x, mod, g_mix, w_qkv, w_kvt, seq=seq, **where)
    ya, yb = _attn_ctx(sink, qa, qb, *new_cache, seq=seq)
    x = _merge(x, ya, yb, mod, g_mix, *merge_w, **where)
    y_prompt = _ffn(x, mod, *ffn2, first=6, final=True, **where).reshape(batch, seq, D_MODEL)

    where = lat_rows
    x = x_lat
    qa, kal, val, qb, kbl, vbl = _proj_lat(x, mod, g_mix, w_qkv, _rope_tables(dec_seq), **where)
    transposed = lambda cache: jnp.transpose(cache[:, layer], (0, 2, 3, 1))
    ya = _attn_na(qa, kal, val, transposed(cache_na_k), transposed(cache_na_v),
                  _na_bias_table(na_rel_bias[layer]), n=dec_seq, ctx=past)
    yb = _attn_swa(sink, qb, kbl, vbl, transposed(cache_swa_k), transposed(cache_swa_v), n=dec_seq, ctx=past)
    x = _merge(x, ya, yb, mod, g_mix, *merge_w, **where)
    y_sample = _ffn(x, mod, *ffn2, first=6, final=True, **where).reshape(dec_batch, dec_seq, D_MODEL)

    new_cache = [jnp.transpose(t, (0, 3, 1, 2))[:, None] for t in new_cache]
    return (y_prompt, y_sample, *new_cache)
```

```python
import functools

import numpy as np
import jax
import jax.numpy as jnp
from jax import lax
from jax.experimental import pallas as pl
from jax.experimental.pallas import tpu as pltpu

F32 = jnp.float32
BF16 = jnp.bfloat16

D_MODEL = 1024
FFN_DIM = 2816
HEAD_DIM = 64
N_MOD = 9
GRID_W = 64
NA_HEADS = 8
NA_ROWS = 8
NA_COLS = 16
SWA_HEADS = 8
SWA_KV_HEADS = 2
SWA_WINDOW = 128
SWA_BLOCK = 128
ROPE_BASE = 10000.0
EPS = 1e-6
NEG_INF = -1e30
NA_WIDTH = NA_HEADS * HEAD_DIM
SWA_Q_WIDTH = SWA_HEADS * HEAD_DIM
SWA_KV_WIDTH = SWA_KV_HEADS * HEAD_DIM
QKV_COLS = 3 * NA_WIDTH + SWA_Q_WIDTH + 2 * SWA_KV_WIDTH
LANES = 128
MOD_ROWS = 16
VMEM_LIMIT = 56 * 1024 * 1024
TOKEN_TILE = 1024
FFN_TILE = 1024
FFN_CHUNK = 256
ADALN_BLOCK = 2304
SWA_QTILE = 256
NA_TILE_ROWS = 4
LOG2E = 1.4426950408889634
Q_SCALE = HEAD_DIM ** -0.5 * LOG2E


def _dot(a, b):
    return jnp.dot(a, b, preferred_element_type=F32)


def _dot_nt(a, b):
    return lax.dot_general(a, b, (((1,), (1,)), ((), ())), preferred_element_type=F32)


def _silu(x):
    return x * jax.nn.sigmoid(x)


def _rms(x, g):
    return x * lax.rsqrt(jnp.mean(x * x, axis=-1, keepdims=True) + EPS) * g


def _modulate(x, g, shift, scale):
    return _rms(x, g) * (1.0 + scale) + shift


def _resident(shape):
    nd = len(shape)
    return pl.BlockSpec(shape, lambda *_: (0,) * nd, pipeline_mode=pl.Buffered(1))


def _params():
    return pltpu.CompilerParams(dimension_semantics=("arbitrary",), vmem_limit_bytes=VMEM_LIMIT)


def _adaln_kernel(c_ref, w_ref, b_ref, o_ref):
    s = _silu(c_ref[...]).astype(BF16)
    o_ref[...] = _dot(s, w_ref[...].astype(BF16)) + b_ref[...]


def _adaln(cond, w_ada, b_ada):
    n = w_ada.shape[1]
    blk = ADALN_BLOCK
    return pl.pallas_call(
        _adaln_kernel,
        grid=(n // blk,),
        in_specs=[pl.BlockSpec((MOD_ROWS, D_MODEL), lambda j: (0, 0)),
                  pl.BlockSpec((D_MODEL, blk), lambda j: (0, j)),
                  pl.BlockSpec((1, blk), lambda j: (0, j))],
        out_specs=pl.BlockSpec((MOD_ROWS, blk), lambda j: (0, j)),
        out_shape=jax.ShapeDtypeStruct((MOD_ROWS, n), F32),
        compiler_params=_params(),
        name="adaln",
    )(cond, w_ada, b_ada.reshape(1, n))


def _mod_spec(tokens_per_row, first_row, tile=TOKEN_TILE):
    tiles_per_row = tokens_per_row // tile
    return pl.BlockSpec((1, N_MOD, D_MODEL), lambda i: (first_row + i // tiles_per_row, 0, 0))


def _ffn_kernel(x_ref, mod_ref, g_ref, wg_ref, wu_ref, wd_ref, gf_ref, *rest, first, final):
    n_cast = (len(rest) - 2) // 2
    o_ref, a_scr = rest[n_cast], rest[-1]
    for src, dst in zip(rest[:n_cast], rest[n_cast + 1:-1]):
        dst[...] = src[...].astype(BF16)
    x = x_ref[...]
    mod = mod_ref[0]
    shift, scale, gate = mod[first:first + 1], mod[first + 1:first + 2], mod[first + 2:first + 3]
    h = _modulate(x, g_ref[...], shift, scale).astype(BF16)
    for c in range(FFN_DIM // FFN_CHUNK):
        sl = slice(c * FFN_CHUNK, (c + 1) * FFN_CHUNK)
        a_scr[:, sl] = (_silu(_dot(h, wg_ref[:, sl])) * _dot(h, wu_ref[:, sl])).astype(BF16)
    y = x + (0.5 * gate) * _dot(a_scr[...], wd_ref[...])
    if final:
        y = _rms(y, gf_ref[...])
    o_ref[...] = y


def _ffn(x, mod, g, wg, wu, wd, gf, *, tokens_per_row, first_row, first, final, cast=()):
    t = x.shape[0]
    steps = t // FFN_TILE
    tile = pl.BlockSpec((FFN_TILE, D_MODEL), lambda i: (i, 0))
    chunk = lambda w: pl.BlockSpec((w.shape[0] // steps, w.shape[1]), lambda i: (i, 0))
    assert all(w.shape[0] % (steps * 16) == 0 for w in cast)
    out = pl.pallas_call(
        functools.partial(_ffn_kernel, first=first, final=final),
        grid=(steps,),
        in_specs=[tile, _mod_spec(tokens_per_row, first_row, FFN_TILE), _resident((1, D_MODEL)),
                  _resident(wg.shape), _resident(wu.shape), _resident(wd.shape), _resident((1, D_MODEL))]
        + [chunk(w) for w in cast],
        out_specs=[tile] + [chunk(w) for w in cast],
        out_shape=[jax.ShapeDtypeStruct((t, D_MODEL), F32)] + [jax.ShapeDtypeStruct(w.shape, BF16) for w in cast],
        scratch_shapes=[pltpu.VMEM((FFN_TILE, FFN_DIM), BF16)],
        compiler_params=_params(),
        name="ffn",
    )(x, mod, g, wg, wu, wd, gf, *cast)
    return out if cast else out[0]


def _swap16(x):
    lane = lax.broadcasted_iota(jnp.int32, x.shape, 1)
    return jnp.where(lane % 32 < 16, pltpu.roll(x, LANES - 16, 1), pltpu.roll(x, 16, 1))


def _proj_ctx_kernel(x_ref, mod_ref, g_ref, w_ref, qa_ref, qb_ref, ka_ref, va_ref, kb_ref, vb_ref, wkvt_ref):
    @pl.when(pl.program_id(0) == 0)
    def _():
        kv_cols = ((NA_WIDTH, 3 * NA_WIDTH), (3 * NA_WIDTH + SWA_Q_WIDTH, QKV_COLS))
        o = 0
        for lo_col, hi_col in kv_cols:
            wkvt_ref[o:o + hi_col - lo_col, :] = w_ref[:, lo_col:hi_col].astype(F32).T.astype(BF16)
            o += hi_col - lo_col

    mod = mod_ref[0]
    h = _modulate(x_ref[...], g_ref[...], mod[3:4], mod[4:5]).astype(BF16)
    qa_ref[...] = (_dot(h, w_ref[:, :NA_WIDTH]) * Q_SCALE).astype(qa_ref.dtype)
    qb_ref[...] = (_dot(h, w_ref[:, 3 * NA_WIDTH:3 * NA_WIDTH + SWA_Q_WIDTH]) * Q_SCALE).astype(qb_ref.dtype)
    kvt = _dot_nt(wkvt_ref[...], h)
    requests, _, _, seq = ka_ref.shape
    o = 0
    for ref in (ka_ref, va_ref, kb_ref, vb_ref):
        width = ref.shape[1] * HEAD_DIM
        for r in range(requests):
            ref[r] = kvt[o:o + width, r * seq:(r + 1) * seq].reshape(ref.shape[1:])
        o += width


def _rope(y, cos, sin):
    blocks = [y[:, j * LANES:(j + 1) * LANES] for j in range(y.shape[1] // LANES)]
    return [b * cos + _swap16(b) * sin for b in blocks]


def _proj_lat_kernel(x_ref, mod_ref, g_ref, w_ref, cos_ref, sin_ref,
                     qa_ref, ka_ref, va_ref, qb_ref, kb_ref, vb_ref):
    mod = mod_ref[0]
    h = _modulate(x_ref[...], g_ref[...], mod[3:4], mod[4:5]).astype(BF16)
    cos, sin = cos_ref[...], sin_ref[...]
    o = 3 * NA_WIDTH
    for j, b in enumerate(_rope(_dot(h, w_ref[:, o:o + SWA_Q_WIDTH]), cos, sin)):
        qb_ref[:, j * LANES:(j + 1) * LANES] = (b * Q_SCALE).astype(qb_ref.dtype)
    o += SWA_Q_WIDTH
    y = _dot(h, w_ref[:, o:o + 2 * SWA_KV_WIDTH])
    kb_ref[...] = _rope(y[:, :SWA_KV_WIDTH], cos, sin)[0].astype(kb_ref.dtype)
    vb_ref[...] = y[:, SWA_KV_WIDTH:].astype(vb_ref.dtype)
    o = 0
    for ref, scale in ((qa_ref, True), (ka_ref, False), (va_ref, False)):
        y = _dot(h, w_ref[:, o:o + NA_WIDTH])
        ref[...] = (y * Q_SCALE if scale else y).astype(ref.dtype)
        o += NA_WIDTH


def _proj_ctx(x, mod, g, w_qkv, *, seq, tokens_per_row, first_row):
    t = x.shape[0]
    tile = lambda w: pl.BlockSpec((TOKEN_TILE, w), lambda i: (i, 0))
    cache = lambda heads: pl.BlockSpec((TOKEN_TILE // seq, heads, HEAD_DIM, seq), lambda i: (i, 0, 0, 0))
    cache_shape = lambda heads: jax.ShapeDtypeStruct((t // seq, heads, HEAD_DIM, seq), F32)
    kv_heads = (NA_HEADS, NA_HEADS, SWA_KV_HEADS, SWA_KV_HEADS)
    return pl.pallas_call(
        _proj_ctx_kernel,
        grid=(t // TOKEN_TILE,),
        in_specs=[tile(D_MODEL), _mod_spec(tokens_per_row, first_row), _resident((1, D_MODEL)),
                  _resident(w_qkv.shape)],
        out_specs=[tile(NA_WIDTH), tile(SWA_Q_WIDTH)] + [cache(nh) for nh in kv_heads],
        out_shape=[jax.ShapeDtypeStruct((t, NA_WIDTH), BF16), jax.ShapeDtypeStruct((t, SWA_Q_WIDTH), BF16)]
        + [cache_shape(nh) for nh in kv_heads],
        scratch_shapes=[pltpu.VMEM((2 * NA_WIDTH + 2 * SWA_KV_WIDTH, D_MODEL), BF16)],
        compiler_params=_params(),
        name="proj_ctx",
    )(x, mod, g, w_qkv)


def _proj_lat(x, mod, g, w_qkv, rope, *, tokens_per_row, first_row):
    t = x.shape[0]
    tile = lambda w: pl.BlockSpec((TOKEN_TILE, w), lambda i: (i, 0))
    widths = (NA_WIDTH, NA_WIDTH, NA_WIDTH, SWA_Q_WIDTH, SWA_KV_WIDTH, SWA_KV_WIDTH)
    tiles_per_row = tokens_per_row // TOKEN_TILE
    return pl.pallas_call(
        _proj_lat_kernel,
        grid=(t // TOKEN_TILE,),
        in_specs=[tile(D_MODEL), _mod_spec(tokens_per_row, first_row), _resident((1, D_MODEL)),
                  _resident(w_qkv.shape)]
        + [pl.BlockSpec((TOKEN_TILE, LANES), lambda i: (i % tiles_per_row, 0))] * 2,
        out_specs=[tile(w) for w in widths],
        out_shape=[jax.ShapeDtypeStruct((t, w), BF16) for w in widths],
        compiler_params=_params(),
        name="proj_lat",
    )(x, mod, g, w_qkv, *rope)


def _lane_halves(shape):
    lane = lax.broadcasted_iota(jnp.int32, shape, 1) % LANES
    return lane < HEAD_DIM, lane >= HEAD_DIM


def _softmax_cols(cols, sink=None):
    m = jnp.max(functools.reduce(jnp.maximum, cols), axis=-1, keepdims=True)
    if sink is not None:
        m = jnp.maximum(m, sink)
    es = [jnp.exp2(c - m) for c in cols]
    l = jnp.sum(functools.reduce(jnp.add, es), axis=-1, keepdims=True)
    if sink is not None:
        l = l + jnp.exp2(sink - m)
    return es, l


def _cols(x):
    return [x[:, j * LANES:(j + 1) * LANES] for j in range(x.shape[1] // LANES)]


def _attn_ctx_kernel(sink_ref, qa_ref, qb_ref, ka_ref, va_ref, kb_ref, vb_ref, ya_ref, yb_ref):
    seq = qa_ref.shape[0]
    lo, hi = _lane_halves((seq, LANES))
    top = lax.broadcasted_iota(jnp.int32, (2 * seq, 1), 0) < seq
    zero = jnp.zeros((), BF16)

    def heads_of_block(q, kt, vt, sink):
        q = jnp.concatenate([jnp.where(lo, q, zero), jnp.where(hi, q, zero)], axis=0)
        es, l = _softmax_cols(_cols(_dot(q, kt)), sink)
        o = _dot_nt(jnp.concatenate(es, axis=1).astype(BF16), vt) * (1.0 / l)
        return jnp.where(lo, o[:seq], o[seq:])

    for p in range(NA_HEADS // 2):
        sl = slice(p * LANES, (p + 1) * LANES)
        kt = ka_ref[2 * p:2 * p + 2].reshape(LANES, seq).astype(BF16)
        vt = va_ref[2 * p:2 * p + 2].reshape(LANES, seq).astype(BF16)
        ya_ref[:, sl] = heads_of_block(qa_ref[:, sl], kt, vt, None).astype(ya_ref.dtype)
    for p in range(SWA_HEADS // 2):
        sl = slice(p * LANES, (p + 1) * LANES)
        kv = (2 * p) // (SWA_HEADS // SWA_KV_HEADS)
        kt, vt = kb_ref[kv].astype(BF16), vb_ref[kv].astype(BF16)
        kt, vt = jnp.concatenate([kt, kt], axis=0), jnp.concatenate([vt, vt], axis=0)
        sink = jnp.where(top, sink_ref[2 * p], sink_ref[2 * p + 1]) * LOG2E
        yb_ref[:, sl] = heads_of_block(qb_ref[:, sl], kt, vt, sink).astype(yb_ref.dtype)


def _attn_ctx(sink, qa, qb, ka, va, kb, vb, *, seq):
    t = qa.shape[0]
    tile = lambda w: pl.BlockSpec((seq, w), lambda b: (b, 0))
    cache = lambda heads: pl.BlockSpec((None, heads, HEAD_DIM, seq), lambda b: (b, 0, 0, 0))
    return pl.pallas_call(
        _attn_ctx_kernel,
        grid=(t // seq,),
        in_specs=[pl.BlockSpec(memory_space=pltpu.SMEM), tile(NA_WIDTH), tile(SWA_Q_WIDTH),
                  cache(NA_HEADS), cache(NA_HEADS), cache(SWA_KV_HEADS), cache(SWA_KV_HEADS)],
        out_specs=[tile(NA_WIDTH), tile(SWA_Q_WIDTH)],
        out_shape=[jax.ShapeDtypeStruct((t, NA_WIDTH), BF16), jax.ShapeDtypeStruct((t, SWA_Q_WIDTH), BF16)],
        compiler_params=_params(),
        name="attn_ctx",
    )(sink, qa, qb, ka, va, kb, vb)


def _attn_na_kernel(q_ref, k_ref, v_ref, ck_ref, cv_ref, bias_ref, y_ref, ck_scr, cv_scr, *, n, ctx):
    rows = n // GRID_W
    tile_q = NA_TILE_ROWS * GRID_W
    zero = jnp.zeros((), BF16)
    ck_scr[...] = ck_ref[...].reshape(NA_WIDTH, ctx).astype(BF16)
    cv_scr[...] = cv_ref[...].reshape(NA_WIDTH, ctx).astype(BF16)
    lo, hi = _lane_halves((tile_q, LANES))
    left_half = lax.broadcasted_iota(jnp.int32, (GRID_W, LANES), 1) < GRID_W
    empty = jnp.zeros((GRID_W, LANES), BF16)

    def tile(q0, k0, key_rows, lead, first):
        pairs = key_rows // 2
        for p in range(NA_HEADS // 2):
            sl = slice(p * LANES, (p + 1) * LANES)
            q = q_ref[pl.ds(q0, tile_q), sl]
            q = jnp.concatenate([jnp.where(lo, q, zero), jnp.where(hi, q, zero)], axis=0)
            s_nb = _dot_nt(q, k_ref[pl.ds(k0, key_rows * GRID_W), sl])
            s_ctx = _dot(q, ck_scr[sl, :])
            e_nb, e_ctx, ls = [], [], []
            for idx in range(2):
                for a in range(NA_TILE_ROWS):
                    qa = slice(idx * tile_q + a * GRID_W, idx * tile_q + (a + 1) * GRID_W)
                    inside = lambda i: first[a] <= i < first[a] + NA_ROWS
                    cols, where = [], []
                    for m in range(pairs):
                        if not (inside(2 * m) or inside(2 * m + 1)):
                            continue
                        blk = s_nb[qa, m * LANES:(m + 1) * LANES] + bias_ref[2 * p + idx, 2 * m - a - lead + NA_ROWS]
                        if not inside(2 * m + 1):
                            blk = jnp.where(left_half, blk, NEG_INF)
                        elif not inside(2 * m):
                            blk = jnp.where(left_half, NEG_INF, blk)
                        cols.append(blk)
                        where.append(m)
                    cols += [s_ctx[qa, j * LANES:(j + 1) * LANES] for j in range(ctx // LANES)]
                    es, l = _softmax_cols(cols)
                    es = [e.astype(BF16) for e in es]
                    e_nb.append(jnp.concatenate(
                        [es[where.index(m)] if m in where else empty for m in range(pairs)], axis=1))
                    e_ctx.append(jnp.concatenate(es[len(where):], axis=1))
                    ls.append(l)
            o = (_dot(jnp.concatenate(e_nb, axis=0), v_ref[pl.ds(k0, key_rows * GRID_W), sl])
                 + _dot_nt(jnp.concatenate(e_ctx, axis=0), cv_scr[sl, :]))
            o = o * (1.0 / jnp.concatenate(ls, axis=0))
            y_ref[pl.ds(q0, tile_q), sl] = jnp.where(lo, o[:tile_q], o[tile_q:]).astype(y_ref.dtype)

    half = NA_ROWS // 2
    tile(0, 0, NA_ROWS, 0, (0,) * NA_TILE_ROWS)

    for t in range(1, rows // NA_TILE_ROWS - 1):
        tile(t * tile_q, t * tile_q - half * GRID_W, NA_ROWS + NA_TILE_ROWS, half, tuple(range(NA_TILE_ROWS)))
    tile(n - tile_q, n - NA_ROWS * GRID_W, NA_ROWS, half, (0,) * NA_TILE_ROWS)


def _attn_na(q, k, v, ck, cv, bias, *, n, ctx):
    t = q.shape[0]
    tile = pl.BlockSpec((n, NA_WIDTH), lambda b: (b, 0))
    ctile = pl.BlockSpec((None, NA_HEADS, HEAD_DIM, ctx), lambda b: (b, 0, 0, 0))
    return pl.pallas_call(
        functools.partial(_attn_na_kernel, n=n, ctx=ctx),
        grid=(t // n,),
        in_specs=[tile, tile, tile, ctile, ctile, _resident(bias.shape)],
        out_specs=tile,
        out_shape=jax.ShapeDtypeStruct((t, NA_WIDTH), BF16),
        scratch_shapes=[pltpu.VMEM((NA_WIDTH, ctx), BF16)] * 2,
        compiler_params=_params(),
        name="attn_na",
    )(q, k, v, ck, cv, bias)


def _na_bias_table(rel_bias):
    rows = _na_bias_rows(rel_bias)
    shape = (NA_HEADS, 2 * NA_ROWS, GRID_W, LANES)
    return pl.pallas_call(
        _fill_na_bias,
        grid=(1,),
        in_specs=[pl.BlockSpec(rows.shape, lambda i: (0, 0, 0))],
        out_specs=pl.BlockSpec(shape, lambda i: (0, 0, 0, 0)),
        out_shape=jax.ShapeDtypeStruct(shape, F32),
        compiler_params=_params(),
        name="na_bias",
    )(rows)


def _na_bias_rows(rel_bias):
    side = GRID_W - NA_COLS
    z = jnp.pad(rel_bias * LOG2E, ((0, 0), (0, 0), (side, side + 1)))
    return jnp.pad(z, ((0, 0), (1, 1), (0, 0)), constant_values=NEG_INF)


def _fill_na_bias(rows_ref, bias_scr):
    q = lax.broadcasted_iota(jnp.int32, (GRID_W, LANES), 0)
    lane = lax.broadcasted_iota(jnp.int32, (GRID_W, LANES), 1)
    kc = lane % GRID_W
    col_start = jnp.clip(q - NA_COLS // 2, 0, GRID_W - NA_COLS)
    in_window = (kc >= col_start) & (kc < col_start + NA_COLS)
    for h in range(NA_HEADS):
        blocks = []
        for j in range(2 * NA_ROWS + 1):
            row = jnp.broadcast_to(rows_ref[h, j:j + 1, :], (GRID_W, LANES))
            rolled = pltpu.roll(row, LANES - GRID_W + 1, 1, stride=1, stride_axis=0)
            blocks.append(jnp.where(in_window, rolled, NEG_INF))
        for j in range(2 * NA_ROWS):
            bias_scr[h, j] = jnp.where(lane < GRID_W, blocks[j], pltpu.roll(blocks[j + 1], GRID_W, 1))


def _attn_swa_kernel(sink_ref, q_ref, k_ref, v_ref, ck_ref, cv_ref, y_ref, k_scr, v_scr, ck_scr, cv_scr, *, n, ctx):
    group = SWA_HEADS // SWA_KV_HEADS
    band = SWA_QTILE + 2 * SWA_BLOCK
    for src, csrc, dst, cdst in ((k_ref, ck_ref, k_scr, ck_scr), (v_ref, cv_ref, v_scr, cv_scr)):
        x = src[...].astype(F32)
        xr = pltpu.roll(x, HEAD_DIM, 1)
        first_half, second_half = _lane_halves(x.shape)
        dst[0] = jnp.where(first_half, x, xr).astype(BF16)
        dst[1] = jnp.where(second_half, x, xr).astype(BF16)
        for kv in range(SWA_KV_HEADS):
            c = csrc[kv].astype(BF16)
            cdst[kv] = jnp.concatenate([c, c], axis=0)

    qi = lax.broadcasted_iota(jnp.int32, (SWA_QTILE, band), 0)
    kj = lax.broadcasted_iota(jnp.int32, (SWA_QTILE, band), 1)
    lo, hi = _lane_halves((SWA_QTILE, LANES))
    top = lax.broadcasted_iota(jnp.int32, (2 * SWA_QTILE, 1), 0) < SWA_QTILE
    zero = jnp.zeros((), BF16)

    def block_body(b, carry):
        q0 = pl.multiple_of(b * SWA_QTILE, SWA_QTILE)
        k0 = pl.multiple_of(jnp.clip(q0 - SWA_BLOCK, 0, n - band), SWA_BLOCK)
        mask = jnp.where(jnp.abs(kj - qi + (k0 - q0)) <= SWA_WINDOW, 0.0, NEG_INF)
        mask2 = jnp.concatenate([mask, mask], axis=0)
        for p in range(SWA_HEADS // 2):
            kv = 2 * p // group
            sl = slice(p * LANES, (p + 1) * LANES)
            q = q_ref[pl.ds(q0, SWA_QTILE), sl]
            q = jnp.concatenate([jnp.where(lo, q, zero), jnp.where(hi, q, zero)], axis=0)
            s_band = _dot_nt(q, k_scr[kv, pl.ds(k0, band), :])
            s_ctx = _dot(q, ck_scr[kv])
            sink = jnp.where(top, sink_ref[2 * p], sink_ref[2 * p + 1]) * LOG2E
            es, l = _softmax_cols(_cols(s_band + mask2) + _cols(s_ctx), sink)
            e_band = jnp.concatenate(es[:band // LANES], axis=1).astype(BF16)
            e_ctx = jnp.concatenate(es[band // LANES:], axis=1).astype(BF16)
            o = (_dot(e_band, v_scr[kv, pl.ds(k0, band), :]) + _dot_nt(e_ctx, cv_scr[kv])) * (1.0 / l)
            y_ref[pl.ds(q0, SWA_QTILE), sl] = jnp.where(lo, o[:SWA_QTILE], o[SWA_QTILE:]).astype(y_ref.dtype)
        return carry

    lax.fori_loop(0, n // SWA_QTILE, block_body, 0, unroll=4)


def _attn_swa(sink, q, k, v, ck, cv, *, n, ctx):
    t = q.shape[0]
    tile = lambda w: pl.BlockSpec((n, w), lambda b: (b, 0))
    ctile = pl.BlockSpec((None, SWA_KV_HEADS, HEAD_DIM, ctx), lambda b: (b, 0, 0, 0))
    return pl.pallas_call(
        functools.partial(_attn_swa_kernel, n=n, ctx=ctx),
        grid=(t // n,),
        in_specs=[pl.BlockSpec(memory_space=pltpu.SMEM), tile(SWA_Q_WIDTH), tile(SWA_KV_WIDTH),
                  tile(SWA_KV_WIDTH), ctile, ctile],
        out_specs=tile(SWA_Q_WIDTH),
        out_shape=jax.ShapeDtypeStruct((t, SWA_Q_WIDTH), BF16),
        scratch_shapes=[pltpu.VMEM((SWA_KV_HEADS, n, SWA_KV_WIDTH), BF16)] * 2
        + [pltpu.VMEM((SWA_KV_HEADS, LANES, ctx), BF16)] * 2,
        compiler_params=_params(),
        name="attn_swa",
    )(sink, q, k, v, ck, cv)


def _rope_tables(n):
    half = HEAD_DIM // 4
    freqs = jnp.power(ROPE_BASE, -jnp.arange(half, dtype=F32) / half)
    t = jnp.arange(n)
    cos, sin = [], []
    for pos in (t // GRID_W, t % GRID_W):
        ang = pos.astype(F32)[:, None] * freqs[None, :]
        cos += [jnp.cos(ang), jnp.cos(ang)]
        sin += [-jnp.sin(ang), jnp.sin(ang)]
    cos, sin = jnp.concatenate(cos, axis=-1), jnp.concatenate(sin, axis=-1)
    reps = LANES // HEAD_DIM
    return jnp.tile(cos, (1, reps)), jnp.tile(sin, (1, reps))


def _merge_kernel(x_ref, ya_ref, yb_ref, mod_ref, g_ref, win_ref, wba_ref, wbb_ref, wout_ref, o_ref):
    x = x_ref[...]
    mod = mod_ref[0]
    h = _modulate(x, g_ref[...], mod[3:4], mod[4:5]).astype(BF16)
    a = jax.nn.sigmoid(_dot(h, win_ref[:, QKV_COLS:QKV_COLS + D_MODEL])) * _dot(ya_ref[...], wba_ref[...])
    b = jax.nn.sigmoid(_dot(h, win_ref[:, QKV_COLS + D_MODEL:])) * _dot(yb_ref[...], wbb_ref[...])
    o_ref[...] = x + mod[5:6] * _dot((a + b).astype(BF16), wout_ref[...])


def _merge(x, ya, yb, mod, g, wgate, wba, wbb, wout, *, tokens_per_row, first_row):
    t = x.shape[0]
    tile = lambda w: pl.BlockSpec((TOKEN_TILE, w), lambda i: (i, 0))
    return pl.pallas_call(
        _merge_kernel,
        grid=(t // TOKEN_TILE,),
        in_specs=[tile(D_MODEL), tile(NA_WIDTH), tile(SWA_Q_WIDTH), _mod_spec(tokens_per_row, first_row),
                  _resident((1, D_MODEL)), _resident(wgate.shape), _resident(wba.shape), _resident(wbb.shape),
                  _resident(wout.shape)],
        out_specs=tile(D_MODEL),
        out_shape=jax.ShapeDtypeStruct((t, D_MODEL), F32),
        compiler_params=_params(),
        name="merge",
    )(x, ya, yb, mod, g, wgate, wba, wbb, wout)


def kernel(x_prompt, x_sample, cache_na_k, cache_na_v, cache_swa_k, cache_swa_v, c, c_ctx, w_ada, b_ada,
           norm_ffn1, ffn1_w_gate, ffn1_w_up, ffn1_w_down, norm_mix, w_in, na_rel_bias, swa_sink,
           w_branch_na, w_branch_swa, w_out, norm_ffn2, ffn2_w_gate, ffn2_w_up, ffn2_w_down, norm_final):
    depth = w_ada.shape[0]
    assert depth == 1
    batch, seq, _ = x_prompt.shape
    dec_batch, dec_seq, _ = x_sample.shape
    past = cache_na_k.shape[2]
    layer = 0
    row = lambda v: v.reshape(1, D_MODEL)
    bf = lambda w: w.astype(BF16)

    cond = jnp.zeros((MOD_ROWS, D_MODEL), F32).at[0].set(c_ctx).at[1:1 + dec_batch].set(c)
    mod = _adaln(cond, w_ada[layer], b_ada[layer]).reshape(MOD_ROWS, N_MOD, D_MODEL)

    ffn1 = (row(norm_ffn1[layer]), bf(ffn1_w_gate[layer]), bf(ffn1_w_up[layer]), bf(ffn1_w_down[layer]),
            row(norm_final))
    g_mix = row(norm_mix[layer])
    sink = swa_sink[layer]
    ctx_rows = dict(tokens_per_row=batch * seq, first_row=0)
    lat_rows = dict(tokens_per_row=dec_seq, first_row=1)

    later = (ffn2_w_gate[layer], ffn2_w_up[layer], ffn2_w_down[layer], w_in[layer],
             w_branch_na[layer], w_branch_swa[layer], w_out[layer])
    x_lat, *later = _ffn(x_sample.reshape(dec_batch * dec_seq, D_MODEL), mod, *ffn1, first=0, final=False,
                         cast=later, **lat_rows)
    ffn2 = (row(norm_ffn2[layer]), *later[:3], row(norm_final))
    w_qkv = later[3]
    merge_w = tuple(later[3:])

    where = ctx_rows
    x = x_prompt.reshape(batch * seq, D_MODEL)
    x = _ffn(x, mod, *ffn1, first=0, final=False, **where)
    qa, qb, *new_cache = _proj_ctx(x, mod, g_mix, w_qkv, seq=seq, **where)
    ya, yb = _attn_ctx(sink, qa, qb, *new_cache, seq=seq)
    x = _merge(x, ya, yb, mod, g_mix, *merge_w, **where)
    y_prompt = _ffn(x, mod, *ffn2, first=6, final=True, **where).reshape(batch, seq, D_MODEL)

    where = lat_rows
    x = x_lat
    qa, kal, val, qb, kbl, vbl = _proj_lat(x, mod, g_mix, w_qkv, _rope_tables(dec_seq), **where)
    transposed = lambda cache: jnp.transpose(cache[:, layer], (0, 2, 3, 1))
    ya = _attn_na(qa, kal, val, transposed(cache_na_k), transposed(cache_na_v),
                  _na_bias_table(na_rel_bias[layer]), n=dec_seq, ctx=past)
    yb = _attn_swa(sink, qb, kbl, vbl, transposed(cache_swa_k), transposed(cache_swa_v), n=dec_seq, ctx=past)
    x = _merge(x, ya, yb, mod, g_mix, *merge_w, **where)
    y_sample = _ffn(x, mod, *ffn2, first=6, final=True, **where).reshape(dec_batch, dec_seq, D_MODEL)

    new_cache = [jnp.transpose(t, (0, 3, 1, 2))[:, None] for t in new_cache]
    return (y_prompt, y_sample, *new_cache)
```

```python
import functools

import jax
import jax.numpy as jnp
from jax import lax
from jax.experimental import pallas as pl
from jax.experimental.pallas import tpu as pltpu

F32 = jnp.float32
BF16 = jnp.bfloat16

D_MODEL = 1024
FFN_DIM = 2816
HEAD_DIM = 64
N_MOD = 9
GRID_W = 64
NA_HEADS = 8
NA_ROWS = 8
NA_COLS = 16
SWA_HEADS = 8
SWA_KV_HEADS = 2
SWA_WINDOW = 128
SWA_BLOCK = 128
ROPE_BASE = 10000.0
EPS = 1e-6
NEG_INF = -1e30
NA_WIDTH = NA_HEADS * HEAD_DIM
SWA_Q_WIDTH = SWA_HEADS * HEAD_DIM
SWA_KV_WIDTH = SWA_KV_HEADS * HEAD_DIM
QKV_COLS = 3 * NA_WIDTH + SWA_Q_WIDTH + 2 * SWA_KV_WIDTH
LANES = 128
MOD_ROWS = 16
VMEM_LIMIT = 56 * 1024 * 1024
TOKEN_TILE = 1024
MERGE_TILE = 512
FFN_TILE = 1024
FFN_CHUNK = 256
ADALN_BLOCK = 2304
SWA_QTILE = 256
NA_TILE_ROWS = 4
LOG2E = 1.4426950408889634
Q_SCALE = HEAD_DIM ** -0.5 * LOG2E


def _dot(a, b):
    return jnp.dot(a, b, preferred_element_type=F32)


def _dot_nt(a, b):
    return lax.dot_general(a, b, (((1,), (1,)), ((), ())), preferred_element_type=F32)


def _silu(x):
    return x * jax.nn.sigmoid(x)


def _rms(x, g):
    return x * lax.rsqrt(jnp.mean(x * x, axis=-1, keepdims=True) + EPS) * g


def _modulate(x, g, shift, scale):
    return _rms(x, g) * (1.0 + scale) + shift


def _resident(shape):
    nd = len(shape)
    return pl.BlockSpec(shape, lambda *_: (0,) * nd, pipeline_mode=pl.Buffered(1))


def _params():
    return pltpu.CompilerParams(dimension_semantics=("arbitrary",), vmem_limit_bytes=VMEM_LIMIT)


def _adaln_kernel(c_ref, w_ref, b_ref, o_ref):
    s = _silu(c_ref[...]).astype(BF16)
    o_ref[...] = _dot(s, w_ref[...].astype(BF16)) + b_ref[...]


def _adaln(cond, w_ada, b_ada):
    n = w_ada.shape[1]
    blk = ADALN_BLOCK
    return pl.pallas_call(
        _adaln_kernel,
        grid=(n // blk,),
        in_specs=[pl.BlockSpec((MOD_ROWS, D_MODEL), lambda j: (0, 0)),
                  pl.BlockSpec((D_MODEL, blk), lambda j: (0, j)),
                  pl.BlockSpec((1, blk), lambda j: (0, j))],
        out_specs=pl.BlockSpec((MOD_ROWS, blk), lambda j: (0, j)),
        out_shape=jax.ShapeDtypeStruct((MOD_ROWS, n), F32),
        compiler_params=_params(),
        name="adaln",
    )(cond, w_ada, b_ada.reshape(1, n))


def _mod_spec(tokens_per_row, first_row, tile=TOKEN_TILE):
    tiles_per_row = tokens_per_row // tile
    return pl.BlockSpec((1, N_MOD, D_MODEL), lambda i: (first_row + i // tiles_per_row, 0, 0))


def _mod_spec_both(ctx_steps, tokens_per_request, tile):
    tiles_per_request = tokens_per_request // tile
    row = lambda i: jnp.where(i < ctx_steps, 0, 1 + (i - ctx_steps) // tiles_per_request)
    return pl.BlockSpec((1, N_MOD, D_MODEL), lambda i: (row(i), 0, 0))


def _ctx_then_lat(width, ctx_steps, tile):
    ctx = pl.BlockSpec((tile, width), lambda i: (jnp.minimum(i, ctx_steps - 1), 0))
    lat = pl.BlockSpec((tile, width), lambda i: (jnp.maximum(i - ctx_steps, 0), 0))
    return [ctx, lat]


def _ffn_kernel(x_ref, mod_ref, g_ref, wg_ref, wu_ref, wd_ref, gf_ref, *rest, first, final, n_cast, ctx_steps):
    n_out = 1 if ctx_steps is None else 2
    outs, a_scr = rest[n_cast:n_cast + n_out], rest[-1]
    for src, dst in zip(rest[:n_cast], rest[n_cast + n_out:-1]):
        dst[...] = src[...].astype(BF16)
    x = x_ref[...]
    mod = mod_ref[0]
    shift, scale, gate = mod[first:first + 1], mod[first + 1:first + 2], mod[first + 2:first + 3]
    h = _modulate(x, g_ref[...], shift, scale).astype(BF16)
    for c in range(FFN_DIM // FFN_CHUNK):
        sl = slice(c * FFN_CHUNK, (c + 1) * FFN_CHUNK)
        a_scr[:, sl] = (_silu(_dot(h, wg_ref[:, sl])) * _dot(h, wu_ref[:, sl])).astype(BF16)
    y = x + (0.5 * gate) * _dot(a_scr[...], wd_ref[...])
    if final:
        y = _rms(y, gf_ref[...])
    if ctx_steps is None:
        outs[0][...] = y
    else:
        is_ctx = pl.program_id(0) < ctx_steps

        @pl.when(is_ctx)
        def _():
            outs[0][...] = y

        @pl.when(jnp.logical_not(is_ctx))
        def _():
            outs[1][...] = y


def _ffn(x, mod, g, wg, wu, wd, gf, mod_spec, *, first, final, cast=(), ctx_tokens=None):
    t = x.shape[0]
    steps = t // FFN_TILE
    tile = pl.BlockSpec((FFN_TILE, D_MODEL), lambda i: (i, 0))
    chunk = lambda w: pl.BlockSpec((w.shape[0] // steps, w.shape[1]), lambda i: (i, 0))
    assert all(w.shape[0] % (steps * 16) == 0 for w in cast)
    if ctx_tokens is None:
        ctx_steps, out_specs, out_shape = None, [tile], [jax.ShapeDtypeStruct((t, D_MODEL), F32)]
    else:
        ctx_steps = ctx_tokens // FFN_TILE
        out_specs = _ctx_then_lat(D_MODEL, ctx_steps, FFN_TILE)
        out_shape = [jax.ShapeDtypeStruct((n, D_MODEL), F32) for n in (ctx_tokens, t - ctx_tokens)]
    out = pl.pallas_call(
        functools.partial(_ffn_kernel, first=first, final=final, n_cast=len(cast), ctx_steps=ctx_steps),
        grid=(steps,),
        in_specs=[tile, mod_spec, _resident((1, D_MODEL)),
                  _resident(wg.shape), _resident(wu.shape), _resident(wd.shape), _resident((1, D_MODEL))]
        + [chunk(w) for w in cast],
        out_specs=out_specs + [chunk(w) for w in cast],
        out_shape=out_shape + [jax.ShapeDtypeStruct(w.shape, BF16) for w in cast],
        scratch_shapes=[pltpu.VMEM((FFN_TILE, FFN_DIM), BF16)],
        compiler_params=_params(),
        name="ffn",
    )(x, mod, g, wg, wu, wd, gf, *cast)
    return out if len(out) > 1 else out[0]


def _swap16(x):
    lane = lax.broadcasted_iota(jnp.int32, x.shape, 1)
    return jnp.where(lane % 32 < 16, pltpu.roll(x, LANES - 16, 1), pltpu.roll(x, 16, 1))


def _proj_ctx_kernel(x_ref, mod_ref, g_ref, w_ref, qa_ref, qb_ref, ka_ref, va_ref, kb_ref, vb_ref, wkvt_ref):
    @pl.when(pl.program_id(0) == 0)
    def _():
        kv_cols = ((NA_WIDTH, 3 * NA_WIDTH), (3 * NA_WIDTH + SWA_Q_WIDTH, QKV_COLS))
        o = 0
        for lo_col, hi_col in kv_cols:
            wkvt_ref[o:o + hi_col - lo_col, :] = w_ref[:, lo_col:hi_col].astype(F32).T.astype(BF16)
            o += hi_col - lo_col

    mod = mod_ref[0]
    h = _modulate(x_ref[...], g_ref[...], mod[3:4], mod[4:5]).astype(BF16)
    qa_ref[...] = (_dot(h, w_ref[:, :NA_WIDTH]) * Q_SCALE).astype(qa_ref.dtype)
    qb_ref[...] = (_dot(h, w_ref[:, 3 * NA_WIDTH:3 * NA_WIDTH + SWA_Q_WIDTH]) * Q_SCALE).astype(qb_ref.dtype)
    kvt = _dot_nt(wkvt_ref[...], h)
    requests, _, _, seq = ka_ref.shape
    o = 0
    for ref in (ka_ref, va_ref, kb_ref, vb_ref):
        width = ref.shape[1] * HEAD_DIM
        for r in range(requests):
            ref[r] = kvt[o:o + width, r * seq:(r + 1) * seq].reshape(ref.shape[1:])
        o += width


def _rope(y, cos, sin):
    blocks = [y[:, j * LANES:(j + 1) * LANES] for j in range(y.shape[1] // LANES)]
    return [b * cos + _swap16(b) * sin for b in blocks]


def _proj_lat_kernel(x_ref, mod_ref, g_ref, w_ref, cos_ref, sin_ref,
                     qa_ref, ka_ref, va_ref, qb_ref, kb_ref, vb_ref):
    mod = mod_ref[0]
    h = _modulate(x_ref[...], g_ref[...], mod[3:4], mod[4:5]).astype(BF16)
    cos, sin = cos_ref[...], sin_ref[...]
    o = 3 * NA_WIDTH
    for j, b in enumerate(_rope(_dot(h, w_ref[:, o:o + SWA_Q_WIDTH]), cos, sin)):
        qb_ref[:, j * LANES:(j + 1) * LANES] = (b * Q_SCALE).astype(qb_ref.dtype)
    o += SWA_Q_WIDTH
    y = _dot(h, w_ref[:, o:o + 2 * SWA_KV_WIDTH])
    kb_ref[...] = _rope(y[:, :SWA_KV_WIDTH], cos, sin)[0].astype(kb_ref.dtype)
    vb_ref[...] = y[:, SWA_KV_WIDTH:].astype(vb_ref.dtype)
    o = 0
    for ref, scale in ((qa_ref, True), (ka_ref, False), (va_ref, False)):
        y = _dot(h, w_ref[:, o:o + NA_WIDTH])
        ref[...] = (y * Q_SCALE if scale else y).astype(ref.dtype)
        o += NA_WIDTH


def _proj_ctx(x, mod, g, w_qkv, *, seq, tokens_per_row, first_row):
    t = x.shape[0]
    tile = lambda w: pl.BlockSpec((TOKEN_TILE, w), lambda i: (i, 0))
    cache = lambda heads: pl.BlockSpec((TOKEN_TILE // seq, heads, HEAD_DIM, seq), lambda i: (i, 0, 0, 0))
    cache_shape = lambda heads: jax.ShapeDtypeStruct((t // seq, heads, HEAD_DIM, seq), F32)
    kv_heads = (NA_HEADS, NA_HEADS, SWA_KV_HEADS, SWA_KV_HEADS)
    return pl.pallas_call(
        _proj_ctx_kernel,
        grid=(t // TOKEN_TILE,),
        in_specs=[tile(D_MODEL), _mod_spec(tokens_per_row, first_row), _resident((1, D_MODEL)),
                  _resident(w_qkv.shape)],
        out_specs=[tile(NA_WIDTH), tile(SWA_Q_WIDTH)] + [cache(nh) for nh in kv_heads],
        out_shape=[jax.ShapeDtypeStruct((t, NA_WIDTH), BF16), jax.ShapeDtypeStruct((t, SWA_Q_WIDTH), BF16)]
        + [cache_shape(nh) for nh in kv_heads],
        scratch_shapes=[pltpu.VMEM((2 * NA_WIDTH + 2 * SWA_KV_WIDTH, D_MODEL), BF16)],
        compiler_params=_params(),
        name="proj_ctx",
    )(x, mod, g, w_qkv)


def _proj_lat(x, mod, g, w_qkv, rope, *, tokens_per_row, first_row):
    t = x.shape[0]
    tile = lambda w: pl.BlockSpec((TOKEN_TILE, w), lambda i: (i, 0))
    widths = (NA_WIDTH, NA_WIDTH, NA_WIDTH, SWA_Q_WIDTH, SWA_KV_WIDTH, SWA_KV_WIDTH)
    tiles_per_row = tokens_per_row // TOKEN_TILE
    return pl.pallas_call(
        _proj_lat_kernel,
        grid=(t // TOKEN_TILE,),
        in_specs=[tile(D_MODEL), _mod_spec(tokens_per_row, first_row), _resident((1, D_MODEL)),
                  _resident(w_qkv.shape)]
        + [pl.BlockSpec((TOKEN_TILE, LANES), lambda i: (i % tiles_per_row, 0))] * 2,
        out_specs=[tile(w) for w in widths],
        out_shape=[jax.ShapeDtypeStruct((t, w), BF16) for w in widths],
        compiler_params=_params(),
        name="proj_lat",
    )(x, mod, g, w_qkv, *rope)


def _lane_halves(shape):
    lane = lax.broadcasted_iota(jnp.int32, shape, 1) % LANES
    return lane < HEAD_DIM, lane >= HEAD_DIM


def _softmax_cols(cols, sink=None):
    m = jnp.max(functools.reduce(jnp.maximum, cols), axis=-1, keepdims=True)
    if sink is not None:
        m = jnp.maximum(m, sink)
    es = [jnp.exp2(c - m) for c in cols]
    l = jnp.sum(functools.reduce(jnp.add, es), axis=-1, keepdims=True)
    if sink is not None:
        l = l + jnp.exp2(sink - m)
    return es, l


def _cols(x):
    return [x[:, j * LANES:(j + 1) * LANES] for j in range(x.shape[1] // LANES)]


def _attn_ctx_kernel(sink_ref, qa_ref, qb_ref, ka_ref, va_ref, kb_ref, vb_ref, ya_ref, yb_ref):
    seq = qa_ref.shape[0]
    lo, hi = _lane_halves((seq, LANES))
    top = lax.broadcasted_iota(jnp.int32, (2 * seq, 1), 0) < seq
    zero = jnp.zeros((), BF16)

    def heads_of_block(q, kt, vt, sink):
        q = jnp.concatenate([jnp.where(lo, q, zero), jnp.where(hi, q, zero)], axis=0)
        es, l = _softmax_cols(_cols(_dot(q, kt)), sink)
        o = _dot_nt(jnp.concatenate(es, axis=1).astype(BF16), vt) * (1.0 / l)
        return jnp.where(lo, o[:seq], o[seq:])

    for p in range(NA_HEADS // 2):
        sl = slice(p * LANES, (p + 1) * LANES)
        kt = ka_ref[2 * p:2 * p + 2].reshape(LANES, seq).astype(BF16)
        vt = va_ref[2 * p:2 * p + 2].reshape(LANES, seq).astype(BF16)
        ya_ref[:, sl] = heads_of_block(qa_ref[:, sl], kt, vt, None).astype(ya_ref.dtype)
    for p in range(SWA_HEADS // 2):
        sl = slice(p * LANES, (p + 1) * LANES)
        kv = (2 * p) // (SWA_HEADS // SWA_KV_HEADS)
        kt, vt = kb_ref[kv].astype(BF16), vb_ref[kv].astype(BF16)
        kt, vt = jnp.concatenate([kt, kt], axis=0), jnp.concatenate([vt, vt], axis=0)
        sink = jnp.where(top, sink_ref[2 * p], sink_ref[2 * p + 1]) * LOG2E
        yb_ref[:, sl] = heads_of_block(qb_ref[:, sl], kt, vt, sink).astype(yb_ref.dtype)


def _attn_ctx(sink, qa, qb, ka, va, kb, vb, *, seq):
    t = qa.shape[0]
    tile = lambda w: pl.BlockSpec((seq, w), lambda b: (b, 0))
    cache = lambda heads: pl.BlockSpec((None, heads, HEAD_DIM, seq), lambda b: (b, 0, 0, 0))
    return pl.pallas_call(
        _attn_ctx_kernel,
        grid=(t // seq,),
        in_specs=[pl.BlockSpec(memory_space=pltpu.SMEM), tile(NA_WIDTH), tile(SWA_Q_WIDTH),
                  cache(NA_HEADS), cache(NA_HEADS), cache(SWA_KV_HEADS), cache(SWA_KV_HEADS)],
        out_specs=[tile(NA_WIDTH), tile(SWA_Q_WIDTH)],
        out_shape=[jax.ShapeDtypeStruct((t, NA_WIDTH), BF16), jax.ShapeDtypeStruct((t, SWA_Q_WIDTH), BF16)],
        compiler_params=_params(),
        name="attn_ctx",
    )(sink, qa, qb, ka, va, kb, vb)


def _attn_na_kernel(q_ref, k_ref, v_ref, ck_ref, cv_ref, bias_ref, y_ref, ck_scr, cv_scr, *, n, ctx):
    rows = n // GRID_W
    tile_q = NA_TILE_ROWS * GRID_W
    zero = jnp.zeros((), BF16)
    ck_scr[...] = ck_ref[...].reshape(NA_WIDTH, ctx).astype(BF16)
    cv_scr[...] = cv_ref[...].reshape(NA_WIDTH, ctx).astype(BF16)
    lo, hi = _lane_halves((tile_q, LANES))
    left_half = lax.broadcasted_iota(jnp.int32, (GRID_W, LANES), 1) < GRID_W
    empty = jnp.zeros((GRID_W, LANES), BF16)

    def tile(q0, k0, key_rows, lead, first):
        pairs = key_rows // 2
        for p in range(NA_HEADS // 2):
            sl = slice(p * LANES, (p + 1) * LANES)
            q = q_ref[pl.ds(q0, tile_q), sl]
            q = jnp.concatenate([jnp.where(lo, q, zero), jnp.where(hi, q, zero)], axis=0)
            s_nb = _dot_nt(q, k_ref[pl.ds(k0, key_rows * GRID_W), sl])
            s_ctx = _dot(q, ck_scr[sl, :])
            e_nb, e_ctx, ls = [], [], []
            for idx in range(2):
                for a in range(NA_TILE_ROWS):
                    qa = slice(idx * tile_q + a * GRID_W, idx * tile_q + (a + 1) * GRID_W)
                    inside = lambda i: first[a] <= i < first[a] + NA_ROWS
                    cols, where = [], []
                    for m in range(pairs):
                        if not (inside(2 * m) or inside(2 * m + 1)):
                            continue
                        blk = s_nb[qa, m * LANES:(m + 1) * LANES] + bias_ref[2 * p + idx, 2 * m - a - lead + NA_ROWS]
                        if not inside(2 * m + 1):
                            blk = jnp.where(left_half, blk, NEG_INF)
                        elif not inside(2 * m):
                            blk = jnp.where(left_half, NEG_INF, blk)
                        cols.append(blk)
                        where.append(m)
                    cols += [s_ctx[qa, j * LANES:(j + 1) * LANES] for j in range(ctx // LANES)]
                    es, l = _softmax_cols(cols)
                    es = [e.astype(BF16) for e in es]
                    e_nb.append(jnp.concatenate(
                        [es[where.index(m)] if m in where else empty for m in range(pairs)], axis=1))
                    e_ctx.append(jnp.concatenate(es[len(where):], axis=1))
                    ls.append(l)
            o = (_dot(jnp.concatenate(e_nb, axis=0), v_ref[pl.ds(k0, key_rows * GRID_W), sl])
                 + _dot_nt(jnp.concatenate(e_ctx, axis=0), cv_scr[sl, :]))
            o = o * (1.0 / jnp.concatenate(ls, axis=0))
            y_ref[pl.ds(q0, tile_q), sl] = jnp.where(lo, o[:tile_q], o[tile_q:]).astype(y_ref.dtype)

    half = NA_ROWS // 2
    tile(0, 0, NA_ROWS, 0, (0,) * NA_TILE_ROWS)

    for t in range(1, rows // NA_TILE_ROWS - 1):
        tile(t * tile_q, t * tile_q - half * GRID_W, NA_ROWS + NA_TILE_ROWS, half, tuple(range(NA_TILE_ROWS)))
    tile(n - tile_q, n - NA_ROWS * GRID_W, NA_ROWS, half, (0,) * NA_TILE_ROWS)


def _attn_na(q, k, v, ck, cv, bias, *, n, ctx):
    t = q.shape[0]
    tile = pl.BlockSpec((n, NA_WIDTH), lambda b: (b, 0))
    ctile = pl.BlockSpec((None, NA_HEADS, HEAD_DIM, ctx), lambda b: (b, 0, 0, 0))
    return pl.pallas_call(
        functools.partial(_attn_na_kernel, n=n, ctx=ctx),
        grid=(t // n,),
        in_specs=[tile, tile, tile, ctile, ctile, _resident(bias.shape)],
        out_specs=tile,
        out_shape=jax.ShapeDtypeStruct((t, NA_WIDTH), BF16),
        scratch_shapes=[pltpu.VMEM((NA_WIDTH, ctx), BF16)] * 2,
        compiler_params=_params(),
        name="attn_na",
    )(q, k, v, ck, cv, bias)


def _na_bias_table(rel_bias):
    rows = _na_bias_rows(rel_bias)
    shape = (NA_HEADS, 2 * NA_ROWS, GRID_W, LANES)
    return pl.pallas_call(
        _fill_na_bias,
        grid=(1,),
        in_specs=[pl.BlockSpec(rows.shape, lambda i: (0, 0, 0))],
        out_specs=pl.BlockSpec(shape, lambda i: (0, 0, 0, 0)),
        out_shape=jax.ShapeDtypeStruct(shape, F32),
        compiler_params=_params(),
        name="na_bias",
    )(rows)


def _na_bias_rows(rel_bias):
    side = GRID_W - NA_COLS
    z = jnp.pad(rel_bias * LOG2E, ((0, 0), (0, 0), (side, side + 1)))
    return jnp.pad(z, ((0, 0), (1, 1), (0, 0)), constant_values=NEG_INF)


def _fill_na_bias(rows_ref, bias_scr):
    q = lax.broadcasted_iota(jnp.int32, (GRID_W, LANES), 0)
    lane = lax.broadcasted_iota(jnp.int32, (GRID_W, LANES), 1)
    kc = lane % GRID_W
    col_start = jnp.clip(q - NA_COLS // 2, 0, GRID_W - NA_COLS)
    in_window = (kc >= col_start) & (kc < col_start + NA_COLS)
    for h in range(NA_HEADS):
        blocks = []
        for j in range(2 * NA_ROWS + 1):
            row = jnp.broadcast_to(rows_ref[h, j:j + 1, :], (GRID_W, LANES))
            rolled = pltpu.roll(row, LANES - GRID_W + 1, 1, stride=1, stride_axis=0)
            blocks.append(jnp.where(in_window, rolled, NEG_INF))
        for j in range(2 * NA_ROWS):
            bias_scr[h, j] = jnp.where(lane < GRID_W, blocks[j], pltpu.roll(blocks[j + 1], GRID_W, 1))


def _attn_swa_kernel(sink_ref, q_ref, k_ref, v_ref, ck_ref, cv_ref, y_ref, k_scr, v_scr, ck_scr, cv_scr, *, n, ctx):
    group = SWA_HEADS // SWA_KV_HEADS
    band = SWA_QTILE + 2 * SWA_BLOCK
    for src, csrc, dst, cdst in ((k_ref, ck_ref, k_scr, ck_scr), (v_ref, cv_ref, v_scr, cv_scr)):
        x = src[...].astype(F32)
        xr = pltpu.roll(x, HEAD_DIM, 1)
        first_half, second_half = _lane_halves(x.shape)
        dst[0] = jnp.where(first_half, x, xr).astype(BF16)
        dst[1] = jnp.where(second_half, x, xr).astype(BF16)
        for kv in range(SWA_KV_HEADS):
            c = csrc[kv].astype(BF16)
            cdst[kv] = jnp.concatenate([c, c], axis=0)

    qi = lax.broadcasted_iota(jnp.int32, (SWA_QTILE, band), 0)
    kj = lax.broadcasted_iota(jnp.int32, (SWA_QTILE, band), 1)
    lo, hi = _lane_halves((SWA_QTILE, LANES))
    top = lax.broadcasted_iota(jnp.int32, (2 * SWA_QTILE, 1), 0) < SWA_QTILE
    zero = jnp.zeros((), BF16)

    def block_body(b, carry):
        q0 = pl.multiple_of(b * SWA_QTILE, SWA_QTILE)
        k0 = pl.multiple_of(jnp.clip(q0 - SWA_BLOCK, 0, n - band), SWA_BLOCK)
        mask = jnp.where(jnp.abs(kj - qi + (k0 - q0)) <= SWA_WINDOW, 0.0, NEG_INF)
        mask2 = jnp.concatenate([mask, mask], axis=0)
        for p in range(SWA_HEADS // 2):
            kv = 2 * p // group
            sl = slice(p * LANES, (p + 1) * LANES)
            q = q_ref[pl.ds(q0, SWA_QTILE), sl]
            q = jnp.concatenate([jnp.where(lo, q, zero), jnp.where(hi, q, zero)], axis=0)
            s_band = _dot_nt(q, k_scr[kv, pl.ds(k0, band), :])
            s_ctx = _dot(q, ck_scr[kv])
            sink = jnp.where(top, sink_ref[2 * p], sink_ref[2 * p + 1]) * LOG2E
            es, l = _softmax_cols(_cols(s_band + mask2) + _cols(s_ctx), sink)
            e_band = jnp.concatenate(es[:band // LANES], axis=1).astype(BF16)
            e_ctx = jnp.concatenate(es[band // LANES:], axis=1).astype(BF16)
            o = (_dot(e_band, v_scr[kv, pl.ds(k0, band), :]) + _dot_nt(e_ctx, cv_scr[kv])) * (1.0 / l)
            y_ref[pl.ds(q0, SWA_QTILE), sl] = jnp.where(lo, o[:SWA_QTILE], o[SWA_QTILE:]).astype(y_ref.dtype)
        return carry

    lax.fori_loop(0, n // SWA_QTILE, block_body, 0, unroll=4)


def _attn_swa(sink, q, k, v, ck, cv, *, n, ctx):
    t = q.shape[0]
    tile = lambda w: pl.BlockSpec((n, w), lambda b: (b, 0))
    ctile = pl.BlockSpec((None, SWA_KV_HEADS, HEAD_DIM, ctx), lambda b: (b, 0, 0, 0))
    return pl.pallas_call(
        functools.partial(_attn_swa_kernel, n=n, ctx=ctx),
        grid=(t // n,),
        in_specs=[pl.BlockSpec(memory_space=pltpu.SMEM), tile(SWA_Q_WIDTH), tile(SWA_KV_WIDTH),
                  tile(SWA_KV_WIDTH), ctile, ctile],
        out_specs=tile(SWA_Q_WIDTH),
        out_shape=jax.ShapeDtypeStruct((t, SWA_Q_WIDTH), BF16),
        scratch_shapes=[pltpu.VMEM((SWA_KV_HEADS, n, SWA_KV_WIDTH), BF16)] * 2
        + [pltpu.VMEM((SWA_KV_HEADS, LANES, ctx), BF16)] * 2,
        compiler_params=_params(),
        name="attn_swa",
    )(sink, q, k, v, ck, cv)


def _rope_tables(n):
    half = HEAD_DIM // 4
    freqs = jnp.power(ROPE_BASE, -jnp.arange(half, dtype=F32) / half)
    t = jnp.arange(n)
    cos, sin = [], []
    for pos in (t // GRID_W, t % GRID_W):
        ang = pos.astype(F32)[:, None] * freqs[None, :]
        cos += [jnp.cos(ang), jnp.cos(ang)]
        sin += [-jnp.sin(ang), jnp.sin(ang)]
    cos, sin = jnp.concatenate(cos, axis=-1), jnp.concatenate(sin, axis=-1)
    reps = LANES // HEAD_DIM
    return jnp.tile(cos, (1, reps)), jnp.tile(sin, (1, reps))


def _merge_kernel(xc_ref, xl_ref, yac_ref, yal_ref, ybc_ref, ybl_ref, mod_ref, g_ref, win_ref, wba_ref, wbb_ref,
                  wout_ref, o_ref, *, ctx_steps):
    is_ctx = pl.program_id(0) < ctx_steps
    pick = lambda ctx_ref, lat_ref: jnp.where(is_ctx, ctx_ref[...], lat_ref[...])
    x, ya, yb = pick(xc_ref, xl_ref), pick(yac_ref, yal_ref), pick(ybc_ref, ybl_ref)
    mod = mod_ref[0]
    h = _modulate(x, g_ref[...], mod[3:4], mod[4:5]).astype(BF16)
    a = jax.nn.sigmoid(_dot(h, win_ref[:, QKV_COLS:QKV_COLS + D_MODEL])) * _dot(ya, wba_ref[...])
    b = jax.nn.sigmoid(_dot(h, win_ref[:, QKV_COLS + D_MODEL:])) * _dot(yb, wbb_ref[...])
    o_ref[...] = x + mod[5:6] * _dot((a + b).astype(BF16), wout_ref[...])


def _merge(x, ya, yb, mod, g, w_in, wba, wbb, wout, *, tokens_per_request):
    ctx_steps = x[0].shape[0] // MERGE_TILE
    t = x[0].shape[0] + x[1].shape[0]
    both = lambda w: _ctx_then_lat(w, ctx_steps, MERGE_TILE)
    return pl.pallas_call(
        functools.partial(_merge_kernel, ctx_steps=ctx_steps),
        grid=(t // MERGE_TILE,),
        in_specs=both(D_MODEL) + both(NA_WIDTH) + both(SWA_Q_WIDTH)
        + [_mod_spec_both(ctx_steps, tokens_per_request, MERGE_TILE), _resident((1, D_MODEL)),
           _resident(w_in.shape), _resident(wba.shape), _resident(wbb.shape), _resident(wout.shape)],
        out_specs=pl.BlockSpec((MERGE_TILE, D_MODEL), lambda i: (i, 0)),
        out_shape=jax.ShapeDtypeStruct((t, D_MODEL), F32),
        compiler_params=_params(),
        name="merge",
    )(*x, *ya, *yb, mod, g, w_in, wba, wbb, wout)


def kernel(x_prompt, x_sample, cache_na_k, cache_na_v, cache_swa_k, cache_swa_v, c, c_ctx, w_ada, b_ada,
           norm_ffn1, ffn1_w_gate, ffn1_w_up, ffn1_w_down, norm_mix, w_in, na_rel_bias, swa_sink,
           w_branch_na, w_branch_swa, w_out, norm_ffn2, ffn2_w_gate, ffn2_w_up, ffn2_w_down, norm_final):
    depth = w_ada.shape[0]
    assert depth == 1
    batch, seq, _ = x_prompt.shape
    dec_batch, dec_seq, _ = x_sample.shape
    past = cache_na_k.shape[2]
    layer = 0
    row = lambda v: v.reshape(1, D_MODEL)
    bf = lambda w: w.astype(BF16)

    cond = jnp.zeros((MOD_ROWS, D_MODEL), F32).at[0].set(c_ctx).at[1:1 + dec_batch].set(c)
    mod = _adaln(cond, w_ada[layer], b_ada[layer]).reshape(MOD_ROWS, N_MOD, D_MODEL)

    ffn1 = (row(norm_ffn1[layer]), bf(ffn1_w_gate[layer]), bf(ffn1_w_up[layer]), bf(ffn1_w_down[layer]),
            row(norm_final))
    g_mix = row(norm_mix[layer])
    sink = swa_sink[layer]
    ctx_rows = dict(tokens_per_row=batch * seq, first_row=0)
    lat_rows = dict(tokens_per_row=dec_seq, first_row=1)
    ctx_tokens = batch * seq

    later = (ffn2_w_gate[layer], ffn2_w_up[layer], ffn2_w_down[layer], w_in[layer],
             w_branch_na[layer], w_branch_swa[layer], w_out[layer])
    x_lat, *later = _ffn(x_sample.reshape(dec_batch * dec_seq, D_MODEL), mod, *ffn1,
                         _mod_spec(tile=FFN_TILE, **lat_rows), first=0, final=False, cast=later)
    ffn2 = (row(norm_ffn2[layer]), *later[:3], row(norm_final))
    w_qkv = later[3]
    merge_w = tuple(later[3:])

    x_ctx = _ffn(x_prompt.reshape(ctx_tokens, D_MODEL), mod, *ffn1, _mod_spec(tile=FFN_TILE, **ctx_rows),
                 first=0, final=False)
    qa, qb, *new_cache = _proj_ctx(x_ctx, mod, g_mix, w_qkv, seq=seq, **ctx_rows)
    ya_ctx, yb_ctx = _attn_ctx(sink, qa, qb, *new_cache, seq=seq)

    qa, kal, val, qb, kbl, vbl = _proj_lat(x_lat, mod, g_mix, w_qkv, _rope_tables(dec_seq), **lat_rows)
    transposed = lambda cache: jnp.transpose(cache[:, layer], (0, 2, 3, 1))
    ya_lat = _attn_na(qa, kal, val, transposed(cache_na_k), transposed(cache_na_v),
                      _na_bias_table(na_rel_bias[layer]), n=dec_seq, ctx=past)
    yb_lat = _attn_swa(sink, qb, kbl, vbl, transposed(cache_swa_k), transposed(cache_swa_v), n=dec_seq, ctx=past)

    x = _merge((x_ctx, x_lat), (ya_ctx, ya_lat), (yb_ctx, yb_lat), mod, g_mix, *merge_w, tokens_per_request=dec_seq)
    y_prompt, y_sample = _ffn(x, mod, *ffn2, _mod_spec_both(ctx_tokens // FFN_TILE, dec_seq, FFN_TILE),
                              first=6, final=True, ctx_tokens=ctx_tokens)
    y_prompt = y_prompt.reshape(batch, seq, D_MODEL)
    y_sample = y_sample.reshape(dec_batch, dec_seq, D_MODEL)

    new_cache = [jnp.transpose(t, (0, 3, 1, 2))[:, None] for t in new_cache]
    return (y_prompt, y_sample, *new_cache)
```

```python
import functools

import jax
import jax.numpy as jnp
from jax import lax
from jax.experimental import pallas as pl
from jax.experimental.pallas import tpu as pltpu

F32 = jnp.float32
BF16 = jnp.bfloat16

D_MODEL = 1024
FFN_DIM = 2816
HEAD_DIM = 64
N_MOD = 9
GRID_W = 64
NA_HEADS = 8
NA_ROWS = 8
NA_COLS = 16
SWA_HEADS = 8
SWA_KV_HEADS = 2
SWA_WINDOW = 128
SWA_BLOCK = 128
ROPE_BASE = 10000.0
EPS = 1e-6
NEG_INF = -1e30
NA_WIDTH = NA_HEADS * HEAD_DIM
SWA_Q_WIDTH = SWA_HEADS * HEAD_DIM
SWA_KV_WIDTH = SWA_KV_HEADS * HEAD_DIM
QKV_COLS = 3 * NA_WIDTH + SWA_Q_WIDTH + 2 * SWA_KV_WIDTH
LANES = 128
MOD_ROWS = 16
VMEM_LIMIT = 56 * 1024 * 1024
TOKEN_TILE = 1024
MERGE_TILE = 512
FFN_TILE = 1024
FFN_CHUNK = 256
ADALN_BLOCK = 2304
SWA_QTILE = 256
NA_TILE_ROWS = 4
LOG2E = 1.4426950408889634
Q_SCALE = HEAD_DIM ** -0.5 * LOG2E


def _dot(a, b):
    return jnp.dot(a, b, preferred_element_type=F32)


def _dot_nt(a, b):
    return lax.dot_general(a, b, (((1,), (1,)), ((), ())), preferred_element_type=F32)


def _silu(x):
    return x * jax.nn.sigmoid(x)


def _rms(x, g):
    return x * lax.rsqrt(jnp.mean(x * x, axis=-1, keepdims=True) + EPS) * g


def _modulate(x, g, shift, scale):
    return _rms(x, g) * (1.0 + scale) + shift


def _resident(shape):
    nd = len(shape)
    return pl.BlockSpec(shape, lambda *_: (0,) * nd, pipeline_mode=pl.Buffered(1))


def _params():
    return pltpu.CompilerParams(dimension_semantics=("arbitrary",), vmem_limit_bytes=VMEM_LIMIT)


def _adaln_kernel(c_ref, w_ref, b_ref, o_ref):
    s = _silu(c_ref[...]).astype(BF16)
    o_ref[...] = _dot(s, w_ref[...].astype(BF16)) + b_ref[...]


def _adaln(cond, w_ada, b_ada):
    n = w_ada.shape[1]
    blk = ADALN_BLOCK
    return pl.pallas_call(
        _adaln_kernel,
        grid=(n // blk,),
        in_specs=[pl.BlockSpec((MOD_ROWS, D_MODEL), lambda j: (0, 0)),
                  pl.BlockSpec((D_MODEL, blk), lambda j: (0, j)),
                  pl.BlockSpec((1, blk), lambda j: (0, j))],
        out_specs=pl.BlockSpec((MOD_ROWS, blk), lambda j: (0, j)),
        out_shape=jax.ShapeDtypeStruct((MOD_ROWS, n), F32),
        compiler_params=_params(),
        name="adaln",
    )(cond, w_ada, b_ada.reshape(1, n))


def _mod_spec(tokens_per_row, first_row, tile=TOKEN_TILE):
    tiles_per_row = tokens_per_row // tile
    return pl.BlockSpec((1, N_MOD, D_MODEL), lambda i: (first_row + i // tiles_per_row, 0, 0))


def _mod_spec_both(ctx_steps, tokens_per_request, tile):
    tiles_per_request = tokens_per_request // tile
    row = lambda i: jnp.where(i < ctx_steps, 0, 1 + (i - ctx_steps) // tiles_per_request)
    return pl.BlockSpec((1, N_MOD, D_MODEL), lambda i: (row(i), 0, 0))


def _ctx_then_lat(width, ctx_steps, tile):
    ctx = pl.BlockSpec((tile, width), lambda i: (jnp.minimum(i, ctx_steps - 1), 0))
    lat = pl.BlockSpec((tile, width), lambda i: (jnp.maximum(i - ctx_steps, 0), 0))
    return [ctx, lat]


def _ffn_kernel(x_ref, mod_ref, g_ref, wg_ref, wu_ref, wd_ref, gf_ref, *rest, first, final, n_cast, ctx_steps):
    n_out = 1 if ctx_steps is None else 2
    outs, a_scr = rest[n_cast:n_cast + n_out], rest[-1]
    for src, dst in zip(rest[:n_cast], rest[n_cast + n_out:-1]):
        dst[...] = src[...].astype(BF16)
    x = x_ref[...]
    mod = mod_ref[0]
    shift, scale, gate = mod[first:first + 1], mod[first + 1:first + 2], mod[first + 2:first + 3]
    h = _modulate(x, g_ref[...], shift, scale).astype(BF16)
    for c in range(FFN_DIM // FFN_CHUNK):
        sl = slice(c * FFN_CHUNK, (c + 1) * FFN_CHUNK)
        a_scr[:, sl] = (_silu(_dot(h, wg_ref[:, sl])) * _dot(h, wu_ref[:, sl])).astype(BF16)
    y = x + (0.5 * gate) * _dot(a_scr[...], wd_ref[...])
    if final:
        y = _rms(y, gf_ref[...])
    if ctx_steps is None:
        outs[0][...] = y
    else:
        is_ctx = pl.program_id(0) < ctx_steps

        @pl.when(is_ctx)
        def _():
            outs[0][...] = y

        @pl.when(jnp.logical_not(is_ctx))
        def _():
            outs[1][...] = y


def _ffn(x, mod, g, wg, wu, wd, gf, mod_spec, *, first, final, cast=(), ctx_tokens=None):
    t = x.shape[0]
    steps = t // FFN_TILE
    tile = pl.BlockSpec((FFN_TILE, D_MODEL), lambda i: (i, 0))
    chunk = lambda w: pl.BlockSpec((w.shape[0] // steps, w.shape[1]), lambda i: (i, 0))
    assert all(w.shape[0] % (steps * 16) == 0 for w in cast)
    if ctx_tokens is None:
        ctx_steps, out_specs, out_shape = None, [tile], [jax.ShapeDtypeStruct((t, D_MODEL), F32)]
    else:
        ctx_steps = ctx_tokens // FFN_TILE
        out_specs = _ctx_then_lat(D_MODEL, ctx_steps, FFN_TILE)
        out_shape = [jax.ShapeDtypeStruct((n, D_MODEL), F32) for n in (ctx_tokens, t - ctx_tokens)]
    out = pl.pallas_call(
        functools.partial(_ffn_kernel, first=first, final=final, n_cast=len(cast), ctx_steps=ctx_steps),
        grid=(steps,),
        in_specs=[tile, mod_spec, _resident((1, D_MODEL)),
                  _resident(wg.shape), _resident(wu.shape), _resident(wd.shape), _resident((1, D_MODEL))]
        + [chunk(w) for w in cast],
        out_specs=out_specs + [chunk(w) for w in cast],
        out_shape=out_shape + [jax.ShapeDtypeStruct(w.shape, BF16) for w in cast],
        scratch_shapes=[pltpu.VMEM((FFN_TILE, FFN_DIM), BF16)],
        compiler_params=_params(),
        name="ffn",
    )(x, mod, g, wg, wu, wd, gf, *cast)
    return out if len(out) > 1 else out[0]


def _swap16(x):
    lane = lax.broadcasted_iota(jnp.int32, x.shape, 1)
    return jnp.where(lane % 32 < 16, pltpu.roll(x, LANES - 16, 1), pltpu.roll(x, 16, 1))


def _proj_ctx_kernel(x_ref, mod_ref, g_ref, w_ref, qa_ref, qb_ref, ka_ref, va_ref, kb_ref, vb_ref, wkvt_ref):
    @pl.when(pl.program_id(0) == 0)
    def _():
        kv_cols = ((NA_WIDTH, 3 * NA_WIDTH), (3 * NA_WIDTH + SWA_Q_WIDTH, QKV_COLS))
        o = 0
        for lo_col, hi_col in kv_cols:
            wkvt_ref[o:o + hi_col - lo_col, :] = w_ref[:, lo_col:hi_col].astype(F32).T.astype(BF16)
            o += hi_col - lo_col

    mod = mod_ref[0]
    h = _modulate(x_ref[...], g_ref[...], mod[3:4], mod[4:5]).astype(BF16)
    qa_ref[...] = (_dot(h, w_ref[:, :NA_WIDTH]) * Q_SCALE).astype(qa_ref.dtype)
    qb_ref[...] = (_dot(h, w_ref[:, 3 * NA_WIDTH:3 * NA_WIDTH + SWA_Q_WIDTH]) * Q_SCALE).astype(qb_ref.dtype)
    kvt = _dot_nt(wkvt_ref[...], h)
    requests, _, _, seq = ka_ref.shape
    o = 0
    for ref in (ka_ref, va_ref, kb_ref, vb_ref):
        width = ref.shape[1] * HEAD_DIM
        for r in range(requests):
            ref[r] = kvt[o:o + width, r * seq:(r + 1) * seq].reshape(ref.shape[1:])
        o += width


def _rope(y, cos, sin):
    blocks = [y[:, j * LANES:(j + 1) * LANES] for j in range(y.shape[1] // LANES)]
    return [b * cos + _swap16(b) * sin for b in blocks]


def _proj_lat_kernel(x_ref, mod_ref, g_ref, w_ref, cos_ref, sin_ref,
                     qa_ref, ka_ref, va_ref, qb_ref, kb_ref, vb_ref):
    mod = mod_ref[0]
    h = _modulate(x_ref[...], g_ref[...], mod[3:4], mod[4:5]).astype(BF16)
    cos, sin = cos_ref[...], sin_ref[...]
    o = 3 * NA_WIDTH
    for j, b in enumerate(_rope(_dot(h, w_ref[:, o:o + SWA_Q_WIDTH]), cos, sin)):
        qb_ref[:, j * LANES:(j + 1) * LANES] = (b * Q_SCALE).astype(qb_ref.dtype)
    o += SWA_Q_WIDTH
    y = _dot(h, w_ref[:, o:o + 2 * SWA_KV_WIDTH])
    kb_ref[...] = _rope(y[:, :SWA_KV_WIDTH], cos, sin)[0].astype(kb_ref.dtype)
    vb_ref[...] = y[:, SWA_KV_WIDTH:].astype(vb_ref.dtype)
    o = 0
    for ref, scale in ((qa_ref, True), (ka_ref, False), (va_ref, False)):
        y = _dot(h, w_ref[:, o:o + NA_WIDTH])
        ref[...] = (y * Q_SCALE if scale else y).astype(ref.dtype)
        o += NA_WIDTH


def _proj_ctx(x, mod, g, w_qkv, *, seq, tokens_per_row, first_row):
    t = x.shape[0]
    tile = lambda w: pl.BlockSpec((TOKEN_TILE, w), lambda i: (i, 0))
    cache = lambda heads: pl.BlockSpec((TOKEN_TILE // seq, heads, HEAD_DIM, seq), lambda i: (i, 0, 0, 0))
    cache_shape = lambda heads: jax.ShapeDtypeStruct((t // seq, heads, HEAD_DIM, seq), F32)
    kv_heads = (NA_HEADS, NA_HEADS, SWA_KV_HEADS, SWA_KV_HEADS)
    return pl.pallas_call(
        _proj_ctx_kernel,
        grid=(t // TOKEN_TILE,),
        in_specs=[tile(D_MODEL), _mod_spec(tokens_per_row, first_row), _resident((1, D_MODEL)),
                  _resident(w_qkv.shape)],
        out_specs=[tile(NA_WIDTH), tile(SWA_Q_WIDTH)] + [cache(nh) for nh in kv_heads],
        out_shape=[jax.ShapeDtypeStruct((t, NA_WIDTH), BF16), jax.ShapeDtypeStruct((t, SWA_Q_WIDTH), BF16)]
        + [cache_shape(nh) for nh in kv_heads],
        scratch_shapes=[pltpu.VMEM((2 * NA_WIDTH + 2 * SWA_KV_WIDTH, D_MODEL), BF16)],
        compiler_params=_params(),
        name="proj_ctx",
    )(x, mod, g, w_qkv)


def _proj_lat(x, mod, g, w_qkv, rope, *, tokens_per_row, first_row):
    t = x.shape[0]
    tile = lambda w: pl.BlockSpec((TOKEN_TILE, w), lambda i: (i, 0))
    widths = (NA_WIDTH, NA_WIDTH, NA_WIDTH, SWA_Q_WIDTH, SWA_KV_WIDTH, SWA_KV_WIDTH)
    tiles_per_row = tokens_per_row // TOKEN_TILE
    return pl.pallas_call(
        _proj_lat_kernel,
        grid=(t // TOKEN_TILE,),
        in_specs=[tile(D_MODEL), _mod_spec(tokens_per_row, first_row), _resident((1, D_MODEL)),
                  _resident(w_qkv.shape)]
        + [pl.BlockSpec((TOKEN_TILE, LANES), lambda i: (i % tiles_per_row, 0))] * 2,
        out_specs=[tile(w) for w in widths],
        out_shape=[jax.ShapeDtypeStruct((t, w), BF16) for w in widths],
        compiler_params=_params(),
        name="proj_lat",
    )(x, mod, g, w_qkv, *rope)


def _lane_halves(shape):
    lane = lax.broadcasted_iota(jnp.int32, shape, 1) % LANES
    return lane < HEAD_DIM, lane >= HEAD_DIM


def _softmax_cols(cols, sink=None):
    m = jnp.max(functools.reduce(jnp.maximum, cols), axis=-1, keepdims=True)
    if sink is not None:
        m = jnp.maximum(m, sink)
    es = [jnp.exp2(c - m) for c in cols]
    l = jnp.sum(functools.reduce(jnp.add, es), axis=-1, keepdims=True)
    if sink is not None:
        l = l + jnp.exp2(sink - m)
    return es, l


def _cols(x):
    return [x[:, j * LANES:(j + 1) * LANES] for j in range(x.shape[1] // LANES)]


def _attn_ctx_kernel(sink_ref, qa_ref, qb_ref, ka_ref, va_ref, kb_ref, vb_ref, ya_ref, yb_ref):
    seq = qa_ref.shape[0]
    lo, hi = _lane_halves((seq, LANES))
    top = lax.broadcasted_iota(jnp.int32, (2 * seq, 1), 0) < seq
    zero = jnp.zeros((), BF16)

    def heads_of_block(q, kt, vt, sink):
        q = jnp.concatenate([jnp.where(lo, q, zero), jnp.where(hi, q, zero)], axis=0)
        es, l = _softmax_cols(_cols(_dot(q, kt)), sink)
        o = _dot_nt(jnp.concatenate(es, axis=1).astype(BF16), vt) * (1.0 / l)
        return jnp.where(lo, o[:seq], o[seq:])

    for p in range(NA_HEADS // 2):
        sl = slice(p * LANES, (p + 1) * LANES)
        kt = ka_ref[2 * p:2 * p + 2].reshape(LANES, seq).astype(BF16)
        vt = va_ref[2 * p:2 * p + 2].reshape(LANES, seq).astype(BF16)
        ya_ref[:, sl] = heads_of_block(qa_ref[:, sl], kt, vt, None).astype(ya_ref.dtype)
    for p in range(SWA_HEADS // 2):
        sl = slice(p * LANES, (p + 1) * LANES)
        kv = (2 * p) // (SWA_HEADS // SWA_KV_HEADS)
        kt, vt = kb_ref[kv].astype(BF16), vb_ref[kv].astype(BF16)
        kt, vt = jnp.concatenate([kt, kt], axis=0), jnp.concatenate([vt, vt], axis=0)
        sink = jnp.where(top, sink_ref[2 * p], sink_ref[2 * p + 1]) * LOG2E
        yb_ref[:, sl] = heads_of_block(qb_ref[:, sl], kt, vt, sink).astype(yb_ref.dtype)


def _attn_ctx(sink, qa, qb, ka, va, kb, vb, *, seq):
    t = qa.shape[0]
    tile = lambda w: pl.BlockSpec((seq, w), lambda b: (b, 0))
    cache = lambda heads: pl.BlockSpec((None, heads, HEAD_DIM, seq), lambda b: (b, 0, 0, 0))
    return pl.pallas_call(
        _attn_ctx_kernel,
        grid=(t // seq,),
        in_specs=[pl.BlockSpec(memory_space=pltpu.SMEM), tile(NA_WIDTH), tile(SWA_Q_WIDTH),
                  cache(NA_HEADS), cache(NA_HEADS), cache(SWA_KV_HEADS), cache(SWA_KV_HEADS)],
        out_specs=[tile(NA_WIDTH), tile(SWA_Q_WIDTH)],
        out_shape=[jax.ShapeDtypeStruct((t, NA_WIDTH), BF16), jax.ShapeDtypeStruct((t, SWA_Q_WIDTH), BF16)],
        compiler_params=_params(),
        name="attn_ctx",
    )(sink, qa, qb, ka, va, kb, vb)


def _attn_na_kernel(q_ref, k_ref, v_ref, ck_ref, cv_ref, bias_ref, y_ref, ck_scr, cv_scr, *, n, ctx):
    rows = n // GRID_W
    tile_q = NA_TILE_ROWS * GRID_W
    zero = jnp.zeros((), BF16)
    ck_scr[...] = ck_ref[...].reshape(NA_WIDTH, ctx).astype(BF16)
    cv_scr[...] = cv_ref[...].reshape(NA_WIDTH, ctx).astype(BF16)
    lo, hi = _lane_halves((tile_q, LANES))
    left_half = lax.broadcasted_iota(jnp.int32, (GRID_W, LANES), 1) < GRID_W
    empty = jnp.zeros((GRID_W, LANES), BF16)

    def tile(q0, k0, key_rows, lead, first):
        pairs = key_rows // 2
        for p in range(NA_HEADS // 2):
            sl = slice(p * LANES, (p + 1) * LANES)
            q = q_ref[pl.ds(q0, tile_q), sl]
            q = jnp.concatenate([jnp.where(lo, q, zero), jnp.where(hi, q, zero)], axis=0)
            s_nb = _dot_nt(q, k_ref[pl.ds(k0, key_rows * GRID_W), sl])
            s_ctx = _dot(q, ck_scr[sl, :])
            e_nb, e_ctx, ls = [], [], []
            for idx in range(2):
                for a in range(NA_TILE_ROWS):
                    qa = slice(idx * tile_q + a * GRID_W, idx * tile_q + (a + 1) * GRID_W)
                    inside = lambda i: first[a] <= i < first[a] + NA_ROWS
                    cols, where = [], []
                    for m in range(pairs):
                        if not (inside(2 * m) or inside(2 * m + 1)):
                            continue
                        blk = s_nb[qa, m * LANES:(m + 1) * LANES] + bias_ref[2 * p + idx, 2 * m - a - lead + NA_ROWS]
                        if not inside(2 * m + 1):
                            blk = jnp.where(left_half, blk, NEG_INF)
                        elif not inside(2 * m):
                            blk = jnp.where(left_half, NEG_INF, blk)
                        cols.append(blk)
                        where.append(m)
                    cols += [s_ctx[qa, j * LANES:(j + 1) * LANES] for j in range(ctx // LANES)]
                    es, l = _softmax_cols(cols)
                    es = [e.astype(BF16) for e in es]
                    e_nb.append(jnp.concatenate(
                        [es[where.index(m)] if m in where else empty for m in range(pairs)], axis=1))
                    e_ctx.append(jnp.concatenate(es[len(where):], axis=1))
                    ls.append(l)
            o = (_dot(jnp.concatenate(e_nb, axis=0), v_ref[pl.ds(k0, key_rows * GRID_W), sl])
                 + _dot_nt(jnp.concatenate(e_ctx, axis=0), cv_scr[sl, :]))
            o = o * (1.0 / jnp.concatenate(ls, axis=0))
            y_ref[pl.ds(q0, tile_q), sl] = jnp.where(lo, o[:tile_q], o[tile_q:]).astype(y_ref.dtype)

    half = NA_ROWS // 2
    tile(0, 0, NA_ROWS, 0, (0,) * NA_TILE_ROWS)

    for t in range(1, rows // NA_TILE_ROWS - 1):
        tile(t * tile_q, t * tile_q - half * GRID_W, NA_ROWS + NA_TILE_ROWS, half, tuple(range(NA_TILE_ROWS)))
    tile(n - tile_q, n - NA_ROWS * GRID_W, NA_ROWS, half, (0,) * NA_TILE_ROWS)


def _attn_na(q, k, v, ck, cv, bias, *, n, ctx):
    t = q.shape[0]
    tile = pl.BlockSpec((n, NA_WIDTH), lambda b: (b, 0))
    ctile = pl.BlockSpec((None, NA_HEADS, HEAD_DIM, ctx), lambda b: (b, 0, 0, 0))
    return pl.pallas_call(
        functools.partial(_attn_na_kernel, n=n, ctx=ctx),
        grid=(t // n,),
        in_specs=[tile, tile, tile, ctile, ctile, _resident(bias.shape)],
        out_specs=tile,
        out_shape=jax.ShapeDtypeStruct((t, NA_WIDTH), BF16),
        scratch_shapes=[pltpu.VMEM((NA_WIDTH, ctx), BF16)] * 2,
        compiler_params=_params(),
        name="attn_na",
    )(q, k, v, ck, cv, bias)


def _na_bias_table(rel_bias):
    rows = _na_bias_rows(rel_bias)
    shape = (NA_HEADS, 2 * NA_ROWS, GRID_W, LANES)
    return pl.pallas_call(
        _fill_na_bias,
        grid=(1,),
        in_specs=[pl.BlockSpec(rows.shape, lambda i: (0, 0, 0))],
        out_specs=pl.BlockSpec(shape, lambda i: (0, 0, 0, 0)),
        out_shape=jax.ShapeDtypeStruct(shape, F32),
        compiler_params=_params(),
        name="na_bias",
    )(rows)


def _na_bias_rows(rel_bias):
    side = GRID_W - NA_COLS
    z = jnp.pad(rel_bias * LOG2E, ((0, 0), (0, 0), (side, side + 1)))
    return jnp.pad(z, ((0, 0), (1, 1), (0, 0)), constant_values=NEG_INF)


def _fill_na_bias(rows_ref, bias_scr):
    q = lax.broadcasted_iota(jnp.int32, (GRID_W, LANES), 0)
    lane = lax.broadcasted_iota(jnp.int32, (GRID_W, LANES), 1)
    kc = lane % GRID_W
    col_start = jnp.clip(q - NA_COLS // 2, 0, GRID_W - NA_COLS)
    in_window = (kc >= col_start) & (kc < col_start + NA_COLS)
    for h in range(NA_HEADS):
        blocks = []
        for j in range(2 * NA_ROWS + 1):
            row = jnp.broadcast_to(rows_ref[h, j:j + 1, :], (GRID_W, LANES))
            rolled = pltpu.roll(row, LANES - GRID_W + 1, 1, stride=1, stride_axis=0)
            blocks.append(jnp.where(in_window, rolled, NEG_INF))
        for j in range(2 * NA_ROWS):
            bias_scr[h, j] = jnp.where(lane < GRID_W, blocks[j], pltpu.roll(blocks[j + 1], GRID_W, 1))


def _attn_swa_kernel(sink_ref, q_ref, k_ref, v_ref, ck_ref, cv_ref, y_ref, k_scr, v_scr, ck_scr, cv_scr, *, n, ctx):
    group = SWA_HEADS // SWA_KV_HEADS
    band = SWA_QTILE + 2 * SWA_BLOCK
    for src, csrc, dst, cdst in ((k_ref, ck_ref, k_scr, ck_scr), (v_ref, cv_ref, v_scr, cv_scr)):
        x = src[...].astype(F32)
        xr = pltpu.roll(x, HEAD_DIM, 1)
        first_half, second_half = _lane_halves(x.shape)
        dst[0] = jnp.where(first_half, x, xr).astype(BF16)
        dst[1] = jnp.where(second_half, x, xr).astype(BF16)
        for kv in range(SWA_KV_HEADS):
            c = csrc[kv].astype(BF16)
            cdst[kv] = jnp.concatenate([c, c], axis=0)

    qi = lax.broadcasted_iota(jnp.int32, (SWA_QTILE, band), 0)
    kj = lax.broadcasted_iota(jnp.int32, (SWA_QTILE, band), 1)
    lo, hi = _lane_halves((SWA_QTILE, LANES))
    top = lax.broadcasted_iota(jnp.int32, (2 * SWA_QTILE, 1), 0) < SWA_QTILE
    zero = jnp.zeros((), BF16)

    def block_body(b, carry):
        q0 = pl.multiple_of(b * SWA_QTILE, SWA_QTILE)
        k0 = pl.multiple_of(jnp.clip(q0 - SWA_BLOCK, 0, n - band), SWA_BLOCK)
        mask = jnp.where(jnp.abs(kj - qi + (k0 - q0)) <= SWA_WINDOW, 0.0, NEG_INF)
        mask2 = jnp.concatenate([mask, mask], axis=0)
        for p in range(SWA_HEADS // 2):
            kv = 2 * p // group
            sl = slice(p * LANES, (p + 1) * LANES)
            q = q_ref[pl.ds(q0, SWA_QTILE), sl]
            q = jnp.concatenate([jnp.where(lo, q, zero), jnp.where(hi, q, zero)], axis=0)
            s_band = _dot_nt(q, k_scr[kv, pl.ds(k0, band), :])
            s_ctx = _dot(q, ck_scr[kv])
            sink = jnp.where(top, sink_ref[2 * p], sink_ref[2 * p + 1]) * LOG2E
            es, l = _softmax_cols(_cols(s_band + mask2) + _cols(s_ctx), sink)
            e_band = jnp.concatenate(es[:band // LANES], axis=1).astype(BF16)
            e_ctx = jnp.concatenate(es[band // LANES:], axis=1).astype(BF16)
            o = (_dot(e_band, v_scr[kv, pl.ds(k0, band), :]) + _dot_nt(e_ctx, cv_scr[kv])) * (1.0 / l)
            y_ref[pl.ds(q0, SWA_QTILE), sl] = jnp.where(lo, o[:SWA_QTILE], o[SWA_QTILE:]).astype(y_ref.dtype)
        return carry

    lax.fori_loop(0, n // SWA_QTILE, block_body, 0, unroll=True)


def _attn_swa(sink, q, k, v, ck, cv, *, n, ctx):
    t = q.shape[0]
    tile = lambda w: pl.BlockSpec((n, w), lambda b: (b, 0))
    ctile = pl.BlockSpec((None, SWA_KV_HEADS, HEAD_DIM, ctx), lambda b: (b, 0, 0, 0))
    return pl.pallas_call(
        functools.partial(_attn_swa_kernel, n=n, ctx=ctx),
        grid=(t // n,),
        in_specs=[pl.BlockSpec(memory_space=pltpu.SMEM), tile(SWA_Q_WIDTH), tile(SWA_KV_WIDTH),
                  tile(SWA_KV_WIDTH), ctile, ctile],
        out_specs=tile(SWA_Q_WIDTH),
        out_shape=jax.ShapeDtypeStruct((t, SWA_Q_WIDTH), BF16),
        scratch_shapes=[pltpu.VMEM((SWA_KV_HEADS, n, SWA_KV_WIDTH), BF16)] * 2
        + [pltpu.VMEM((SWA_KV_HEADS, LANES, ctx), BF16)] * 2,
        compiler_params=_params(),
        name="attn_swa",
    )(sink, q, k, v, ck, cv)


def _rope_tables(n):
    half = HEAD_DIM // 4
    freqs = jnp.power(ROPE_BASE, -jnp.arange(half, dtype=F32) / half)
    t = jnp.arange(n)
    cos, sin = [], []
    for pos in (t // GRID_W, t % GRID_W):
        ang = pos.astype(F32)[:, None] * freqs[None, :]
        cos += [jnp.cos(ang), jnp.cos(ang)]
        sin += [-jnp.sin(ang), jnp.sin(ang)]
    cos, sin = jnp.concatenate(cos, axis=-1), jnp.concatenate(sin, axis=-1)
    reps = LANES // HEAD_DIM
    return jnp.tile(cos, (1, reps)), jnp.tile(sin, (1, reps))


def _merge_kernel(xc_ref, xl_ref, yac_ref, yal_ref, ybc_ref, ybl_ref, mod_ref, g_ref, win_ref, wba_ref, wbb_ref,
                  wout_ref, o_ref, *, ctx_steps):
    is_ctx = pl.program_id(0) < ctx_steps
    pick = lambda ctx_ref, lat_ref: jnp.where(is_ctx, ctx_ref[...], lat_ref[...])
    x, ya, yb = pick(xc_ref, xl_ref), pick(yac_ref, yal_ref), pick(ybc_ref, ybl_ref)
    mod = mod_ref[0]
    h = _modulate(x, g_ref[...], mod[3:4], mod[4:5]).astype(BF16)
    a = jax.nn.sigmoid(_dot(h, win_ref[:, QKV_COLS:QKV_COLS + D_MODEL])) * _dot(ya, wba_ref[...])
    b = jax.nn.sigmoid(_dot(h, win_ref[:, QKV_COLS + D_MODEL:])) * _dot(yb, wbb_ref[...])
    o_ref[...] = x + mod[5:6] * _dot((a + b).astype(BF16), wout_ref[...])


def _merge(x, ya, yb, mod, g, w_in, wba, wbb, wout, *, tokens_per_request):
    ctx_steps = x[0].shape[0] // MERGE_TILE
    t = x[0].shape[0] + x[1].shape[0]
    both = lambda w: _ctx_then_lat(w, ctx_steps, MERGE_TILE)
    return pl.pallas_call(
        functools.partial(_merge_kernel, ctx_steps=ctx_steps),
        grid=(t // MERGE_TILE,),
        in_specs=both(D_MODEL) + both(NA_WIDTH) + both(SWA_Q_WIDTH)
        + [_mod_spec_both(ctx_steps, tokens_per_request, MERGE_TILE), _resident((1, D_MODEL)),
           _resident(w_in.shape), _resident(wba.shape), _resident(wbb.shape), _resident(wout.shape)],
        out_specs=pl.BlockSpec((MERGE_TILE, D_MODEL), lambda i: (i, 0)),
        out_shape=jax.ShapeDtypeStruct((t, D_MODEL), F32),
        compiler_params=_params(),
        name="merge",
    )(*x, *ya, *yb, mod, g, w_in, wba, wbb, wout)


def kernel(x_prompt, x_sample, cache_na_k, cache_na_v, cache_swa_k, cache_swa_v, c, c_ctx, w_ada, b_ada,
           norm_ffn1, ffn1_w_gate, ffn1_w_up, ffn1_w_down, norm_mix, w_in, na_rel_bias, swa_sink,
           w_branch_na, w_branch_swa, w_out, norm_ffn2, ffn2_w_gate, ffn2_w_up, ffn2_w_down, norm_final):
    depth = w_ada.shape[0]
    assert depth == 1
    batch, seq, _ = x_prompt.shape
    dec_batch, dec_seq, _ = x_sample.shape
    past = cache_na_k.shape[2]
    layer = 0
    row = lambda v: v.reshape(1, D_MODEL)
    bf = lambda w: w.astype(BF16)

    cond = jnp.zeros((MOD_ROWS, D_MODEL), F32).at[0].set(c_ctx).at[1:1 + dec_batch].set(c)
    mod = _adaln(cond, w_ada[layer], b_ada[layer]).reshape(MOD_ROWS, N_MOD, D_MODEL)

    ffn1 = (row(norm_ffn1[layer]), bf(ffn1_w_gate[layer]), bf(ffn1_w_up[layer]), bf(ffn1_w_down[layer]),
            row(norm_final))
    g_mix = row(norm_mix[layer])
    sink = swa_sink[layer]
    ctx_rows = dict(tokens_per_row=batch * seq, first_row=0)
    lat_rows = dict(tokens_per_row=dec_seq, first_row=1)
    ctx_tokens = batch * seq

    later = (ffn2_w_gate[layer], ffn2_w_up[layer], ffn2_w_down[layer], w_in[layer],
             w_branch_na[layer], w_branch_swa[layer], w_out[layer])
    x_lat, *later = _ffn(x_sample.reshape(dec_batch * dec_seq, D_MODEL), mod, *ffn1,
                         _mod_spec(tile=FFN_TILE, **lat_rows), first=0, final=False, cast=later)
    ffn2 = (row(norm_ffn2[layer]), *later[:3], row(norm_final))
    w_qkv = later[3]
    merge_w = tuple(later[3:])

    x_ctx = _ffn(x_prompt.reshape(ctx_tokens, D_MODEL), mod, *ffn1, _mod_spec(tile=FFN_TILE, **ctx_rows),
                 first=0, final=False)
    qa, qb, *new_cache = _proj_ctx(x_ctx, mod, g_mix, w_qkv, seq=seq, **ctx_rows)
    ya_ctx, yb_ctx = _attn_ctx(sink, qa, qb, *new_cache, seq=seq)

    qa, kal, val, qb, kbl, vbl = _proj_lat(x_lat, mod, g_mix, w_qkv, _rope_tables(dec_seq), **lat_rows)
    transposed = lambda cache: jnp.transpose(cache[:, layer], (0, 2, 3, 1))
    ya_lat = _attn_na(qa, kal, val, transposed(cache_na_k), transposed(cache_na_v),
                      _na_bias_table(na_rel_bias[layer]), n=dec_seq, ctx=past)
    yb_lat = _attn_swa(sink, qb, kbl, vbl, transposed(cache_swa_k), transposed(cache_swa_v), n=dec_seq, ctx=past)

    x = _merge((x_ctx, x_lat), (ya_ctx, ya_lat), (yb_ctx, yb_lat), mod, g_mix, *merge_w, tokens_per_request=dec_seq)
    y_prompt, y_sample = _ffn(x, mod, *ffn2, _mod_spec_both(ctx_tokens // FFN_TILE, dec_seq, FFN_TILE),
                              first=6, final=True, ctx_tokens=ctx_tokens)
    y_prompt = y_prompt.reshape(batch, seq, D_MODEL)
    y_sample = y_sample.reshape(dec_batch, dec_seq, D_MODEL)

    new_cache = [jnp.transpose(t, (0, 3, 1, 2))[:, None] for t in new_cache]
    return (y_prompt, y_sample, *new_cache)
```

```python
import functools

import jax
import jax.numpy as jnp
from jax import lax
from jax.experimental import pallas as pl
from jax.experimental.pallas import tpu as pltpu

F32 = jnp.float32
BF16 = jnp.bfloat16

D_MODEL = 1024
FFN_DIM = 2816
HEAD_DIM = 64
N_MOD = 9
GRID_W = 64
NA_HEADS = 8
NA_ROWS = 8
NA_COLS = 16
SWA_HEADS = 8
SWA_KV_HEADS = 2
SWA_WINDOW = 128
SWA_BLOCK = 128
ROPE_BASE = 10000.0
EPS = 1e-6
NEG_INF = -1e30
NA_WIDTH = NA_HEADS * HEAD_DIM
SWA_Q_WIDTH = SWA_HEADS * HEAD_DIM
SWA_KV_WIDTH = SWA_KV_HEADS * HEAD_DIM
QKV_COLS = 3 * NA_WIDTH + SWA_Q_WIDTH + 2 * SWA_KV_WIDTH
LANES = 128
MOD_ROWS = 16
VMEM_LIMIT = 56 * 1024 * 1024
TOKEN_TILE = 1024
MERGE_TILE = 512
FFN_TILE = 1024
FFN_CHUNK = 256
ADALN_BLOCK = 2304
SWA_QTILE = 256
NA_TILE_ROWS = 4
LOG2E = 1.4426950408889634
Q_SCALE = HEAD_DIM ** -0.5 * LOG2E


def _dot(a, b):
    return jnp.dot(a, b, preferred_element_type=F32)


def _dot_nt(a, b):
    return lax.dot_general(a, b, (((1,), (1,)), ((), ())), preferred_element_type=F32)


def _silu(x):
    return x * jax.nn.sigmoid(x)


def _rms(x, g):
    return x * lax.rsqrt(jnp.mean(x * x, axis=-1, keepdims=True) + EPS) * g


def _modulate(x, g, shift, scale):
    return _rms(x, g) * (1.0 + scale) + shift


def _resident(shape):
    nd = len(shape)
    return pl.BlockSpec(shape, lambda *_: (0,) * nd, pipeline_mode=pl.Buffered(1))


def _params():
    return pltpu.CompilerParams(dimension_semantics=("arbitrary",), vmem_limit_bytes=VMEM_LIMIT)


def _adaln_kernel(c_ref, w_ref, b_ref, o_ref):
    s = _silu(c_ref[...]).astype(BF16)
    o_ref[...] = _dot(s, w_ref[...].astype(BF16)) + b_ref[...]


def _adaln(cond, w_ada, b_ada):
    n = w_ada.shape[1]
    blk = ADALN_BLOCK
    return pl.pallas_call(
        _adaln_kernel,
        grid=(n // blk,),
        in_specs=[pl.BlockSpec((MOD_ROWS, D_MODEL), lambda j: (0, 0)),
                  pl.BlockSpec((D_MODEL, blk), lambda j: (0, j)),
                  pl.BlockSpec((1, blk), lambda j: (0, j))],
        out_specs=pl.BlockSpec((MOD_ROWS, blk), lambda j: (0, j)),
        out_shape=jax.ShapeDtypeStruct((MOD_ROWS, n), F32),
        compiler_params=_params(),
        name="adaln",
    )(cond, w_ada, b_ada.reshape(1, n))


def _mod_spec(tokens_per_row, first_row, tile=TOKEN_TILE):
    tiles_per_row = tokens_per_row // tile
    return pl.BlockSpec((1, N_MOD, D_MODEL), lambda i: (first_row + i // tiles_per_row, 0, 0))


def _mod_spec_both(ctx_steps, tokens_per_request, tile):
    tiles_per_request = tokens_per_request // tile
    row = lambda i: jnp.where(i < ctx_steps, 0, 1 + (i - ctx_steps) // tiles_per_request)
    return pl.BlockSpec((1, N_MOD, D_MODEL), lambda i: (row(i), 0, 0))


def _ctx_then_lat(width, ctx_steps, tile):
    ctx = pl.BlockSpec((tile, width), lambda i: (jnp.minimum(i, ctx_steps - 1), 0))
    lat = pl.BlockSpec((tile, width), lambda i: (jnp.maximum(i - ctx_steps, 0), 0))
    return [ctx, lat]


def _ffn_kernel(x_ref, mod_ref, g_ref, wg_ref, wu_ref, wd_ref, gf_ref, *rest, first, final, n_cast, ctx_steps):
    n_out = 1 if ctx_steps is None else 2
    outs, a_scr = rest[n_cast:n_cast + n_out], rest[-1]
    for src, dst in zip(rest[:n_cast], rest[n_cast + n_out:-1]):
        dst[...] = src[...].astype(BF16)
    x = x_ref[...]
    mod = mod_ref[0]
    shift, scale, gate = mod[first:first + 1], mod[first + 1:first + 2], mod[first + 2:first + 3]
    h = _modulate(x, g_ref[...], shift, scale).astype(BF16)
    for c in range(FFN_DIM // FFN_CHUNK):
        sl = slice(c * FFN_CHUNK, (c + 1) * FFN_CHUNK)
        a_scr[:, sl] = (_silu(_dot(h, wg_ref[:, sl])) * _dot(h, wu_ref[:, sl])).astype(BF16)
    y = x + (0.5 * gate) * _dot(a_scr[...], wd_ref[...])
    if final:
        y = _rms(y, gf_ref[...])
    if ctx_steps is None:
        outs[0][...] = y
    else:
        is_ctx = pl.program_id(0) < ctx_steps

        @pl.when(is_ctx)
        def _():
            outs[0][...] = y

        @pl.when(jnp.logical_not(is_ctx))
        def _():
            outs[1][...] = y


def _ffn(x, mod, g, wg, wu, wd, gf, mod_spec, *, first, final, cast=(), ctx_tokens=None):
    t = x.shape[0]
    steps = t // FFN_TILE
    tile = pl.BlockSpec((FFN_TILE, D_MODEL), lambda i: (i, 0))
    chunk = lambda w: pl.BlockSpec((w.shape[0] // steps, w.shape[1]), lambda i: (i, 0))
    assert all(w.shape[0] % (steps * 16) == 0 for w in cast)
    if ctx_tokens is None:
        ctx_steps, out_specs, out_shape = None, [tile], [jax.ShapeDtypeStruct((t, D_MODEL), F32)]
    else:
        ctx_steps = ctx_tokens // FFN_TILE
        out_specs = _ctx_then_lat(D_MODEL, ctx_steps, FFN_TILE)
        out_shape = [jax.ShapeDtypeStruct((n, D_MODEL), F32) for n in (ctx_tokens, t - ctx_tokens)]
    out = pl.pallas_call(
        functools.partial(_ffn_kernel, first=first, final=final, n_cast=len(cast), ctx_steps=ctx_steps),
        grid=(steps,),
        in_specs=[tile, mod_spec, _resident((1, D_MODEL)),
                  _resident(wg.shape), _resident(wu.shape), _resident(wd.shape), _resident((1, D_MODEL))]
        + [chunk(w) for w in cast],
        out_specs=out_specs + [chunk(w) for w in cast],
        out_shape=out_shape + [jax.ShapeDtypeStruct(w.shape, BF16) for w in cast],
        scratch_shapes=[pltpu.VMEM((FFN_TILE, FFN_DIM), BF16)],
        compiler_params=_params(),
        name="ffn",
    )(x, mod, g, wg, wu, wd, gf, *cast)
    return out if len(out) > 1 else out[0]


def _swap16(x):
    lane = lax.broadcasted_iota(jnp.int32, x.shape, 1)
    return jnp.where(lane % 32 < 16, pltpu.roll(x, LANES - 16, 1), pltpu.roll(x, 16, 1))


def _proj_ctx_kernel(x_ref, mod_ref, g_ref, w_ref, qa_ref, qb_ref, ka_ref, va_ref, kb_ref, vb_ref, wkvt_ref):
    @pl.when(pl.program_id(0) == 0)
    def _():
        kv_cols = ((NA_WIDTH, 3 * NA_WIDTH), (3 * NA_WIDTH + SWA_Q_WIDTH, QKV_COLS))
        o = 0
        for lo_col, hi_col in kv_cols:
            wkvt_ref[o:o + hi_col - lo_col, :] = w_ref[:, lo_col:hi_col].astype(F32).T.astype(BF16)
            o += hi_col - lo_col

    mod = mod_ref[0]
    h = _modulate(x_ref[...], g_ref[...], mod[3:4], mod[4:5]).astype(BF16)
    qa_ref[...] = (_dot(h, w_ref[:, :NA_WIDTH]) * Q_SCALE).astype(qa_ref.dtype)
    qb_ref[...] = (_dot(h, w_ref[:, 3 * NA_WIDTH:3 * NA_WIDTH + SWA_Q_WIDTH]) * Q_SCALE).astype(qb_ref.dtype)
    kvt = _dot_nt(wkvt_ref[...], h)
    requests, _, _, seq = ka_ref.shape
    o = 0
    for ref in (ka_ref, va_ref, kb_ref, vb_ref):
        width = ref.shape[1] * HEAD_DIM
        for r in range(requests):
            ref[r] = kvt[o:o + width, r * seq:(r + 1) * seq].reshape(ref.shape[1:])
        o += width


def _rope(y, cos, sin):
    blocks = [y[:, j * LANES:(j + 1) * LANES] for j in range(y.shape[1] // LANES)]
    return [b * cos + _swap16(b) * sin for b in blocks]


def _proj_lat_kernel(x_ref, mod_ref, g_ref, w_ref, cos_ref, sin_ref,
                     qa_ref, ka_ref, va_ref, qb_ref, kb_ref, vb_ref):
    mod = mod_ref[0]
    h = _modulate(x_ref[...], g_ref[...], mod[3:4], mod[4:5]).astype(BF16)
    cos, sin = cos_ref[...], sin_ref[...]
    o = 3 * NA_WIDTH
    for j, b in enumerate(_rope(_dot(h, w_ref[:, o:o + SWA_Q_WIDTH]), cos, sin)):
        qb_ref[:, j * LANES:(j + 1) * LANES] = (b * Q_SCALE).astype(qb_ref.dtype)
    o += SWA_Q_WIDTH
    y = _dot(h, w_ref[:, o:o + 2 * SWA_KV_WIDTH])
    kb_ref[...] = _rope(y[:, :SWA_KV_WIDTH], cos, sin)[0].astype(kb_ref.dtype)
    vb_ref[...] = y[:, SWA_KV_WIDTH:].astype(vb_ref.dtype)
    o = 0
    for ref, scale in ((qa_ref, True), (ka_ref, False), (va_ref, False)):
        y = _dot(h, w_ref[:, o:o + NA_WIDTH])
        ref[...] = (y * Q_SCALE if scale else y).astype(ref.dtype)
        o += NA_WIDTH


def _proj_ctx(x, mod, g, w_qkv, *, seq, tokens_per_row, first_row):
    t = x.shape[0]
    tile = lambda w: pl.BlockSpec((TOKEN_TILE, w), lambda i: (i, 0))
    cache = lambda heads: pl.BlockSpec((TOKEN_TILE // seq, heads, HEAD_DIM, seq), lambda i: (i, 0, 0, 0))
    cache_shape = lambda heads: jax.ShapeDtypeStruct((t // seq, heads, HEAD_DIM, seq), F32)
    kv_heads = (NA_HEADS, NA_HEADS, SWA_KV_HEADS, SWA_KV_HEADS)
    return pl.pallas_call(
        _proj_ctx_kernel,
        grid=(t // TOKEN_TILE,),
        in_specs=[tile(D_MODEL), _mod_spec(tokens_per_row, first_row), _resident((1, D_MODEL)),
                  _resident(w_qkv.shape)],
        out_specs=[tile(NA_WIDTH), tile(SWA_Q_WIDTH)] + [cache(nh) for nh in kv_heads],
        out_shape=[jax.ShapeDtypeStruct((t, NA_WIDTH), BF16), jax.ShapeDtypeStruct((t, SWA_Q_WIDTH), BF16)]
        + [cache_shape(nh) for nh in kv_heads],
        scratch_shapes=[pltpu.VMEM((2 * NA_WIDTH + 2 * SWA_KV_WIDTH, D_MODEL), BF16)],
        compiler_params=_params(),
        name="proj_ctx",
    )(x, mod, g, w_qkv)


def _proj_lat(x, mod, g, w_qkv, rope, *, tokens_per_row, first_row):
    t = x.shape[0]
    tile = lambda w: pl.BlockSpec((TOKEN_TILE, w), lambda i: (i, 0))
    widths = (NA_WIDTH, NA_WIDTH, NA_WIDTH, SWA_Q_WIDTH, SWA_KV_WIDTH, SWA_KV_WIDTH)
    tiles_per_row = tokens_per_row // TOKEN_TILE
    return pl.pallas_call(
        _proj_lat_kernel,
        grid=(t // TOKEN_TILE,),
        in_specs=[tile(D_MODEL), _mod_spec(tokens_per_row, first_row), _resident((1, D_MODEL)),
                  _resident(w_qkv.shape)]
        + [pl.BlockSpec((TOKEN_TILE, LANES), lambda i: (i % tiles_per_row, 0))] * 2,
        out_specs=[tile(w) for w in widths],
        out_shape=[jax.ShapeDtypeStruct((t, w), BF16) for w in widths],
        compiler_params=_params(),
        name="proj_lat",
    )(x, mod, g, w_qkv, *rope)


def _lane_halves(shape):
    lane = lax.broadcasted_iota(jnp.int32, shape, 1) % LANES
    return lane < HEAD_DIM, lane >= HEAD_DIM


def _softmax_cols(cols, sink=None):
    m = jnp.max(functools.reduce(jnp.maximum, cols), axis=-1, keepdims=True)
    if sink is not None:
        m = jnp.maximum(m, sink)
    es = [jnp.exp2(c - m) for c in cols]
    l = jnp.sum(functools.reduce(jnp.add, es), axis=-1, keepdims=True)
    if sink is not None:
        l = l + jnp.exp2(sink - m)
    return es, l


def _cols(x):
    return [x[:, j * LANES:(j + 1) * LANES] for j in range(x.shape[1] // LANES)]


def _attn_ctx_kernel(sink_ref, qa_ref, qb_ref, ka_ref, va_ref, kb_ref, vb_ref, ya_ref, yb_ref):
    seq = qa_ref.shape[0]
    lo, hi = _lane_halves((seq, LANES))
    top = lax.broadcasted_iota(jnp.int32, (2 * seq, 1), 0) < seq
    zero = jnp.zeros((), BF16)

    def heads_of_block(q, kt, vt, sink):
        q = jnp.concatenate([jnp.where(lo, q, zero), jnp.where(hi, q, zero)], axis=0)
        es, l = _softmax_cols(_cols(_dot(q, kt)), sink)
        o = _dot_nt(jnp.concatenate(es, axis=1).astype(BF16), vt) * (1.0 / l)
        return jnp.where(lo, o[:seq], o[seq:])

    for p in range(NA_HEADS // 2):
        sl = slice(p * LANES, (p + 1) * LANES)
        kt = ka_ref[2 * p:2 * p + 2].reshape(LANES, seq).astype(BF16)
        vt = va_ref[2 * p:2 * p + 2].reshape(LANES, seq).astype(BF16)
        ya_ref[:, sl] = heads_of_block(qa_ref[:, sl], kt, vt, None).astype(ya_ref.dtype)
    for p in range(SWA_HEADS // 2):
        sl = slice(p * LANES, (p + 1) * LANES)
        kv = (2 * p) // (SWA_HEADS // SWA_KV_HEADS)
        kt, vt = kb_ref[kv].astype(BF16), vb_ref[kv].astype(BF16)
        kt, vt = jnp.concatenate([kt, kt], axis=0), jnp.concatenate([vt, vt], axis=0)
        sink = jnp.where(top, sink_ref[2 * p], sink_ref[2 * p + 1]) * LOG2E
        yb_ref[:, sl] = heads_of_block(qb_ref[:, sl], kt, vt, sink).astype(yb_ref.dtype)


def _attn_ctx(sink, qa, qb, ka, va, kb, vb, *, seq):
    t = qa.shape[0]
    tile = lambda w: pl.BlockSpec((seq, w), lambda b: (b, 0))
    cache = lambda heads: pl.BlockSpec((None, heads, HEAD_DIM, seq), lambda b: (b, 0, 0, 0))
    return pl.pallas_call(
        _attn_ctx_kernel,
        grid=(t // seq,),
        in_specs=[pl.BlockSpec(memory_space=pltpu.SMEM), tile(NA_WIDTH), tile(SWA_Q_WIDTH),
                  cache(NA_HEADS), cache(NA_HEADS), cache(SWA_KV_HEADS), cache(SWA_KV_HEADS)],
        out_specs=[tile(NA_WIDTH), tile(SWA_Q_WIDTH)],
        out_shape=[jax.ShapeDtypeStruct((t, NA_WIDTH), BF16), jax.ShapeDtypeStruct((t, SWA_Q_WIDTH), BF16)],
        compiler_params=_params(),
        name="attn_ctx",
    )(sink, qa, qb, ka, va, kb, vb)


def _attn_na_kernel(q_ref, k_ref, v_ref, ck_ref, cv_ref, bias_ref, y_ref, ck_scr, cv_scr, *, n, ctx):
    rows = n // GRID_W
    tile_q = NA_TILE_ROWS * GRID_W
    zero = jnp.zeros((), BF16)
    ck_scr[...] = ck_ref[...].reshape(NA_WIDTH, ctx).astype(BF16)
    cv_scr[...] = cv_ref[...].reshape(NA_WIDTH, ctx).astype(BF16)
    lo, hi = _lane_halves((tile_q, LANES))
    left_half = lax.broadcasted_iota(jnp.int32, (GRID_W, LANES), 1) < GRID_W
    empty = jnp.zeros((GRID_W, LANES), BF16)

    def tile(q0, k0, key_rows, lead, first):
        pairs = key_rows // 2
        for p in range(NA_HEADS // 2):
            sl = slice(p * LANES, (p + 1) * LANES)
            q = q_ref[pl.ds(q0, tile_q), sl]
            q = jnp.concatenate([jnp.where(lo, q, zero), jnp.where(hi, q, zero)], axis=0)
            s_nb = _dot_nt(q, k_ref[pl.ds(k0, key_rows * GRID_W), sl])
            s_ctx = _dot(q, ck_scr[sl, :])
            e_nb, e_ctx, ls = [], [], []
            for idx in range(2):
                for a in range(NA_TILE_ROWS):
                    qa = slice(idx * tile_q + a * GRID_W, idx * tile_q + (a + 1) * GRID_W)
                    inside = lambda i: first[a] <= i < first[a] + NA_ROWS
                    cols, where = [], []
                    for m in range(pairs):
                        if not (inside(2 * m) or inside(2 * m + 1)):
                            continue
                        blk = s_nb[qa, m * LANES:(m + 1) * LANES] + bias_ref[2 * p + idx, 2 * m - a - lead + NA_ROWS]
                        if not inside(2 * m + 1):
                            blk = jnp.where(left_half, blk, NEG_INF)
                        elif not inside(2 * m):
                            blk = jnp.where(left_half, NEG_INF, blk)
                        cols.append(blk)
                        where.append(m)
                    cols += [s_ctx[qa, j * LANES:(j + 1) * LANES] for j in range(ctx // LANES)]
                    es, l = _softmax_cols(cols)
                    es = [e.astype(BF16) for e in es]
                    e_nb.append(jnp.concatenate(
                        [es[where.index(m)] if m in where else empty for m in range(pairs)], axis=1))
                    e_ctx.append(jnp.concatenate(es[len(where):], axis=1))
                    ls.append(l)
            o = (_dot(jnp.concatenate(e_nb, axis=0), v_ref[pl.ds(k0, key_rows * GRID_W), sl])
                 + _dot_nt(jnp.concatenate(e_ctx, axis=0), cv_scr[sl, :]))
            o = o * (1.0 / jnp.concatenate(ls, axis=0))
            y_ref[pl.ds(q0, tile_q), sl] = jnp.where(lo, o[:tile_q], o[tile_q:]).astype(y_ref.dtype)

    half = NA_ROWS // 2
    tile(0, 0, NA_ROWS, 0, (0,) * NA_TILE_ROWS)

    for t in range(1, rows // NA_TILE_ROWS - 1):
        tile(t * tile_q, t * tile_q - half * GRID_W, NA_ROWS + NA_TILE_ROWS, half, tuple(range(NA_TILE_ROWS)))
    tile(n - tile_q, n - NA_ROWS * GRID_W, NA_ROWS, half, (0,) * NA_TILE_ROWS)


def _attn_na(q, k, v, ck, cv, bias, *, n, ctx):
    t = q.shape[0]
    tile = pl.BlockSpec((n, NA_WIDTH), lambda b: (b, 0))
    ctile = pl.BlockSpec((None, NA_HEADS, HEAD_DIM, ctx), lambda b: (b, 0, 0, 0))
    return pl.pallas_call(
        functools.partial(_attn_na_kernel, n=n, ctx=ctx),
        grid=(t // n,),
        in_specs=[tile, tile, tile, ctile, ctile, _resident(bias.shape)],
        out_specs=tile,
        out_shape=jax.ShapeDtypeStruct((t, NA_WIDTH), BF16),
        scratch_shapes=[pltpu.VMEM((NA_WIDTH, ctx), BF16)] * 2,
        compiler_params=_params(),
        name="attn_na",
    )(q, k, v, ck, cv, bias)


def _na_bias_table(rel_bias):
    rows = _na_bias_rows(rel_bias)
    shape = (NA_HEADS, 2 * NA_ROWS, GRID_W, LANES)
    return pl.pallas_call(
        _fill_na_bias,
        grid=(1,),
        in_specs=[pl.BlockSpec(rows.shape, lambda i: (0, 0, 0))],
        out_specs=pl.BlockSpec(shape, lambda i: (0, 0, 0, 0)),
        out_shape=jax.ShapeDtypeStruct(shape, F32),
        compiler_params=_params(),
        name="na_bias",
    )(rows)


def _na_bias_rows(rel_bias):
    side = GRID_W - NA_COLS
    z = jnp.pad(rel_bias * LOG2E, ((0, 0), (0, 0), (side, side + 1)))
    return jnp.pad(z, ((0, 0), (1, 1), (0, 0)), constant_values=NEG_INF)


def _fill_na_bias(rows_ref, bias_scr):
    q = lax.broadcasted_iota(jnp.int32, (GRID_W, LANES), 0)
    lane = lax.broadcasted_iota(jnp.int32, (GRID_W, LANES), 1)
    kc = lane % GRID_W
    col_start = jnp.clip(q - NA_COLS // 2, 0, GRID_W - NA_COLS)
    in_window = (kc >= col_start) & (kc < col_start + NA_COLS)
    for h in range(NA_HEADS):
        blocks = []
        for j in range(2 * NA_ROWS + 1):
            row = jnp.broadcast_to(rows_ref[h, j:j + 1, :], (GRID_W, LANES))
            rolled = pltpu.roll(row, LANES - GRID_W + 1, 1, stride=1, stride_axis=0)
            blocks.append(jnp.where(in_window, rolled, NEG_INF))
        for j in range(2 * NA_ROWS):
            bias_scr[h, j] = jnp.where(lane < GRID_W, blocks[j], pltpu.roll(blocks[j + 1], GRID_W, 1))


def _attn_swa_kernel(sink_ref, q_ref, k_ref, v_ref, ck_ref, cv_ref, y_ref, k_scr, v_scr, ck_scr, cv_scr, *, n, ctx):
    group = SWA_HEADS // SWA_KV_HEADS
    band = SWA_QTILE + 2 * SWA_BLOCK
    for src, csrc, dst, cdst in ((k_ref, ck_ref, k_scr, ck_scr), (v_ref, cv_ref, v_scr, cv_scr)):
        x = src[...].astype(F32)
        xr = pltpu.roll(x, HEAD_DIM, 1)
        first_half, second_half = _lane_halves(x.shape)
        dst[0] = jnp.where(first_half, x, xr).astype(BF16)
        dst[1] = jnp.where(second_half, x, xr).astype(BF16)
        for kv in range(SWA_KV_HEADS):
            c = csrc[kv].astype(BF16)
            cdst[kv] = jnp.concatenate([c, c], axis=0)

    qi = lax.broadcasted_iota(jnp.int32, (SWA_QTILE, band), 0)
    kj = lax.broadcasted_iota(jnp.int32, (SWA_QTILE, band), 1)
    lo, hi = _lane_halves((SWA_QTILE, LANES))
    top = lax.broadcasted_iota(jnp.int32, (2 * SWA_QTILE, 1), 0) < SWA_QTILE
    zero = jnp.zeros((), BF16)

    def block_body(b, carry):
        q0 = pl.multiple_of(b * SWA_QTILE, SWA_QTILE)
        k0 = pl.multiple_of(jnp.clip(q0 - SWA_BLOCK, 0, n - band), SWA_BLOCK)
        mask = jnp.where(jnp.abs(kj - qi + (k0 - q0)) <= SWA_WINDOW, 0.0, NEG_INF)
        mask2 = jnp.concatenate([mask, mask], axis=0)
        for p in range(SWA_HEADS // 2):
            kv = 2 * p // group
            sl = slice(p * LANES, (p + 1) * LANES)
            q = q_ref[pl.ds(q0, SWA_QTILE), sl]
            q = jnp.concatenate([jnp.where(lo, q, zero), jnp.where(hi, q, zero)], axis=0)
            s_band = _dot_nt(q, k_scr[kv, pl.ds(k0, band), :])
            s_ctx = _dot(q, ck_scr[kv])
            sink = jnp.where(top, sink_ref[2 * p], sink_ref[2 * p + 1]) * LOG2E
            es, l = _softmax_cols(_cols(s_band + mask2) + _cols(s_ctx), sink)
            e_band = jnp.concatenate(es[:band // LANES], axis=1).astype(BF16)
            e_ctx = jnp.concatenate(es[band // LANES:], axis=1).astype(BF16)
            o = (_dot(e_band, v_scr[kv, pl.ds(k0, band), :]) + _dot_nt(e_ctx, cv_scr[kv])) * (1.0 / l)
            y_ref[pl.ds(q0, SWA_QTILE), sl] = jnp.where(lo, o[:SWA_QTILE], o[SWA_QTILE:]).astype(y_ref.dtype)
        return carry

    lax.fori_loop(0, n // SWA_QTILE, block_body, 0, unroll=True)


def _attn_swa(sink, q, k, v, ck, cv, *, n, ctx):
    t = q.shape[0]
    tile = lambda w: pl.BlockSpec((n, w), lambda b: (b, 0))
    ctile = pl.BlockSpec((None, SWA_KV_HEADS, HEAD_DIM, ctx), lambda b: (b, 0, 0, 0))
    return pl.pallas_call(
        functools.partial(_attn_swa_kernel, n=n, ctx=ctx),
        grid=(t // n,),
        in_specs=[pl.BlockSpec(memory_space=pltpu.SMEM), tile(SWA_Q_WIDTH), tile(SWA_KV_WIDTH),
                  tile(SWA_KV_WIDTH), ctile, ctile],
        out_specs=tile(SWA_Q_WIDTH),
        out_shape=jax.ShapeDtypeStruct((t, SWA_Q_WIDTH), BF16),
        scratch_shapes=[pltpu.VMEM((SWA_KV_HEADS, n, SWA_KV_WIDTH), BF16)] * 2
        + [pltpu.VMEM((SWA_KV_HEADS, LANES, ctx), BF16)] * 2,
        compiler_params=_params(),
        name="attn_swa",
    )(sink, q, k, v, ck, cv)


def _rope_tables(n):
    half = HEAD_DIM // 4
    freqs = jnp.power(ROPE_BASE, -jnp.arange(half, dtype=F32) / half)
    t = jnp.arange(n)
    cos, sin = [], []
    for pos in (t // GRID_W, t % GRID_W):
        ang = pos.astype(F32)[:, None] * freqs[None, :]
        cos += [jnp.cos(ang), jnp.cos(ang)]
        sin += [-jnp.sin(ang), jnp.sin(ang)]
    cos, sin = jnp.concatenate(cos, axis=-1), jnp.concatenate(sin, axis=-1)
    reps = LANES // HEAD_DIM
    return jnp.tile(cos, (1, reps)), jnp.tile(sin, (1, reps))


def _merge_kernel(xc_ref, xl_ref, yac_ref, yal_ref, ybc_ref, ybl_ref, mod_ref, g_ref, win_ref, wba_ref, wbb_ref,
                  wout_ref, o_ref, *, ctx_steps):
    is_ctx = pl.program_id(0) < ctx_steps
    pick = lambda ctx_ref, lat_ref: jnp.where(is_ctx, ctx_ref[...], lat_ref[...])
    x, ya, yb = pick(xc_ref, xl_ref), pick(yac_ref, yal_ref), pick(ybc_ref, ybl_ref)
    mod = mod_ref[0]
    h = _modulate(x, g_ref[...], mod[3:4], mod[4:5]).astype(BF16)
    a = jax.nn.sigmoid(_dot(h, win_ref[:, QKV_COLS:QKV_COLS + D_MODEL])) * _dot(ya, wba_ref[...])
    b = jax.nn.sigmoid(_dot(h, win_ref[:, QKV_COLS + D_MODEL:])) * _dot(yb, wbb_ref[...])
    o_ref[...] = x + mod[5:6] * _dot((a + b).astype(BF16), wout_ref[...])


def _merge(x, ya, yb, mod, g, w_in, wba, wbb, wout, *, tokens_per_request):
    ctx_steps = x[0].shape[0] // MERGE_TILE
    t = x[0].shape[0] + x[1].shape[0]
    both = lambda w: _ctx_then_lat(w, ctx_steps, MERGE_TILE)
    return pl.pallas_call(
        functools.partial(_merge_kernel, ctx_steps=ctx_steps),
        grid=(t // MERGE_TILE,),
        in_specs=both(D_MODEL) + both(NA_WIDTH) + both(SWA_Q_WIDTH)
        + [_mod_spec_both(ctx_steps, tokens_per_request, MERGE_TILE), _resident((1, D_MODEL)),
           _resident(w_in.shape), _resident(wba.shape), _resident(wbb.shape), _resident(wout.shape)],
        out_specs=pl.BlockSpec((MERGE_TILE, D_MODEL), lambda i: (i, 0)),
        out_shape=jax.ShapeDtypeStruct((t, D_MODEL), F32),
        compiler_params=_params(),
        name="merge",
    )(*x, *ya, *yb, mod, g, w_in, wba, wbb, wout)


def kernel(x_prompt, x_sample, cache_na_k, cache_na_v, cache_swa_k, cache_swa_v, c, c_ctx, w_ada, b_ada,
           norm_ffn1, ffn1_w_gate, ffn1_w_up, ffn1_w_down, norm_mix, w_in, na_rel_bias, swa_sink,
           w_branch_na, w_branch_swa, w_out, norm_ffn2, ffn2_w_gate, ffn2_w_up, ffn2_w_down, norm_final):
    depth = w_ada.shape[0]
    assert depth == 1
    batch, seq, _ = x_prompt.shape
    dec_batch, dec_seq, _ = x_sample.shape
    past = cache_na_k.shape[2]
    layer = 0
    row = lambda v: v.reshape(1, D_MODEL)
    bf = lambda w: w.astype(BF16)

    cond = jnp.zeros((MOD_ROWS, D_MODEL), F32).at[0].set(c_ctx).at[1:1 + dec_batch].set(c)
    mod = _adaln(cond, w_ada[layer], b_ada[layer]).reshape(MOD_ROWS, N_MOD, D_MODEL)

    ffn1 = (row(norm_ffn1[layer]), bf(ffn1_w_gate[layer]), bf(ffn1_w_up[layer]), bf(ffn1_w_down[layer]),
            row(norm_final))
    g_mix = row(norm_mix[layer])
    sink = swa_sink[layer]
    ctx_rows = dict(tokens_per_row=batch * seq, first_row=0)
    lat_rows = dict(tokens_per_row=dec_seq, first_row=1)
    ctx_tokens = batch * seq

    later = (ffn2_w_gate[layer], ffn2_w_up[layer], ffn2_w_down[layer], w_in[layer],
             w_branch_na[layer], w_branch_swa[layer], w_out[layer])
    x_lat, *later = _ffn(x_sample.reshape(dec_batch * dec_seq, D_MODEL), mod, *ffn1,
                         _mod_spec(tile=FFN_TILE, **lat_rows), first=0, final=False, cast=later)
    ffn2 = (row(norm_ffn2[layer]), *later[:3], row(norm_final))
    w_qkv = later[3]
    merge_w = tuple(later[3:])

    x_ctx = _ffn(x_prompt.reshape(ctx_tokens, D_MODEL), mod, *ffn1, _mod_spec(tile=FFN_TILE, **ctx_rows),
                 first=0, final=False)
    qa, qb, *new_cache = _proj_ctx(x_ctx, mod, g_mix, w_qkv, seq=seq, **ctx_rows)
    ya_ctx, yb_ctx = _attn_ctx(sink, qa, qb, *new_cache, seq=seq)

    qa, kal, val, qb, kbl, vbl = _proj_lat(x_lat, mod, g_mix, w_qkv, _rope_tables(dec_seq), **lat_rows)
    transposed = lambda cache: jnp.transpose(cache[:, layer], (0, 2, 3, 1))
    yb_lat = _attn_swa(sink, qb, kbl, vbl, transposed(cache_swa_k), transposed(cache_swa_v), n=dec_seq, ctx=past)
    ya_lat = _attn_na(qa, kal, val, transposed(cache_na_k), transposed(cache_na_v),
                      _na_bias_table(na_rel_bias[layer]), n=dec_seq, ctx=past)

    x = _merge((x_ctx, x_lat), (ya_ctx, ya_lat), (yb_ctx, yb_lat), mod, g_mix, *merge_w, tokens_per_request=dec_seq)
    y_prompt, y_sample = _ffn(x, mod, *ffn2, _mod_spec_both(ctx_tokens // FFN_TILE, dec_seq, FFN_TILE),
                              first=6, final=True, ctx_tokens=ctx_tokens)
    y_prompt = y_prompt.reshape(batch, seq, D_MODEL)
    y_sample = y_sample.reshape(dec_batch, dec_seq, D_MODEL)

    new_cache = [jnp.transpose(t, (0, 3, 1, 2))[:, None] for t in new_cache]
    return (y_prompt, y_sample, *new_cache)
```

```python
import functools

import jax
import jax.numpy as jnp
from jax import lax
from jax.experimental import pallas as pl
from jax.experimental.pallas import tpu as pltpu

F32 = jnp.float32
BF16 = jnp.bfloat16

D_MODEL = 1024
FFN_DIM = 2816
HEAD_DIM = 64
N_MOD = 9
GRID_W = 64
NA_HEADS = 8
NA_ROWS = 8
NA_COLS = 16
SWA_HEADS = 8
SWA_KV_HEADS = 2
SWA_WINDOW = 128
SWA_BLOCK = 128
ROPE_BASE = 10000.0
EPS = 1e-6
NEG_INF = -1e30
NA_WIDTH = NA_HEADS * HEAD_DIM
SWA_Q_WIDTH = SWA_HEADS * HEAD_DIM
SWA_KV_WIDTH = SWA_KV_HEADS * HEAD_DIM
QKV_COLS = 3 * NA_WIDTH + SWA_Q_WIDTH + 2 * SWA_KV_WIDTH
LANES = 128
MOD_ROWS = 16
VMEM_LIMIT = 56 * 1024 * 1024
TOKEN_TILE = 1024
MERGE_TILE = 512
FFN_TILE = 1024
FFN_CHUNK = 256
ADALN_BLOCK = 2304
CTX_REQUESTS_PER_STEP = 4
SWA_QTILE = 256
NA_TILE_ROWS = 4
LOG2E = 1.4426950408889634
Q_SCALE = HEAD_DIM ** -0.5 * LOG2E


def _dot(a, b):
    return jnp.dot(a, b, preferred_element_type=F32)


def _dot_nt(a, b):
    return lax.dot_general(a, b, (((1,), (1,)), ((), ())), preferred_element_type=F32)


def _silu(x):
    return x * jax.nn.sigmoid(x)


def _rms(x, g):
    return x * lax.rsqrt(jnp.mean(x * x, axis=-1, keepdims=True) + EPS) * g


def _modulate(x, g, shift, scale):
    return _rms(x, g) * (1.0 + scale) + shift


def _resident(shape):
    nd = len(shape)
    return pl.BlockSpec(shape, lambda *_: (0,) * nd, pipeline_mode=pl.Buffered(1))


def _params():
    return pltpu.CompilerParams(dimension_semantics=("arbitrary",), vmem_limit_bytes=VMEM_LIMIT)


def _adaln_kernel(c_ref, w_ref, b_ref, o_ref):
    s = _silu(c_ref[...]).astype(BF16)
    o_ref[...] = _dot(s, w_ref[...].astype(BF16)) + b_ref[...]


def _adaln(cond, w_ada, b_ada):
    n = w_ada.shape[1]
    blk = ADALN_BLOCK
    return pl.pallas_call(
        _adaln_kernel,
        grid=(n // blk,),
        in_specs=[pl.BlockSpec((MOD_ROWS, D_MODEL), lambda j: (0, 0)),
                  pl.BlockSpec((D_MODEL, blk), lambda j: (0, j)),
                  pl.BlockSpec((1, blk), lambda j: (0, j))],
        out_specs=pl.BlockSpec((MOD_ROWS, blk), lambda j: (0, j)),
        out_shape=jax.ShapeDtypeStruct((MOD_ROWS, n), F32),
        compiler_params=_params(),
        name="adaln",
    )(cond, w_ada, b_ada.reshape(1, n))


def _mod_spec(tokens_per_row, first_row, tile=TOKEN_TILE):
    tiles_per_row = tokens_per_row // tile
    return pl.BlockSpec((1, N_MOD, D_MODEL), lambda i: (first_row + i // tiles_per_row, 0, 0))


def _mod_spec_both(ctx_steps, tokens_per_request, tile):
    tiles_per_request = tokens_per_request // tile
    row = lambda i: jnp.where(i < ctx_steps, 0, 1 + (i - ctx_steps) // tiles_per_request)
    return pl.BlockSpec((1, N_MOD, D_MODEL), lambda i: (row(i), 0, 0))


def _ctx_then_lat(width, ctx_steps, tile):
    ctx = pl.BlockSpec((tile, width), lambda i: (jnp.minimum(i, ctx_steps - 1), 0))
    lat = pl.BlockSpec((tile, width), lambda i: (jnp.maximum(i - ctx_steps, 0), 0))
    return [ctx, lat]


def _ffn_kernel(x_ref, mod_ref, g_ref, wg_ref, wu_ref, wd_ref, gf_ref, *rest, first, final, n_cast, ctx_steps):
    n_out = 1 if ctx_steps is None else 2
    outs, a_scr = rest[n_cast:n_cast + n_out], rest[-1]
    for src, dst in zip(rest[:n_cast], rest[n_cast + n_out:-1]):
        dst[...] = src[...].astype(BF16)
    x = x_ref[...]
    mod = mod_ref[0]
    shift, scale, gate = mod[first:first + 1], mod[first + 1:first + 2], mod[first + 2:first + 3]
    h = _modulate(x, g_ref[...], shift, scale).astype(BF16)
    for c in range(FFN_DIM // FFN_CHUNK):
        sl = slice(c * FFN_CHUNK, (c + 1) * FFN_CHUNK)
        a_scr[:, sl] = (_silu(_dot(h, wg_ref[:, sl])) * _dot(h, wu_ref[:, sl])).astype(BF16)
    y = x + (0.5 * gate) * _dot(a_scr[...], wd_ref[...])
    if final:
        y = _rms(y, gf_ref[...])
    if ctx_steps is None:
        outs[0][...] = y
    else:
        is_ctx = pl.program_id(0) < ctx_steps

        @pl.when(is_ctx)
        def _():
            outs[0][...] = y

        @pl.when(jnp.logical_not(is_ctx))
        def _():
            outs[1][...] = y


def _ffn(x, mod, g, wg, wu, wd, gf, mod_spec, *, first, final, cast=(), ctx_tokens=None):
    t = x.shape[0]
    steps = t // FFN_TILE
    tile = pl.BlockSpec((FFN_TILE, D_MODEL), lambda i: (i, 0))
    chunk = lambda w: pl.BlockSpec((w.shape[0] // steps, w.shape[1]), lambda i: (i, 0))
    assert all(w.shape[0] % (steps * 16) == 0 for w in cast)
    if ctx_tokens is None:
        ctx_steps, out_specs, out_shape = None, [tile], [jax.ShapeDtypeStruct((t, D_MODEL), F32)]
    else:
        ctx_steps = ctx_tokens // FFN_TILE
        out_specs = _ctx_then_lat(D_MODEL, ctx_steps, FFN_TILE)
        out_shape = [jax.ShapeDtypeStruct((n, D_MODEL), F32) for n in (ctx_tokens, t - ctx_tokens)]
    out = pl.pallas_call(
        functools.partial(_ffn_kernel, first=first, final=final, n_cast=len(cast), ctx_steps=ctx_steps),
        grid=(steps,),
        in_specs=[tile, mod_spec, _resident((1, D_MODEL)),
                  _resident(wg.shape), _resident(wu.shape), _resident(wd.shape), _resident((1, D_MODEL))]
        + [chunk(w) for w in cast],
        out_specs=out_specs + [chunk(w) for w in cast],
        out_shape=out_shape + [jax.ShapeDtypeStruct(w.shape, BF16) for w in cast],
        scratch_shapes=[pltpu.VMEM((FFN_TILE, FFN_DIM), BF16)],
        compiler_params=_params(),
        name="ffn",
    )(x, mod, g, wg, wu, wd, gf, *cast)
    return out if len(out) > 1 else out[0]


def _swap16(x):
    lane = lax.broadcasted_iota(jnp.int32, x.shape, 1)
    return jnp.where(lane % 32 < 16, pltpu.roll(x, LANES - 16, 1), pltpu.roll(x, 16, 1))


def _proj_ctx_kernel(x_ref, mod_ref, g_ref, w_ref, qa_ref, qb_ref, ka_ref, va_ref, kb_ref, vb_ref, wkvt_ref):
    @pl.when(pl.program_id(0) == 0)
    def _():
        kv_cols = ((NA_WIDTH, 3 * NA_WIDTH), (3 * NA_WIDTH + SWA_Q_WIDTH, QKV_COLS))
        o = 0
        for lo_col, hi_col in kv_cols:
            wkvt_ref[o:o + hi_col - lo_col, :] = w_ref[:, lo_col:hi_col].astype(F32).T.astype(BF16)
            o += hi_col - lo_col

    mod = mod_ref[0]
    h = _modulate(x_ref[...], g_ref[...], mod[3:4], mod[4:5]).astype(BF16)
    qa_ref[...] = (_dot(h, w_ref[:, :NA_WIDTH]) * Q_SCALE).astype(qa_ref.dtype)
    qb_ref[...] = (_dot(h, w_ref[:, 3 * NA_WIDTH:3 * NA_WIDTH + SWA_Q_WIDTH]) * Q_SCALE).astype(qb_ref.dtype)
    kvt = _dot_nt(wkvt_ref[...], h)
    requests, _, _, seq = ka_ref.shape
    o = 0
    for ref in (ka_ref, va_ref, kb_ref, vb_ref):
        width = ref.shape[1] * HEAD_DIM
        for r in range(requests):
            ref[r] = kvt[o:o + width, r * seq:(r + 1) * seq].reshape(ref.shape[1:])
        o += width


def _rope(y, cos, sin):
    blocks = [y[:, j * LANES:(j + 1) * LANES] for j in range(y.shape[1] // LANES)]
    return [b * cos + _swap16(b) * sin for b in blocks]


def _proj_lat_kernel(x_ref, mod_ref, g_ref, w_ref, cos_ref, sin_ref,
                     qa_ref, ka_ref, va_ref, qb_ref, kb_ref, vb_ref):
    mod = mod_ref[0]
    h = _modulate(x_ref[...], g_ref[...], mod[3:4], mod[4:5]).astype(BF16)
    cos, sin = cos_ref[...], sin_ref[...]
    o = 3 * NA_WIDTH
    for j, b in enumerate(_rope(_dot(h, w_ref[:, o:o + SWA_Q_WIDTH]), cos, sin)):
        qb_ref[:, j * LANES:(j + 1) * LANES] = (b * Q_SCALE).astype(qb_ref.dtype)
    o += SWA_Q_WIDTH
    y = _dot(h, w_ref[:, o:o + 2 * SWA_KV_WIDTH])
    kb_ref[...] = _rope(y[:, :SWA_KV_WIDTH], cos, sin)[0].astype(kb_ref.dtype)
    vb_ref[...] = y[:, SWA_KV_WIDTH:].astype(vb_ref.dtype)
    o = 0
    for ref, scale in ((qa_ref, True), (ka_ref, False), (va_ref, False)):
        y = _dot(h, w_ref[:, o:o + NA_WIDTH])
        ref[...] = (y * Q_SCALE if scale else y).astype(ref.dtype)
        o += NA_WIDTH


def _proj_ctx(x, mod, g, w_qkv, *, seq, tokens_per_row, first_row):
    t = x.shape[0]
    tile = lambda w: pl.BlockSpec((TOKEN_TILE, w), lambda i: (i, 0))
    cache = lambda heads: pl.BlockSpec((TOKEN_TILE // seq, heads, HEAD_DIM, seq), lambda i: (i, 0, 0, 0))
    cache_shape = lambda heads: jax.ShapeDtypeStruct((t // seq, heads, HEAD_DIM, seq), F32)
    kv_heads = (NA_HEADS, NA_HEADS, SWA_KV_HEADS, SWA_KV_HEADS)
    return pl.pallas_call(
        _proj_ctx_kernel,
        grid=(t // TOKEN_TILE,),
        in_specs=[tile(D_MODEL), _mod_spec(tokens_per_row, first_row), _resident((1, D_MODEL)),
                  _resident(w_qkv.shape)],
        out_specs=[tile(NA_WIDTH), tile(SWA_Q_WIDTH)] + [cache(nh) for nh in kv_heads],
        out_shape=[jax.ShapeDtypeStruct((t, NA_WIDTH), BF16), jax.ShapeDtypeStruct((t, SWA_Q_WIDTH), BF16)]
        + [cache_shape(nh) for nh in kv_heads],
        scratch_shapes=[pltpu.VMEM((2 * NA_WIDTH + 2 * SWA_KV_WIDTH, D_MODEL), BF16)],
        compiler_params=_params(),
        name="proj_ctx",
    )(x, mod, g, w_qkv)


def _proj_lat(x, mod, g, w_qkv, rope, *, tokens_per_row, first_row):
    t = x.shape[0]
    tile = lambda w: pl.BlockSpec((TOKEN_TILE, w), lambda i: (i, 0))
    widths = (NA_WIDTH, NA_WIDTH, NA_WIDTH, SWA_Q_WIDTH, SWA_KV_WIDTH, SWA_KV_WIDTH)
    tiles_per_row = tokens_per_row // TOKEN_TILE
    return pl.pallas_call(
        _proj_lat_kernel,
        grid=(t // TOKEN_TILE,),
        in_specs=[tile(D_MODEL), _mod_spec(tokens_per_row, first_row), _resident((1, D_MODEL)),
                  _resident(w_qkv.shape)]
        + [pl.BlockSpec((TOKEN_TILE, LANES), lambda i: (i % tiles_per_row, 0))] * 2,
        out_specs=[tile(w) for w in widths],
        out_shape=[jax.ShapeDtypeStruct((t, w), BF16) for w in widths],
        compiler_params=_params(),
        name="proj_lat",
    )(x, mod, g, w_qkv, *rope)


def _lane_halves(shape):
    lane = lax.broadcasted_iota(jnp.int32, shape, 1) % LANES
    return lane < HEAD_DIM, lane >= HEAD_DIM


def _softmax_cols(cols, sink=None):
    m = jnp.max(functools.reduce(jnp.maximum, cols), axis=-1, keepdims=True)
    if sink is not None:
        m = jnp.maximum(m, sink)
    es = [jnp.exp2(c - m) for c in cols]
    l = jnp.sum(functools.reduce(jnp.add, es), axis=-1, keepdims=True)
    if sink is not None:
        l = l + jnp.exp2(sink - m)
    return es, l


def _cols(x):
    return [x[:, j * LANES:(j + 1) * LANES] for j in range(x.shape[1] // LANES)]


def _attn_ctx_kernel(sink_ref, qa_ref, qb_ref, ka_ref, va_ref, kb_ref, vb_ref, ya_ref, yb_ref):
    requests, _, _, seq = ka_ref.shape
    lo, hi = _lane_halves((seq, LANES))
    top = lax.broadcasted_iota(jnp.int32, (2 * seq, 1), 0) < seq
    zero = jnp.zeros((), BF16)

    def heads_of_block(q, kt, vt, sink):
        q = jnp.concatenate([jnp.where(lo, q, zero), jnp.where(hi, q, zero)], axis=0)
        es, l = _softmax_cols(_cols(_dot(q, kt)), sink)
        o = _dot_nt(jnp.concatenate(es, axis=1).astype(BF16), vt) * (1.0 / l)
        return jnp.where(lo, o[:seq], o[seq:])

    for r in range(requests):
        rows = slice(r * seq, (r + 1) * seq)
        for p in range(NA_HEADS // 2):
            sl = slice(p * LANES, (p + 1) * LANES)
            kt = ka_ref[r, 2 * p:2 * p + 2].reshape(LANES, seq).astype(BF16)
            vt = va_ref[r, 2 * p:2 * p + 2].reshape(LANES, seq).astype(BF16)
            ya_ref[rows, sl] = heads_of_block(qa_ref[rows, sl], kt, vt, None).astype(ya_ref.dtype)
        for p in range(SWA_HEADS // 2):
            sl = slice(p * LANES, (p + 1) * LANES)
            kv = (2 * p) // (SWA_HEADS // SWA_KV_HEADS)
            kt, vt = kb_ref[r, kv].astype(BF16), vb_ref[r, kv].astype(BF16)
            kt, vt = jnp.concatenate([kt, kt], axis=0), jnp.concatenate([vt, vt], axis=0)
            sink = jnp.where(top, sink_ref[2 * p], sink_ref[2 * p + 1]) * LOG2E
            yb_ref[rows, sl] = heads_of_block(qb_ref[rows, sl], kt, vt, sink).astype(yb_ref.dtype)


def _attn_ctx(sink, qa, qb, ka, va, kb, vb, *, seq):
    t = qa.shape[0]
    per_step = CTX_REQUESTS_PER_STEP
    tile = lambda w: pl.BlockSpec((per_step * seq, w), lambda b: (b, 0))
    cache = lambda heads: pl.BlockSpec((per_step, heads, HEAD_DIM, seq), lambda b: (b, 0, 0, 0))
    return pl.pallas_call(
        _attn_ctx_kernel,
        grid=(t // (per_step * seq),),
        in_specs=[pl.BlockSpec(memory_space=pltpu.SMEM), tile(NA_WIDTH), tile(SWA_Q_WIDTH),
                  cache(NA_HEADS), cache(NA_HEADS), cache(SWA_KV_HEADS), cache(SWA_KV_HEADS)],
        out_specs=[tile(NA_WIDTH), tile(SWA_Q_WIDTH)],
        out_shape=[jax.ShapeDtypeStruct((t, NA_WIDTH), BF16), jax.ShapeDtypeStruct((t, SWA_Q_WIDTH), BF16)],
        compiler_params=_params(),
        name="attn_ctx",
    )(sink, qa, qb, ka, va, kb, vb)


def _attn_na_kernel(q_ref, k_ref, v_ref, ck_ref, cv_ref, bias_ref, y_ref, ck_scr, cv_scr, *, n, ctx):
    rows = n // GRID_W
    tile_q = NA_TILE_ROWS * GRID_W
    zero = jnp.zeros((), BF16)
    ck_scr[...] = ck_ref[...].reshape(NA_WIDTH, ctx).astype(BF16)
    cv_scr[...] = cv_ref[...].reshape(NA_WIDTH, ctx).astype(BF16)
    lo, hi = _lane_halves((tile_q, LANES))
    left_half = lax.broadcasted_iota(jnp.int32, (GRID_W, LANES), 1) < GRID_W
    empty = jnp.zeros((GRID_W, LANES), BF16)

    def tile(q0, k0, key_rows, lead, first):
        pairs = key_rows // 2
        for p in range(NA_HEADS // 2):
            sl = slice(p * LANES, (p + 1) * LANES)
            q = q_ref[pl.ds(q0, tile_q), sl]
            q = jnp.concatenate([jnp.where(lo, q, zero), jnp.where(hi, q, zero)], axis=0)
            s_nb = _dot_nt(q, k_ref[pl.ds(k0, key_rows * GRID_W), sl])
            s_ctx = _dot(q, ck_scr[sl, :])
            e_nb, e_ctx, ls = [], [], []
            for idx in range(2):
                for a in range(NA_TILE_ROWS):
                    qa = slice(idx * tile_q + a * GRID_W, idx * tile_q + (a + 1) * GRID_W)
                    inside = lambda i: first[a] <= i < first[a] + NA_ROWS
                    cols, where = [], []
                    for m in range(pairs):
                        if not (inside(2 * m) or inside(2 * m + 1)):
                            continue
                        blk = s_nb[qa, m * LANES:(m + 1) * LANES] + bias_ref[2 * p + idx, 2 * m - a - lead + NA_ROWS]
                        if not inside(2 * m + 1):
                            blk = jnp.where(left_half, blk, NEG_INF)
                        elif not inside(2 * m):
                            blk = jnp.where(left_half, NEG_INF, blk)
                        cols.append(blk)
                        where.append(m)
                    cols += [s_ctx[qa, j * LANES:(j + 1) * LANES] for j in range(ctx // LANES)]
                    es, l = _softmax_cols(cols)
                    es = [e.astype(BF16) for e in es]
                    e_nb.append(jnp.concatenate(
                        [es[where.index(m)] if m in where else empty for m in range(pairs)], axis=1))
                    e_ctx.append(jnp.concatenate(es[len(where):], axis=1))
                    ls.append(l)
            o = (_dot(jnp.concatenate(e_nb, axis=0), v_ref[pl.ds(k0, key_rows * GRID_W), sl])
                 + _dot_nt(jnp.concatenate(e_ctx, axis=0), cv_scr[sl, :]))
            o = o * (1.0 / jnp.concatenate(ls, axis=0))
            y_ref[pl.ds(q0, tile_q), sl] = jnp.where(lo, o[:tile_q], o[tile_q:]).astype(y_ref.dtype)

    half = NA_ROWS // 2
    tile(0, 0, NA_ROWS, 0, (0,) * NA_TILE_ROWS)

    for t in range(1, rows // NA_TILE_ROWS - 1):
        tile(t * tile_q, t * tile_q - half * GRID_W, NA_ROWS + NA_TILE_ROWS, half, tuple(range(NA_TILE_ROWS)))
    tile(n - tile_q, n - NA_ROWS * GRID_W, NA_ROWS, half, (0,) * NA_TILE_ROWS)


def _attn_na(q, k, v, ck, cv, bias, *, n, ctx):
    t = q.shape[0]
    tile = pl.BlockSpec((n, NA_WIDTH), lambda b: (b, 0))
    ctile = pl.BlockSpec((None, NA_HEADS, HEAD_DIM, ctx), lambda b: (b, 0, 0, 0))
    return pl.pallas_call(
        functools.partial(_attn_na_kernel, n=n, ctx=ctx),
        grid=(t // n,),
        in_specs=[tile, tile, tile, ctile, ctile, _resident(bias.shape)],
        out_specs=tile,
        out_shape=jax.ShapeDtypeStruct((t, NA_WIDTH), BF16),
        scratch_shapes=[pltpu.VMEM((NA_WIDTH, ctx), BF16)] * 2,
        compiler_params=_params(),
        name="attn_na",
    )(q, k, v, ck, cv, bias)


def _na_bias_table(rel_bias):
    rows = _na_bias_rows(rel_bias)
    shape = (NA_HEADS, 2 * NA_ROWS, GRID_W, LANES)
    return pl.pallas_call(
        _fill_na_bias,
        grid=(1,),
        in_specs=[pl.BlockSpec(rows.shape, lambda i: (0, 0, 0))],
        out_specs=pl.BlockSpec(shape, lambda i: (0, 0, 0, 0)),
        out_shape=jax.ShapeDtypeStruct(shape, F32),
        compiler_params=_params(),
        name="na_bias",
    )(rows)


def _na_bias_rows(rel_bias):
    side = GRID_W - NA_COLS
    z = jnp.pad(rel_bias * LOG2E, ((0, 0), (0, 0), (side, side + 1)))
    return jnp.pad(z, ((0, 0), (1, 1), (0, 0)), constant_values=NEG_INF)


def _fill_na_bias(rows_ref, bias_scr):
    q = lax.broadcasted_iota(jnp.int32, (GRID_W, LANES), 0)
    lane = lax.broadcasted_iota(jnp.int32, (GRID_W, LANES), 1)
    kc = lane % GRID_W
    col_start = jnp.clip(q - NA_COLS // 2, 0, GRID_W - NA_COLS)
    in_window = (kc >= col_start) & (kc < col_start + NA_COLS)
    for h in range(NA_HEADS):
        blocks = []
        for j in range(2 * NA_ROWS + 1):
            row = jnp.broadcast_to(rows_ref[h, j:j + 1, :], (GRID_W, LANES))
            rolled = pltpu.roll(row, LANES - GRID_W + 1, 1, stride=1, stride_axis=0)
            blocks.append(jnp.where(in_window, rolled, NEG_INF))
        for j in range(2 * NA_ROWS):
            bias_scr[h, j] = jnp.where(lane < GRID_W, blocks[j], pltpu.roll(blocks[j + 1], GRID_W, 1))


def _attn_swa_kernel(sink_ref, q_ref, k_ref, v_ref, ck_ref, cv_ref, y_ref, k_scr, v_scr, ck_scr, cv_scr, *, n, ctx):
    group = SWA_HEADS // SWA_KV_HEADS
    band = SWA_QTILE + 2 * SWA_BLOCK
    for src, csrc, dst, cdst in ((k_ref, ck_ref, k_scr, ck_scr), (v_ref, cv_ref, v_scr, cv_scr)):
        x = src[...].astype(F32)
        xr = pltpu.roll(x, HEAD_DIM, 1)
        first_half, second_half = _lane_halves(x.shape)
        dst[0] = jnp.where(first_half, x, xr).astype(BF16)
        dst[1] = jnp.where(second_half, x, xr).astype(BF16)
        for kv in range(SWA_KV_HEADS):
            c = csrc[kv].astype(BF16)
            cdst[kv] = jnp.concatenate([c, c], axis=0)

    qi = lax.broadcasted_iota(jnp.int32, (SWA_QTILE, band), 0)
    kj = lax.broadcasted_iota(jnp.int32, (SWA_QTILE, band), 1)
    lo, hi = _lane_halves((SWA_QTILE, LANES))
    top = lax.broadcasted_iota(jnp.int32, (2 * SWA_QTILE, 1), 0) < SWA_QTILE
    zero = jnp.zeros((), BF16)

    def block_body(b, carry):
        q0 = pl.multiple_of(b * SWA_QTILE, SWA_QTILE)
        k0 = pl.multiple_of(jnp.clip(q0 - SWA_BLOCK, 0, n - band), SWA_BLOCK)
        mask = jnp.where(jnp.abs(kj - qi + (k0 - q0)) <= SWA_WINDOW, 0.0, NEG_INF)
        mask2 = jnp.concatenate([mask, mask], axis=0)
        for p in range(SWA_HEADS // 2):
            kv = 2 * p // group
            sl = slice(p * LANES, (p + 1) * LANES)
            q = q_ref[pl.ds(q0, SWA_QTILE), sl]
            q = jnp.concatenate([jnp.where(lo, q, zero), jnp.where(hi, q, zero)], axis=0)
            s_band = _dot_nt(q, k_scr[kv, pl.ds(k0, band), :])
            s_ctx = _dot(q, ck_scr[kv])
            sink = jnp.where(top, sink_ref[2 * p], sink_ref[2 * p + 1]) * LOG2E
            es, l = _softmax_cols(_cols(s_band + mask2) + _cols(s_ctx), sink)
            e_band = jnp.concatenate(es[:band // LANES], axis=1).astype(BF16)
            e_ctx = jnp.concatenate(es[band // LANES:], axis=1).astype(BF16)
            o = (_dot(e_band, v_scr[kv, pl.ds(k0, band), :]) + _dot_nt(e_ctx, cv_scr[kv])) * (1.0 / l)
            y_ref[pl.ds(q0, SWA_QTILE), sl] = jnp.where(lo, o[:SWA_QTILE], o[SWA_QTILE:]).astype(y_ref.dtype)
        return carry

    lax.fori_loop(0, n // SWA_QTILE, block_body, 0, unroll=4)


def _attn_swa(sink, q, k, v, ck, cv, *, n, ctx):
    t = q.shape[0]
    tile = lambda w: pl.BlockSpec((n, w), lambda b: (b, 0))
    ctile = pl.BlockSpec((None, SWA_KV_HEADS, HEAD_DIM, ctx), lambda b: (b, 0, 0, 0))
    return pl.pallas_call(
        functools.partial(_attn_swa_kernel, n=n, ctx=ctx),
        grid=(t // n,),
        in_specs=[pl.BlockSpec(memory_space=pltpu.SMEM), tile(SWA_Q_WIDTH), tile(SWA_KV_WIDTH),
                  tile(SWA_KV_WIDTH), ctile, ctile],
        out_specs=tile(SWA_Q_WIDTH),
        out_shape=jax.ShapeDtypeStruct((t, SWA_Q_WIDTH), BF16),
        scratch_shapes=[pltpu.VMEM((SWA_KV_HEADS, n, SWA_KV_WIDTH), BF16)] * 2
        + [pltpu.VMEM((SWA_KV_HEADS, LANES, ctx), BF16)] * 2,
        compiler_params=_params(),
        name="attn_swa",
    )(sink, q, k, v, ck, cv)


def _rope_tables(n):
    half = HEAD_DIM // 4
    freqs = jnp.power(ROPE_BASE, -jnp.arange(half, dtype=F32) / half)
    t = jnp.arange(n)
    cos, sin = [], []
    for pos in (t // GRID_W, t % GRID_W):
        ang = pos.astype(F32)[:, None] * freqs[None, :]
        cos += [jnp.cos(ang), jnp.cos(ang)]
        sin += [-jnp.sin(ang), jnp.sin(ang)]
    cos, sin = jnp.concatenate(cos, axis=-1), jnp.concatenate(sin, axis=-1)
    reps = LANES // HEAD_DIM
    return jnp.tile(cos, (1, reps)), jnp.tile(sin, (1, reps))


def _merge_kernel(xc_ref, xl_ref, yac_ref, yal_ref, ybc_ref, ybl_ref, mod_ref, g_ref, win_ref, wba_ref, wbb_ref,
                  wout_ref, o_ref, *, ctx_steps):
    is_ctx = pl.program_id(0) < ctx_steps
    pick = lambda ctx_ref, lat_ref: jnp.where(is_ctx, ctx_ref[...], lat_ref[...])
    x, ya, yb = pick(xc_ref, xl_ref), pick(yac_ref, yal_ref), pick(ybc_ref, ybl_ref)
    mod = mod_ref[0]
    h = _modulate(x, g_ref[...], mod[3:4], mod[4:5]).astype(BF16)
    a = jax.nn.sigmoid(_dot(h, win_ref[:, QKV_COLS:QKV_COLS + D_MODEL])) * _dot(ya, wba_ref[...])
    b = jax.nn.sigmoid(_dot(h, win_ref[:, QKV_COLS + D_MODEL:])) * _dot(yb, wbb_ref[...])
    o_ref[...] = x + mod[5:6] * _dot((a + b).astype(BF16), wout_ref[...])


def _merge(x, ya, yb, mod, g, w_in, wba, wbb, wout, *, tokens_per_request):
    ctx_steps = x[0].shape[0] // MERGE_TILE
    t = x[0].shape[0] + x[1].shape[0]
    both = lambda w: _ctx_then_lat(w, ctx_steps, MERGE_TILE)
    return pl.pallas_call(
        functools.partial(_merge_kernel, ctx_steps=ctx_steps),
        grid=(t // MERGE_TILE,),
        in_specs=both(D_MODEL) + both(NA_WIDTH) + both(SWA_Q_WIDTH)
        + [_mod_spec_both(ctx_steps, tokens_per_request, MERGE_TILE), _resident((1, D_MODEL)),
           _resident(w_in.shape), _resident(wba.shape), _resident(wbb.shape), _resident(wout.shape)],
        out_specs=pl.BlockSpec((MERGE_TILE, D_MODEL), lambda i: (i, 0)),
        out_shape=jax.ShapeDtypeStruct((t, D_MODEL), F32),
        compiler_params=_params(),
        name="merge",
    )(*x, *ya, *yb, mod, g, w_in, wba, wbb, wout)


def kernel(x_prompt, x_sample, cache_na_k, cache_na_v, cache_swa_k, cache_swa_v, c, c_ctx, w_ada, b_ada,
           norm_ffn1, ffn1_w_gate, ffn1_w_up, ffn1_w_down, norm_mix, w_in, na_rel_bias, swa_sink,
           w_branch_na, w_branch_swa, w_out, norm_ffn2, ffn2_w_gate, ffn2_w_up, ffn2_w_down, norm_final):
    depth = w_ada.shape[0]
    assert depth == 1
    batch, seq, _ = x_prompt.shape
    dec_batch, dec_seq, _ = x_sample.shape
    past = cache_na_k.shape[2]
    layer = 0
    row = lambda v: v.reshape(1, D_MODEL)
    bf = lambda w: w.astype(BF16)

    cond = jnp.zeros((MOD_ROWS, D_MODEL), F32).at[0].set(c_ctx).at[1:1 + dec_batch].set(c)
    mod = _adaln(cond, w_ada[layer], b_ada[layer]).reshape(MOD_ROWS, N_MOD, D_MODEL)

    ffn1 = (row(norm_ffn1[layer]), bf(ffn1_w_gate[layer]), bf(ffn1_w_up[layer]), bf(ffn1_w_down[layer]),
            row(norm_final))
    g_mix = row(norm_mix[layer])
    sink = swa_sink[layer]
    ctx_rows = dict(tokens_per_row=batch * seq, first_row=0)
    lat_rows = dict(tokens_per_row=dec_seq, first_row=1)
    ctx_tokens = batch * seq

    later = (ffn2_w_gate[layer], ffn2_w_up[layer], ffn2_w_down[layer], w_in[layer],
             w_branch_na[layer], w_branch_swa[layer], w_out[layer])
    x_lat, *later = _ffn(x_sample.reshape(dec_batch * dec_seq, D_MODEL), mod, *ffn1,
                         _mod_spec(tile=FFN_TILE, **lat_rows), first=0, final=False, cast=later)
    ffn2 = (row(norm_ffn2[layer]), *later[:3], row(norm_final))
    w_qkv = later[3]
    merge_w = tuple(later[3:])

    x_ctx = _ffn(x_prompt.reshape(ctx_tokens, D_MODEL), mod, *ffn1, _mod_spec(tile=FFN_TILE, **ctx_rows),
                 first=0, final=False)
    qa, qb, *new_cache = _proj_ctx(x_ctx, mod, g_mix, w_qkv, seq=seq, **ctx_rows)
    ya_ctx, yb_ctx = _attn_ctx(sink, qa, qb, *new_cache, seq=seq)

    qa, kal, val, qb, kbl, vbl = _proj_lat(x_lat, mod, g_mix, w_qkv, _rope_tables(dec_seq), **lat_rows)
    transposed = lambda cache: jnp.transpose(cache[:, layer], (0, 2, 3, 1))
    ya_lat = _attn_na(qa, kal, val, transposed(cache_na_k), transposed(cache_na_v),
                      _na_bias_table(na_rel_bias[layer]), n=dec_seq, ctx=past)
    yb_lat = _attn_swa(sink, qb, kbl, vbl, transposed(cache_swa_k), transposed(cache_swa_v), n=dec_seq, ctx=past)

    x = _merge((x_ctx, x_lat), (ya_ctx, ya_lat), (yb_ctx, yb_lat), mod, g_mix, *merge_w, tokens_per_request=dec_seq)
    y_prompt, y_sample = _ffn(x, mod, *ffn2, _mod_spec_both(ctx_tokens // FFN_TILE, dec_seq, FFN_TILE),
                              first=6, final=True, ctx_tokens=ctx_tokens)
    y_prompt = y_prompt.reshape(batch, seq, D_MODEL)
    y_sample = y_sample.reshape(dec_batch, dec_seq, D_MODEL)

    new_cache = [jnp.transpose(t, (0, 3, 1, 2))[:, None] for t in new_cache]
    return (y_prompt, y_sample, *new_cache)
```

```python
import functools

import numpy as np
import jax
import jax.numpy as jnp
from jax import lax
from jax.experimental import pallas as pl
from jax.experimental.pallas import tpu as pltpu

F32 = jnp.float32
BF16 = jnp.bfloat16

D_MODEL = 1024
FFN_DIM = 2816
HEAD_DIM = 64
N_MOD = 9
GRID_W = 64
NA_HEADS = 8
NA_ROWS = 8
NA_COLS = 16
SWA_HEADS = 8
SWA_KV_HEADS = 2
SWA_WINDOW = 128
SWA_BLOCK = 128
ROPE_BASE = 10000.0
EPS = 1e-6
NEG_INF = -1e30
NA_WIDTH = NA_HEADS * HEAD_DIM
SWA_Q_WIDTH = SWA_HEADS * HEAD_DIM
SWA_KV_WIDTH = SWA_KV_HEADS * HEAD_DIM
QKV_COLS = 3 * NA_WIDTH + SWA_Q_WIDTH + 2 * SWA_KV_WIDTH
LANES = 128
MOD_ROWS = 16
VMEM_LIMIT = 56 * 1024 * 1024
TOKEN_TILE = 1024
MERGE_TILE = 512
FFN_TILE = 1024
FFN_CHUNK = 256
ADALN_BLOCK = 2304
CTX_REQUESTS_PER_STEP = 4
SWA_QTILE = 256
NA_TILE_ROWS = 4
LOG2E = 1.4426950408889634
Q_SCALE = HEAD_DIM ** -0.5 * LOG2E


def _dot(a, b):
    return jnp.dot(a, b, preferred_element_type=F32)


def _dot_nt(a, b):
    return lax.dot_general(a, b, (((1,), (1,)), ((), ())), preferred_element_type=F32)


def _silu(x):
    return x * jax.nn.sigmoid(x)


def _rms(x, g):
    return x * lax.rsqrt(jnp.mean(x * x, axis=-1, keepdims=True) + EPS) * g


def _modulate(x, g, shift, scale):
    return _rms(x, g) * (1.0 + scale) + shift


def _resident(shape):
    nd = len(shape)
    return pl.BlockSpec(shape, lambda *_: (0,) * nd, pipeline_mode=pl.Buffered(1))


def _params():
    return pltpu.CompilerParams(dimension_semantics=("arbitrary",), vmem_limit_bytes=VMEM_LIMIT)


def _adaln_kernel(c_ref, w_ref, b_ref, o_ref):
    s = _silu(c_ref[...]).astype(BF16)
    o_ref[...] = _dot(s, w_ref[...].astype(BF16)) + b_ref[...]


def _adaln(cond, w_ada, b_ada):
    n = w_ada.shape[1]
    blk = ADALN_BLOCK
    return pl.pallas_call(
        _adaln_kernel,
        grid=(n // blk,),
        in_specs=[pl.BlockSpec((MOD_ROWS, D_MODEL), lambda j: (0, 0)),
                  pl.BlockSpec((D_MODEL, blk), lambda j: (0, j)),
                  pl.BlockSpec((1, blk), lambda j: (0, j))],
        out_specs=pl.BlockSpec((MOD_ROWS, blk), lambda j: (0, j)),
        out_shape=jax.ShapeDtypeStruct((MOD_ROWS, n), F32),
        compiler_params=_params(),
        name="adaln",
    )(cond, w_ada, b_ada.reshape(1, n))


def _mod_spec(tokens_per_row, first_row, tile=TOKEN_TILE):
    tiles_per_row = tokens_per_row // tile
    return pl.BlockSpec((1, N_MOD, D_MODEL), lambda i: (first_row + i // tiles_per_row, 0, 0))


def _mod_spec_both(ctx_steps, tokens_per_request, tile):
    tiles_per_request = tokens_per_request // tile
    row = lambda i: jnp.where(i < ctx_steps, 0, 1 + (i - ctx_steps) // tiles_per_request)
    return pl.BlockSpec((1, N_MOD, D_MODEL), lambda i: (row(i), 0, 0))


def _ctx_then_lat(width, ctx_steps, tile):
    ctx = pl.BlockSpec((tile, width), lambda i: (jnp.minimum(i, ctx_steps - 1), 0))
    lat = pl.BlockSpec((tile, width), lambda i: (jnp.maximum(i - ctx_steps, 0), 0))
    return [ctx, lat]


def _ffn_kernel(x_ref, mod_ref, g_ref, wg_ref, wu_ref, wd_ref, gf_ref, *rest, first, final, n_cast, ctx_steps):
    n_out = 1 if ctx_steps is None else 2
    outs, a_scr = rest[n_cast:n_cast + n_out], rest[-1]
    for src, dst in zip(rest[:n_cast], rest[n_cast + n_out:-1]):
        dst[...] = src[...].astype(BF16)
    x = x_ref[...]
    mod = mod_ref[0]
    shift, scale, gate = mod[first:first + 1], mod[first + 1:first + 2], mod[first + 2:first + 3]
    h = _modulate(x, g_ref[...], shift, scale).astype(BF16)
    for c in range(FFN_DIM // FFN_CHUNK):
        sl = slice(c * FFN_CHUNK, (c + 1) * FFN_CHUNK)
        a_scr[:, sl] = (_silu(_dot(h, wg_ref[:, sl])) * _dot(h, wu_ref[:, sl])).astype(BF16)
    y = x + (0.5 * gate) * _dot(a_scr[...], wd_ref[...])
    if final:
        y = _rms(y, gf_ref[...])
    if ctx_steps is None:
        outs[0][...] = y
    else:
        is_ctx = pl.program_id(0) < ctx_steps

        @pl.when(is_ctx)
        def _():
            outs[0][...] = y

        @pl.when(jnp.logical_not(is_ctx))
        def _():
            outs[1][...] = y


def _ffn(x, mod, g, wg, wu, wd, gf, mod_spec, *, first, final, cast=(), ctx_tokens=None):
    t = x.shape[0]
    steps = t // FFN_TILE
    tile = pl.BlockSpec((FFN_TILE, D_MODEL), lambda i: (i, 0))
    chunk = lambda w: pl.BlockSpec((w.shape[0] // steps, w.shape[1]), lambda i: (i, 0))
    assert all(w.shape[0] % (steps * 16) == 0 for w in cast)
    if ctx_tokens is None:
        ctx_steps, out_specs, out_shape = None, [tile], [jax.ShapeDtypeStruct((t, D_MODEL), F32)]
    else:
        ctx_steps = ctx_tokens // FFN_TILE
        out_specs = _ctx_then_lat(D_MODEL, ctx_steps, FFN_TILE)
        out_shape = [jax.ShapeDtypeStruct((n, D_MODEL), F32) for n in (ctx_tokens, t - ctx_tokens)]
    out = pl.pallas_call(
        functools.partial(_ffn_kernel, first=first, final=final, n_cast=len(cast), ctx_steps=ctx_steps),
        grid=(steps,),
        in_specs=[tile, mod_spec, _resident((1, D_MODEL)),
                  _resident(wg.shape), _resident(wu.shape), _resident(wd.shape), _resident((1, D_MODEL))]
        + [chunk(w) for w in cast],
        out_specs=out_specs + [chunk(w) for w in cast],
        out_shape=out_shape + [jax.ShapeDtypeStruct(w.shape, BF16) for w in cast],
        scratch_shapes=[pltpu.VMEM((FFN_TILE, FFN_DIM), BF16)],
        compiler_params=_params(),
        name="ffn",
    )(x, mod, g, wg, wu, wd, gf, *cast)
    return out if len(out) > 1 else out[0]


def _swap16(x):
    lane = lax.broadcasted_iota(jnp.int32, x.shape, 1)
    return jnp.where(lane % 32 < 16, pltpu.roll(x, LANES - 16, 1), pltpu.roll(x, 16, 1))


def _proj_ctx_kernel(x_ref, mod_ref, g_ref, w_ref, qa_ref, qb_ref, ka_ref, va_ref, kb_ref, vb_ref, wkvt_ref):
    @pl.when(pl.program_id(0) == 0)
    def _():
        kv_cols = ((NA_WIDTH, 3 * NA_WIDTH), (3 * NA_WIDTH + SWA_Q_WIDTH, QKV_COLS))
        o = 0
        for lo_col, hi_col in kv_cols:
            wkvt_ref[o:o + hi_col - lo_col, :] = w_ref[:, lo_col:hi_col].astype(F32).T.astype(BF16)
            o += hi_col - lo_col

    mod = mod_ref[0]
    h = _modulate(x_ref[...], g_ref[...], mod[3:4], mod[4:5]).astype(BF16)
    qa_ref[...] = (_dot(h, w_ref[:, :NA_WIDTH]) * Q_SCALE).astype(qa_ref.dtype)
    qb_ref[...] = (_dot(h, w_ref[:, 3 * NA_WIDTH:3 * NA_WIDTH + SWA_Q_WIDTH]) * Q_SCALE).astype(qb_ref.dtype)
    kvt = _dot_nt(wkvt_ref[...], h)
    requests, _, _, seq = ka_ref.shape
    o = 0
    for ref in (ka_ref, va_ref, kb_ref, vb_ref):
        width = ref.shape[1] * HEAD_DIM
        for r in range(requests):
            ref[r] = kvt[o:o + width, r * seq:(r + 1) * seq].reshape(ref.shape[1:])
        o += width


def _rope(y, cos, sin):
    blocks = [y[:, j * LANES:(j + 1) * LANES] for j in range(y.shape[1] // LANES)]
    return [b * cos + _swap16(b) * sin for b in blocks]


def _proj_lat_kernel(x_ref, mod_ref, g_ref, w_ref, cos_ref, sin_ref,
                     qa_ref, ka_ref, va_ref, qb_ref, kb_ref, vb_ref):
    mod = mod_ref[0]
    h = _modulate(x_ref[...], g_ref[...], mod[3:4], mod[4:5]).astype(BF16)
    cos, sin = cos_ref[...], sin_ref[...]
    o = 3 * NA_WIDTH
    for j, b in enumerate(_rope(_dot(h, w_ref[:, o:o + SWA_Q_WIDTH]), cos, sin)):
        qb_ref[:, j * LANES:(j + 1) * LANES] = (b * Q_SCALE).astype(qb_ref.dtype)
    o += SWA_Q_WIDTH
    y = _dot(h, w_ref[:, o:o + 2 * SWA_KV_WIDTH])
    kb_ref[...] = _rope(y[:, :SWA_KV_WIDTH], cos, sin)[0].astype(kb_ref.dtype)
    vb_ref[...] = y[:, SWA_KV_WIDTH:].astype(vb_ref.dtype)
    o = 0
    for ref, scale in ((qa_ref, True), (ka_ref, False), (va_ref, False)):
        y = _dot(h, w_ref[:, o:o + NA_WIDTH])
        ref[...] = (y * Q_SCALE if scale else y).astype(ref.dtype)
        o += NA_WIDTH


def _proj_ctx(x, mod, g, w_qkv, *, seq, tokens_per_row, first_row):
    t = x.shape[0]
    tile = lambda w: pl.BlockSpec((TOKEN_TILE, w), lambda i: (i, 0))
    cache = lambda heads: pl.BlockSpec((TOKEN_TILE // seq, heads, HEAD_DIM, seq), lambda i: (i, 0, 0, 0))
    cache_shape = lambda heads: jax.ShapeDtypeStruct((t // seq, heads, HEAD_DIM, seq), F32)
    kv_heads = (NA_HEADS, NA_HEADS, SWA_KV_HEADS, SWA_KV_HEADS)
    return pl.pallas_call(
        _proj_ctx_kernel,
        grid=(t // TOKEN_TILE,),
        in_specs=[tile(D_MODEL), _mod_spec(tokens_per_row, first_row), _resident((1, D_MODEL)),
                  _resident(w_qkv.shape)],
        out_specs=[tile(NA_WIDTH), tile(SWA_Q_WIDTH)] + [cache(nh) for nh in kv_heads],
        out_shape=[jax.ShapeDtypeStruct((t, NA_WIDTH), BF16), jax.ShapeDtypeStruct((t, SWA_Q_WIDTH), BF16)]
        + [cache_shape(nh) for nh in kv_heads],
        scratch_shapes=[pltpu.VMEM((2 * NA_WIDTH + 2 * SWA_KV_WIDTH, D_MODEL), BF16)],
        compiler_params=_params(),
        name="proj_ctx",
    )(x, mod, g, w_qkv)


def _proj_lat(x, mod, g, w_qkv, rope, *, tokens_per_row, first_row):
    t = x.shape[0]
    tile = lambda w: pl.BlockSpec((TOKEN_TILE, w), lambda i: (i, 0))
    widths = (NA_WIDTH, NA_WIDTH, NA_WIDTH, SWA_Q_WIDTH, SWA_KV_WIDTH, SWA_KV_WIDTH)
    tiles_per_row = tokens_per_row // TOKEN_TILE
    return pl.pallas_call(
        _proj_lat_kernel,
        grid=(t // TOKEN_TILE,),
        in_specs=[tile(D_MODEL), _mod_spec(tokens_per_row, first_row), _resident((1, D_MODEL)),
                  _resident(w_qkv.shape)]
        + [pl.BlockSpec((TOKEN_TILE, LANES), lambda i: (i % tiles_per_row, 0))] * 2,
        out_specs=[tile(w) for w in widths],
        out_shape=[jax.ShapeDtypeStruct((t, w), BF16) for w in widths],
        compiler_params=_params(),
        name="proj_lat",
    )(x, mod, g, w_qkv, *rope)


def _lane_halves(shape):
    lane = lax.broadcasted_iota(jnp.int32, shape, 1) % LANES
    return lane < HEAD_DIM, lane >= HEAD_DIM


def _softmax_cols(cols, sink=None):
    m = jnp.max(functools.reduce(jnp.maximum, cols), axis=-1, keepdims=True)
    if sink is not None:
        m = jnp.maximum(m, sink)
    es = [jnp.exp2(c - m) for c in cols]
    l = jnp.sum(functools.reduce(jnp.add, es), axis=-1, keepdims=True)
    if sink is not None:
        l = l + jnp.exp2(sink - m)
    return es, l


def _cols(x):
    return [x[:, j * LANES:(j + 1) * LANES] for j in range(x.shape[1] // LANES)]


def _attn_ctx_kernel(sink_ref, qa_ref, qb_ref, ka_ref, va_ref, kb_ref, vb_ref, ya_ref, yb_ref):
    requests, _, _, seq = ka_ref.shape
    lo, hi = _lane_halves((seq, LANES))
    top = lax.broadcasted_iota(jnp.int32, (2 * seq, 1), 0) < seq
    zero = jnp.zeros((), BF16)

    def heads_of_block(q, kt, vt, sink):
        q = jnp.concatenate([jnp.where(lo, q, zero), jnp.where(hi, q, zero)], axis=0)
        es, l = _softmax_cols(_cols(_dot(q, kt)), sink)
        o = _dot_nt(jnp.concatenate(es, axis=1).astype(BF16), vt) * (1.0 / l)
        return jnp.where(lo, o[:seq], o[seq:])

    for r in range(requests):
        rows = slice(r * seq, (r + 1) * seq)
        for p in range(NA_HEADS // 2):
            sl = slice(p * LANES, (p + 1) * LANES)
            kt = ka_ref[r, 2 * p:2 * p + 2].reshape(LANES, seq).astype(BF16)
            vt = va_ref[r, 2 * p:2 * p + 2].reshape(LANES, seq).astype(BF16)
            ya_ref[rows, sl] = heads_of_block(qa_ref[rows, sl], kt, vt, None).astype(ya_ref.dtype)
        for p in range(SWA_HEADS // 2):
            sl = slice(p * LANES, (p + 1) * LANES)
            kv = (2 * p) // (SWA_HEADS // SWA_KV_HEADS)
            kt, vt = kb_ref[r, kv].astype(BF16), vb_ref[r, kv].astype(BF16)
            kt, vt = jnp.concatenate([kt, kt], axis=0), jnp.concatenate([vt, vt], axis=0)
            sink = jnp.where(top, sink_ref[2 * p], sink_ref[2 * p + 1]) * LOG2E
            yb_ref[rows, sl] = heads_of_block(qb_ref[rows, sl], kt, vt, sink).astype(yb_ref.dtype)


def _attn_ctx(sink, qa, qb, ka, va, kb, vb, *, seq):
    t = qa.shape[0]
    per_step = CTX_REQUESTS_PER_STEP
    tile = lambda w: pl.BlockSpec((per_step * seq, w), lambda b: (b, 0))
    cache = lambda heads: pl.BlockSpec((per_step, heads, HEAD_DIM, seq), lambda b: (b, 0, 0, 0))
    return pl.pallas_call(
        _attn_ctx_kernel,
        grid=(t // (per_step * seq),),
        in_specs=[pl.BlockSpec(memory_space=pltpu.SMEM), tile(NA_WIDTH), tile(SWA_Q_WIDTH),
                  cache(NA_HEADS), cache(NA_HEADS), cache(SWA_KV_HEADS), cache(SWA_KV_HEADS)],
        out_specs=[tile(NA_WIDTH), tile(SWA_Q_WIDTH)],
        out_shape=[jax.ShapeDtypeStruct((t, NA_WIDTH), BF16), jax.ShapeDtypeStruct((t, SWA_Q_WIDTH), BF16)],
        compiler_params=_params(),
        name="attn_ctx",
    )(sink, qa, qb, ka, va, kb, vb)


def _attn_na_kernel(q_ref, k_ref, v_ref, ck_ref, cv_ref, bias_ref, y_ref, ck_scr, cv_scr, *, n, ctx):
    rows = n // GRID_W
    tile_q = NA_TILE_ROWS * GRID_W
    zero = jnp.zeros((), BF16)
    ck_scr[...] = ck_ref[...].reshape(NA_WIDTH, ctx).astype(BF16)
    cv_scr[...] = cv_ref[...].reshape(NA_WIDTH, ctx).astype(BF16)
    lo, hi = _lane_halves((tile_q, LANES))
    left_half = lax.broadcasted_iota(jnp.int32, (GRID_W, LANES), 1) < GRID_W
    empty = jnp.zeros((GRID_W, LANES), BF16)

    def tile(q0, k0, key_rows, lead, first):
        pairs = key_rows // 2
        for p in range(NA_HEADS // 2):
            sl = slice(p * LANES, (p + 1) * LANES)
            q = q_ref[pl.ds(q0, tile_q), sl]
            q = jnp.concatenate([jnp.where(lo, q, zero), jnp.where(hi, q, zero)], axis=0)
            s_nb = _dot_nt(q, k_ref[pl.ds(k0, key_rows * GRID_W), sl])
            s_ctx = _dot(q, ck_scr[sl, :])
            e_nb, e_ctx, ls = [], [], []
            for idx in range(2):
                for a in range(NA_TILE_ROWS):
                    qa = slice(idx * tile_q + a * GRID_W, idx * tile_q + (a + 1) * GRID_W)
                    inside = lambda i: first[a] <= i < first[a] + NA_ROWS
                    cols, where = [], []
                    for m in range(pairs):
                        if not (inside(2 * m) or inside(2 * m + 1)):
                            continue
                        blk = s_nb[qa, m * LANES:(m + 1) * LANES] + bias_ref[2 * p + idx, 2 * m - a - lead + NA_ROWS]
                        if not inside(2 * m + 1):
                            blk = jnp.where(left_half, blk, NEG_INF)
                        elif not inside(2 * m):
                            blk = jnp.where(left_half, NEG_INF, blk)
                        cols.append(blk)
                        where.append(m)
                    cols += [s_ctx[qa, j * LANES:(j + 1) * LANES] for j in range(ctx // LANES)]
                    es, l = _softmax_cols(cols)
                    es = [e.astype(BF16) for e in es]
                    e_nb.append(jnp.concatenate(
                        [es[where.index(m)] if m in where else empty for m in range(pairs)], axis=1))
                    e_ctx.append(jnp.concatenate(es[len(where):], axis=1))
                    ls.append(l)
            o = (_dot(jnp.concatenate(e_nb, axis=0), v_ref[pl.ds(k0, key_rows * GRID_W), sl])
                 + _dot_nt(jnp.concatenate(e_ctx, axis=0), cv_scr[sl, :]))
            o = o * (1.0 / jnp.concatenate(ls, axis=0))
            y_ref[pl.ds(q0, tile_q), sl] = jnp.where(lo, o[:tile_q], o[tile_q:]).astype(y_ref.dtype)

    half = NA_ROWS // 2
    tile(0, 0, NA_ROWS, 0, (0,) * NA_TILE_ROWS)

    for t in range(1, rows // NA_TILE_ROWS - 1):
        tile(t * tile_q, t * tile_q - half * GRID_W, NA_ROWS + NA_TILE_ROWS, half, tuple(range(NA_TILE_ROWS)))
    tile(n - tile_q, n - NA_ROWS * GRID_W, NA_ROWS, half, (0,) * NA_TILE_ROWS)


def _attn_na(q, k, v, ck, cv, bias, *, n, ctx):
    t = q.shape[0]
    tile = pl.BlockSpec((n, NA_WIDTH), lambda b: (b, 0))
    ctile = pl.BlockSpec((None, NA_HEADS, HEAD_DIM, ctx), lambda b: (b, 0, 0, 0))
    return pl.pallas_call(
        functools.partial(_attn_na_kernel, n=n, ctx=ctx),
        grid=(t // n,),
        in_specs=[tile, tile, tile, ctile, ctile, _resident(bias.shape)],
        out_specs=tile,
        out_shape=jax.ShapeDtypeStruct((t, NA_WIDTH), BF16),
        scratch_shapes=[pltpu.VMEM((NA_WIDTH, ctx), BF16)] * 2,
        compiler_params=_params(),
        name="attn_na",
    )(q, k, v, ck, cv, bias)


def _na_bias_table(rel_bias):
    rows = _na_bias_rows(rel_bias)
    shape = (NA_HEADS, 2 * NA_ROWS, GRID_W, LANES)
    return pl.pallas_call(
        _fill_na_bias,
        grid=(1,),
        in_specs=[pl.BlockSpec(rows.shape, lambda i: (0, 0, 0))],
        out_specs=pl.BlockSpec(shape, lambda i: (0, 0, 0, 0)),
        out_shape=jax.ShapeDtypeStruct(shape, F32),
        compiler_params=_params(),
        name="na_bias",
    )(rows)


def _na_bias_rows(rel_bias):
    side = GRID_W - NA_COLS
    z = jnp.pad(rel_bias * LOG2E, ((0, 0), (0, 0), (side, side + 1)))
    return jnp.pad(z, ((0, 0), (1, 1), (0, 0)), constant_values=NEG_INF)


def _fill_na_bias(rows_ref, bias_scr):
    q = lax.broadcasted_iota(jnp.int32, (GRID_W, LANES), 0)
    lane = lax.broadcasted_iota(jnp.int32, (GRID_W, LANES), 1)
    kc = lane % GRID_W
    col_start = jnp.clip(q - NA_COLS // 2, 0, GRID_W - NA_COLS)
    in_window = (kc >= col_start) & (kc < col_start + NA_COLS)
    for h in range(NA_HEADS):
        blocks = []
        for j in range(2 * NA_ROWS + 1):
            row = jnp.broadcast_to(rows_ref[h, j:j + 1, :], (GRID_W, LANES))
            rolled = pltpu.roll(row, LANES - GRID_W + 1, 1, stride=1, stride_axis=0)
            blocks.append(jnp.where(in_window, rolled, NEG_INF))
        for j in range(2 * NA_ROWS):
            bias_scr[h, j] = jnp.where(lane < GRID_W, blocks[j], pltpu.roll(blocks[j + 1], GRID_W, 1))


def _attn_swa_kernel(sink_ref, q_ref, k_ref, v_ref, ck_ref, cv_ref, y_ref, k_scr, v_scr, ck_scr, cv_scr, *, n, ctx):
    group = SWA_HEADS // SWA_KV_HEADS
    band = SWA_QTILE + 2 * SWA_BLOCK
    for src, csrc, dst, cdst in ((k_ref, ck_ref, k_scr, ck_scr), (v_ref, cv_ref, v_scr, cv_scr)):
        x = src[...].astype(F32)
        xr = pltpu.roll(x, HEAD_DIM, 1)
        first_half, second_half = _lane_halves(x.shape)
        dst[0] = jnp.where(first_half, x, xr).astype(BF16)
        dst[1] = jnp.where(second_half, x, xr).astype(BF16)
        for kv in range(SWA_KV_HEADS):
            c = csrc[kv].astype(BF16)
            cdst[kv] = jnp.concatenate([c, c], axis=0)

    qi = lax.broadcasted_iota(jnp.int32, (SWA_QTILE, band), 0)
    kj = lax.broadcasted_iota(jnp.int32, (SWA_QTILE, band), 1)
    lo, hi = _lane_halves((SWA_QTILE, LANES))
    top = lax.broadcasted_iota(jnp.int32, (2 * SWA_QTILE, 1), 0) < SWA_QTILE
    zero = jnp.zeros((), BF16)

    def block_body(b, carry):
        q0 = pl.multiple_of(b * SWA_QTILE, SWA_QTILE)
        k0 = pl.multiple_of(jnp.clip(q0 - SWA_BLOCK, 0, n - band), SWA_BLOCK)
        mask = jnp.where(jnp.abs(kj - qi + (k0 - q0)) <= SWA_WINDOW, 0.0, NEG_INF)
        mask2 = jnp.concatenate([mask, mask], axis=0)
        for p in range(SWA_HEADS // 2):
            kv = 2 * p // group
            sl = slice(p * LANES, (p + 1) * LANES)
            q = q_ref[pl.ds(q0, SWA_QTILE), sl]
            q = jnp.concatenate([jnp.where(lo, q, zero), jnp.where(hi, q, zero)], axis=0)
            s_band = _dot_nt(q, k_scr[kv, pl.ds(k0, band), :])
            s_ctx = _dot(q, ck_scr[kv])
            sink = jnp.where(top, sink_ref[2 * p], sink_ref[2 * p + 1]) * LOG2E
            es, l = _softmax_cols(_cols(s_band + mask2) + _cols(s_ctx), sink)
            e_band = jnp.concatenate(es[:band // LANES], axis=1).astype(BF16)
            e_ctx = jnp.concatenate(es[band // LANES:], axis=1).astype(BF16)
            o = (_dot(e_band, v_scr[kv, pl.ds(k0, band), :]) + _dot_nt(e_ctx, cv_scr[kv])) * (1.0 / l)
            y_ref[pl.ds(q0, SWA_QTILE), sl] = jnp.where(lo, o[:SWA_QTILE], o[SWA_QTILE:]).astype(y_ref.dtype)
        return carry

    lax.fori_loop(0, n // SWA_QTILE, block_body, 0, unroll=4)


def _attn_swa(sink, q, k, v, ck, cv, *, n, ctx):
    t = q.shape[0]
    tile = lambda w: pl.BlockSpec((n, w), lambda b: (b, 0))
    ctile = pl.BlockSpec((None, SWA_KV_HEADS, HEAD_DIM, ctx), lambda b: (b, 0, 0, 0))
    return pl.pallas_call(
        functools.partial(_attn_swa_kernel, n=n, ctx=ctx),
        grid=(t // n,),
        in_specs=[pl.BlockSpec(memory_space=pltpu.SMEM), tile(SWA_Q_WIDTH), tile(SWA_KV_WIDTH),
                  tile(SWA_KV_WIDTH), ctile, ctile],
        out_specs=tile(SWA_Q_WIDTH),
        out_shape=jax.ShapeDtypeStruct((t, SWA_Q_WIDTH), BF16),
        scratch_shapes=[pltpu.VMEM((SWA_KV_HEADS, n, SWA_KV_WIDTH), BF16)] * 2
        + [pltpu.VMEM((SWA_KV_HEADS, LANES, ctx), BF16)] * 2,
        compiler_params=_params(),
        name="attn_swa",
    )(sink, q, k, v, ck, cv)


def _rope_tables(n):
    half = HEAD_DIM // 4
    freqs = np.power(ROPE_BASE, -np.arange(half, dtype=np.float64) / half)
    t = np.arange(n)
    cos, sin = [], []
    for pos in (t // GRID_W, t % GRID_W):
        ang = pos.astype(np.float64)[:, None] * freqs[None, :]
        cos += [np.cos(ang), np.cos(ang)]
        sin += [-np.sin(ang), np.sin(ang)]
    cos, sin = np.concatenate(cos, axis=-1), np.concatenate(sin, axis=-1)
    reps = LANES // HEAD_DIM
    return jnp.asarray(np.tile(cos, (1, reps)), F32), jnp.asarray(np.tile(sin, (1, reps)), F32)


def _merge_kernel(xc_ref, xl_ref, yac_ref, yal_ref, ybc_ref, ybl_ref, mod_ref, g_ref, win_ref, wba_ref, wbb_ref,
                  wout_ref, o_ref, *, ctx_steps):
    is_ctx = pl.program_id(0) < ctx_steps
    pick = lambda ctx_ref, lat_ref: jnp.where(is_ctx, ctx_ref[...], lat_ref[...])
    x, ya, yb = pick(xc_ref, xl_ref), pick(yac_ref, yal_ref), pick(ybc_ref, ybl_ref)
    mod = mod_ref[0]
    h = _modulate(x, g_ref[...], mod[3:4], mod[4:5]).astype(BF16)
    a = jax.nn.sigmoid(_dot(h, win_ref[:, QKV_COLS:QKV_COLS + D_MODEL])) * _dot(ya, wba_ref[...])
    b = jax.nn.sigmoid(_dot(h, win_ref[:, QKV_COLS + D_MODEL:])) * _dot(yb, wbb_ref[...])
    o_ref[...] = x + mod[5:6] * _dot((a + b).astype(BF16), wout_ref[...])


def _merge(x, ya, yb, mod, g, w_in, wba, wbb, wout, *, tokens_per_request):
    ctx_steps = x[0].shape[0] // MERGE_TILE
    t = x[0].shape[0] + x[1].shape[0]
    both = lambda w: _ctx_then_lat(w, ctx_steps, MERGE_TILE)
    return pl.pallas_call(
        functools.partial(_merge_kernel, ctx_steps=ctx_steps),
        grid=(t // MERGE_TILE,),
        in_specs=both(D_MODEL) + both(NA_WIDTH) + both(SWA_Q_WIDTH)
        + [_mod_spec_both(ctx_steps, tokens_per_request, MERGE_TILE), _resident((1, D_MODEL)),
           _resident(w_in.shape), _resident(wba.shape), _resident(wbb.shape), _resident(wout.shape)],
        out_specs=pl.BlockSpec((MERGE_TILE, D_MODEL), lambda i: (i, 0)),
        out_shape=jax.ShapeDtypeStruct((t, D_MODEL), F32),
        compiler_params=_params(),
        name="merge",
    )(*x, *ya, *yb, mod, g, w_in, wba, wbb, wout)


def kernel(x_prompt, x_sample, cache_na_k, cache_na_v, cache_swa_k, cache_swa_v, c, c_ctx, w_ada, b_ada,
           norm_ffn1, ffn1_w_gate, ffn1_w_up, ffn1_w_down, norm_mix, w_in, na_rel_bias, swa_sink,
           w_branch_na, w_branch_swa, w_out, norm_ffn2, ffn2_w_gate, ffn2_w_up, ffn2_w_down, norm_final):
    depth = w_ada.shape[0]
    assert depth == 1
    batch, seq, _ = x_prompt.shape
    dec_batch, dec_seq, _ = x_sample.shape
    past = cache_na_k.shape[2]
    layer = 0
    row = lambda v: v.reshape(1, D_MODEL)
    bf = lambda w: w.astype(BF16)

    cond = jnp.zeros((MOD_ROWS, D_MODEL), F32).at[0].set(c_ctx).at[1:1 + dec_batch].set(c)
    mod = _adaln(cond, w_ada[layer], b_ada[layer]).reshape(MOD_ROWS, N_MOD, D_MODEL)

    ffn1 = (row(norm_ffn1[layer]), bf(ffn1_w_gate[layer]), bf(ffn1_w_up[layer]), bf(ffn1_w_down[layer]),
            row(norm_final))
    g_mix = row(norm_mix[layer])
    sink = swa_sink[layer]
    ctx_rows = dict(tokens_per_row=batch * seq, first_row=0)
    lat_rows = dict(tokens_per_row=dec_seq, first_row=1)
    ctx_tokens = batch * seq

    later = (ffn2_w_gate[layer], ffn2_w_up[layer], ffn2_w_down[layer], w_in[layer],
             w_branch_na[layer], w_branch_swa[layer], w_out[layer])
    x_lat, *later = _ffn(x_sample.reshape(dec_batch * dec_seq, D_MODEL), mod, *ffn1,
                         _mod_spec(tile=FFN_TILE, **lat_rows), first=0, final=False, cast=later)
    ffn2 = (row(norm_ffn2[layer]), *later[:3], row(norm_final))
    w_qkv = later[3]
    merge_w = tuple(later[3:])

    x_ctx = _ffn(x_prompt.reshape(ctx_tokens, D_MODEL), mod, *ffn1, _mod_spec(tile=FFN_TILE, **ctx_rows),
                 first=0, final=False)
    qa, qb, *new_cache = _proj_ctx(x_ctx, mod, g_mix, w_qkv, seq=seq, **ctx_rows)
    ya_ctx, yb_ctx = _attn_ctx(sink, qa, qb, *new_cache, seq=seq)

    qa, kal, val, qb, kbl, vbl = _proj_lat(x_lat, mod, g_mix, w_qkv, _rope_tables(dec_seq), **lat_rows)
    transposed = lambda cache: jnp.transpose(cache[:, layer], (0, 2, 3, 1))
    ya_lat = _attn_na(qa, kal, val, transposed(cache_na_k), transposed(cache_na_v),
                      _na_bias_table(na_rel_bias[layer]), n=dec_seq, ctx=past)
    yb_lat = _attn_swa(sink, qb, kbl, vbl, transposed(cache_swa_k), transposed(cache_swa_v), n=dec_seq, ctx=past)

    x = _merge((x_ctx, x_lat), (ya_ctx, ya_lat), (yb_ctx, yb_lat), mod, g_mix, *merge_w, tokens_per_request=dec_seq)
    y_prompt, y_sample = _ffn(x, mod, *ffn2, _mod_spec_both(ctx_tokens // FFN_TILE, dec_seq, FFN_TILE),
                              first=6, final=True, ctx_tokens=ctx_tokens)
    y_prompt = y_prompt.reshape(batch, seq, D_MODEL)
    y_sample = y_sample.reshape(dec_batch, dec_seq, D_MODEL)

    new_cache = [jnp.transpose(t, (0, 3, 1, 2))[:, None] for t in new_cache]
    return (y_prompt, y_sample, *new_cache)
```

```python
import functools

import numpy as np
import jax
import jax.numpy as jnp
from jax import lax
from jax.experimental import pallas as pl
from jax.experimental.pallas import tpu as pltpu

F32 = jnp.float32
BF16 = jnp.bfloat16

D_MODEL = 1024
FFN_DIM = 2816
HEAD_DIM = 64
N_MOD = 9
GRID_W = 64
NA_HEADS = 8
NA_ROWS = 8
NA_COLS = 16
SWA_HEADS = 8
SWA_KV_HEADS = 2
SWA_WINDOW = 128
SWA_BLOCK = 128
ROPE_BASE = 10000.0
EPS = 1e-6
NEG_INF = -1e30
NA_WIDTH = NA_HEADS * HEAD_DIM
SWA_Q_WIDTH = SWA_HEADS * HEAD_DIM
SWA_KV_WIDTH = SWA_KV_HEADS * HEAD_DIM
QKV_COLS = 3 * NA_WIDTH + SWA_Q_WIDTH + 2 * SWA_KV_WIDTH
LANES = 128
MOD_ROWS = 16
VMEM_LIMIT = 56 * 1024 * 1024
TOKEN_TILE = 1024
MERGE_TILE = 512
FFN_TILE = 1024
FFN_CHUNK = 256
ADALN_ROWS = 256
CTX_REQUESTS_PER_STEP = 4
SWA_QTILE = 256
NA_TILE_ROWS = 4
LOG2E = 1.4426950408889634
Q_SCALE = HEAD_DIM ** -0.5 * LOG2E


def _dot(a, b):
    return jnp.dot(a, b, preferred_element_type=F32)


def _dot_nt(a, b):
    return lax.dot_general(a, b, (((1,), (1,)), ((), ())), preferred_element_type=F32)


def _silu(x):
    return x * jax.nn.sigmoid(x)


def _rms(x, g):
    return x * lax.rsqrt(jnp.mean(x * x, axis=-1, keepdims=True) + EPS) * g


def _modulate(x, g, shift, scale):
    return _rms(x, g) * (1.0 + scale) + shift


def _resident(shape):
    nd = len(shape)
    return pl.BlockSpec(shape, lambda *_: (0,) * nd, pipeline_mode=pl.Buffered(1))


def _params():
    return pltpu.CompilerParams(dimension_semantics=("arbitrary",), vmem_limit_bytes=VMEM_LIMIT)


def _adaln_kernel(c_ref, w_ref, b_ref, o_ref):
    @pl.when(pl.program_id(0) == 0)
    def _():
        o_ref[...] = jnp.broadcast_to(b_ref[...], o_ref.shape)

    s = _silu(c_ref[0]).astype(BF16)
    o_ref[...] += _dot(s, w_ref[...].astype(BF16))


def _adaln(cond, w_ada, b_ada):
    k, n = w_ada.shape
    steps = k // ADALN_ROWS
    cond = cond.reshape(MOD_ROWS, steps, ADALN_ROWS).transpose(1, 0, 2)
    return pl.pallas_call(
        _adaln_kernel,
        grid=(steps,),
        in_specs=[pl.BlockSpec((1, MOD_ROWS, ADALN_ROWS), lambda j: (j, 0, 0)),
                  pl.BlockSpec((ADALN_ROWS, n), lambda j: (j, 0)),
                  pl.BlockSpec((1, n), lambda j: (0, 0))],
        out_specs=pl.BlockSpec((MOD_ROWS, n), lambda j: (0, 0)),
        out_shape=jax.ShapeDtypeStruct((MOD_ROWS, n), F32),
        compiler_params=_params(),
        name="adaln",
    )(cond, w_ada, b_ada.reshape(1, n))


def _mod_spec(tokens_per_row, first_row, tile=TOKEN_TILE):
    tiles_per_row = tokens_per_row // tile
    return pl.BlockSpec((1, N_MOD, D_MODEL), lambda i: (first_row + i // tiles_per_row, 0, 0))


def _mod_spec_both(ctx_steps, tokens_per_request, tile):
    tiles_per_request = tokens_per_request // tile
    row = lambda i: jnp.where(i < ctx_steps, 0, 1 + (i - ctx_steps) // tiles_per_request)
    return pl.BlockSpec((1, N_MOD, D_MODEL), lambda i: (row(i), 0, 0))


def _ctx_then_lat(width, ctx_steps, tile):
    ctx = pl.BlockSpec((tile, width), lambda i: (jnp.minimum(i, ctx_steps - 1), 0))
    lat = pl.BlockSpec((tile, width), lambda i: (jnp.maximum(i - ctx_steps, 0), 0))
    return [ctx, lat]


def _ffn_kernel(x_ref, mod_ref, g_ref, wg_ref, wu_ref, wd_ref, gf_ref, *rest, first, final, n_cast, ctx_steps):
    n_out = 1 if ctx_steps is None else 2
    outs, a_scr = rest[n_cast:n_cast + n_out], rest[-1]
    for src, dst in zip(rest[:n_cast], rest[n_cast + n_out:-1]):
        dst[...] = src[...].astype(BF16)
    x = x_ref[...]
    mod = mod_ref[0]
    shift, scale, gate = mod[first:first + 1], mod[first + 1:first + 2], mod[first + 2:first + 3]
    h = _modulate(x, g_ref[...], shift, scale).astype(BF16)
    for c in range(FFN_DIM // FFN_CHUNK):
        sl = slice(c * FFN_CHUNK, (c + 1) * FFN_CHUNK)
        a_scr[:, sl] = (_silu(_dot(h, wg_ref[:, sl])) * _dot(h, wu_ref[:, sl])).astype(BF16)
    y = x + (0.5 * gate) * _dot(a_scr[...], wd_ref[...])
    if final:
        y = _rms(y, gf_ref[...])
    if ctx_steps is None:
        outs[0][...] = y
    else:
        is_ctx = pl.program_id(0) < ctx_steps

        @pl.when(is_ctx)
        def _():
            outs[0][...] = y

        @pl.when(jnp.logical_not(is_ctx))
        def _():
            outs[1][...] = y


def _ffn(x, mod, g, wg, wu, wd, gf, mod_spec, *, first, final, cast=(), ctx_tokens=None):
    t = x.shape[0]
    steps = t // FFN_TILE
    tile = pl.BlockSpec((FFN_TILE, D_MODEL), lambda i: (i, 0))
    chunk = lambda w: pl.BlockSpec((w.shape[0] // steps, w.shape[1]), lambda i: (i, 0))
    assert all(w.shape[0] % (steps * 16) == 0 for w in cast)
    if ctx_tokens is None:
        ctx_steps, out_specs, out_shape = None, [tile], [jax.ShapeDtypeStruct((t, D_MODEL), F32)]
    else:
        ctx_steps = ctx_tokens // FFN_TILE
        out_specs = _ctx_then_lat(D_MODEL, ctx_steps, FFN_TILE)
        out_shape = [jax.ShapeDtypeStruct((n, D_MODEL), F32) for n in (ctx_tokens, t - ctx_tokens)]
    out = pl.pallas_call(
        functools.partial(_ffn_kernel, first=first, final=final, n_cast=len(cast), ctx_steps=ctx_steps),
        grid=(steps,),
        in_specs=[tile, mod_spec, _resident((1, D_MODEL)),
                  _resident(wg.shape), _resident(wu.shape), _resident(wd.shape), _resident((1, D_MODEL))]
        + [chunk(w) for w in cast],
        out_specs=out_specs + [chunk(w) for w in cast],
        out_shape=out_shape + [jax.ShapeDtypeStruct(w.shape, BF16) for w in cast],
        scratch_shapes=[pltpu.VMEM((FFN_TILE, FFN_DIM), BF16)],
        compiler_params=_params(),
        name="ffn",
    )(x, mod, g, wg, wu, wd, gf, *cast)
    return out if len(out) > 1 else out[0]


def _swap16(x):
    lane = lax.broadcasted_iota(jnp.int32, x.shape, 1)
    return jnp.where(lane % 32 < 16, pltpu.roll(x, LANES - 16, 1), pltpu.roll(x, 16, 1))


def _proj_ctx_kernel(x_ref, mod_ref, g_ref, w_ref, qa_ref, qb_ref, ka_ref, va_ref, kb_ref, vb_ref, wkvt_ref):
    @pl.when(pl.program_id(0) == 0)
    def _():
        kv_cols = ((NA_WIDTH, 3 * NA_WIDTH), (3 * NA_WIDTH + SWA_Q_WIDTH, QKV_COLS))
        o = 0
        for lo_col, hi_col in kv_cols:
            wkvt_ref[o:o + hi_col - lo_col, :] = w_ref[:, lo_col:hi_col].astype(F32).T.astype(BF16)
            o += hi_col - lo_col

    mod = mod_ref[0]
    h = _modulate(x_ref[...], g_ref[...], mod[3:4], mod[4:5]).astype(BF16)
    qa_ref[...] = (_dot(h, w_ref[:, :NA_WIDTH]) * Q_SCALE).astype(qa_ref.dtype)
    qb_ref[...] = (_dot(h, w_ref[:, 3 * NA_WIDTH:3 * NA_WIDTH + SWA_Q_WIDTH]) * Q_SCALE).astype(qb_ref.dtype)
    kvt = _dot_nt(wkvt_ref[...], h)
    requests, _, _, seq = ka_ref.shape
    o = 0
    for ref in (ka_ref, va_ref, kb_ref, vb_ref):
        width = ref.shape[1] * HEAD_DIM
        for r in range(requests):
            ref[r] = kvt[o:o + width, r * seq:(r + 1) * seq].reshape(ref.shape[1:])
        o += width


def _rope(y, cos, sin):
    blocks = [y[:, j * LANES:(j + 1) * LANES] for j in range(y.shape[1] // LANES)]
    return [b * cos + _swap16(b) * sin for b in blocks]


def _proj_lat_kernel(x_ref, mod_ref, g_ref, w_ref, cos_ref, sin_ref,
                     qa_ref, ka_ref, va_ref, qb_ref, kb_ref, vb_ref):
    mod = mod_ref[0]
    h = _modulate(x_ref[...], g_ref[...], mod[3:4], mod[4:5]).astype(BF16)
    cos, sin = cos_ref[...], sin_ref[...]
    o = 3 * NA_WIDTH
    for j, b in enumerate(_rope(_dot(h, w_ref[:, o:o + SWA_Q_WIDTH]), cos, sin)):
        qb_ref[:, j * LANES:(j + 1) * LANES] = (b * Q_SCALE).astype(qb_ref.dtype)
    o += SWA_Q_WIDTH
    y = _dot(h, w_ref[:, o:o + 2 * SWA_KV_WIDTH])
    kb_ref[...] = _rope(y[:, :SWA_KV_WIDTH], cos, sin)[0].astype(kb_ref.dtype)
    vb_ref[...] = y[:, SWA_KV_WIDTH:].astype(vb_ref.dtype)
    o = 0
    for ref, scale in ((qa_ref, True), (ka_ref, False), (va_ref, False)):
        y = _dot(h, w_ref[:, o:o + NA_WIDTH])
        ref[...] = (y * Q_SCALE if scale else y).astype(ref.dtype)
        o += NA_WIDTH


def _proj_ctx(x, mod, g, w_qkv, *, seq, tokens_per_row, first_row):
    t = x.shape[0]
    tile = lambda w: pl.BlockSpec((TOKEN_TILE, w), lambda i: (i, 0))
    cache = lambda heads: pl.BlockSpec((TOKEN_TILE // seq, heads, HEAD_DIM, seq), lambda i: (i, 0, 0, 0))
    cache_shape = lambda heads: jax.ShapeDtypeStruct((t // seq, heads, HEAD_DIM, seq), F32)
    kv_heads = (NA_HEADS, NA_HEADS, SWA_KV_HEADS, SWA_KV_HEADS)
    return pl.pallas_call(
        _proj_ctx_kernel,
        grid=(t // TOKEN_TILE,),
        in_specs=[tile(D_MODEL), _mod_spec(tokens_per_row, first_row), _resident((1, D_MODEL)),
                  _resident(w_qkv.shape)],
        out_specs=[tile(NA_WIDTH), tile(SWA_Q_WIDTH)] + [cache(nh) for nh in kv_heads],
        out_shape=[jax.ShapeDtypeStruct((t, NA_WIDTH), BF16), jax.ShapeDtypeStruct((t, SWA_Q_WIDTH), BF16)]
        + [cache_shape(nh) for nh in kv_heads],
        scratch_shapes=[pltpu.VMEM((2 * NA_WIDTH + 2 * SWA_KV_WIDTH, D_MODEL), BF16)],
        compiler_params=_params(),
        name="proj_ctx",
    )(x, mod, g, w_qkv)


def _proj_lat(x, mod, g, w_qkv, rope, *, tokens_per_row, first_row):
    t = x.shape[0]
    tile = lambda w: pl.BlockSpec((TOKEN_TILE, w), lambda i: (i, 0))
    widths = (NA_WIDTH, NA_WIDTH, NA_WIDTH, SWA_Q_WIDTH, SWA_KV_WIDTH, SWA_KV_WIDTH)
    tiles_per_row = tokens_per_row // TOKEN_TILE
    return pl.pallas_call(
        _proj_lat_kernel,
        grid=(t // TOKEN_TILE,),
        in_specs=[tile(D_MODEL), _mod_spec(tokens_per_row, first_row), _resident((1, D_MODEL)),
                  _resident(w_qkv.shape)]
        + [pl.BlockSpec((TOKEN_TILE, LANES), lambda i: (i % tiles_per_row, 0))] * 2,
        out_specs=[tile(w) for w in widths],
        out_shape=[jax.ShapeDtypeStruct((t, w), BF16) for w in widths],
        compiler_params=_params(),
        name="proj_lat",
    )(x, mod, g, w_qkv, *rope)


def _lane_halves(shape):
    lane = lax.broadcasted_iota(jnp.int32, shape, 1) % LANES
    return lane < HEAD_DIM, lane >= HEAD_DIM


def _softmax_cols(cols, sink=None):
    m = jnp.max(functools.reduce(jnp.maximum, cols), axis=-1, keepdims=True)
    if sink is not None:
        m = jnp.maximum(m, sink)
    es = [jnp.exp2(c - m) for c in cols]
    l = jnp.sum(functools.reduce(jnp.add, es), axis=-1, keepdims=True)
    if sink is not None:
        l = l + jnp.exp2(sink - m)
    return es, l


def _cols(x):
    return [x[:, j * LANES:(j + 1) * LANES] for j in range(x.shape[1] // LANES)]


def _attn_ctx_kernel(sink_ref, qa_ref, qb_ref, ka_ref, va_ref, kb_ref, vb_ref, ya_ref, yb_ref):
    requests, _, _, seq = ka_ref.shape
    lo, hi = _lane_halves((seq, LANES))
    top = lax.broadcasted_iota(jnp.int32, (2 * seq, 1), 0) < seq
    zero = jnp.zeros((), BF16)

    def heads_of_block(q, kt, vt, sink):
        q = jnp.concatenate([jnp.where(lo, q, zero), jnp.where(hi, q, zero)], axis=0)
        es, l = _softmax_cols(_cols(_dot(q, kt)), sink)
        o = _dot_nt(jnp.concatenate(es, axis=1).astype(BF16), vt) * (1.0 / l)
        return jnp.where(lo, o[:seq], o[seq:])

    for r in range(requests):
        rows = slice(r * seq, (r + 1) * seq)
        for p in range(NA_HEADS // 2):
            sl = slice(p * LANES, (p + 1) * LANES)
            kt = ka_ref[r, 2 * p:2 * p + 2].reshape(LANES, seq).astype(BF16)
            vt = va_ref[r, 2 * p:2 * p + 2].reshape(LANES, seq).astype(BF16)
            ya_ref[rows, sl] = heads_of_block(qa_ref[rows, sl], kt, vt, None).astype(ya_ref.dtype)
        for p in range(SWA_HEADS // 2):
            sl = slice(p * LANES, (p + 1) * LANES)
            kv = (2 * p) // (SWA_HEADS // SWA_KV_HEADS)
            kt, vt = kb_ref[r, kv].astype(BF16), vb_ref[r, kv].astype(BF16)
            kt, vt = jnp.concatenate([kt, kt], axis=0), jnp.concatenate([vt, vt], axis=0)
            sink = jnp.where(top, sink_ref[2 * p], sink_ref[2 * p + 1]) * LOG2E
            yb_ref[rows, sl] = heads_of_block(qb_ref[rows, sl], kt, vt, sink).astype(yb_ref.dtype)


def _attn_ctx(sink, qa, qb, ka, va, kb, vb, *, seq):
    t = qa.shape[0]
    per_step = CTX_REQUESTS_PER_STEP
    tile = lambda w: pl.BlockSpec((per_step * seq, w), lambda b: (b, 0))
    cache = lambda heads: pl.BlockSpec((per_step, heads, HEAD_DIM, seq), lambda b: (b, 0, 0, 0))
    return pl.pallas_call(
        _attn_ctx_kernel,
        grid=(t // (per_step * seq),),
        in_specs=[pl.BlockSpec(memory_space=pltpu.SMEM), tile(NA_WIDTH), tile(SWA_Q_WIDTH),
                  cache(NA_HEADS), cache(NA_HEADS), cache(SWA_KV_HEADS), cache(SWA_KV_HEADS)],
        out_specs=[tile(NA_WIDTH), tile(SWA_Q_WIDTH)],
        out_shape=[jax.ShapeDtypeStruct((t, NA_WIDTH), BF16), jax.ShapeDtypeStruct((t, SWA_Q_WIDTH), BF16)],
        compiler_params=_params(),
        name="attn_ctx",
    )(sink, qa, qb, ka, va, kb, vb)


def _attn_na_kernel(q_ref, k_ref, v_ref, ck_ref, cv_ref, bias_ref, y_ref, ck_scr, cv_scr, *, n, ctx):
    rows = n // GRID_W
    tile_q = NA_TILE_ROWS * GRID_W
    zero = jnp.zeros((), BF16)
    ck_scr[...] = ck_ref[...].reshape(NA_WIDTH, ctx).astype(BF16)
    cv_scr[...] = cv_ref[...].reshape(NA_WIDTH, ctx).astype(BF16)
    lo, hi = _lane_halves((tile_q, LANES))
    left_half = lax.broadcasted_iota(jnp.int32, (GRID_W, LANES), 1) < GRID_W
    empty = jnp.zeros((GRID_W, LANES), BF16)

    def tile(q0, k0, key_rows, lead, first):
        pairs = key_rows // 2
        for p in range(NA_HEADS // 2):
            sl = slice(p * LANES, (p + 1) * LANES)
            q = q_ref[pl.ds(q0, tile_q), sl]
            q = jnp.concatenate([jnp.where(lo, q, zero), jnp.where(hi, q, zero)], axis=0)
            s_nb = _dot_nt(q, k_ref[pl.ds(k0, key_rows * GRID_W), sl])
            s_ctx = _dot(q, ck_scr[sl, :])
            e_nb, e_ctx, ls = [], [], []
            for idx in range(2):
                for a in range(NA_TILE_ROWS):
                    qa = slice(idx * tile_q + a * GRID_W, idx * tile_q + (a + 1) * GRID_W)
                    inside = lambda i: first[a] <= i < first[a] + NA_ROWS
                    cols, where = [], []
                    for m in range(pairs):
                        if not (inside(2 * m) or inside(2 * m + 1)):
                            continue
                        blk = s_nb[qa, m * LANES:(m + 1) * LANES] + bias_ref[2 * p + idx, 2 * m - a - lead + NA_ROWS]
                        if not inside(2 * m + 1):
                            blk = jnp.where(left_half, blk, NEG_INF)
                        elif not inside(2 * m):
                            blk = jnp.where(left_half, NEG_INF, blk)
                        cols.append(blk)
                        where.append(m)
                    cols += [s_ctx[qa, j * LANES:(j + 1) * LANES] for j in range(ctx // LANES)]
                    es, l = _softmax_cols(cols)
                    es = [e.astype(BF16) for e in es]
                    e_nb.append(jnp.concatenate(
                        [es[where.index(m)] if m in where else empty for m in range(pairs)], axis=1))
                    e_ctx.append(jnp.concatenate(es[len(where):], axis=1))
                    ls.append(l)
            o = (_dot(jnp.concatenate(e_nb, axis=0), v_ref[pl.ds(k0, key_rows * GRID_W), sl])
                 + _dot_nt(jnp.concatenate(e_ctx, axis=0), cv_scr[sl, :]))
            o = o * (1.0 / jnp.concatenate(ls, axis=0))
            y_ref[pl.ds(q0, tile_q), sl] = jnp.where(lo, o[:tile_q], o[tile_q:]).astype(y_ref.dtype)

    half = NA_ROWS // 2
    tile(0, 0, NA_ROWS, 0, (0,) * NA_TILE_ROWS)

    for t in range(1, rows // NA_TILE_ROWS - 1):
        tile(t * tile_q, t * tile_q - half * GRID_W, NA_ROWS + NA_TILE_ROWS, half, tuple(range(NA_TILE_ROWS)))
    tile(n - tile_q, n - NA_ROWS * GRID_W, NA_ROWS, half, (0,) * NA_TILE_ROWS)


def _attn_na(q, k, v, ck, cv, bias, *, n, ctx):
    t = q.shape[0]
    tile = pl.BlockSpec((n, NA_WIDTH), lambda b: (b, 0))
    ctile = pl.BlockSpec((None, NA_HEADS, HEAD_DIM, ctx), lambda b: (b, 0, 0, 0))
    return pl.pallas_call(
        functools.partial(_attn_na_kernel, n=n, ctx=ctx),
        grid=(t // n,),
        in_specs=[tile, tile, tile, ctile, ctile, _resident(bias.shape)],
        out_specs=tile,
        out_shape=jax.ShapeDtypeStruct((t, NA_WIDTH), BF16),
        scratch_shapes=[pltpu.VMEM((NA_WIDTH, ctx), BF16)] * 2,
        compiler_params=_params(),
        name="attn_na",
    )(q, k, v, ck, cv, bias)


def _na_bias_table(rel_bias):
    rows = _na_bias_rows(rel_bias)
    shape = (NA_HEADS, 2 * NA_ROWS, GRID_W, LANES)
    return pl.pallas_call(
        _fill_na_bias,
        grid=(1,),
        in_specs=[pl.BlockSpec(rows.shape, lambda i: (0, 0, 0))],
        out_specs=pl.BlockSpec(shape, lambda i: (0, 0, 0, 0)),
        out_shape=jax.ShapeDtypeStruct(shape, F32),
        compiler_params=_params(),
        name="na_bias",
    )(rows)


def _na_bias_rows(rel_bias):
    side = GRID_W - NA_COLS
    z = jnp.pad(rel_bias * LOG2E, ((0, 0), (0, 0), (side, side + 1)))
    return jnp.pad(z, ((0, 0), (1, 1), (0, 0)), constant_values=NEG_INF)


def _fill_na_bias(rows_ref, bias_scr):
    q = lax.broadcasted_iota(jnp.int32, (GRID_W, LANES), 0)
    lane = lax.broadcasted_iota(jnp.int32, (GRID_W, LANES), 1)
    kc = lane % GRID_W
    col_start = jnp.clip(q - NA_COLS // 2, 0, GRID_W - NA_COLS)
    in_window = (kc >= col_start) & (kc < col_start + NA_COLS)
    for h in range(NA_HEADS):
        blocks = []
        for j in range(2 * NA_ROWS + 1):
            row = jnp.broadcast_to(rows_ref[h, j:j + 1, :], (GRID_W, LANES))
            rolled = pltpu.roll(row, LANES - GRID_W + 1, 1, stride=1, stride_axis=0)
            blocks.append(jnp.where(in_window, rolled, NEG_INF))
        for j in range(2 * NA_ROWS):
            bias_scr[h, j] = jnp.where(lane < GRID_W, blocks[j], pltpu.roll(blocks[j + 1], GRID_W, 1))


def _attn_swa_kernel(sink_ref, q_ref, k_ref, v_ref, ck_ref, cv_ref, y_ref, k_scr, v_scr, ck_scr, cv_scr, *, n, ctx):
    group = SWA_HEADS // SWA_KV_HEADS
    band = SWA_QTILE + 2 * SWA_BLOCK
    for src, csrc, dst, cdst in ((k_ref, ck_ref, k_scr, ck_scr), (v_ref, cv_ref, v_scr, cv_scr)):
        x = src[...].astype(F32)
        xr = pltpu.roll(x, HEAD_DIM, 1)
        first_half, second_half = _lane_halves(x.shape)
        dst[0] = jnp.where(first_half, x, xr).astype(BF16)
        dst[1] = jnp.where(second_half, x, xr).astype(BF16)
        for kv in range(SWA_KV_HEADS):
            c = csrc[kv].astype(BF16)
            cdst[kv] = jnp.concatenate([c, c], axis=0)

    qi = lax.broadcasted_iota(jnp.int32, (SWA_QTILE, band), 0)
    kj = lax.broadcasted_iota(jnp.int32, (SWA_QTILE, band), 1)
    lo, hi = _lane_halves((SWA_QTILE, LANES))
    top = lax.broadcasted_iota(jnp.int32, (2 * SWA_QTILE, 1), 0) < SWA_QTILE
    zero = jnp.zeros((), BF16)

    def block_body(b, carry):
        q0 = pl.multiple_of(b * SWA_QTILE, SWA_QTILE)
        k0 = pl.multiple_of(jnp.clip(q0 - SWA_BLOCK, 0, n - band), SWA_BLOCK)
        mask = jnp.where(jnp.abs(kj - qi + (k0 - q0)) <= SWA_WINDOW, 0.0, NEG_INF)
        mask2 = jnp.concatenate([mask, mask], axis=0)
        for p in range(SWA_HEADS // 2):
            kv = 2 * p // group
            sl = slice(p * LANES, (p + 1) * LANES)
            q = q_ref[pl.ds(q0, SWA_QTILE), sl]
            q = jnp.concatenate([jnp.where(lo, q, zero), jnp.where(hi, q, zero)], axis=0)
            s_band = _dot_nt(q, k_scr[kv, pl.ds(k0, band), :])
            s_ctx = _dot(q, ck_scr[kv])
            sink = jnp.where(top, sink_ref[2 * p], sink_ref[2 * p + 1]) * LOG2E
            es, l = _softmax_cols(_cols(s_band + mask2) + _cols(s_ctx), sink)
            e_band = jnp.concatenate(es[:band // LANES], axis=1).astype(BF16)
            e_ctx = jnp.concatenate(es[band // LANES:], axis=1).astype(BF16)
            o = (_dot(e_band, v_scr[kv, pl.ds(k0, band), :]) + _dot_nt(e_ctx, cv_scr[kv])) * (1.0 / l)
            y_ref[pl.ds(q0, SWA_QTILE), sl] = jnp.where(lo, o[:SWA_QTILE], o[SWA_QTILE:]).astype(y_ref.dtype)
        return carry

    lax.fori_loop(0, n // SWA_QTILE, block_body, 0, unroll=4)


def _attn_swa(sink, q, k, v, ck, cv, *, n, ctx):
    t = q.shape[0]
    tile = lambda w: pl.BlockSpec((n, w), lambda b: (b, 0))
    ctile = pl.BlockSpec((None, SWA_KV_HEADS, HEAD_DIM, ctx), lambda b: (b, 0, 0, 0))
    return pl.pallas_call(
        functools.partial(_attn_swa_kernel, n=n, ctx=ctx),
        grid=(t // n,),
        in_specs=[pl.BlockSpec(memory_space=pltpu.SMEM), tile(SWA_Q_WIDTH), tile(SWA_KV_WIDTH),
                  tile(SWA_KV_WIDTH), ctile, ctile],
        out_specs=tile(SWA_Q_WIDTH),
        out_shape=jax.ShapeDtypeStruct((t, SWA_Q_WIDTH), BF16),
        scratch_shapes=[pltpu.VMEM((SWA_KV_HEADS, n, SWA_KV_WIDTH), BF16)] * 2
        + [pltpu.VMEM((SWA_KV_HEADS, LANES, ctx), BF16)] * 2,
        compiler_params=_params(),
        name="attn_swa",
    )(sink, q, k, v, ck, cv)


def _rope_tables(n):
    half = HEAD_DIM // 4
    freqs = np.power(ROPE_BASE, -np.arange(half, dtype=np.float64) / half)
    t = np.arange(n)
    cos, sin = [], []
    for pos in (t // GRID_W, t % GRID_W):
        ang = pos.astype(np.float64)[:, None] * freqs[None, :]
        cos += [np.cos(ang), np.cos(ang)]
        sin += [-np.sin(ang), np.sin(ang)]
    cos, sin = np.concatenate(cos, axis=-1), np.concatenate(sin, axis=-1)
    reps = LANES // HEAD_DIM
    return jnp.asarray(np.tile(cos, (1, reps)), F32), jnp.asarray(np.tile(sin, (1, reps)), F32)


def _merge_kernel(xc_ref, xl_ref, yac_ref, yal_ref, ybc_ref, ybl_ref, mod_ref, g_ref, win_ref, wba_ref, wbb_ref,
                  wout_ref, o_ref, *, ctx_steps):
    is_ctx = pl.program_id(0) < ctx_steps
    pick = lambda ctx_ref, lat_ref: jnp.where(is_ctx, ctx_ref[...], lat_ref[...])
    x, ya, yb = pick(xc_ref, xl_ref), pick(yac_ref, yal_ref), pick(ybc_ref, ybl_ref)
    mod = mod_ref[0]
    h = _modulate(x, g_ref[...], mod[3:4], mod[4:5]).astype(BF16)
    a = jax.nn.sigmoid(_dot(h, win_ref[:, QKV_COLS:QKV_COLS + D_MODEL])) * _dot(ya, wba_ref[...])
    b = jax.nn.sigmoid(_dot(h, win_ref[:, QKV_COLS + D_MODEL:])) * _dot(yb, wbb_ref[...])
    o_ref[...] = x + mod[5:6] * _dot((a + b).astype(BF16), wout_ref[...])


def _merge(x, ya, yb, mod, g, w_in, wba, wbb, wout, *, tokens_per_request):
    ctx_steps = x[0].shape[0] // MERGE_TILE
    t = x[0].shape[0] + x[1].shape[0]
    both = lambda w: _ctx_then_lat(w, ctx_steps, MERGE_TILE)
    return pl.pallas_call(
        functools.partial(_merge_kernel, ctx_steps=ctx_steps),
        grid=(t // MERGE_TILE,),
        in_specs=both(D_MODEL) + both(NA_WIDTH) + both(SWA_Q_WIDTH)
        + [_mod_spec_both(ctx_steps, tokens_per_request, MERGE_TILE), _resident((1, D_MODEL)),
           _resident(w_in.shape), _resident(wba.shape), _resident(wbb.shape), _resident(wout.shape)],
        out_specs=pl.BlockSpec((MERGE_TILE, D_MODEL), lambda i: (i, 0)),
        out_shape=jax.ShapeDtypeStruct((t, D_MODEL), F32),
        compiler_params=_params(),
        name="merge",
    )(*x, *ya, *yb, mod, g, w_in, wba, wbb, wout)


def kernel(x_prompt, x_sample, cache_na_k, cache_na_v, cache_swa_k, cache_swa_v, c, c_ctx, w_ada, b_ada,
           norm_ffn1, ffn1_w_gate, ffn1_w_up, ffn1_w_down, norm_mix, w_in, na_rel_bias, swa_sink,
           w_branch_na, w_branch_swa, w_out, norm_ffn2, ffn2_w_gate, ffn2_w_up, ffn2_w_down, norm_final):
    depth = w_ada.shape[0]
    assert depth == 1
    batch, seq, _ = x_prompt.shape
    dec_batch, dec_seq, _ = x_sample.shape
    past = cache_na_k.shape[2]
    layer = 0
    row = lambda v: v.reshape(1, D_MODEL)
    bf = lambda w: w.astype(BF16)

    cond = jnp.zeros((MOD_ROWS, D_MODEL), F32).at[0].set(c_ctx).at[1:1 + dec_batch].set(c)
    mod = _adaln(cond, w_ada[layer], b_ada[layer]).reshape(MOD_ROWS, N_MOD, D_MODEL)

    ffn1 = (row(norm_ffn1[layer]), bf(ffn1_w_gate[layer]), bf(ffn1_w_up[layer]), bf(ffn1_w_down[layer]),
            row(norm_final))
    g_mix = row(norm_mix[layer])
    sink = swa_sink[layer]
    ctx_rows = dict(tokens_per_row=batch * seq, first_row=0)
    lat_rows = dict(tokens_per_row=dec_seq, first_row=1)
    ctx_tokens = batch * seq

    later = (ffn2_w_gate[layer], ffn2_w_up[layer], ffn2_w_down[layer], w_in[layer],
             w_branch_na[layer], w_branch_swa[layer], w_out[layer])
    x_lat, *later = _ffn(x_sample.reshape(dec_batch * dec_seq, D_MODEL), mod, *ffn1,
                         _mod_spec(tile=FFN_TILE, **lat_rows), first=0, final=False, cast=later)
    ffn2 = (row(norm_ffn2[layer]), *later[:3], row(norm_final))
    w_qkv = later[3]
    merge_w = tuple(later[3:])

    x_ctx = _ffn(x_prompt.reshape(ctx_tokens, D_MODEL), mod, *ffn1, _mod_spec(tile=FFN_TILE, **ctx_rows),
                 first=0, final=False)
    qa, qb, *new_cache = _proj_ctx(x_ctx, mod, g_mix, w_qkv, seq=seq, **ctx_rows)
    ya_ctx, yb_ctx = _attn_ctx(sink, qa, qb, *new_cache, seq=seq)

    qa, kal, val, qb, kbl, vbl = _proj_lat(x_lat, mod, g_mix, w_qkv, _rope_tables(dec_seq), **lat_rows)
    transposed = lambda cache: jnp.transpose(cache[:, layer], (0, 2, 3, 1))
    ya_lat = _attn_na(qa, kal, val, transposed(cache_na_k), transposed(cache_na_v),
                      _na_bias_table(na_rel_bias[layer]), n=dec_seq, ctx=past)
    yb_lat = _attn_swa(sink, qb, kbl, vbl, transposed(cache_swa_k), transposed(cache_swa_v), n=dec_seq, ctx=past)

    x = _merge((x_ctx, x_lat), (ya_ctx, ya_lat), (yb_ctx, yb_lat), mod, g_mix, *merge_w, tokens_per_request=dec_seq)
    y_prompt, y_sample = _ffn(x, mod, *ffn2, _mod_spec_both(ctx_tokens // FFN_TILE, dec_seq, FFN_TILE),
                              first=6, final=True, ctx_tokens=ctx_tokens)
    y_prompt = y_prompt.reshape(batch, seq, D_MODEL)
    y_sample = y_sample.reshape(dec_batch, dec_seq, D_MODEL)

    new_cache = [jnp.transpose(t, (0, 3, 1, 2))[:, None] for t in new_cache]
    return (y_prompt, y_sample, *new_cache)
```

```python
import functools

import numpy as np
import jax
import jax.numpy as jnp
from jax import lax
from jax.experimental import pallas as pl
from jax.experimental.pallas import tpu as pltpu

F32 = jnp.float32
BF16 = jnp.bfloat16

D_MODEL = 1024
FFN_DIM = 2816
HEAD_DIM = 64
N_MOD = 9
GRID_W = 64
NA_HEADS = 8
NA_ROWS = 8
NA_COLS = 16
SWA_HEADS = 8
SWA_KV_HEADS = 2
SWA_WINDOW = 128
SWA_BLOCK = 128
ROPE_BASE = 10000.0
EPS = 1e-6
NEG_INF = -1e30
NA_WIDTH = NA_HEADS * HEAD_DIM
SWA_Q_WIDTH = SWA_HEADS * HEAD_DIM
SWA_KV_WIDTH = SWA_KV_HEADS * HEAD_DIM
QKV_COLS = 3 * NA_WIDTH + SWA_Q_WIDTH + 2 * SWA_KV_WIDTH
LANES = 128
MOD_ROWS = 16
VMEM_LIMIT = 56 * 1024 * 1024
TOKEN_TILE = 1024
MERGE_TILE = 512
FFN_TILE = 1024
FFN_CHUNK = 256
ADALN_ROWS = 128
CTX_REQUESTS_PER_STEP = 4
SWA_QTILE = 256
NA_TILE_ROWS = 4
LOG2E = 1.4426950408889634
Q_SCALE = HEAD_DIM ** -0.5 * LOG2E


def _dot(a, b):
    return jnp.dot(a, b, preferred_element_type=F32)


def _dot_nt(a, b):
    return lax.dot_general(a, b, (((1,), (1,)), ((), ())), preferred_element_type=F32)


def _silu(x):
    return x * jax.nn.sigmoid(x)


def _rms(x, g):
    return x * lax.rsqrt(jnp.mean(x * x, axis=-1, keepdims=True) + EPS) * g


def _modulate(x, g, shift, scale):
    return _rms(x, g) * (1.0 + scale) + shift


def _resident(shape):
    nd = len(shape)
    return pl.BlockSpec(shape, lambda *_: (0,) * nd, pipeline_mode=pl.Buffered(1))


def _params():
    return pltpu.CompilerParams(dimension_semantics=("arbitrary",), vmem_limit_bytes=VMEM_LIMIT)


def _adaln_kernel(c_ref, w_ref, b_ref, o_ref):
    @pl.when(pl.program_id(0) == 0)
    def _():
        o_ref[...] = jnp.broadcast_to(b_ref[...], o_ref.shape)

    s = _silu(c_ref[0]).astype(BF16)
    o_ref[...] += _dot(s, w_ref[...].astype(BF16))


def _adaln(cond, w_ada, b_ada):
    k, n = w_ada.shape
    steps = k // ADALN_ROWS
    cond = cond.reshape(MOD_ROWS, steps, ADALN_ROWS).transpose(1, 0, 2)
    return pl.pallas_call(
        _adaln_kernel,
        grid=(steps,),
        in_specs=[pl.BlockSpec((1, MOD_ROWS, ADALN_ROWS), lambda j: (j, 0, 0)),
                  pl.BlockSpec((ADALN_ROWS, n), lambda j: (j, 0)),
                  pl.BlockSpec((1, n), lambda j: (0, 0))],
        out_specs=pl.BlockSpec((MOD_ROWS, n), lambda j: (0, 0)),
        out_shape=jax.ShapeDtypeStruct((MOD_ROWS, n), F32),
        compiler_params=_params(),
        name="adaln",
    )(cond, w_ada, b_ada.reshape(1, n))


def _mod_spec(tokens_per_row, first_row, tile=TOKEN_TILE):
    tiles_per_row = tokens_per_row // tile
    return pl.BlockSpec((1, N_MOD, D_MODEL), lambda i: (first_row + i // tiles_per_row, 0, 0))


def _mod_spec_both(ctx_steps, tokens_per_request, tile):
    tiles_per_request = tokens_per_request // tile
    row = lambda i: jnp.where(i < ctx_steps, 0, 1 + (i - ctx_steps) // tiles_per_request)
    return pl.BlockSpec((1, N_MOD, D_MODEL), lambda i: (row(i), 0, 0))


def _ctx_then_lat(width, ctx_steps, tile):
    ctx = pl.BlockSpec((tile, width), lambda i: (jnp.minimum(i, ctx_steps - 1), 0))
    lat = pl.BlockSpec((tile, width), lambda i: (jnp.maximum(i - ctx_steps, 0), 0))
    return [ctx, lat]


def _ffn_kernel(x_ref, mod_ref, g_ref, wg_ref, wu_ref, wd_ref, gf_ref, *rest, first, final, n_cast, ctx_steps):
    n_out = 1 if ctx_steps is None else 2
    outs, a_scr = rest[n_cast:n_cast + n_out], rest[-1]
    for src, dst in zip(rest[:n_cast], rest[n_cast + n_out:-1]):
        dst[...] = src[...].astype(BF16)
    x = x_ref[...]
    mod = mod_ref[0]
    shift, scale, gate = mod[first:first + 1], mod[first + 1:first + 2], mod[first + 2:first + 3]
    h = _modulate(x, g_ref[...], shift, scale).astype(BF16)
    for c in range(FFN_DIM // FFN_CHUNK):
        sl = slice(c * FFN_CHUNK, (c + 1) * FFN_CHUNK)
        a_scr[:, sl] = (_silu(_dot(h, wg_ref[:, sl])) * _dot(h, wu_ref[:, sl])).astype(BF16)
    y = x + (0.5 * gate) * _dot(a_scr[...], wd_ref[...])
    if final:
        y = _rms(y, gf_ref[...])
    if ctx_steps is None:
        outs[0][...] = y
    else:
        is_ctx = pl.program_id(0) < ctx_steps

        @pl.when(is_ctx)
        def _():
            outs[0][...] = y

        @pl.when(jnp.logical_not(is_ctx))
        def _():
            outs[1][...] = y


def _ffn(x, mod, g, wg, wu, wd, gf, mod_spec, *, first, final, cast=(), ctx_tokens=None):
    t = x.shape[0]
    steps = t // FFN_TILE
    tile = pl.BlockSpec((FFN_TILE, D_MODEL), lambda i: (i, 0))
    chunk = lambda w: pl.BlockSpec((w.shape[0] // steps, w.shape[1]), lambda i: (i, 0))
    assert all(w.shape[0] % (steps * 16) == 0 for w in cast)
    if ctx_tokens is None:
        ctx_steps, out_specs, out_shape = None, [tile], [jax.ShapeDtypeStruct((t, D_MODEL), F32)]
    else:
        ctx_steps = ctx_tokens // FFN_TILE
        out_specs = _ctx_then_lat(D_MODEL, ctx_steps, FFN_TILE)
        out_shape = [jax.ShapeDtypeStruct((n, D_MODEL), F32) for n in (ctx_tokens, t - ctx_tokens)]
    out = pl.pallas_call(
        functools.partial(_ffn_kernel, first=first, final=final, n_cast=len(cast), ctx_steps=ctx_steps),
        grid=(steps,),
        in_specs=[tile, mod_spec, _resident((1, D_MODEL)),
                  _resident(wg.shape), _resident(wu.shape), _resident(wd.shape), _resident((1, D_MODEL))]
        + [chunk(w) for w in cast],
        out_specs=out_specs + [chunk(w) for w in cast],
        out_shape=out_shape + [jax.ShapeDtypeStruct(w.shape, BF16) for w in cast],
        scratch_shapes=[pltpu.VMEM((FFN_TILE, FFN_DIM), BF16)],
        compiler_params=_params(),
        name="ffn",
    )(x, mod, g, wg, wu, wd, gf, *cast)
    return out if len(out) > 1 else out[0]


def _swap16(x):
    lane = lax.broadcasted_iota(jnp.int32, x.shape, 1)
    return jnp.where(lane % 32 < 16, pltpu.roll(x, LANES - 16, 1), pltpu.roll(x, 16, 1))


def _proj_ctx_kernel(x_ref, mod_ref, g_ref, w_ref, qa_ref, qb_ref, ka_ref, va_ref, kb_ref, vb_ref, wkvt_ref):
    @pl.when(pl.program_id(0) == 0)
    def _():
        kv_cols = ((NA_WIDTH, 3 * NA_WIDTH), (3 * NA_WIDTH + SWA_Q_WIDTH, QKV_COLS))
        o = 0
        for lo_col, hi_col in kv_cols:
            wkvt_ref[o:o + hi_col - lo_col, :] = w_ref[:, lo_col:hi_col].astype(F32).T.astype(BF16)
            o += hi_col - lo_col

    mod = mod_ref[0]
    h = _modulate(x_ref[...], g_ref[...], mod[3:4], mod[4:5]).astype(BF16)
    qa_ref[...] = (_dot(h, w_ref[:, :NA_WIDTH]) * Q_SCALE).astype(qa_ref.dtype)
    qb_ref[...] = (_dot(h, w_ref[:, 3 * NA_WIDTH:3 * NA_WIDTH + SWA_Q_WIDTH]) * Q_SCALE).astype(qb_ref.dtype)
    kvt = _dot_nt(wkvt_ref[...], h)
    requests, _, _, seq = ka_ref.shape
    o = 0
    for ref in (ka_ref, va_ref, kb_ref, vb_ref):
        width = ref.shape[1] * HEAD_DIM
        for r in range(requests):
            ref[r] = kvt[o:o + width, r * seq:(r + 1) * seq].reshape(ref.shape[1:])
        o += width


def _rope(y, cos, sin):
    blocks = [y[:, j * LANES:(j + 1) * LANES] for j in range(y.shape[1] // LANES)]
    return [b * cos + _swap16(b) * sin for b in blocks]


def _proj_lat_kernel(x_ref, mod_ref, g_ref, w_ref, cos_ref, sin_ref,
                     qa_ref, ka_ref, va_ref, qb_ref, kb_ref, vb_ref):
    mod = mod_ref[0]
    h = _modulate(x_ref[...], g_ref[...], mod[3:4], mod[4:5]).astype(BF16)
    cos, sin = cos_ref[...], sin_ref[...]
    o = 3 * NA_WIDTH
    for j, b in enumerate(_rope(_dot(h, w_ref[:, o:o + SWA_Q_WIDTH]), cos, sin)):
        qb_ref[:, j * LANES:(j + 1) * LANES] = (b * Q_SCALE).astype(qb_ref.dtype)
    o += SWA_Q_WIDTH
    y = _dot(h, w_ref[:, o:o + 2 * SWA_KV_WIDTH])
    kb_ref[...] = _rope(y[:, :SWA_KV_WIDTH], cos, sin)[0].astype(kb_ref.dtype)
    vb_ref[...] = y[:, SWA_KV_WIDTH:].astype(vb_ref.dtype)
    o = 0
    for ref, scale in ((qa_ref, True), (ka_ref, False), (va_ref, False)):
        y = _dot(h, w_ref[:, o:o + NA_WIDTH])
        ref[...] = (y * Q_SCALE if scale else y).astype(ref.dtype)
        o += NA_WIDTH


def _proj_ctx(x, mod, g, w_qkv, *, seq, tokens_per_row, first_row):
    t = x.shape[0]
    tile = lambda w: pl.BlockSpec((TOKEN_TILE, w), lambda i: (i, 0))
    cache = lambda heads: pl.BlockSpec((TOKEN_TILE // seq, heads, HEAD_DIM, seq), lambda i: (i, 0, 0, 0))
    cache_shape = lambda heads: jax.ShapeDtypeStruct((t // seq, heads, HEAD_DIM, seq), F32)
    kv_heads = (NA_HEADS, NA_HEADS, SWA_KV_HEADS, SWA_KV_HEADS)
    return pl.pallas_call(
        _proj_ctx_kernel,
        grid=(t // TOKEN_TILE,),
        in_specs=[tile(D_MODEL), _mod_spec(tokens_per_row, first_row), _resident((1, D_MODEL)),
                  _resident(w_qkv.shape)],
        out_specs=[tile(NA_WIDTH), tile(SWA_Q_WIDTH)] + [cache(nh) for nh in kv_heads],
        out_shape=[jax.ShapeDtypeStruct((t, NA_WIDTH), BF16), jax.ShapeDtypeStruct((t, SWA_Q_WIDTH), BF16)]
        + [cache_shape(nh) for nh in kv_heads],
        scratch_shapes=[pltpu.VMEM((2 * NA_WIDTH + 2 * SWA_KV_WIDTH, D_MODEL), BF16)],
        compiler_params=_params(),
        name="proj_ctx",
    )(x, mod, g, w_qkv)


def _proj_lat(x, mod, g, w_qkv, rope, *, tokens_per_row, first_row):
    t = x.shape[0]
    tile = lambda w: pl.BlockSpec((TOKEN_TILE, w), lambda i: (i, 0))
    widths = (NA_WIDTH, NA_WIDTH, NA_WIDTH, SWA_Q_WIDTH, SWA_KV_WIDTH, SWA_KV_WIDTH)
    tiles_per_row = tokens_per_row // TOKEN_TILE
    return pl.pallas_call(
        _proj_lat_kernel,
        grid=(t // TOKEN_TILE,),
        in_specs=[tile(D_MODEL), _mod_spec(tokens_per_row, first_row), _resident((1, D_MODEL)),
                  _resident(w_qkv.shape)]
        + [pl.BlockSpec((TOKEN_TILE, LANES), lambda i: (i % tiles_per_row, 0))] * 2,
        out_specs=[tile(w) for w in widths],
        out_shape=[jax.ShapeDtypeStruct((t, w), BF16) for w in widths],
        compiler_params=_params(),
        name="proj_lat",
    )(x, mod, g, w_qkv, *rope)


def _lane_halves(shape):
    lane = lax.broadcasted_iota(jnp.int32, shape, 1) % LANES
    return lane < HEAD_DIM, lane >= HEAD_DIM


def _softmax_cols(cols, sink=None):
    m = jnp.max(functools.reduce(jnp.maximum, cols), axis=-1, keepdims=True)
    if sink is not None:
        m = jnp.maximum(m, sink)
    es = [jnp.exp2(c - m) for c in cols]
    l = jnp.sum(functools.reduce(jnp.add, es), axis=-1, keepdims=True)
    if sink is not None:
        l = l + jnp.exp2(sink - m)
    return es, l


def _cols(x):
    return [x[:, j * LANES:(j + 1) * LANES] for j in range(x.shape[1] // LANES)]


def _attn_ctx_kernel(sink_ref, qa_ref, qb_ref, ka_ref, va_ref, kb_ref, vb_ref, ya_ref, yb_ref):
    requests, _, _, seq = ka_ref.shape
    lo, hi = _lane_halves((seq, LANES))
    top = lax.broadcasted_iota(jnp.int32, (2 * seq, 1), 0) < seq
    zero = jnp.zeros((), BF16)

    def heads_of_block(q, kt, vt, sink):
        q = jnp.concatenate([jnp.where(lo, q, zero), jnp.where(hi, q, zero)], axis=0)
        es, l = _softmax_cols(_cols(_dot(q, kt)), sink)
        o = _dot_nt(jnp.concatenate(es, axis=1).astype(BF16), vt) * (1.0 / l)
        return jnp.where(lo, o[:seq], o[seq:])

    for r in range(requests):
        rows = slice(r * seq, (r + 1) * seq)
        for p in range(NA_HEADS // 2):
            sl = slice(p * LANES, (p + 1) * LANES)
            kt = ka_ref[r, 2 * p:2 * p + 2].reshape(LANES, seq).astype(BF16)
            vt = va_ref[r, 2 * p:2 * p + 2].reshape(LANES, seq).astype(BF16)
            ya_ref[rows, sl] = heads_of_block(qa_ref[rows, sl], kt, vt, None).astype(ya_ref.dtype)
        for p in range(SWA_HEADS // 2):
            sl = slice(p * LANES, (p + 1) * LANES)
            kv = (2 * p) // (SWA_HEADS // SWA_KV_HEADS)
            kt, vt = kb_ref[r, kv].astype(BF16), vb_ref[r, kv].astype(BF16)
            kt, vt = jnp.concatenate([kt, kt], axis=0), jnp.concatenate([vt, vt], axis=0)
            sink = jnp.where(top, sink_ref[2 * p], sink_ref[2 * p + 1]) * LOG2E
            yb_ref[rows, sl] = heads_of_block(qb_ref[rows, sl], kt, vt, sink).astype(yb_ref.dtype)


def _attn_ctx(sink, qa, qb, ka, va, kb, vb, *, seq):
    t = qa.shape[0]
    per_step = CTX_REQUESTS_PER_STEP
    tile = lambda w: pl.BlockSpec((per_step * seq, w), lambda b: (b, 0))
    cache = lambda heads: pl.BlockSpec((per_step, heads, HEAD_DIM, seq), lambda b: (b, 0, 0, 0))
    return pl.pallas_call(
        _attn_ctx_kernel,
        grid=(t // (per_step * seq),),
        in_specs=[pl.BlockSpec(memory_space=pltpu.SMEM), tile(NA_WIDTH), tile(SWA_Q_WIDTH),
                  cache(NA_HEADS), cache(NA_HEADS), cache(SWA_KV_HEADS), cache(SWA_KV_HEADS)],
        out_specs=[tile(NA_WIDTH), tile(SWA_Q_WIDTH)],
        out_shape=[jax.ShapeDtypeStruct((t, NA_WIDTH), BF16), jax.ShapeDtypeStruct((t, SWA_Q_WIDTH), BF16)],
        compiler_params=_params(),
        name="attn_ctx",
    )(sink, qa, qb, ka, va, kb, vb)


def _attn_na_kernel(q_ref, k_ref, v_ref, ck_ref, cv_ref, bias_ref, y_ref, ck_scr, cv_scr, *, n, ctx):
    rows = n // GRID_W
    tile_q = NA_TILE_ROWS * GRID_W
    zero = jnp.zeros((), BF16)
    ck_scr[...] = ck_ref[...].reshape(NA_WIDTH, ctx).astype(BF16)
    cv_scr[...] = cv_ref[...].reshape(NA_WIDTH, ctx).astype(BF16)
    lo, hi = _lane_halves((tile_q, LANES))
    left_half = lax.broadcasted_iota(jnp.int32, (GRID_W, LANES), 1) < GRID_W
    empty = jnp.zeros((GRID_W, LANES), BF16)

    def tile(q0, k0, key_rows, lead, first):
        pairs = key_rows // 2
        for p in range(NA_HEADS // 2):
            sl = slice(p * LANES, (p + 1) * LANES)
            q = q_ref[pl.ds(q0, tile_q), sl]
            q = jnp.concatenate([jnp.where(lo, q, zero), jnp.where(hi, q, zero)], axis=0)
            s_nb = _dot_nt(q, k_ref[pl.ds(k0, key_rows * GRID_W), sl])
            s_ctx = _dot(q, ck_scr[sl, :])
            e_nb, e_ctx, ls = [], [], []
            for idx in range(2):
                for a in range(NA_TILE_ROWS):
                    qa = slice(idx * tile_q + a * GRID_W, idx * tile_q + (a + 1) * GRID_W)
                    inside = lambda i: first[a] <= i < first[a] + NA_ROWS
                    cols, where = [], []
                    for m in range(pairs):
                        if not (inside(2 * m) or inside(2 * m + 1)):
                            continue
                        blk = s_nb[qa, m * LANES:(m + 1) * LANES] + bias_ref[2 * p + idx, 2 * m - a - lead + NA_ROWS]
                        if not inside(2 * m + 1):
                            blk = jnp.where(left_half, blk, NEG_INF)
                        elif not inside(2 * m):
                            blk = jnp.where(left_half, NEG_INF, blk)
                        cols.append(blk)
                        where.append(m)
                    cols += [s_ctx[qa, j * LANES:(j + 1) * LANES] for j in range(ctx // LANES)]
                    es, l = _softmax_cols(cols)
                    es = [e.astype(BF16) for e in es]
                    e_nb.append(jnp.concatenate(
                        [es[where.index(m)] if m in where else empty for m in range(pairs)], axis=1))
                    e_ctx.append(jnp.concatenate(es[len(where):], axis=1))
                    ls.append(l)
            o = (_dot(jnp.concatenate(e_nb, axis=0), v_ref[pl.ds(k0, key_rows * GRID_W), sl])
                 + _dot_nt(jnp.concatenate(e_ctx, axis=0), cv_scr[sl, :]))
            o = o * (1.0 / jnp.concatenate(ls, axis=0))
            y_ref[pl.ds(q0, tile_q), sl] = jnp.where(lo, o[:tile_q], o[tile_q:]).astype(y_ref.dtype)

    half = NA_ROWS // 2
    tile(0, 0, NA_ROWS, 0, (0,) * NA_TILE_ROWS)

    for t in range(1, rows // NA_TILE_ROWS - 1):
        tile(t * tile_q, t * tile_q - half * GRID_W, NA_ROWS + NA_TILE_ROWS, half, tuple(range(NA_TILE_ROWS)))
    tile(n - tile_q, n - NA_ROWS * GRID_W, NA_ROWS, half, (0,) * NA_TILE_ROWS)


def _attn_na(q, k, v, ck, cv, bias, *, n, ctx):
    t = q.shape[0]
    tile = pl.BlockSpec((n, NA_WIDTH), lambda b: (b, 0))
    ctile = pl.BlockSpec((None, NA_HEADS, HEAD_DIM, ctx), lambda b: (b, 0, 0, 0))
    return pl.pallas_call(
        functools.partial(_attn_na_kernel, n=n, ctx=ctx),
        grid=(t // n,),
        in_specs=[tile, tile, tile, ctile, ctile, _resident(bias.shape)],
        out_specs=tile,
        out_shape=jax.ShapeDtypeStruct((t, NA_WIDTH), BF16),
        scratch_shapes=[pltpu.VMEM((NA_WIDTH, ctx), BF16)] * 2,
        compiler_params=_params(),
        name="attn_na",
    )(q, k, v, ck, cv, bias)


def _na_bias_table(rel_bias):
    rows = _na_bias_rows(rel_bias)
    shape = (NA_HEADS, 2 * NA_ROWS, GRID_W, LANES)
    return pl.pallas_call(
        _fill_na_bias,
        grid=(1,),
        in_specs=[pl.BlockSpec(rows.shape, lambda i: (0, 0, 0))],
        out_specs=pl.BlockSpec(shape, lambda i: (0, 0, 0, 0)),
        out_shape=jax.ShapeDtypeStruct(shape, F32),
        compiler_params=_params(),
        name="na_bias",
    )(rows)


def _na_bias_rows(rel_bias):
    side = GRID_W - NA_COLS
    z = jnp.pad(rel_bias * LOG2E, ((0, 0), (0, 0), (side, side + 1)))
    return jnp.pad(z, ((0, 0), (1, 1), (0, 0)), constant_values=NEG_INF)


def _fill_na_bias(rows_ref, bias_scr):
    q = lax.broadcasted_iota(jnp.int32, (GRID_W, LANES), 0)
    lane = lax.broadcasted_iota(jnp.int32, (GRID_W, LANES), 1)
    kc = lane % GRID_W
    col_start = jnp.clip(q - NA_COLS // 2, 0, GRID_W - NA_COLS)
    in_window = (kc >= col_start) & (kc < col_start + NA_COLS)
    for h in range(NA_HEADS):
        blocks = []
        for j in range(2 * NA_ROWS + 1):
            row = jnp.broadcast_to(rows_ref[h, j:j + 1, :], (GRID_W, LANES))
            rolled = pltpu.roll(row, LANES - GRID_W + 1, 1, stride=1, stride_axis=0)
            blocks.append(jnp.where(in_window, rolled, NEG_INF))
        for j in range(2 * NA_ROWS):
            bias_scr[h, j] = jnp.where(lane < GRID_W, blocks[j], pltpu.roll(blocks[j + 1], GRID_W, 1))


def _attn_swa_kernel(sink_ref, q_ref, k_ref, v_ref, ck_ref, cv_ref, y_ref, k_scr, v_scr, ck_scr, cv_scr, *, n, ctx):
    group = SWA_HEADS // SWA_KV_HEADS
    band = SWA_QTILE + 2 * SWA_BLOCK
    for src, csrc, dst, cdst in ((k_ref, ck_ref, k_scr, ck_scr), (v_ref, cv_ref, v_scr, cv_scr)):
        x = src[...].astype(F32)
        xr = pltpu.roll(x, HEAD_DIM, 1)
        first_half, second_half = _lane_halves(x.shape)
        dst[0] = jnp.where(first_half, x, xr).astype(BF16)
        dst[1] = jnp.where(second_half, x, xr).astype(BF16)
        for kv in range(SWA_KV_HEADS):
            c = csrc[kv].astype(BF16)
            cdst[kv] = jnp.concatenate([c, c], axis=0)

    qi = lax.broadcasted_iota(jnp.int32, (SWA_QTILE, band), 0)
    kj = lax.broadcasted_iota(jnp.int32, (SWA_QTILE, band), 1)
    lo, hi = _lane_halves((SWA_QTILE, LANES))
    top = lax.broadcasted_iota(jnp.int32, (2 * SWA_QTILE, 1), 0) < SWA_QTILE
    zero = jnp.zeros((), BF16)

    def block_body(b, carry):
        q0 = pl.multiple_of(b * SWA_QTILE, SWA_QTILE)
        k0 = pl.multiple_of(jnp.clip(q0 - SWA_BLOCK, 0, n - band), SWA_BLOCK)
        mask = jnp.where(jnp.abs(kj - qi + (k0 - q0)) <= SWA_WINDOW, 0.0, NEG_INF)
        mask2 = jnp.concatenate([mask, mask], axis=0)
        for p in range(SWA_HEADS // 2):
            kv = 2 * p // group
            sl = slice(p * LANES, (p + 1) * LANES)
            q = q_ref[pl.ds(q0, SWA_QTILE), sl]
            q = jnp.concatenate([jnp.where(lo, q, zero), jnp.where(hi, q, zero)], axis=0)
            s_band = _dot_nt(q, k_scr[kv, pl.ds(k0, band), :])
            s_ctx = _dot(q, ck_scr[kv])
            sink = jnp.where(top, sink_ref[2 * p], sink_ref[2 * p + 1]) * LOG2E
            es, l = _softmax_cols(_cols(s_band + mask2) + _cols(s_ctx), sink)
            e_band = jnp.concatenate(es[:band // LANES], axis=1).astype(BF16)
            e_ctx = jnp.concatenate(es[band // LANES:], axis=1).astype(BF16)
            o = (_dot(e_band, v_scr[kv, pl.ds(k0, band), :]) + _dot_nt(e_ctx, cv_scr[kv])) * (1.0 / l)
            y_ref[pl.ds(q0, SWA_QTILE), sl] = jnp.where(lo, o[:SWA_QTILE], o[SWA_QTILE:]).astype(y_ref.dtype)
        return carry

    lax.fori_loop(0, n // SWA_QTILE, block_body, 0, unroll=4)


def _attn_swa(sink, q, k, v, ck, cv, *, n, ctx):
    t = q.shape[0]
    tile = lambda w: pl.BlockSpec((n, w), lambda b: (b, 0))
    ctile = pl.BlockSpec((None, SWA_KV_HEADS, HEAD_DIM, ctx), lambda b: (b, 0, 0, 0))
    return pl.pallas_call(
        functools.partial(_attn_swa_kernel, n=n, ctx=ctx),
        grid=(t // n,),
        in_specs=[pl.BlockSpec(memory_space=pltpu.SMEM), tile(SWA_Q_WIDTH), tile(SWA_KV_WIDTH),
                  tile(SWA_KV_WIDTH), ctile, ctile],
        out_specs=tile(SWA_Q_WIDTH),
        out_shape=jax.ShapeDtypeStruct((t, SWA_Q_WIDTH), BF16),
        scratch_shapes=[pltpu.VMEM((SWA_KV_HEADS, n, SWA_KV_WIDTH), BF16)] * 2
        + [pltpu.VMEM((SWA_KV_HEADS, LANES, ctx), BF16)] * 2,
        compiler_params=_params(),
        name="attn_swa",
    )(sink, q, k, v, ck, cv)


def _rope_tables(n):
    half = HEAD_DIM // 4
    freqs = np.power(ROPE_BASE, -np.arange(half, dtype=np.float64) / half)
    t = np.arange(n)
    cos, sin = [], []
    for pos in (t // GRID_W, t % GRID_W):
        ang = pos.astype(np.float64)[:, None] * freqs[None, :]
        cos += [np.cos(ang), np.cos(ang)]
        sin += [-np.sin(ang), np.sin(ang)]
    cos, sin = np.concatenate(cos, axis=-1), np.concatenate(sin, axis=-1)
    reps = LANES // HEAD_DIM
    return jnp.asarray(np.tile(cos, (1, reps)), F32), jnp.asarray(np.tile(sin, (1, reps)), F32)


def _merge_kernel(xc_ref, xl_ref, yac_ref, yal_ref, ybc_ref, ybl_ref, mod_ref, g_ref, win_ref, wba_ref, wbb_ref,
                  wout_ref, o_ref, *, ctx_steps):
    is_ctx = pl.program_id(0) < ctx_steps
    pick = lambda ctx_ref, lat_ref: jnp.where(is_ctx, ctx_ref[...], lat_ref[...])
    x, ya, yb = pick(xc_ref, xl_ref), pick(yac_ref, yal_ref), pick(ybc_ref, ybl_ref)
    mod = mod_ref[0]
    h = _modulate(x, g_ref[...], mod[3:4], mod[4:5]).astype(BF16)
    a = jax.nn.sigmoid(_dot(h, win_ref[:, QKV_COLS:QKV_COLS + D_MODEL])) * _dot(ya, wba_ref[...])
    b = jax.nn.sigmoid(_dot(h, win_ref[:, QKV_COLS + D_MODEL:])) * _dot(yb, wbb_ref[...])
    o_ref[...] = x + mod[5:6] * _dot((a + b).astype(BF16), wout_ref[...])


def _merge(x, ya, yb, mod, g, w_in, wba, wbb, wout, *, tokens_per_request):
    ctx_steps = x[0].shape[0] // MERGE_TILE
    t = x[0].shape[0] + x[1].shape[0]
    both = lambda w: _ctx_then_lat(w, ctx_steps, MERGE_TILE)
    return pl.pallas_call(
        functools.partial(_merge_kernel, ctx_steps=ctx_steps),
        grid=(t // MERGE_TILE,),
        in_specs=both(D_MODEL) + both(NA_WIDTH) + both(SWA_Q_WIDTH)
        + [_mod_spec_both(ctx_steps, tokens_per_request, MERGE_TILE), _resident((1, D_MODEL)),
           _resident(w_in.shape), _resident(wba.shape), _resident(wbb.shape), _resident(wout.shape)],
        out_specs=pl.BlockSpec((MERGE_TILE, D_MODEL), lambda i: (i, 0)),
        out_shape=jax.ShapeDtypeStruct((t, D_MODEL), F32),
        compiler_params=_params(),
        name="merge",
    )(*x, *ya, *yb, mod, g, w_in, wba, wbb, wout)


def kernel(x_prompt, x_sample, cache_na_k, cache_na_v, cache_swa_k, cache_swa_v, c, c_ctx, w_ada, b_ada,
           norm_ffn1, ffn1_w_gate, ffn1_w_up, ffn1_w_down, norm_mix, w_in, na_rel_bias, swa_sink,
           w_branch_na, w_branch_swa, w_out, norm_ffn2, ffn2_w_gate, ffn2_w_up, ffn2_w_down, norm_final):
    depth = w_ada.shape[0]
    assert depth == 1
    batch, seq, _ = x_prompt.shape
    dec_batch, dec_seq, _ = x_sample.shape
    past = cache_na_k.shape[2]
    layer = 0
    row = lambda v: v.reshape(1, D_MODEL)
    bf = lambda w: w.astype(BF16)

    cond = jnp.zeros((MOD_ROWS, D_MODEL), F32).at[0].set(c_ctx).at[1:1 + dec_batch].set(c)
    mod = _adaln(cond, w_ada[layer], b_ada[layer]).reshape(MOD_ROWS, N_MOD, D_MODEL)

    ffn1 = (row(norm_ffn1[layer]), bf(ffn1_w_gate[layer]), bf(ffn1_w_up[layer]), bf(ffn1_w_down[layer]),
            row(norm_final))
    g_mix = row(norm_mix[layer])
    sink = swa_sink[layer]
    ctx_rows = dict(tokens_per_row=batch * seq, first_row=0)
    lat_rows = dict(tokens_per_row=dec_seq, first_row=1)
    ctx_tokens = batch * seq

    later = (ffn2_w_gate[layer], ffn2_w_up[layer], ffn2_w_down[layer], w_in[layer],
             w_branch_na[layer], w_branch_swa[layer], w_out[layer])
    x_lat, *later = _ffn(x_sample.reshape(dec_batch * dec_seq, D_MODEL), mod, *ffn1,
                         _mod_spec(tile=FFN_TILE, **lat_rows), first=0, final=False, cast=later)
    ffn2 = (row(norm_ffn2[layer]), *later[:3], row(norm_final))
    w_qkv = later[3]
    merge_w = tuple(later[3:])

    x_ctx = _ffn(x_prompt.reshape(ctx_tokens, D_MODEL), mod, *ffn1, _mod_spec(tile=FFN_TILE, **ctx_rows),
                 first=0, final=False)
    qa, qb, *new_cache = _proj_ctx(x_ctx, mod, g_mix, w_qkv, seq=seq, **ctx_rows)
    ya_ctx, yb_ctx = _attn_ctx(sink, qa, qb, *new_cache, seq=seq)

    qa, kal, val, qb, kbl, vbl = _proj_lat(x_lat, mod, g_mix, w_qkv, _rope_tables(dec_seq), **lat_rows)
    transposed = lambda cache: jnp.transpose(cache[:, layer], (0, 2, 3, 1))
    ya_lat = _attn_na(qa, kal, val, transposed(cache_na_k), transposed(cache_na_v),
                      _na_bias_table(na_rel_bias[layer]), n=dec_seq, ctx=past)
    yb_lat = _attn_swa(sink, qb, kbl, vbl, transposed(cache_swa_k), transposed(cache_swa_v), n=dec_seq, ctx=past)

    x = _merge((x_ctx, x_lat), (ya_ctx, ya_lat), (yb_ctx, yb_lat), mod, g_mix, *merge_w, tokens_per_request=dec_seq)
    y_prompt, y_sample = _ffn(x, mod, *ffn2, _mod_spec_both(ctx_tokens // FFN_TILE, dec_seq, FFN_TILE),
                              first=6, final=True, ctx_tokens=ctx_tokens)
    y_prompt = y_prompt.reshape(batch, seq, D_MODEL)
    y_sample = y_sample.reshape(dec_batch, dec_seq, D_MODEL)

    new_cache = [jnp.transpose(t, (0, 3, 1, 2))[:, None] for t in new_cache]
    return (y_prompt, y_sample, *new_cache)
```

```python
import functools

import numpy as np
import jax
import jax.numpy as jnp
from jax import lax
from jax.experimental import pallas as pl
from jax.experimental.pallas import tpu as pltpu

F32 = jnp.float32
BF16 = jnp.bfloat16

D_MODEL = 1024
FFN_DIM = 2816
HEAD_DIM = 64
N_MOD = 9
GRID_W = 64
NA_HEADS = 8
NA_ROWS = 8
NA_COLS = 16
SWA_HEADS = 8
SWA_KV_HEADS = 2
SWA_WINDOW = 128
SWA_BLOCK = 128
ROPE_BASE = 10000.0
EPS = 1e-6
NEG_INF = -1e30
NA_WIDTH = NA_HEADS * HEAD_DIM
SWA_Q_WIDTH = SWA_HEADS * HEAD_DIM
SWA_KV_WIDTH = SWA_KV_HEADS * HEAD_DIM
QKV_COLS = 3 * NA_WIDTH + SWA_Q_WIDTH + 2 * SWA_KV_WIDTH
LANES = 128
MOD_ROWS = 16
VMEM_LIMIT = 56 * 1024 * 1024
TOKEN_TILE = 1024
MERGE_TILE = 512
FFN_TILE = 1024
FFN_CHUNK = 256
ADALN_ROWS = 256
CTX_REQUESTS_PER_STEP = 8
SWA_QTILE = 256
NA_TILE_ROWS = 4
LOG2E = 1.4426950408889634
Q_SCALE = HEAD_DIM ** -0.5 * LOG2E


def _dot(a, b):
    return jnp.dot(a, b, preferred_element_type=F32)


def _dot_nt(a, b):
    return lax.dot_general(a, b, (((1,), (1,)), ((), ())), preferred_element_type=F32)


def _silu(x):
    return x * jax.nn.sigmoid(x)


def _rms(x, g):
    return x * lax.rsqrt(jnp.mean(x * x, axis=-1, keepdims=True) + EPS) * g


def _modulate(x, g, shift, scale):
    return _rms(x, g) * (1.0 + scale) + shift


def _resident(shape):
    nd = len(shape)
    return pl.BlockSpec(shape, lambda *_: (0,) * nd, pipeline_mode=pl.Buffered(1))


def _params():
    return pltpu.CompilerParams(dimension_semantics=("arbitrary",), vmem_limit_bytes=VMEM_LIMIT)


def _adaln_kernel(c_ref, w_ref, b_ref, o_ref):
    @pl.when(pl.program_id(0) == 0)
    def _():
        o_ref[...] = jnp.broadcast_to(b_ref[...], o_ref.shape)

    s = _silu(c_ref[0]).astype(BF16)
    o_ref[...] += _dot(s, w_ref[...].astype(BF16))


def _adaln(cond, w_ada, b_ada):
    k, n = w_ada.shape
    steps = k // ADALN_ROWS
    cond = cond.reshape(MOD_ROWS, steps, ADALN_ROWS).transpose(1, 0, 2)
    return pl.pallas_call(
        _adaln_kernel,
        grid=(steps,),
        in_specs=[pl.BlockSpec((1, MOD_ROWS, ADALN_ROWS), lambda j: (j, 0, 0)),
                  pl.BlockSpec((ADALN_ROWS, n), lambda j: (j, 0)),
                  pl.BlockSpec((1, n), lambda j: (0, 0))],
        out_specs=pl.BlockSpec((MOD_ROWS, n), lambda j: (0, 0)),
        out_shape=jax.ShapeDtypeStruct((MOD_ROWS, n), F32),
        compiler_params=_params(),
        name="adaln",
    )(cond, w_ada, b_ada.reshape(1, n))


def _mod_spec(tokens_per_row, first_row, tile=TOKEN_TILE):
    tiles_per_row = tokens_per_row // tile
    return pl.BlockSpec((1, N_MOD, D_MODEL), lambda i: (first_row + i // tiles_per_row, 0, 0))


def _mod_spec_both(ctx_steps, tokens_per_request, tile):
    tiles_per_request = tokens_per_request // tile
    row = lambda i: jnp.where(i < ctx_steps, 0, 1 + (i - ctx_steps) // tiles_per_request)
    return pl.BlockSpec((1, N_MOD, D_MODEL), lambda i: (row(i), 0, 0))


def _ctx_then_lat(width, ctx_steps, tile):
    ctx = pl.BlockSpec((tile, width), lambda i: (jnp.minimum(i, ctx_steps - 1), 0))
    lat = pl.BlockSpec((tile, width), lambda i: (jnp.maximum(i - ctx_steps, 0), 0))
    return [ctx, lat]


def _ffn_kernel(x_ref, mod_ref, g_ref, wg_ref, wu_ref, wd_ref, gf_ref, *rest, first, final, n_cast, ctx_steps):
    n_out = 1 if ctx_steps is None else 2
    outs, a_scr = rest[n_cast:n_cast + n_out], rest[-1]
    for src, dst in zip(rest[:n_cast], rest[n_cast + n_out:-1]):
        dst[...] = src[...].astype(BF16)
    x = x_ref[...]
    mod = mod_ref[0]
    shift, scale, gate = mod[first:first + 1], mod[first + 1:first + 2], mod[first + 2:first + 3]
    h = _modulate(x, g_ref[...], shift, scale).astype(BF16)
    for c in range(FFN_DIM // FFN_CHUNK):
        sl = slice(c * FFN_CHUNK, (c + 1) * FFN_CHUNK)
        a_scr[:, sl] = (_silu(_dot(h, wg_ref[:, sl])) * _dot(h, wu_ref[:, sl])).astype(BF16)
    y = x + (0.5 * gate) * _dot(a_scr[...], wd_ref[...])
    if final:
        y = _rms(y, gf_ref[...])
    if ctx_steps is None:
        outs[0][...] = y
    else:
        is_ctx = pl.program_id(0) < ctx_steps

        @pl.when(is_ctx)
        def _():
            outs[0][...] = y

        @pl.when(jnp.logical_not(is_ctx))
        def _():
            outs[1][...] = y


def _ffn(x, mod, g, wg, wu, wd, gf, mod_spec, *, first, final, cast=(), ctx_tokens=None):
    t = x.shape[0]
    steps = t // FFN_TILE
    tile = pl.BlockSpec((FFN_TILE, D_MODEL), lambda i: (i, 0))
    chunk = lambda w: pl.BlockSpec((w.shape[0] // steps, w.shape[1]), lambda i: (i, 0))
    assert all(w.shape[0] % (steps * 16) == 0 for w in cast)
    if ctx_tokens is None:
        ctx_steps, out_specs, out_shape = None, [tile], [jax.ShapeDtypeStruct((t, D_MODEL), F32)]
    else:
        ctx_steps = ctx_tokens // FFN_TILE
        out_specs = _ctx_then_lat(D_MODEL, ctx_steps, FFN_TILE)
        out_shape = [jax.ShapeDtypeStruct((n, D_MODEL), F32) for n in (ctx_tokens, t - ctx_tokens)]
    out = pl.pallas_call(
        functools.partial(_ffn_kernel, first=first, final=final, n_cast=len(cast), ctx_steps=ctx_steps),
        grid=(steps,),
        in_specs=[tile, mod_spec, _resident((1, D_MODEL)),
                  _resident(wg.shape), _resident(wu.shape), _resident(wd.shape), _resident((1, D_MODEL))]
        + [chunk(w) for w in cast],
        out_specs=out_specs + [chunk(w) for w in cast],
        out_shape=out_shape + [jax.ShapeDtypeStruct(w.shape, BF16) for w in cast],
        scratch_shapes=[pltpu.VMEM((FFN_TILE, FFN_DIM), BF16)],
        compiler_params=_params(),
        name="ffn",
    )(x, mod, g, wg, wu, wd, gf, *cast)
    return out if len(out) > 1 else out[0]


def _swap16(x):
    lane = lax.broadcasted_iota(jnp.int32, x.shape, 1)
    return jnp.where(lane % 32 < 16, pltpu.roll(x, LANES - 16, 1), pltpu.roll(x, 16, 1))


def _proj_ctx_kernel(x_ref, mod_ref, g_ref, w_ref, qa_ref, qb_ref, ka_ref, va_ref, kb_ref, vb_ref, wkvt_ref):
    @pl.when(pl.program_id(0) == 0)
    def _():
        kv_cols = ((NA_WIDTH, 3 * NA_WIDTH), (3 * NA_WIDTH + SWA_Q_WIDTH, QKV_COLS))
        o = 0
        for lo_col, hi_col in kv_cols:
            wkvt_ref[o:o + hi_col - lo_col, :] = w_ref[:, lo_col:hi_col].astype(F32).T.astype(BF16)
            o += hi_col - lo_col

    mod = mod_ref[0]
    h = _modulate(x_ref[...], g_ref[...], mod[3:4], mod[4:5]).astype(BF16)
    qa_ref[...] = (_dot(h, w_ref[:, :NA_WIDTH]) * Q_SCALE).astype(qa_ref.dtype)
    qb_ref[...] = (_dot(h, w_ref[:, 3 * NA_WIDTH:3 * NA_WIDTH + SWA_Q_WIDTH]) * Q_SCALE).astype(qb_ref.dtype)
    kvt = _dot_nt(wkvt_ref[...], h)
    requests, _, _, seq = ka_ref.shape
    o = 0
    for ref in (ka_ref, va_ref, kb_ref, vb_ref):
        width = ref.shape[1] * HEAD_DIM
        for r in range(requests):
            ref[r] = kvt[o:o + width, r * seq:(r + 1) * seq].reshape(ref.shape[1:])
        o += width


def _rope(y, cos, sin):
    blocks = [y[:, j * LANES:(j + 1) * LANES] for j in range(y.shape[1] // LANES)]
    return [b * cos + _swap16(b) * sin for b in blocks]


def _proj_lat_kernel(x_ref, mod_ref, g_ref, w_ref, cos_ref, sin_ref,
                     qa_ref, ka_ref, va_ref, qb_ref, kb_ref, vb_ref):
    mod = mod_ref[0]
    h = _modulate(x_ref[...], g_ref[...], mod[3:4], mod[4:5]).astype(BF16)
    cos, sin = cos_ref[...], sin_ref[...]
    o = 3 * NA_WIDTH
    for j, b in enumerate(_rope(_dot(h, w_ref[:, o:o + SWA_Q_WIDTH]), cos, sin)):
        qb_ref[:, j * LANES:(j + 1) * LANES] = (b * Q_SCALE).astype(qb_ref.dtype)
    o += SWA_Q_WIDTH
    y = _dot(h, w_ref[:, o:o + 2 * SWA_KV_WIDTH])
    kb_ref[...] = _rope(y[:, :SWA_KV_WIDTH], cos, sin)[0].astype(kb_ref.dtype)
    vb_ref[...] = y[:, SWA_KV_WIDTH:].astype(vb_ref.dtype)
    o = 0
    for ref, scale in ((qa_ref, True), (ka_ref, False), (va_ref, False)):
        y = _dot(h, w_ref[:, o:o + NA_WIDTH])
        ref[...] = (y * Q_SCALE if scale else y).astype(ref.dtype)
        o += NA_WIDTH


def _proj_ctx(x, mod, g, w_qkv, *, seq, tokens_per_row, first_row):
    t = x.shape[0]
    tile = lambda w: pl.BlockSpec((TOKEN_TILE, w), lambda i: (i, 0))
    cache = lambda heads: pl.BlockSpec((TOKEN_TILE // seq, heads, HEAD_DIM, seq), lambda i: (i, 0, 0, 0))
    cache_shape = lambda heads: jax.ShapeDtypeStruct((t // seq, heads, HEAD_DIM, seq), F32)
    kv_heads = (NA_HEADS, NA_HEADS, SWA_KV_HEADS, SWA_KV_HEADS)
    return pl.pallas_call(
        _proj_ctx_kernel,
        grid=(t // TOKEN_TILE,),
        in_specs=[tile(D_MODEL), _mod_spec(tokens_per_row, first_row), _resident((1, D_MODEL)),
                  _resident(w_qkv.shape)],
        out_specs=[tile(NA_WIDTH), tile(SWA_Q_WIDTH)] + [cache(nh) for nh in kv_heads],
        out_shape=[jax.ShapeDtypeStruct((t, NA_WIDTH), BF16), jax.ShapeDtypeStruct((t, SWA_Q_WIDTH), BF16)]
        + [cache_shape(nh) for nh in kv_heads],
        scratch_shapes=[pltpu.VMEM((2 * NA_WIDTH + 2 * SWA_KV_WIDTH, D_MODEL), BF16)],
        compiler_params=_params(),
        name="proj_ctx",
    )(x, mod, g, w_qkv)


def _proj_lat(x, mod, g, w_qkv, rope, *, tokens_per_row, first_row):
    t = x.shape[0]
    tile = lambda w: pl.BlockSpec((TOKEN_TILE, w), lambda i: (i, 0))
    widths = (NA_WIDTH, NA_WIDTH, NA_WIDTH, SWA_Q_WIDTH, SWA_KV_WIDTH, SWA_KV_WIDTH)
    tiles_per_row = tokens_per_row // TOKEN_TILE
    return pl.pallas_call(
        _proj_lat_kernel,
        grid=(t // TOKEN_TILE,),
        in_specs=[tile(D_MODEL), _mod_spec(tokens_per_row, first_row), _resident((1, D_MODEL)),
                  _resident(w_qkv.shape)]
        + [pl.BlockSpec((TOKEN_TILE, LANES), lambda i: (i % tiles_per_row, 0))] * 2,
        out_specs=[tile(w) for w in widths],
        out_shape=[jax.ShapeDtypeStruct((t, w), BF16) for w in widths],
        compiler_params=_params(),
        name="proj_lat",
    )(x, mod, g, w_qkv, *rope)


def _lane_halves(shape):
    lane = lax.broadcasted_iota(jnp.int32, shape, 1) % LANES
    return lane < HEAD_DIM, lane >= HEAD_DIM


def _softmax_cols(cols, sink=None):
    m = jnp.max(functools.reduce(jnp.maximum, cols), axis=-1, keepdims=True)
    if sink is not None:
        m = jnp.maximum(m, sink)
    es = [jnp.exp2(c - m) for c in cols]
    l = jnp.sum(functools.reduce(jnp.add, es), axis=-1, keepdims=True)
    if sink is not None:
        l = l + jnp.exp2(sink - m)
    return es, l


def _cols(x):
    return [x[:, j * LANES:(j + 1) * LANES] for j in range(x.shape[1] // LANES)]


def _attn_ctx_kernel(sink_ref, qa_ref, qb_ref, ka_ref, va_ref, kb_ref, vb_ref, ya_ref, yb_ref):
    requests, _, _, seq = ka_ref.shape
    lo, hi = _lane_halves((seq, LANES))
    top = lax.broadcasted_iota(jnp.int32, (2 * seq, 1), 0) < seq
    zero = jnp.zeros((), BF16)

    def heads_of_block(q, kt, vt, sink):
        q = jnp.concatenate([jnp.where(lo, q, zero), jnp.where(hi, q, zero)], axis=0)
        es, l = _softmax_cols(_cols(_dot(q, kt)), sink)
        o = _dot_nt(jnp.concatenate(es, axis=1).astype(BF16), vt) * (1.0 / l)
        return jnp.where(lo, o[:seq], o[seq:])

    for r in range(requests):
        rows = slice(r * seq, (r + 1) * seq)
        for p in range(NA_HEADS // 2):
            sl = slice(p * LANES, (p + 1) * LANES)
            kt = ka_ref[r, 2 * p:2 * p + 2].reshape(LANES, seq).astype(BF16)
            vt = va_ref[r, 2 * p:2 * p + 2].reshape(LANES, seq).astype(BF16)
            ya_ref[rows, sl] = heads_of_block(qa_ref[rows, sl], kt, vt, None).astype(ya_ref.dtype)
        for p in range(SWA_HEADS // 2):
            sl = slice(p * LANES, (p + 1) * LANES)
            kv = (2 * p) // (SWA_HEADS // SWA_KV_HEADS)
            kt, vt = kb_ref[r, kv].astype(BF16), vb_ref[r, kv].astype(BF16)
            kt, vt = jnp.concatenate([kt, kt], axis=0), jnp.concatenate([vt, vt], axis=0)
            sink = jnp.where(top, sink_ref[2 * p], sink_ref[2 * p + 1]) * LOG2E
            yb_ref[rows, sl] = heads_of_block(qb_ref[rows, sl], kt, vt, sink).astype(yb_ref.dtype)


def _attn_ctx(sink, qa, qb, ka, va, kb, vb, *, seq):
    t = qa.shape[0]
    per_step = CTX_REQUESTS_PER_STEP
    tile = lambda w: pl.BlockSpec((per_step * seq, w), lambda b: (b, 0))
    cache = lambda heads: pl.BlockSpec((per_step, heads, HEAD_DIM, seq), lambda b: (b, 0, 0, 0))
    return pl.pallas_call(
        _attn_ctx_kernel,
        grid=(t // (per_step * seq),),
        in_specs=[pl.BlockSpec(memory_space=pltpu.SMEM), tile(NA_WIDTH), tile(SWA_Q_WIDTH),
                  cache(NA_HEADS), cache(NA_HEADS), cache(SWA_KV_HEADS), cache(SWA_KV_HEADS)],
        out_specs=[tile(NA_WIDTH), tile(SWA_Q_WIDTH)],
        out_shape=[jax.ShapeDtypeStruct((t, NA_WIDTH), BF16), jax.ShapeDtypeStruct((t, SWA_Q_WIDTH), BF16)],
        compiler_params=_params(),
        name="attn_ctx",
    )(sink, qa, qb, ka, va, kb, vb)


def _attn_na_kernel(q_ref, k_ref, v_ref, ck_ref, cv_ref, bias_ref, y_ref, ck_scr, cv_scr, *, n, ctx):
    rows = n // GRID_W
    tile_q = NA_TILE_ROWS * GRID_W
    zero = jnp.zeros((), BF16)
    ck_scr[...] = ck_ref[...].reshape(NA_WIDTH, ctx).astype(BF16)
    cv_scr[...] = cv_ref[...].reshape(NA_WIDTH, ctx).astype(BF16)
    lo, hi = _lane_halves((tile_q, LANES))
    left_half = lax.broadcasted_iota(jnp.int32, (GRID_W, LANES), 1) < GRID_W
    empty = jnp.zeros((GRID_W, LANES), BF16)

    def tile(q0, k0, key_rows, lead, first):
        pairs = key_rows // 2
        for p in range(NA_HEADS // 2):
            sl = slice(p * LANES, (p + 1) * LANES)
            q = q_ref[pl.ds(q0, tile_q), sl]
            q = jnp.concatenate([jnp.where(lo, q, zero), jnp.where(hi, q, zero)], axis=0)
            s_nb = _dot_nt(q, k_ref[pl.ds(k0, key_rows * GRID_W), sl])
            s_ctx = _dot(q, ck_scr[sl, :])
            e_nb, e_ctx, ls = [], [], []
            for idx in range(2):
                for a in range(NA_TILE_ROWS):
                    qa = slice(idx * tile_q + a * GRID_W, idx * tile_q + (a + 1) * GRID_W)
                    inside = lambda i: first[a] <= i < first[a] + NA_ROWS
                    cols, where = [], []
                    for m in range(pairs):
                        if not (inside(2 * m) or inside(2 * m + 1)):
                            continue
                        blk = s_nb[qa, m * LANES:(m + 1) * LANES] + bias_ref[2 * p + idx, 2 * m - a - lead + NA_ROWS]
                        if not inside(2 * m + 1):
                            blk = jnp.where(left_half, blk, NEG_INF)
                        elif not inside(2 * m):
                            blk = jnp.where(left_half, NEG_INF, blk)
                        cols.append(blk)
                        where.append(m)
                    cols += [s_ctx[qa, j * LANES:(j + 1) * LANES] for j in range(ctx // LANES)]
                    es, l = _softmax_cols(cols)
                    es = [e.astype(BF16) for e in es]
                    e_nb.append(jnp.concatenate(
                        [es[where.index(m)] if m in where else empty for m in range(pairs)], axis=1))
                    e_ctx.append(jnp.concatenate(es[len(where):], axis=1))
                    ls.append(l)
            o = (_dot(jnp.concatenate(e_nb, axis=0), v_ref[pl.ds(k0, key_rows * GRID_W), sl])
                 + _dot_nt(jnp.concatenate(e_ctx, axis=0), cv_scr[sl, :]))
            o = o * (1.0 / jnp.concatenate(ls, axis=0))
            y_ref[pl.ds(q0, tile_q), sl] = jnp.where(lo, o[:tile_q], o[tile_q:]).astype(y_ref.dtype)

    half = NA_ROWS // 2
    tile(0, 0, NA_ROWS, 0, (0,) * NA_TILE_ROWS)

    for t in range(1, rows // NA_TILE_ROWS - 1):
        tile(t * tile_q, t * tile_q - half * GRID_W, NA_ROWS + NA_TILE_ROWS, half, tuple(range(NA_TILE_ROWS)))
    tile(n - tile_q, n - NA_ROWS * GRID_W, NA_ROWS, half, (0,) * NA_TILE_ROWS)


def _attn_na(q, k, v, ck, cv, bias, *, n, ctx):
    t = q.shape[0]
    tile = pl.BlockSpec((n, NA_WIDTH), lambda b: (b, 0))
    ctile = pl.BlockSpec((None, NA_HEADS, HEAD_DIM, ctx), lambda b: (b, 0, 0, 0))
    return pl.pallas_call(
        functools.partial(_attn_na_kernel, n=n, ctx=ctx),
        grid=(t // n,),
        in_specs=[tile, tile, tile, ctile, ctile, _resident(bias.shape)],
        out_specs=tile,
        out_shape=jax.ShapeDtypeStruct((t, NA_WIDTH), BF16),
        scratch_shapes=[pltpu.VMEM((NA_WIDTH, ctx), BF16)] * 2,
        compiler_params=_params(),
        name="attn_na",
    )(q, k, v, ck, cv, bias)


def _na_bias_table(rel_bias):
    rows = _na_bias_rows(rel_bias)
    shape = (NA_HEADS, 2 * NA_ROWS, GRID_W, LANES)
    return pl.pallas_call(
        _fill_na_bias,
        grid=(1,),
        in_specs=[pl.BlockSpec(rows.shape, lambda i: (0, 0, 0))],
        out_specs=pl.BlockSpec(shape, lambda i: (0, 0, 0, 0)),
        out_shape=jax.ShapeDtypeStruct(shape, F32),
        compiler_params=_params(),
        name="na_bias",
    )(rows)


def _na_bias_rows(rel_bias):
    side = GRID_W - NA_COLS
    z = jnp.pad(rel_bias * LOG2E, ((0, 0), (0, 0), (side, side + 1)))
    return jnp.pad(z, ((0, 0), (1, 1), (0, 0)), constant_values=NEG_INF)


def _fill_na_bias(rows_ref, bias_scr):
    q = lax.broadcasted_iota(jnp.int32, (GRID_W, LANES), 0)
    lane = lax.broadcasted_iota(jnp.int32, (GRID_W, LANES), 1)
    kc = lane % GRID_W
    col_start = jnp.clip(q - NA_COLS // 2, 0, GRID_W - NA_COLS)
    in_window = (kc >= col_start) & (kc < col_start + NA_COLS)
    for h in range(NA_HEADS):
        blocks = []
        for j in range(2 * NA_ROWS + 1):
            row = jnp.broadcast_to(rows_ref[h, j:j + 1, :], (GRID_W, LANES))
            rolled = pltpu.roll(row, LANES - GRID_W + 1, 1, stride=1, stride_axis=0)
            blocks.append(jnp.where(in_window, rolled, NEG_INF))
        for j in range(2 * NA_ROWS):
            bias_scr[h, j] = jnp.where(lane < GRID_W, blocks[j], pltpu.roll(blocks[j + 1], GRID_W, 1))


def _attn_swa_kernel(sink_ref, q_ref, k_ref, v_ref, ck_ref, cv_ref, y_ref, k_scr, v_scr, ck_scr, cv_scr, *, n, ctx):
    group = SWA_HEADS // SWA_KV_HEADS
    band = SWA_QTILE + 2 * SWA_BLOCK
    for src, csrc, dst, cdst in ((k_ref, ck_ref, k_scr, ck_scr), (v_ref, cv_ref, v_scr, cv_scr)):
        x = src[...].astype(F32)
        xr = pltpu.roll(x, HEAD_DIM, 1)
        first_half, second_half = _lane_halves(x.shape)
        dst[0] = jnp.where(first_half, x, xr).astype(BF16)
        dst[1] = jnp.where(second_half, x, xr).astype(BF16)
        for kv in range(SWA_KV_HEADS):
            c = csrc[kv].astype(BF16)
            cdst[kv] = jnp.concatenate([c, c], axis=0)

    qi = lax.broadcasted_iota(jnp.int32, (SWA_QTILE, band), 0)
    kj = lax.broadcasted_iota(jnp.int32, (SWA_QTILE, band), 1)
    lo, hi = _lane_halves((SWA_QTILE, LANES))
    top = lax.broadcasted_iota(jnp.int32, (2 * SWA_QTILE, 1), 0) < SWA_QTILE
    zero = jnp.zeros((), BF16)

    def block_body(b, carry):
        q0 = pl.multiple_of(b * SWA_QTILE, SWA_QTILE)
        k0 = pl.multiple_of(jnp.clip(q0 - SWA_BLOCK, 0, n - band), SWA_BLOCK)
        mask = jnp.where(jnp.abs(kj - qi + (k0 - q0)) <= SWA_WINDOW, 0.0, NEG_INF)
        mask2 = jnp.concatenate([mask, mask], axis=0)
        for p in range(SWA_HEADS // 2):
            kv = 2 * p // group
            sl = slice(p * LANES, (p + 1) * LANES)
            q = q_ref[pl.ds(q0, SWA_QTILE), sl]
            q = jnp.concatenate([jnp.where(lo, q, zero), jnp.where(hi, q, zero)], axis=0)
            s_band = _dot_nt(q, k_scr[kv, pl.ds(k0, band), :])
            s_ctx = _dot(q, ck_scr[kv])
            sink = jnp.where(top, sink_ref[2 * p], sink_ref[2 * p + 1]) * LOG2E
            es, l = _softmax_cols(_cols(s_band + mask2) + _cols(s_ctx), sink)
            e_band = jnp.concatenate(es[:band // LANES], axis=1).astype(BF16)
            e_ctx = jnp.concatenate(es[band // LANES:], axis=1).astype(BF16)
            o = (_dot(e_band, v_scr[kv, pl.ds(k0, band), :]) + _dot_nt(e_ctx, cv_scr[kv])) * (1.0 / l)
            y_ref[pl.ds(q0, SWA_QTILE), sl] = jnp.where(lo, o[:SWA_QTILE], o[SWA_QTILE:]).astype(y_ref.dtype)
        return carry

    lax.fori_loop(0, n // SWA_QTILE, block_body, 0, unroll=4)


def _attn_swa(sink, q, k, v, ck, cv, *, n, ctx):
    t = q.shape[0]
    tile = lambda w: pl.BlockSpec((n, w), lambda b: (b, 0))
    ctile = pl.BlockSpec((None, SWA_KV_HEADS, HEAD_DIM, ctx), lambda b: (b, 0, 0, 0))
    return pl.pallas_call(
        functools.partial(_attn_swa_kernel, n=n, ctx=ctx),
        grid=(t // n,),
        in_specs=[pl.BlockSpec(memory_space=pltpu.SMEM), tile(SWA_Q_WIDTH), tile(SWA_KV_WIDTH),
                  tile(SWA_KV_WIDTH), ctile, ctile],
        out_specs=tile(SWA_Q_WIDTH),
        out_shape=jax.ShapeDtypeStruct((t, SWA_Q_WIDTH), BF16),
        scratch_shapes=[pltpu.VMEM((SWA_KV_HEADS, n, SWA_KV_WIDTH), BF16)] * 2
        + [pltpu.VMEM((SWA_KV_HEADS, LANES, ctx), BF16)] * 2,
        compiler_params=_params(),
        name="attn_swa",
    )(sink, q, k, v, ck, cv)


def _rope_tables(n):
    half = HEAD_DIM // 4
    freqs = np.power(ROPE_BASE, -np.arange(half, dtype=np.float64) / half)
    t = np.arange(n)
    cos, sin = [], []
    for pos in (t // GRID_W, t % GRID_W):
        ang = pos.astype(np.float64)[:, None] * freqs[None, :]
        cos += [np.cos(ang), np.cos(ang)]
        sin += [-np.sin(ang), np.sin(ang)]
    cos, sin = np.concatenate(cos, axis=-1), np.concatenate(sin, axis=-1)
    reps = LANES // HEAD_DIM
    return jnp.asarray(np.tile(cos, (1, reps)), F32), jnp.asarray(np.tile(sin, (1, reps)), F32)


def _merge_kernel(xc_ref, xl_ref, yac_ref, yal_ref, ybc_ref, ybl_ref, mod_ref, g_ref, win_ref, wba_ref, wbb_ref,
                  wout_ref, o_ref, *, ctx_steps):
    is_ctx = pl.program_id(0) < ctx_steps
    pick = lambda ctx_ref, lat_ref: jnp.where(is_ctx, ctx_ref[...], lat_ref[...])
    x, ya, yb = pick(xc_ref, xl_ref), pick(yac_ref, yal_ref), pick(ybc_ref, ybl_ref)
    mod = mod_ref[0]
    h = _modulate(x, g_ref[...], mod[3:4], mod[4:5]).astype(BF16)
    a = jax.nn.sigmoid(_dot(h, win_ref[:, QKV_COLS:QKV_COLS + D_MODEL])) * _dot(ya, wba_ref[...])
    b = jax.nn.sigmoid(_dot(h, win_ref[:, QKV_COLS + D_MODEL:])) * _dot(yb, wbb_ref[...])
    o_ref[...] = x + mod[5:6] * _dot((a + b).astype(BF16), wout_ref[...])


def _merge(x, ya, yb, mod, g, w_in, wba, wbb, wout, *, tokens_per_request):
    ctx_steps = x[0].shape[0] // MERGE_TILE
    t = x[0].shape[0] + x[1].shape[0]
    both = lambda w: _ctx_then_lat(w, ctx_steps, MERGE_TILE)
    return pl.pallas_call(
        functools.partial(_merge_kernel, ctx_steps=ctx_steps),
        grid=(t // MERGE_TILE,),
        in_specs=both(D_MODEL) + both(NA_WIDTH) + both(SWA_Q_WIDTH)
        + [_mod_spec_both(ctx_steps, tokens_per_request, MERGE_TILE), _resident((1, D_MODEL)),
           _resident(w_in.shape), _resident(wba.shape), _resident(wbb.shape), _resident(wout.shape)],
        out_specs=pl.BlockSpec((MERGE_TILE, D_MODEL), lambda i: (i, 0)),
        out_shape=jax.ShapeDtypeStruct((t, D_MODEL), F32),
        compiler_params=_params(),
        name="merge",
    )(*x, *ya, *yb, mod, g, w_in, wba, wbb, wout)


def kernel(x_prompt, x_sample, cache_na_k, cache_na_v, cache_swa_k, cache_swa_v, c, c_ctx, w_ada, b_ada,
           norm_ffn1, ffn1_w_gate, ffn1_w_up, ffn1_w_down, norm_mix, w_in, na_rel_bias, swa_sink,
           w_branch_na, w_branch_swa, w_out, norm_ffn2, ffn2_w_gate, ffn2_w_up, ffn2_w_down, norm_final):
    depth = w_ada.shape[0]
    assert depth == 1
    batch, seq, _ = x_prompt.shape
    dec_batch, dec_seq, _ = x_sample.shape
    past = cache_na_k.shape[2]
    layer = 0
    row = lambda v: v.reshape(1, D_MODEL)
    bf = lambda w: w.astype(BF16)

    cond = jnp.zeros((MOD_ROWS, D_MODEL), F32).at[0].set(c_ctx).at[1:1 + dec_batch].set(c)
    mod = _adaln(cond, w_ada[layer], b_ada[layer]).reshape(MOD_ROWS, N_MOD, D_MODEL)

    ffn1 = (row(norm_ffn1[layer]), bf(ffn1_w_gate[layer]), bf(ffn1_w_up[layer]), bf(ffn1_w_down[layer]),
            row(norm_final))
    g_mix = row(norm_mix[layer])
    sink = swa_sink[layer]
    ctx_rows = dict(tokens_per_row=batch * seq, first_row=0)
    lat_rows = dict(tokens_per_row=dec_seq, first_row=1)
    ctx_tokens = batch * seq

    later = (ffn2_w_gate[layer], ffn2_w_up[layer], ffn2_w_down[layer], w_in[layer],
             w_branch_na[layer], w_branch_swa[layer], w_out[layer])
    x_lat, *later = _ffn(x_sample.reshape(dec_batch * dec_seq, D_MODEL), mod, *ffn1,
                         _mod_spec(tile=FFN_TILE, **lat_rows), first=0, final=False, cast=later)
    ffn2 = (row(norm_ffn2[layer]), *later[:3], row(norm_final))
    w_qkv = later[3]
    merge_w = tuple(later[3:])

    x_ctx = _ffn(x_prompt.reshape(ctx_tokens, D_MODEL), mod, *ffn1, _mod_spec(tile=FFN_TILE, **ctx_rows),
                 first=0, final=False)
    qa, qb, *new_cache = _proj_ctx(x_ctx, mod, g_mix, w_qkv, seq=seq, **ctx_rows)
    ya_ctx, yb_ctx = _attn_ctx(sink, qa, qb, *new_cache, seq=seq)

    qa, kal, val, qb, kbl, vbl = _proj_lat(x_lat, mod, g_mix, w_qkv, _rope_tables(dec_seq), **lat_rows)
    transposed = lambda cache: jnp.transpose(cache[:, layer], (0, 2, 3, 1))
    ya_lat = _attn_na(qa, kal, val, transposed(cache_na_k), transposed(cache_na_v),
                      _na_bias_table(na_rel_bias[layer]), n=dec_seq, ctx=past)
    yb_lat = _attn_swa(sink, qb, kbl, vbl, transposed(cache_swa_k), transposed(cache_swa_v), n=dec_seq, ctx=past)

    x = _merge((x_ctx, x_lat), (ya_ctx, ya_lat), (yb_ctx, yb_lat), mod, g_mix, *merge_w, tokens_per_request=dec_seq)
    y_prompt, y_sample = _ffn(x, mod, *ffn2, _mod_spec_both(ctx_tokens // FFN_TILE, dec_seq, FFN_TILE),
                              first=6, final=True, ctx_tokens=ctx_tokens)
    y_prompt = y_prompt.reshape(batch, seq, D_MODEL)
    y_sample = y_sample.reshape(dec_batch, dec_seq, D_MODEL)

    new_cache = [jnp.transpose(t, (0, 3, 1, 2))[:, None] for t in new_cache]
    return (y_prompt, y_sample, *new_cache)
```
